```python
import math
import jax
import jax.numpy as jnp
from jax import lax
import numpy as np


D_MODEL = 1024
BATCH = 4
SEQ = 4096
DEPTH = 2

GRID_W = 64
CTX_LEN = 256
HEAD_DIM = 64
NA_WIDTH = D_MODEL // 2
NA_HEADS = NA_WIDTH // HEAD_DIM
S5_WIDTH = D_MODEL - NA_WIDTH
S5_CH = 16
S5_GROUPS = S5_WIDTH // S5_CH
S5_STATE = 64
WIN_ROWS_MAX = 8
WIN_COLS = 16
IN_COLS = 3 * NA_WIDTH + S5_WIDTH
N_EXPERTS = 16
N_EXPERT_GROUPS = 4
EXPERTS_PER_GROUP = N_EXPERTS // N_EXPERT_GROUPS
TOP_K = 2
D_EXPERT = D_MODEL
N_MOD = 6
EPS = 1e-6
DT_MIN = 1e-3
DT_MAX = 1e-1

kernel_name = 'hybrid_natten_s5_moe_dit'


def rms_norm(x, g):
    xf = x.astype(jnp.float32)
    y = xf * lax.rsqrt(jnp.mean(xf * xf, axis=-1, keepdims=True) + EPS)
    return (y * g.astype(jnp.float32)).astype(x.dtype)


def to_heads(t):
    return t.reshape(t.shape[0], t.shape[1], NA_HEADS, HEAD_DIM)


def window_starts(n, k):
    return jnp.clip(jnp.arange(n) - k // 2, 0, n - k)


def neighborhood_attention(q, k, v, k_ctx, v_ctx, rpb):
    b, l, h, dh = q.shape
    rows = l // GRID_W
    kh = min(WIN_ROWS_MAX, rows)
    kw = WIN_COLS
    scale = dh ** -0.5
    qg = q.reshape(b, rows, GRID_W, h, dh)
    kg = k.reshape(b, rows, GRID_W, h, dh)
    vg = v.reshape(b, rows, GRID_W, h, dh)
    row_start = window_starts(rows, kh)
    col_idx = window_starts(GRID_W, kw)[:, None] + jnp.arange(kw)[None, :]
    col_off = col_idx - jnp.arange(GRID_W)[:, None] + (WIN_COLS - 1)
    rpb_f = rpb.astype(jnp.float32)

    def one_row(r):
        rs = row_start[r]
        q_r = lax.dynamic_index_in_dim(qg, r, axis=1, keepdims=False)
        k_nb = lax.dynamic_slice_in_dim(kg, rs, kh, axis=1)[:, :, col_idx]
        v_nb = lax.dynamic_slice_in_dim(vg, rs, kh, axis=1)[:, :, col_idx]
        row_off = rs + jnp.arange(kh) - r + (WIN_ROWS_MAX - 1)
        bias = rpb_f[:, row_off[None, :, None], col_off[:, None, :]]
        s_nb = jnp.einsum('bwhd,biwjhd->bhwij', q_r, k_nb).astype(jnp.float32) * scale + bias[None]
        s_ctx = jnp.einsum('bwhd,bchd->bhwc', q_r, k_ctx).astype(jnp.float32) * scale
        s = jnp.concatenate([s_nb.reshape(b, h, GRID_W, kh * kw), s_ctx], axis=-1)
        p = jax.nn.softmax(s, axis=-1).astype(v.dtype)
        p_nb = p[..., :kh * kw].reshape(b, h, GRID_W, kh, kw)
        p_ctx = p[..., kh * kw:]
        return (jnp.einsum('bhwij,biwjhd->bwhd', p_nb, v_nb)
                + jnp.einsum('bhwc,bchd->bwhd', p_ctx, v_ctx))

    out = lax.map(one_row, jnp.arange(rows))
    return jnp.moveaxis(out, 0, 1).reshape(b, l, h * dh)


def context_attention(q, k, v):
    b, n, h, dh = q.shape
    s = jnp.einsum('bqhd,bkhd->bhqk', q, k).astype(jnp.float32) * (dh ** -0.5)
    p = jax.nn.softmax(s, axis=-1).astype(v.dtype)
    return jnp.einsum('bhqk,bkhd->bqhd', p, v).reshape(b, n, h * dh)


def s5_discretize(lam_re, lam_im, log_dt, b_re, b_im):
    f32 = jnp.float32
    lr = lam_re.astype(f32)
    li = lam_im.astype(f32)
    dt = jnp.exp(log_dt.astype(f32))[:, None]
    mag = jnp.exp(lr * dt)
    ab_re = mag * jnp.cos(li * dt)
    ab_im = mag * jnp.sin(li * dt)
    den = lr * lr + li * li
    nr = ab_re - 1.0
    z_re = (nr * lr + ab_im * li) / den
    z_im = (ab_im * lr - nr * li) / den
    br = b_re.astype(f32)
    bi = b_im.astype(f32)
    bb_re = z_re[..., None] * br - z_im[..., None] * bi
    bb_im = z_re[..., None] * bi + z_im[..., None] * br
    return ab_re, ab_im, bb_re, bb_im


def _linear_recurrence_combine(e1, e2):
    a1r, a1i, b1r, b1i = e1
    a2r, a2i, b2r, b2i = e2
    return (a2r * a1r - a2i * a1i,
            a2r * a1i + a2i * a1r,
            a2r * b1r - a2i * b1i + b2r,
            a2r * b1i + a2i * b1r + b2i)


def diag_scan(ab_re, ab_im, bu_re, bu_im, reverse, h0=None):
    n = bu_re.shape[1]
    if h0 is not None:
        first = n - 1 if reverse else 0
        h0_re, h0_im = h0
        bu_re = bu_re.at[:, first].add(ab_re * h0_re - ab_im * h0_im)
        bu_im = bu_im.at[:, first].add(ab_re * h0_im + ab_im * h0_re)
    shape = (1, n) + ab_re.shape
    a_re = jnp.broadcast_to(ab_re, shape)
    a_im = jnp.broadcast_to(ab_im, shape)
    _, _, h_re, h_im = lax.associative_scan(
        _linear_recurrence_combine, (a_re, a_im, bu_re, bu_im), reverse=reverse, axis=1)
    return h_re, h_im


def s5_mixer(u_lat, u_ctx, lam_re, lam_im, log_dt, b_re, b_im, c_re, c_im, d_skip,
             w_glu, b_glu, with_ctx_out):
    f32 = jnp.float32
    bsz, n_lat, _ = u_lat.shape
    n_ctx = u_ctx.shape[1]
    ul = u_lat.astype(f32).reshape(bsz, n_lat, S5_GROUPS, S5_CH)
    uc = u_ctx.astype(f32).reshape(bsz, n_ctx, S5_GROUPS, S5_CH)
    y_lat = jnp.zeros_like(ul)
    y_ctx = jnp.zeros_like(uc) if with_ctx_out else None
    for direction in range(2):
        reverse = direction == 1
        ab_re, ab_im, bb_re, bb_im = s5_discretize(
            lam_re[direction], lam_im[direction], log_dt[direction], b_re[direction], b_im[direction])
        cr = c_re[direction].astype(f32)
        ci = c_im[direction].astype(f32)
        hc_re, hc_im = diag_scan(ab_re, ab_im,
                                 jnp.einsum('bngh,gph->bngp', uc, bb_re),
                                 jnp.einsum('bngh,gph->bngp', uc, bb_im), reverse)
        end = 0 if reverse else n_ctx - 1
        h0 = (hc_re[:, end], hc_im[:, end])
        hl_re, hl_im = diag_scan(ab_re, ab_im,
                                 jnp.einsum('blgh,gph->blgp', ul, bb_re),
                                 jnp.einsum('blgh,gph->blgp', ul, bb_im), reverse, h0)
        y_lat = y_lat + jnp.einsum('blgp,ghp->blgh', hl_re, cr) - jnp.einsum('blgp,ghp->blgh', hl_im, ci)
        if with_ctx_out:
            y_ctx = y_ctx + jnp.einsum('bngp,ghp->bngh', hc_re, cr) - jnp.einsum('bngp,ghp->bngh', hc_im, ci)
    d_g = d_skip.astype(f32).reshape(S5_GROUPS, S5_CH)

    def readout(y, u):
        y = (y + d_g * u).reshape(y.shape[0], y.shape[1], S5_WIDTH)
        y = jax.nn.gelu(y).astype(u_lat.dtype)
        return y * jax.nn.sigmoid(y @ w_glu + b_glu)

    out_lat = readout(y_lat, ul)
    out_ctx = readout(y_ctx, uc) if with_ctx_out else None
    return out_lat, out_ctx


def shared_router(h, router_w, router_bias):
    f32 = jnp.float32
    n_tok = h.shape[0]
    probs = jax.nn.softmax((h @ router_w).astype(f32), axis=-1)
    sel = (probs + router_bias.astype(f32)).reshape(n_tok, N_EXPERT_GROUPS, EXPERTS_PER_GROUP)
    group_score = jnp.sum(lax.top_k(sel, TOP_K)[0], axis=-1)
    group = jnp.argmax(group_score, axis=-1)
    sel_in = sel[jnp.arange(n_tok), group]
    _, local = lax.top_k(sel_in, TOP_K)
    idx = group[:, None] * EXPERTS_PER_GROUP + local
    w = jnp.take_along_axis(probs, idx, axis=-1)
    w = w / jnp.sum(w, axis=-1, keepdims=True)
    return jnp.sum(jax.nn.one_hot(idx, N_EXPERTS, dtype=f32) * w[..., None], axis=1)


def moe_ffn(h, gates, w_gate, w_up, w_down):
    out = jnp.zeros_like(h)
    for e in range(N_EXPERTS):
        a = jax.nn.silu(h @ w_gate[e]) * (h @ w_up[e])
        out = out + gates[:, e:e + 1].astype(h.dtype) * (a @ w_down[e])
    return out


def setup_inputs(seed: int = 0) -> dict:
    key = jax.random.key(seed)
    ks = iter(jax.random.split(key, 40))
    f32 = jnp.float32

    def nrm(shape, std):
        return std * jax.random.normal(next(ks), shape, f32)

    g, p, hc = S5_GROUPS, S5_STATE, S5_CH
    n_idx = jnp.arange(S5_STATE, dtype=f32)
    return {
        'x': nrm((BATCH, SEQ, D_MODEL), 1.0),
        'c': nrm((BATCH, D_MODEL), 1.0),
        'ctx': nrm((BATCH, CTX_LEN, D_MODEL), 1.0),
        'c_ctx': nrm((D_MODEL,), 1.0),
        'w_mod': nrm((DEPTH, D_MODEL, N_MOD * D_MODEL), 0.5 * D_MODEL ** -0.5),
        'b_mod': nrm((DEPTH, N_MOD * D_MODEL), 0.02),
        'norm1_g': 1.0 + nrm((DEPTH, D_MODEL), 0.05),
        'norm2_g': 1.0 + nrm((DEPTH, D_MODEL), 0.05),
        'w_in': nrm((DEPTH, D_MODEL, IN_COLS), D_MODEL ** -0.5),
        'w_out': nrm((DEPTH, NA_WIDTH + S5_WIDTH, D_MODEL), (NA_WIDTH + S5_WIDTH) ** -0.5),
        'q_norm_g': 1.0 + nrm((DEPTH, HEAD_DIM), 0.05),
        'k_norm_g': 1.0 + nrm((DEPTH, HEAD_DIM), 0.05),
        'na_rpb': nrm((DEPTH, NA_HEADS, 2 * WIN_ROWS_MAX - 1, 2 * WIN_COLS - 1), 0.05),
        's5_lam_re': -0.5 + nrm((DEPTH, 2, g, p), 0.01),
        's5_lam_im': jnp.pi * n_idx + nrm((DEPTH, 2, g, p), 0.01),
        's5_log_dt': jax.random.uniform(next(ks), (DEPTH, 2, g), f32,
                                        math.log(DT_MIN), math.log(DT_MAX)),
        's5_b_re': nrm((DEPTH, 2, g, p, hc), (2 * hc) ** -0.5),
        's5_b_im': nrm((DEPTH, 2, g, p, hc), (2 * hc) ** -0.5),
        's5_c_re': nrm((DEPTH, 2, g, hc, p), 0.25),
        's5_c_im': nrm((DEPTH, 2, g, hc, p), 0.25),
        's5_d': nrm((DEPTH, S5_WIDTH), 1.0),
        's5_w_glu': nrm((DEPTH, S5_WIDTH, S5_WIDTH), S5_WIDTH ** -0.5),
        's5_b_glu': nrm((DEPTH, S5_WIDTH), 0.01),
        'router_w': nrm((D_MODEL, N_EXPERTS), D_MODEL ** -0.5),
        'router_bias': nrm((N_EXPERTS,), 0.01),
        'moe_w_gate': nrm((DEPTH, N_EXPERTS, D_MODEL, D_EXPERT), D_MODEL ** -0.5),
        'moe_w_up': nrm((DEPTH, N_EXPERTS, D_MODEL, D_EXPERT), D_MODEL ** -0.5),
        'moe_w_down': nrm((DEPTH, N_EXPERTS, D_EXPERT, D_MODEL), D_EXPERT ** -0.5),
    }


def reference(x, c, ctx, c_ctx, w_mod, b_mod, norm1_g, norm2_g, w_in, w_out, q_norm_g, k_norm_g,
              na_rpb, s5_lam_re, s5_lam_im, s5_log_dt, s5_b_re, s5_b_im, s5_c_re, s5_c_im, s5_d,
              s5_w_glu, s5_b_glu, router_w, router_bias, moe_w_gate, moe_w_up, moe_w_down):
    bsz, n_lat, d = x.shape
    n_ctx = ctx.shape[1]
    silu_c = jax.nn.silu(c)[:, None, :]
    silu_cc = jax.nn.silu(c_ctx)
    h_lat, h_ctx = x, ctx
    splits = [NA_WIDTH, 2 * NA_WIDTH, 3 * NA_WIDTH]
    for layer in range(DEPTH):
        ctx_out = layer < DEPTH - 1
        m_lat = jnp.split(silu_c @ w_mod[layer] + b_mod[layer], N_MOD, axis=-1)
        m_ctx = jnp.split(silu_cc @ w_mod[layer] + b_mod[layer], N_MOD, axis=-1)

        a_lat = rms_norm(h_lat, norm1_g[layer]) * (1.0 + m_lat[1]) + m_lat[0]
        a_ctx = rms_norm(h_ctx, norm1_g[layer]) * (1.0 + m_ctx[1]) + m_ctx[0]
        q_l, k_l, v_l, u_l = jnp.split(a_lat @ w_in[layer], splits, axis=-1)
        q_c, k_c, v_c, u_c = jnp.split(a_ctx @ w_in[layer], splits, axis=-1)
        q_l = rms_norm(to_heads(q_l), q_norm_g[layer])
        k_l = rms_norm(to_heads(k_l), k_norm_g[layer])
        v_l = to_heads(v_l)
        k_c = rms_norm(to_heads(k_c), k_norm_g[layer])
        v_c = to_heads(v_c)
        na_lat = neighborhood_attention(q_l, k_l, v_l, k_c, v_c, na_rpb[layer])
        s5_lat, s5_ctx = s5_mixer(u_l, u_c, s5_lam_re[layer], s5_lam_im[layer], s5_log_dt[layer],
                                  s5_b_re[layer], s5_b_im[layer], s5_c_re[layer], s5_c_im[layer],
                                  s5_d[layer], s5_w_glu[layer], s5_b_glu[layer], ctx_out)
        h_lat = h_lat + m_lat[2] * (jnp.concatenate([na_lat, s5_lat], axis=-1) @ w_out[layer])
        if ctx_out:
            q_c = rms_norm(to_heads(q_c), q_norm_g[layer])
            na_ctx = context_attention(q_c, k_c, v_c)
            h_ctx = h_ctx + m_ctx[2] * (jnp.concatenate([na_ctx, s5_ctx], axis=-1) @ w_out[layer])

        f_lat = rms_norm(h_lat, norm2_g[layer]) * (1.0 + m_lat[4]) + m_lat[3]
        tokens = f_lat.reshape(bsz * n_lat, d)
        if ctx_out:
            f_ctx = rms_norm(h_ctx, norm2_g[layer]) * (1.0 + m_ctx[4]) + m_ctx[3]
            tokens = jnp.concatenate([tokens, f_ctx.reshape(bsz * n_ctx, d)], axis=0)
        gates = shared_router(tokens, router_w, router_bias)
        y = moe_ffn(tokens, gates, moe_w_gate[layer], moe_w_up[layer], moe_w_down[layer])
        h_lat = h_lat + m_lat[5] * y[:bsz * n_lat].reshape(bsz, n_lat, d)
        if ctx_out:
            h_ctx = h_ctx + m_ctx[5] * y[bsz * n_lat:].reshape(bsz, n_ctx, d)
    return h_lat
```

```python
import functools
import math

import jax
import jax.numpy as jnp
from jax import lax
from jax.experimental import pallas as pl
from jax.experimental.pallas import tpu as pltpu

F32 = jnp.float32
BF16 = jnp.bfloat16
HIGHEST = lax.Precision.HIGHEST

D_MODEL = 1024
GRID_W = 64
HEAD_DIM = 64
NA_WIDTH = 512
S5_WIDTH = 512
S5_CH = 16
S5_GROUPS = 32
S5_STATE = 64
WIN_ROWS = 8
WIN_COLS = 16
N_EXPERTS = 16
N_GROUPS = 4
EPG = 4
N_MOD = 6
EPS = 1e-6

LANES = 128
SUBLANES = 8
VMEM_LIMIT = 56 * 1024 * 1024

TM_PROJ = 256
Q_ROWS = 8
Q_COLS = 16
K_ROWS = 16
K_COLS = 32
CHUNK = 16
PAIRS_PER_STEP = 2
TM_EXP = 256
TM_ROUTE = 512
MASK_VALUE = -1e30


def _cparams(sem):
    return pltpu.CompilerParams(dimension_semantics=sem, vmem_limit_bytes=VMEM_LIMIT)


def _dot(a, b):
    return jnp.dot(a, b, preferred_element_type=F32)


def _dot_nt(a, b):
    return lax.dot_general(a, b, (((1,), (1,)), ((), ())), preferred_element_type=F32)


def _mod_kernel(c_ref, w_ref, b_ref, o_ref):
    a = c_ref[...]
    a = a * jax.nn.sigmoid(a)
    o_ref[0] = jnp.dot(a, w_ref[0], precision=HIGHEST, preferred_element_type=F32) + b_ref[0]


def _modulation(c_rows, w_mod, b_mod):
    depth, d, n = w_mod.shape
    tn = 1536
    return pl.pallas_call(
        _mod_kernel,
        out_shape=jax.ShapeDtypeStruct((depth, SUBLANES, n), F32),
        grid=(depth, n // tn),
        in_specs=[
            pl.BlockSpec((SUBLANES, d), lambda l, j: (0, 0)),
            pl.BlockSpec((1, d, tn), lambda l, j: (l, 0, j)),
            pl.BlockSpec((1, 1, tn), lambda l, j: (l, 0, j)),
        ],
        out_specs=pl.BlockSpec((1, SUBLANES, tn), lambda l, j: (l, 0, j)),
        compiler_params=_cparams(("arbitrary", "arbitrary")),
        name="modulation",
    )(c_rows, w_mod, b_mod.reshape(depth, 1, n))


def _inproj_kernel(x_ref, g_ref, mod_ref, w_ref, o_ref):
    x = x_ref[...]
    ms = jnp.mean(x * x, axis=-1, keepdims=True)
    y = x * lax.rsqrt(ms + EPS) * g_ref[...]
    shift = mod_ref[0, :, 0:D_MODEL]
    scale = mod_ref[0, :, D_MODEL:2 * D_MODEL]
    a = y * (1.0 + scale) + shift
    o_ref[...] = _dot(a.astype(BF16), w_ref[...])


def _mod_row_map(rows_per_batch, mod_row0, per_batch):
    tiles_per_batch = rows_per_batch // TM_PROJ
    if per_batch:
        return lambda i: (mod_row0 + i // tiles_per_batch, 0, 0)
    return lambda i: (mod_row0, 0, 0)


def _in_projection(x2d, g, mod3, w_bf16, mod_map):
    r, d = x2d.shape
    n = w_bf16.shape[1]
    return pl.pallas_call(
        _inproj_kernel,
        out_shape=jax.ShapeDtypeStruct((r, n), F32),
        grid=(r // TM_PROJ,),
        in_specs=[
            pl.BlockSpec((TM_PROJ, d), lambda i: (i, 0)),
            pl.BlockSpec((1, d), lambda i: (0, 0)),
            pl.BlockSpec((1, 1, N_MOD * d), mod_map),
            pl.BlockSpec((d, n), lambda i: (0, 0)),
        ],
        out_specs=pl.BlockSpec((TM_PROJ, n), lambda i: (i, 0)),
        compiler_params=_cparams(("parallel",)),
        name="in_projection",
    )(x2d, g.reshape(1, d), mod3, w_bf16)


def _pair_rms(x, g, lo):
    ss = x * x
    sa = jnp.sum(jnp.where(lo, ss, 0.0), axis=-1, keepdims=True)
    sb = jnp.sum(jnp.where(lo, 0.0, ss), axis=-1, keepdims=True)
    ms = jnp.where(lo, sa, sb) * (1.0 / HEAD_DIM)
    return x * lax.rsqrt(ms + EPS) * g


def _softmax_pv(qm, kw, kcb, vw, vcb, bias):
    s_nb = _dot_nt(qm, kw) + bias
    s_cx = _dot_nt(qm, kcb)
    m = jnp.maximum(jnp.max(s_nb, axis=-1, keepdims=True), jnp.max(s_cx, axis=-1, keepdims=True))
    p_nb = jnp.exp(s_nb - m)
    p_cx = jnp.exp(s_cx - m)
    l = jnp.sum(p_nb, axis=-1, keepdims=True) + jnp.sum(p_cx, axis=-1, keepdims=True)
    o = _dot(p_nb.astype(BF16), vw) + _dot(p_cx.astype(BF16), vcb)
    return o / l


def _na_kernel(q_ref, k_ref, v_ref, kc_ref, vc_ref, bias_ref, qg_ref, kg_ref, o_ref, kn_scr, kcn_scr):
    t = pl.program_id(2)
    lo = lax.broadcasted_iota(jnp.int32, (1, LANES), 1) < HEAD_DIM
    n_rows = k_ref.shape[1]
    col_tiles = GRID_W // Q_COLS

    @pl.when(t == 0)
    def _():
        def body(r, carry):
            kn_scr[r] = _pair_rms(k_ref[0, r], kg_ref[...], lo)
            return carry
        lax.fori_loop(0, n_rows, body, 0)
        kcn_scr[...] = _pair_rms(kc_ref[0], kg_ref[...], lo).astype(BF16)

    i = t // col_tiles
    j = t % col_tiles
    kr0 = jnp.clip(Q_ROWS * i - WIN_ROWS // 2, 0, n_rows - K_ROWS)
    kc0 = pl.multiple_of(jnp.clip(Q_COLS * j - WIN_COLS // 2, 0, GRID_W - K_COLS), SUBLANES)
    rt = jnp.where(i == 0, 0, jnp.where(i == n_rows // Q_ROWS - 1, 2, 1))
    ct = jnp.where(j == 0, 0, jnp.where(j == col_tiles - 1, 2, 1))
    typ = rt * 3 + ct

    nq = Q_ROWS * Q_COLS
    nk = K_ROWS * K_COLS
    q = q_ref[0].reshape(nq, LANES)
    qn = _pair_rms(q, qg_ref[...], lo) * (HEAD_DIM ** -0.5)
    kw = kn_scr[pl.ds(kr0, K_ROWS), pl.ds(kc0, K_COLS), :].reshape(nk, LANES).astype(BF16)
    vw = v_ref[0, pl.ds(kr0, K_ROWS), pl.ds(kc0, K_COLS), :].reshape(nk, LANES).astype(BF16)
    kcb = kcn_scr[...]
    vcb = vc_ref[0].astype(BF16)
    o_a = _softmax_pv(jnp.where(lo, qn, 0.0).astype(BF16), kw, kcb, vw, vcb, bias_ref[0, typ, 0])
    o_b = _softmax_pv(jnp.where(lo, 0.0, qn).astype(BF16), kw, kcb, vw, vcb, bias_ref[0, typ, 1])
    o_ref[0] = jnp.where(lo, o_a, o_b).reshape(Q_ROWS, Q_COLS, LANES)


def _na_bias_table(rpb):
    h = rpb.shape[0]
    rpb = rpb.astype(F32)

    def axis_tables(q_n, k_n, win, d_opts, origin_opts, extent):
        onehots, valids = [], []
        for d_rel, origin in zip(d_opts, origin_opts):
            qa = origin + jnp.arange(q_n)[:, None]
            ka = origin + d_rel + jnp.arange(k_n)[None, :]
            start = jnp.clip(qa - win // 2, 0, extent - win)
            valid = (ka >= start) & (ka < start + win)
            off = jnp.clip(ka - qa + win - 1, 0, 2 * win - 2)
            onehots.append(jax.nn.one_hot(off, 2 * win - 1, dtype=F32))
            valids.append(valid)
        return jnp.stack(onehots), jnp.stack(valids)

    rows = GRID_W
    oh_r, va_r = axis_tables(Q_ROWS, K_ROWS, WIN_ROWS, (0, -4, -8), (0, Q_ROWS, rows - Q_ROWS), rows)
    oh_c, va_c = axis_tables(Q_COLS, K_COLS, WIN_COLS, (0, -8, -16), (0, Q_COLS, GRID_W - Q_COLS), GRID_W)
    t1 = jnp.einsum('raku,huv->hrakv', oh_r, rpb, precision=HIGHEST)
    bias = jnp.einsum('hrakv,cbjv->hrcabkj', t1, oh_c, precision=HIGHEST)
    valid = va_r[:, None, :, None, :, None] & va_c[None, :, None, :, None, :]
    bias = jnp.where(valid[None], bias, MASK_VALUE)
    bias = bias.reshape(h // 2, 2, 9, Q_ROWS * Q_COLS, K_ROWS * K_COLS)
    return jnp.transpose(bias, (0, 2, 1, 3, 4))


def _neighborhood_attention(qkvu_lat, qkvu_ctx, bias_tab, qg, kg, bsz):
    n_lat = qkvu_lat.shape[0] // bsz
    n_ctx = qkvu_ctx.shape[0] // bsz
    rows = n_lat // GRID_W
    n_cols = qkvu_lat.shape[1]
    lat4 = qkvu_lat.reshape(bsz, rows, GRID_W, n_cols)
    ctx3 = qkvu_ctx.reshape(bsz, n_ctx, n_cols)
    n_pairs = NA_WIDTH // LANES
    col_tiles = GRID_W // Q_COLS
    n_tiles = (rows // Q_ROWS) * col_tiles
    g2 = lambda g: jnp.concatenate([g, g]).reshape(1, LANES).astype(F32)
    out = pl.pallas_call(
        _na_kernel,
        out_shape=jax.ShapeDtypeStruct((bsz, rows, GRID_W, NA_WIDTH), F32),
        grid=(bsz, n_pairs, n_tiles),
        in_specs=[
            pl.BlockSpec((1, Q_ROWS, Q_COLS, LANES), lambda b, p, t: (b, t // col_tiles, t % col_tiles, p)),
            pl.BlockSpec((1, rows, GRID_W, LANES), lambda b, p, t: (b, 0, 0, n_pairs + p)),
            pl.BlockSpec((1, rows, GRID_W, LANES), lambda b, p, t: (b, 0, 0, 2 * n_pairs + p)),
            pl.BlockSpec((1, n_ctx, LANES), lambda b, p, t: (b, 0, n_pairs + p)),
            pl.BlockSpec((1, n_ctx, LANES), lambda b, p, t: (b, 0, 2 * n_pairs + p)),
            pl.BlockSpec((1, 9, 2, Q_ROWS * Q_COLS, K_ROWS * K_COLS), lambda b, p, t: (p, 0, 0, 0, 0)),
            pl.BlockSpec((1, LANES), lambda b, p, t: (0, 0)),
            pl.BlockSpec((1, LANES), lambda b, p, t: (0, 0)),
        ],
        out_specs=pl.BlockSpec((1, Q_ROWS, Q_COLS, LANES),
                               lambda b, p, t: (b, t // col_tiles, t % col_tiles, p)),
        scratch_shapes=[pltpu.VMEM((rows, GRID_W, LANES), F32), pltpu.VMEM((n_ctx, LANES), BF16)],
        compiler_params=_cparams(("parallel", "parallel", "arbitrary")),
        name="neighborhood_attention",
    )(lat4, lat4, lat4, ctx3, ctx3, bias_tab, g2(qg), g2(kg))
    return out.reshape(bsz * n_lat, NA_WIDTH)


def _ctx_attn_kernel(q_ref, k_ref, v_ref, qg_ref, kg_ref, o_ref):
    lo = lax.broadcasted_iota(jnp.int32, (1, LANES), 1) < HEAD_DIM
    qn = _pair_rms(q_ref[0], qg_ref[...], lo) * (HEAD_DIM ** -0.5)
    kn = _pair_rms(k_ref[0], kg_ref[...], lo).astype(BF16)
    vb = v_ref[0].astype(BF16)

    def one(qm):
        s = _dot_nt(qm, kn)
        m = jnp.max(s, axis=-1, keepdims=True)
        p = jnp.exp(s - m)
        l = jnp.sum(p, axis=-1, keepdims=True)
        return _dot(p.astype(BF16), vb) / l

    o_a = one(jnp.where(lo, qn, 0.0).astype(BF16))
    o_b = one(jnp.where(lo, 0.0, qn).astype(BF16))
    o_ref[0] = jnp.where(lo, o_a, o_b)


def _context_attention(qkvu_ctx, qg, kg, bsz):
    n_ctx = qkvu_ctx.shape[0] // bsz
    ctx3 = qkvu_ctx.reshape(bsz, n_ctx, qkvu_ctx.shape[1])
    n_pairs = NA_WIDTH // LANES
    g2 = lambda g: jnp.concatenate([g, g]).reshape(1, LANES).astype(F32)
    out = pl.pallas_call(
        _ctx_attn_kernel,
        out_shape=jax.ShapeDtypeStruct((bsz, n_ctx, NA_WIDTH), F32),
        grid=(bsz, n_pairs),
        in_specs=[
            pl.BlockSpec((1, n_ctx, LANES), lambda b, p: (b, 0, p)),
            pl.BlockSpec((1, n_ctx, LANES), lambda b, p: (b, 0, n_pairs + p)),
            pl.BlockSpec((1, n_ctx, LANES), lambda b, p: (b, 0, 2 * n_pairs + p)),
            pl.BlockSpec((1, LANES), lambda b, p: (0, 0)),
            pl.BlockSpec((1, LANES), lambda b, p: (0, 0)),
        ],
        out_specs=pl.BlockSpec((1, n_ctx, LANES), lambda b, p: (b, 0, p)),
        compiler_params=_cparams(("parallel", "parallel")),
        name="context_attention",
    )(ctx3, ctx3, ctx3, g2(qg), g2(kg))
    return out.reshape(bsz * n_ctx, NA_WIDTH)


def _s5_kernel(x_ref, w_ref, m_ref, v_ref, a_ref, y_ref, s_scr, hf_scr, hr_scr, *, n_ctx_chunks):
    rows = x_ref.shape[1]
    n_chunks = rows // SUBLANES
    n_blk = 8
    rb = rows // n_blk
    half = 2 * LANES
    first_group = (lax.broadcasted_iota(jnp.int32, (rb, half), 0) & (SUBLANES // 2)) == 0

    for p in range(PAIRS_PER_STEP):
        for b in range(n_blk):
            sl = slice(b * rb, (b + 1) * rb)
            r = _dot(x_ref[p, sl, :], w_ref[p])
            s_scr[p, sl, :] = jnp.where(first_group, r[:, :half], r[:, half:])

    is_fwd = lax.broadcasted_iota(jnp.int32, (SUBLANES, LANES), 1) < S5_STATE
    a_re = [a_ref[p, :, 0:LANES] for p in range(PAIRS_PER_STEP)]
    a_im = [a_ref[p, :, LANES:half] for p in range(PAIRS_PER_STEP)]

    def body(k, carry):
        kr = jnp.where(k < n_ctx_chunks, n_ctx_chunks - 1 - k, n_chunks + n_ctx_chunks - 1 - k)
        rf = pl.multiple_of(k * SUBLANES, SUBLANES)
        rr = pl.multiple_of(kr * SUBLANES, SUBLANES)
        new = []
        for p in range(PAIRS_PER_STEP):
            h_re, h_im = carry[2 * p], carry[2 * p + 1]
            hf_scr[p, pl.ds(rf, SUBLANES), 0:LANES] = h_re
            hf_scr[p, pl.ds(rf, SUBLANES), LANES:half] = h_im
            hr_scr[p, pl.ds(rr, SUBLANES), 0:LANES] = h_re
            hr_scr[p, pl.ds(rr, SUBLANES), LANES:half] = h_im
            sf = s_scr[p, pl.ds(rf, SUBLANES), :]
            sr = s_scr[p, pl.ds(rr, SUBLANES), :]
            s_re = jnp.where(is_fwd, sf[:, 0:LANES], sr[:, 0:LANES])
            s_im = jnp.where(is_fwd, sf[:, LANES:half], sr[:, LANES:half])
            new.append(a_re[p] * h_re - a_im[p] * h_im + s_re)
            new.append(a_re[p] * h_im + a_im[p] * h_re + s_im)
        return tuple(new)

    zero = jnp.zeros((SUBLANES, LANES), F32)
    lax.fori_loop(0, n_chunks, body, (zero,) * (2 * PAIRS_PER_STEP))

    fwd_cols = (lax.broadcasted_iota(jnp.int32, (rb, half), 1) & (LANES - 1)) < S5_STATE
    for p in range(PAIRS_PER_STEP):
        for b in range(n_blk):
            sl = slice(b * rb, (b + 1) * rb)
            h_in = jnp.where(fwd_cols, hf_scr[p, sl, :], hr_scr[p, sl, :]).astype(BF16)
            r = _dot(x_ref[p, sl, :], m_ref[p]) + _dot(h_in, v_ref[p])
            y_ref[p, sl, :] = jnp.where(first_group, r[:, :half], r[:, half:])


def _s5_scan(x_pairs, w_c, m_c, v_c, a_c, n_ctx_chunks):
    n_pairs, rows, width = x_pairs.shape
    pb = PAIRS_PER_STEP
    wspec = pl.BlockSpec((pb, width, 2 * width), lambda i: (i, 0, 0))
    return pl.pallas_call(
        functools.partial(_s5_kernel, n_ctx_chunks=n_ctx_chunks),
        out_shape=jax.ShapeDtypeStruct((n_pairs, rows, width), F32),
        grid=(n_pairs // pb,),
        in_specs=[
            pl.BlockSpec((pb, rows, width), lambda i: (i, 0, 0)),
            wspec, wspec, wspec,
            pl.BlockSpec((pb, SUBLANES, width), lambda i: (i, 0, 0)),
        ],
        out_specs=pl.BlockSpec((pb, rows, width), lambda i: (i, 0, 0)),
        scratch_shapes=[pltpu.VMEM((pb, rows, width), F32)] * 3,
        compiler_params=_cparams(("parallel",)),
        name="s5_scan",
    )(x_pairs, w_c, m_c, v_c, a_c)


def _s5_matrices(lam_re, lam_im, log_dt, b_re, b_im, c_re, c_im):
    t = CHUNK
    g, p, hc = S5_GROUPS, S5_STATE, S5_CH
    lr = lam_re.astype(F32)
    li = lam_im.astype(F32)
    dt = jnp.exp(log_dt.astype(F32))[..., None]
    mag = jnp.exp(lr * dt)
    ab_re = mag * jnp.cos(li * dt)
    ab_im = mag * jnp.sin(li * dt)
    den = lr * lr + li * li
    nr = ab_re - 1.0
    z_re = (nr * lr + ab_im * li) / den
    z_im = (ab_im * lr - nr * li) / den
    br = b_re.astype(F32)
    bi = b_im.astype(F32)
    bb_re = z_re[..., None] * br - z_im[..., None] * bi
    bb_im = z_re[..., None] * bi + z_im[..., None] * br
    n = jnp.arange(t + 1, dtype=F32)[:, None, None, None]
    pmag = jnp.exp(n * (lr * dt)[None])
    pw_re = pmag * jnp.cos(n * (li * dt)[None])
    pw_im = pmag * jnp.sin(n * (li * dt)[None])
    cr = c_re.astype(F32)
    ci = c_im.astype(F32)
    ca_re = cr[None] * pw_re[:, :, :, None, :] - ci[None] * pw_im[:, :, :, None, :]
    ca_im = cr[None] * pw_im[:, :, :, None, :] + ci[None] * pw_re[:, :, :, None, :]
    kern = (jnp.einsum('ndgop,dgpi->ndgoi', ca_re[:t], bb_re, precision=HIGHEST)
            - jnp.einsum('ndgop,dgpi->ndgoi', ca_im[:t], bb_im, precision=HIGHEST))
    idx = jnp.arange(t)
    lag_f = jax.nn.one_hot(idx[None, :] - idx[:, None], t, dtype=F32)
    lag_r = jax.nn.one_hot(idx[:, None] - idx[None, :], t, dtype=F32)
    m = (jnp.einsum('stn,ngoi->gsito', lag_f, kern[:, 0], precision=HIGHEST)
         + jnp.einsum('stn,ngoi->gsito', lag_r, kern[:, 1], precision=HIGHEST))
    m = m.reshape(g, t * hc, t * hc)
    pf_re, pf_im = pw_re[:t, 0][::-1], pw_im[:t, 0][::-1]
    pr_re, pr_im = pw_re[:t, 1], pw_im[:t, 1]

    def state_w(p_re, p_im, d):
        w_re = p_re[..., None] * bb_re[d][None] - p_im[..., None] * bb_im[d][None]
        w_im = p_re[..., None] * bb_im[d][None] + p_im[..., None] * bb_re[d][None]
        to_cols = lambda w: jnp.transpose(w, (1, 0, 3, 2)).reshape(g, t * hc, p)
        return to_cols(w_re), to_cols(w_im)

    wf_re, wf_im = state_w(pf_re, pf_im, 0)
    wr_re, wr_im = state_w(pr_re, pr_im, 1)
    w = jnp.concatenate([wf_re, wr_re, wf_im, wr_im], axis=-1)
    to_rows = lambda c: jnp.transpose(c, (1, 3, 0, 2)).reshape(g, p, t * hc)
    vf_re, vf_im = to_rows(ca_re[1:, 0]), to_rows(-ca_im[1:, 0])
    vr_re, vr_im = to_rows(ca_re[1:, 1][::-1]), to_rows(-ca_im[1:, 1][::-1])
    v = jnp.concatenate([vf_re, vr_re, vf_im, vr_im], axis=1)
    a16 = jnp.concatenate([pw_re[t, 0], pw_re[t, 1], pw_im[t, 0], pw_im[t, 1]], axis=-1)

    def pair_cols(x):
        x = x.reshape(g // 2, 2, t * hc, t * hc)
        return jnp.concatenate([x[:, 0], x[:, 1]], axis=-1).astype(BF16)

    a_rows = jnp.repeat(a16.reshape(g // 2, 2, 1, 4 * p), SUBLANES // 2, axis=2).reshape(g // 2, SUBLANES, 4 * p)
    return pair_cols(w), pair_cols(m), pair_cols(v), a_rows


def _s5_mixer(qkvu_lat, qkvu_ctx, mats, bsz):
    n_lat = qkvu_lat.shape[0] // bsz
    n_ctx = qkvu_ctx.shape[0] // bsz
    u_off = 3 * NA_WIDTH
    u = jnp.concatenate([qkvu_ctx[:, u_off:].reshape(bsz, n_ctx, S5_WIDTH),
                         qkvu_lat[:, u_off:].reshape(bsz, n_lat, S5_WIDTH)], axis=1)
    n_seq = n_ctx + n_lat
    n_chunks = n_seq // CHUNK
    gp = S5_GROUPS // 2
    x = u.reshape(bsz, n_chunks, CHUNK, gp, 2, S5_CH)
    x = jnp.transpose(x, (3, 1, 4, 0, 2, 5)).reshape(gp, n_chunks * 2 * bsz, CHUNK * S5_CH).astype(BF16)
    w_c, m_c, v_c, a_c = mats
    y = _s5_scan(x, w_c, m_c, v_c, a_c, n_ctx // CHUNK)
    y = y.reshape(gp, n_chunks, 2, bsz, CHUNK, S5_CH)
    return jnp.transpose(y, (3, 1, 4, 0, 2, 5)).reshape(bsz, n_seq, S5_WIDTH)


def _outproj_kernel(na_ref, y_ref, u_ref, h_ref, mod_ref, d_ref, wglu_ref, bglu_ref, wout_ref, g2_ref,
                    ho_ref, f_ref):
    z = jax.nn.gelu(y_ref[0] + d_ref[...] * u_ref[...])
    s5 = z * jax.nn.sigmoid(_dot(z.astype(BF16), wglu_ref[...]) + bglu_ref[...])
    mix = (_dot(na_ref[...].astype(BF16), wout_ref[0:NA_WIDTH, :])
           + _dot(s5.astype(BF16), wout_ref[NA_WIDTH:NA_WIDTH + S5_WIDTH, :]))
    d = D_MODEL
    gate = mod_ref[0, :, 2 * d:3 * d]
    h = h_ref[...] + gate * mix
    ho_ref[...] = h
    ms = jnp.mean(h * h, axis=-1, keepdims=True)
    y2 = h * lax.rsqrt(ms + EPS) * g2_ref[...]
    f_ref[...] = (y2 * (1.0 + mod_ref[0, :, 4 * d:5 * d]) + mod_ref[0, :, 3 * d:4 * d]).astype(BF16)


def _out_projection(na, y_all, qkvu, h2d, mod3, d_skip, wglu_bf16, b_glu, wout_bf16, g2,
                    mod_map, rows_per_batch, y_blocks_per_batch, y_block0):
    r, d = h2d.shape
    tiles_per_batch = rows_per_batch // TM_PROJ
    n_seq = y_all.shape[1]
    y3 = y_all.reshape(y_all.shape[0] * n_seq // TM_PROJ, TM_PROJ, S5_WIDTH)
    u_blk = 3 * NA_WIDTH // S5_WIDTH

    def y_map(i):
        return ((i // tiles_per_batch) * y_blocks_per_batch + y_block0 + i % tiles_per_batch, 0, 0)

    const = lambda i: (0, 0)
    return pl.pallas_call(
        _outproj_kernel,
        out_shape=(jax.ShapeDtypeStruct((r, d), F32), jax.ShapeDtypeStruct((r, d), BF16)),
        grid=(r // TM_PROJ,),
        in_specs=[
            pl.BlockSpec((TM_PROJ, NA_WIDTH), lambda i: (i, 0)),
            pl.BlockSpec((1, TM_PROJ, S5_WIDTH), y_map),
            pl.BlockSpec((TM_PROJ, S5_WIDTH), lambda i: (i, u_blk)),
            pl.BlockSpec((TM_PROJ, d), lambda i: (i, 0)),
            pl.BlockSpec((1, 1, N_MOD * d), mod_map),
            pl.BlockSpec((1, S5_WIDTH), const),
            pl.BlockSpec((S5_WIDTH, S5_WIDTH), const),
            pl.BlockSpec((1, S5_WIDTH), const),
            pl.BlockSpec((NA_WIDTH + S5_WIDTH, d), const),
            pl.BlockSpec((1, d), const),
        ],
        out_specs=(pl.BlockSpec((TM_PROJ, d), lambda i: (i, 0)), pl.BlockSpec((TM_PROJ, d), lambda i: (i, 0))),
        compiler_params=_cparams(("parallel",)),
        name="out_projection",
    )(na, y3, qkvu, h2d, mod3, d_skip.reshape(1, -1), wglu_bf16, b_glu.reshape(1, -1), wout_bf16,
      g2.reshape(1, d))


def _top2(vals):
    best = vals[0]
    bi = jnp.zeros(best.shape, jnp.int32)
    for i in range(1, len(vals)):
        gt = vals[i] > best
        best = jnp.where(gt, vals[i], best)
        bi = jnp.where(gt, i, bi)
    second = jnp.full(best.shape, -jnp.inf, F32)
    si = jnp.zeros(best.shape, jnp.int32)
    for i in range(len(vals)):
        cand = jnp.where(bi == i, -jnp.inf, vals[i])
        gt = cand > second
        second = jnp.where(gt, cand, second)
        si = jnp.where(gt, i, si)
    return best, bi, second, si


def _router_kernel(f_ref, rwt_ref, rb_ref, idx_ref, gate_ref):
    logits = lax.dot_general(rwt_ref[...], f_ref[...].astype(F32), (((1,), (1,)), ((), ())),
                             precision=HIGHEST, preferred_element_type=F32)
    m = jnp.max(logits, axis=0, keepdims=True)
    e = jnp.exp(logits - m)
    probs = e / jnp.sum(e, axis=0, keepdims=True)
    sel = probs + rb_ref[...]
    sel_rows = [sel[i:i + 1, :] for i in range(N_EXPERTS)]
    prob_rows = [probs[i:i + 1, :] for i in range(N_EXPERTS)]
    scores = []
    for g in range(N_GROUPS):
        b, _, s, _ = _top2(sel_rows[g * EPG:(g + 1) * EPG])
        scores.append(b + s)
    grp = jnp.zeros(scores[0].shape, jnp.int32)
    gbest = scores[0]
    for g in range(1, N_GROUPS):
        gt = scores[g] > gbest
        gbest = jnp.where(gt, scores[g], gbest)
        grp = jnp.where(gt, g, grp)
    in_rows = []
    for j in range(EPG):
        v = sel_rows[j]
        for g in range(1, N_GROUPS):
            v = jnp.where(grp == g, sel_rows[g * EPG + j], v)
        in_rows.append(v)
    _, l1, _, l2 = _top2(in_rows)
    i1 = grp * EPG + l1
    i2 = grp * EPG + l2
    w1 = jnp.zeros(gbest.shape, F32)
    w2 = jnp.zeros(gbest.shape, F32)
    for i in range(N_EXPERTS):
        w1 = jnp.where(i1 == i, prob_rows[i], w1)
        w2 = jnp.where(i2 == i, prob_rows[i], w2)
    tot = w1 + w2
    idx_ref[0:1, :] = i1
    idx_ref[1:2, :] = i2
    gate_ref[0:1, :] = w1 / tot
    gate_ref[1:2, :] = w2 / tot


def _router(f_all, router_w, router_bias):
    n, d = f_all.shape
    return pl.pallas_call(
        _router_kernel,
        out_shape=(jax.ShapeDtypeStruct((2, n), jnp.int32), jax.ShapeDtypeStruct((2, n), F32)),
        grid=(n // TM_ROUTE,),
        in_specs=[
            pl.BlockSpec((TM_ROUTE, d), lambda i: (i, 0)),
            pl.BlockSpec((N_EXPERTS, d), lambda i: (0, 0)),
            pl.BlockSpec((N_EXPERTS, 1), lambda i: (0, 0)),
        ],
        out_specs=(pl.BlockSpec((2, TM_ROUTE), lambda i: (0, i)), pl.BlockSpec((2, TM_ROUTE), lambda i: (0, i))),
        compiler_params=_cparams(("parallel",)),
        name="router",
    )(f_all, router_w.T.astype(F32), router_bias.reshape(N_EXPERTS, 1).astype(F32))


def _experts_kernel(te_ref, nv_ref, x_ref, wg_ref, wu_ref, wd_ref, gate_ref, o_ref, wg_scr, wu_scr, wd_scr):
    i = pl.program_id(0)
    e = te_ref[i]
    prev = te_ref[jnp.maximum(i - 1, 0)]
    rows = 128

    @pl.when((i == 0) | (e != prev))
    def _():
        def body(r, carry):
            sl = pl.ds(pl.multiple_of(r * rows, rows), rows)
            wg_scr[sl, :] = wg_ref[0, sl, :].astype(BF16)
            wu_scr[sl, :] = wu_ref[0, sl, :].astype(BF16)
            wd_scr[sl, :] = wd_ref[0, sl, :].astype(BF16)
            return carry
        lax.fori_loop(0, wg_scr.shape[0] // rows, body, 0)

    @pl.when(i < nv_ref[0])
    def _():
        x = x_ref[...]
        g = _dot(x, wg_scr[...])
        u = _dot(x, wu_scr[...])
        a = (g * jax.nn.sigmoid(g)) * u
        y = _dot(a.astype(BF16), wd_scr[...])
        o_ref[...] = (y * gate_ref[...]).astype(BF16)

    @pl.when(i >= nv_ref[0])
    def _():
        o_ref[...] = jnp.zeros(o_ref.shape, BF16)


def _experts(xs, gate_sorted, tile_expert, n_valid, w_gate, w_up, w_down):
    r, d = xs.shape
    de = w_gate.shape[2]
    n_tiles = r // TM_EXP
    grid_spec = pltpu.PrefetchScalarGridSpec(
        num_scalar_prefetch=2,
        grid=(n_tiles,),
        in_specs=[
            pl.BlockSpec((TM_EXP, d), lambda i, te, nv: (i, 0)),
            pl.BlockSpec((1, d, de), lambda i, te, nv: (te[i], 0, 0)),
            pl.BlockSpec((1, d, de), lambda i, te, nv: (te[i], 0, 0)),
            pl.BlockSpec((1, de, d), lambda i, te, nv: (te[i], 0, 0)),
            pl.BlockSpec((TM_EXP, 1), lambda i, te, nv: (i, 0)),
        ],
        out_specs=pl.BlockSpec((TM_EXP, d), lambda i, te, nv: (i, 0)),
        scratch_shapes=[pltpu.VMEM((d, de), BF16), pltpu.VMEM((d, de), BF16), pltpu.VMEM((de, d), BF16)],
    )
    return pl.pallas_call(
        _experts_kernel,
        out_shape=jax.ShapeDtypeStruct((r, d), BF16),
        grid_spec=grid_spec,
        compiler_params=_cparams(("arbitrary",)),
        name="experts",
    )(tile_expert, n_valid, xs, w_gate, w_up, w_down, gate_sorted.reshape(r, 1))


def _dispatch(idx, gates):
    n = idx.shape[1]
    e_flat = idx.reshape(-1)
    onehot = (e_flat[:, None] == jnp.arange(N_EXPERTS, dtype=jnp.int32)[None, :]).astype(jnp.int32)
    csum = jnp.cumsum(onehot, axis=0)
    rank = jnp.sum(csum * onehot, axis=1) - 1
    counts = csum[-1]
    padded = ((counts + TM_EXP - 1) // TM_EXP) * TM_EXP
    ends = jnp.cumsum(padded)
    starts = ends - padded
    dest = jnp.sum(onehot * starts[None, :], axis=1) + rank
    r_max = 2 * n + N_EXPERTS * TM_EXP
    tok = jnp.tile(jnp.arange(n, dtype=jnp.int32), 2)
    src_tok = jnp.zeros((r_max,), jnp.int32).at[dest].set(tok)
    gate_sorted = jnp.zeros((r_max,), F32).at[dest].set(gates.reshape(-1))
    tile_start = jnp.arange(r_max // TM_EXP, dtype=jnp.int32) * TM_EXP
    tile_expert = jnp.minimum(jnp.sum((tile_start[:, None] >= ends[None, :]).astype(jnp.int32), axis=1),
                              N_EXPERTS - 1).astype(jnp.int32)
    n_valid = (ends[-1] // TM_EXP).astype(jnp.int32).reshape(1)
    return src_tok, gate_sorted, tile_expert, n_valid, dest.reshape(2, n)


def _combine_kernel(h_ref, y1_ref, y2_ref, mod_ref, o_ref):
    d = D_MODEL
    y = y1_ref[...].astype(F32) + y2_ref[...].astype(F32)
    o_ref[...] = h_ref[...] + mod_ref[0, :, 5 * d:6 * d] * y


def _combine(h2d, y1, y2, mod3, mod_map):
    r, d = h2d.shape
    row = lambda i: (i, 0)
    return pl.pallas_call(
        _combine_kernel,
        out_shape=jax.ShapeDtypeStruct((r, d), F32),
        grid=(r // TM_PROJ,),
        in_specs=[
            pl.BlockSpec((TM_PROJ, d), row),
            pl.BlockSpec((TM_PROJ, d), row),
            pl.BlockSpec((TM_PROJ, d), row),
            pl.BlockSpec((1, 1, N_MOD * d), mod_map),
        ],
        out_specs=pl.BlockSpec((TM_PROJ, d), row),
        compiler_params=_cparams(("parallel",)),
        name="moe_combine",
    )(h2d, y1, y2, mod3)


def kernel(x, c, ctx, c_ctx, w_mod, b_mod, norm1_g, norm2_g, w_in, w_out, q_norm_g, k_norm_g, na_rpb,
           s5_lam_re, s5_lam_im, s5_log_dt, s5_b_re, s5_b_im, s5_c_re, s5_c_im, s5_d, s5_w_glu, s5_b_glu,
           router_w, router_bias, moe_w_gate, moe_w_up, moe_w_down):
    bsz, n_lat, d = x.shape
    n_ctx = ctx.shape[1]
    depth = w_mod.shape[0]
    ctx_row = bsz
    c_rows = jnp.concatenate([c.astype(F32), c_ctx.astype(F32)[None],
                              jnp.zeros((SUBLANES - bsz - 1, d), F32)], axis=0)
    mod_all = _modulation(c_rows, w_mod.astype(F32), b_mod.astype(F32))

    h_lat = x.reshape(bsz * n_lat, d).astype(F32)
    h_ctx = ctx.reshape(bsz * n_ctx, d).astype(F32)
    y_blocks = (n_ctx + n_lat) // TM_PROJ
    lat_map = _mod_row_map(n_lat, 0, True)
    ctx_map = _mod_row_map(n_ctx, ctx_row, False)

    for layer in range(depth):
        ctx_out = layer < depth - 1
        mod3 = mod_all[layer].reshape(SUBLANES, 1, N_MOD * d)
        w_in_b = w_in[layer].astype(BF16)
        qkvu_lat = _in_projection(h_lat, norm1_g[layer], mod3, w_in_b, lat_map)
        qkvu_ctx = _in_projection(h_ctx, norm1_g[layer], mod3, w_in_b, ctx_map)
        bias_tab = _na_bias_table(na_rpb[layer])
        na_lat = _neighborhood_attention(qkvu_lat, qkvu_ctx, bias_tab, q_norm_g[layer], k_norm_g[layer], bsz)
        mats = _s5_matrices(s5_lam_re[layer], s5_lam_im[layer], s5_log_dt[layer], s5_b_re[layer],
                            s5_b_im[layer], s5_c_re[layer], s5_c_im[layer])
        y_all = _s5_mixer(qkvu_lat, qkvu_ctx, mats, bsz)
        wglu_b = s5_w_glu[layer].astype(BF16)
        wout_b = w_out[layer].astype(BF16)
        h_lat, f_lat = _out_projection(na_lat, y_all, qkvu_lat, h_lat, mod3, s5_d[layer], wglu_b,
                                       s5_b_glu[layer], wout_b, norm2_g[layer],
                                       lat_map, n_lat, y_blocks, n_ctx // TM_PROJ)
        if ctx_out:
            na_ctx = _context_attention(qkvu_ctx, q_norm_g[layer], k_norm_g[layer], bsz)
            h_ctx, f_ctx = _out_projection(na_ctx, y_all, qkvu_ctx, h_ctx, mod3, s5_d[layer], wglu_b,
                                           s5_b_glu[layer], wout_b, norm2_g[layer],
                                           ctx_map, n_ctx, y_blocks, 0)
            f_all = jnp.concatenate([f_lat, f_ctx], axis=0)
        else:
            f_all = f_lat
        idx, gates = _router(f_all, router_w, router_bias)
        src_tok, gate_sorted, tile_expert, n_valid, dest = _dispatch(idx, gates)
        xs = jnp.take(f_all, src_tok, axis=0)
        ys = _experts(xs, gate_sorted, tile_expert, n_valid, moe_w_gate[layer], moe_w_up[layer],
                      moe_w_down[layer])
        y1 = jnp.take(ys, dest[0], axis=0)
        y2 = jnp.take(ys, dest[1], axis=0)
        n_l = bsz * n_lat
        h_lat = _combine(h_lat, y1[:n_l], y2[:n_l], mod3, lat_map)
        if ctx_out:
            h_ctx = _combine(h_ctx, y1[n_l:], y2[n_l:], mod3, ctx_map)
    return h_lat.reshape(bsz, n_lat, d).astype(x.dtype)
```

```python
import functools
import math

import jax
import jax.numpy as jnp
from jax import lax
from jax.experimental import pallas as pl
from jax.experimental.pallas import tpu as pltpu

F32 = jnp.float32
BF16 = jnp.bfloat16
HIGHEST = lax.Precision.HIGHEST

D_MODEL = 1024
GRID_W = 64
HEAD_DIM = 64
NA_WIDTH = 512
S5_WIDTH = 512
S5_CH = 16
S5_GROUPS = 32
S5_STATE = 64
WIN_ROWS = 8
WIN_COLS = 16
N_EXPERTS = 16
N_GROUPS = 4
EPG = 4
N_MOD = 6
EPS = 1e-6

LANES = 128
SUBLANES = 8
VMEM_LIMIT = 56 * 1024 * 1024

TM_PROJ = 256
Q_ROWS = 8
Q_COLS = 16
K_ROWS = 16
K_COLS = 32
CHUNK = 16
PAIRS_PER_STEP = 2
TM_EXP = 256
TM_ROUTE = 512
MASK_VALUE = -1e30


def _cparams(sem):
    return pltpu.CompilerParams(dimension_semantics=sem, vmem_limit_bytes=VMEM_LIMIT)


def _dot(a, b):
    return jnp.dot(a, b, preferred_element_type=F32)


def _dot_nt(a, b):
    return lax.dot_general(a, b, (((1,), (1,)), ((), ())), preferred_element_type=F32)


def _mod_kernel(c_ref, w_ref, b_ref, o_ref):
    a = c_ref[...]
    a = a * jax.nn.sigmoid(a)
    o_ref[0] = jnp.dot(a, w_ref[0], precision=HIGHEST, preferred_element_type=F32) + b_ref[0]


def _modulation(c_rows, w_mod, b_mod):
    depth, d, n = w_mod.shape
    tn = 1536
    return pl.pallas_call(
        _mod_kernel,
        out_shape=jax.ShapeDtypeStruct((depth, SUBLANES, n), F32),
        grid=(depth, n // tn),
        in_specs=[
            pl.BlockSpec((SUBLANES, d), lambda l, j: (0, 0)),
            pl.BlockSpec((1, d, tn), lambda l, j: (l, 0, j)),
            pl.BlockSpec((1, 1, tn), lambda l, j: (l, 0, j)),
        ],
        out_specs=pl.BlockSpec((1, SUBLANES, tn), lambda l, j: (l, 0, j)),
        compiler_params=_cparams(("arbitrary", "arbitrary")),
        name="modulation",
    )(c_rows, w_mod, b_mod.reshape(depth, 1, n))


def _inproj_kernel(x_ref, g_ref, mod_ref, w_ref, o_ref):
    x = x_ref[...]
    ms = jnp.mean(x * x, axis=-1, keepdims=True)
    y = x * lax.rsqrt(ms + EPS) * g_ref[...]
    shift = mod_ref[0, :, 0:D_MODEL]
    scale = mod_ref[0, :, D_MODEL:2 * D_MODEL]
    a = y * (1.0 + scale) + shift
    o_ref[...] = _dot(a.astype(BF16), w_ref[...])


def _mod_row_map(rows_per_batch, mod_row0, per_batch):
    tiles_per_batch = rows_per_batch // TM_PROJ
    if per_batch:
        return lambda i: (mod_row0 + i // tiles_per_batch, 0, 0)
    return lambda i: (mod_row0, 0, 0)


def _in_projection(x2d, g, mod3, w_bf16, mod_map):
    r, d = x2d.shape
    n = w_bf16.shape[1]
    return pl.pallas_call(
        _inproj_kernel,
        out_shape=jax.ShapeDtypeStruct((r, n), F32),
        grid=(r // TM_PROJ,),
        in_specs=[
            pl.BlockSpec((TM_PROJ, d), lambda i: (i, 0)),
            pl.BlockSpec((1, d), lambda i: (0, 0)),
            pl.BlockSpec((1, 1, N_MOD * d), mod_map),
            pl.BlockSpec((d, n), lambda i: (0, 0)),
        ],
        out_specs=pl.BlockSpec((TM_PROJ, n), lambda i: (i, 0)),
        compiler_params=_cparams(("parallel",)),
        name="in_projection",
    )(x2d, g.reshape(1, d), mod3, w_bf16)


def _pair_rms(x, g, lo):
    ss = x * x
    sa = jnp.sum(jnp.where(lo, ss, 0.0), axis=-1, keepdims=True)
    sb = jnp.sum(jnp.where(lo, 0.0, ss), axis=-1, keepdims=True)
    ms = jnp.where(lo, sa, sb) * (1.0 / HEAD_DIM)
    return x * lax.rsqrt(ms + EPS) * g


def _softmax_pv(qm, kw, kcb, vw, vcb, bias):
    s_nb = _dot_nt(qm, kw) + bias
    s_cx = _dot_nt(qm, kcb)
    m = jnp.maximum(jnp.max(s_nb, axis=-1, keepdims=True), jnp.max(s_cx, axis=-1, keepdims=True))
    p_nb = jnp.exp(s_nb - m)
    p_cx = jnp.exp(s_cx - m)
    l = jnp.sum(p_nb, axis=-1, keepdims=True) + jnp.sum(p_cx, axis=-1, keepdims=True)
    o = _dot(p_nb.astype(BF16), vw) + _dot(p_cx.astype(BF16), vcb)
    return o / l


def _na_kernel(q_ref, k_ref, v_ref, kc_ref, vc_ref, bias_ref, qg_ref, kg_ref, o_ref, kn_scr, kcn_scr):
    t = pl.program_id(2)
    lo = lax.broadcasted_iota(jnp.int32, (1, LANES), 1) < HEAD_DIM
    n_rows = k_ref.shape[1]
    col_tiles = GRID_W // Q_COLS

    @pl.when(t == 0)
    def _():
        def body(r, carry):
            kn_scr[r] = _pair_rms(k_ref[0, r], kg_ref[...], lo)
            return carry
        lax.fori_loop(0, n_rows, body, 0)
        kcn_scr[...] = _pair_rms(kc_ref[0], kg_ref[...], lo).astype(BF16)

    i = t // col_tiles
    j = t % col_tiles
    kr0 = jnp.clip(Q_ROWS * i - WIN_ROWS // 2, 0, n_rows - K_ROWS)
    kc0 = pl.multiple_of(jnp.clip(Q_COLS * j - WIN_COLS // 2, 0, GRID_W - K_COLS), SUBLANES)
    rt = jnp.where(i == 0, 0, jnp.where(i == n_rows // Q_ROWS - 1, 2, 1))
    ct = jnp.where(j == 0, 0, jnp.where(j == col_tiles - 1, 2, 1))
    typ = rt * 3 + ct

    nq = Q_ROWS * Q_COLS
    nk = K_ROWS * K_COLS
    q = q_ref[0].reshape(nq, LANES)
    qn = _pair_rms(q, qg_ref[...], lo) * (HEAD_DIM ** -0.5)
    kw = kn_scr[pl.ds(kr0, K_ROWS), pl.ds(kc0, K_COLS), :].reshape(nk, LANES).astype(BF16)
    vw = v_ref[0, pl.ds(kr0, K_ROWS), pl.ds(kc0, K_COLS), :].reshape(nk, LANES).astype(BF16)
    kcb = kcn_scr[...]
    vcb = vc_ref[0].astype(BF16)
    o_a = _softmax_pv(jnp.where(lo, qn, 0.0).astype(BF16), kw, kcb, vw, vcb, bias_ref[0, typ, 0])
    o_b = _softmax_pv(jnp.where(lo, 0.0, qn).astype(BF16), kw, kcb, vw, vcb, bias_ref[0, typ, 1])
    o_ref[0] = jnp.where(lo, o_a, o_b).reshape(Q_ROWS, Q_COLS, LANES)


def _na_bias_table(rpb):
    h = rpb.shape[0]
    rpb = rpb.astype(F32)

    def axis_tables(q_n, k_n, win, d_opts, origin_opts, extent):
        onehots, valids = [], []
        for d_rel, origin in zip(d_opts, origin_opts):
            qa = origin + jnp.arange(q_n)[:, None]
            ka = origin + d_rel + jnp.arange(k_n)[None, :]
            start = jnp.clip(qa - win // 2, 0, extent - win)
            valid = (ka >= start) & (ka < start + win)
            off = jnp.clip(ka - qa + win - 1, 0, 2 * win - 2)
            onehots.append(jax.nn.one_hot(off, 2 * win - 1, dtype=F32))
            valids.append(valid)
        return jnp.stack(onehots), jnp.stack(valids)

    rows = GRID_W
    oh_r, va_r = axis_tables(Q_ROWS, K_ROWS, WIN_ROWS, (0, -4, -8), (0, Q_ROWS, rows - Q_ROWS), rows)
    oh_c, va_c = axis_tables(Q_COLS, K_COLS, WIN_COLS, (0, -8, -16), (0, Q_COLS, GRID_W - Q_COLS), GRID_W)
    t1 = jnp.einsum('raku,huv->hrakv', oh_r, rpb, precision=HIGHEST)
    bias = jnp.einsum('hrakv,cbjv->hrcabkj', t1, oh_c, precision=HIGHEST)
    valid = va_r[:, None, :, None, :, None] & va_c[None, :, None, :, None, :]
    bias = jnp.where(valid[None], bias, MASK_VALUE)
    bias = bias.reshape(h // 2, 2, 9, Q_ROWS * Q_COLS, K_ROWS * K_COLS)
    return jnp.transpose(bias, (0, 2, 1, 3, 4))


def _neighborhood_attention(qkvu_lat, qkvu_ctx, bias_tab, qg, kg, bsz):
    n_lat = qkvu_lat.shape[0] // bsz
    n_ctx = qkvu_ctx.shape[0] // bsz
    rows = n_lat // GRID_W
    n_cols = qkvu_lat.shape[1]
    lat4 = qkvu_lat.reshape(bsz, rows, GRID_W, n_cols)
    ctx3 = qkvu_ctx.reshape(bsz, n_ctx, n_cols)
    n_pairs = NA_WIDTH // LANES
    col_tiles = GRID_W // Q_COLS
    n_tiles = (rows // Q_ROWS) * col_tiles
    g2 = lambda g: jnp.concatenate([g, g]).reshape(1, LANES).astype(F32)
    out = pl.pallas_call(
        _na_kernel,
        out_shape=jax.ShapeDtypeStruct((bsz, rows, GRID_W, NA_WIDTH), F32),
        grid=(bsz, n_pairs, n_tiles),
        in_specs=[
            pl.BlockSpec((1, Q_ROWS, Q_COLS, LANES), lambda b, p, t: (b, t // col_tiles, t % col_tiles, p)),
            pl.BlockSpec((1, rows, GRID_W, LANES), lambda b, p, t: (b, 0, 0, n_pairs + p)),
            pl.BlockSpec((1, rows, GRID_W, LANES), lambda b, p, t: (b, 0, 0, 2 * n_pairs + p)),
            pl.BlockSpec((1, n_ctx, LANES), lambda b, p, t: (b, 0, n_pairs + p)),
            pl.BlockSpec((1, n_ctx, LANES), lambda b, p, t: (b, 0, 2 * n_pairs + p)),
            pl.BlockSpec((1, 9, 2, Q_ROWS * Q_COLS, K_ROWS * K_COLS), lambda b, p, t: (p, 0, 0, 0, 0)),
            pl.BlockSpec((1, LANES), lambda b, p, t: (0, 0)),
            pl.BlockSpec((1, LANES), lambda b, p, t: (0, 0)),
        ],
        out_specs=pl.BlockSpec((1, Q_ROWS, Q_COLS, LANES),
                               lambda b, p, t: (b, t // col_tiles, t % col_tiles, p)),
        scratch_shapes=[pltpu.VMEM((rows, GRID_W, LANES), F32), pltpu.VMEM((n_ctx, LANES), BF16)],
        compiler_params=_cparams(("parallel", "parallel", "arbitrary")),
        name="neighborhood_attention",
    )(lat4, lat4, lat4, ctx3, ctx3, bias_tab, g2(qg), g2(kg))
    return out.reshape(bsz * n_lat, NA_WIDTH)


def _ctx_attn_kernel(q_ref, k_ref, v_ref, qg_ref, kg_ref, o_ref):
    lo = lax.broadcasted_iota(jnp.int32, (1, LANES), 1) < HEAD_DIM
    qn = _pair_rms(q_ref[0], qg_ref[...], lo) * (HEAD_DIM ** -0.5)
    kn = _pair_rms(k_ref[0], kg_ref[...], lo).astype(BF16)
    vb = v_ref[0].astype(BF16)

    def one(qm):
        s = _dot_nt(qm, kn)
        m = jnp.max(s, axis=-1, keepdims=True)
        p = jnp.exp(s - m)
        l = jnp.sum(p, axis=-1, keepdims=True)
        return _dot(p.astype(BF16), vb) / l

    o_a = one(jnp.where(lo, qn, 0.0).astype(BF16))
    o_b = one(jnp.where(lo, 0.0, qn).astype(BF16))
    o_ref[0] = jnp.where(lo, o_a, o_b)


def _context_attention(qkvu_ctx, qg, kg, bsz):
    n_ctx = qkvu_ctx.shape[0] // bsz
    ctx3 = qkvu_ctx.reshape(bsz, n_ctx, qkvu_ctx.shape[1])
    n_pairs = NA_WIDTH // LANES
    g2 = lambda g: jnp.concatenate([g, g]).reshape(1, LANES).astype(F32)
    out = pl.pallas_call(
        _ctx_attn_kernel,
        out_shape=jax.ShapeDtypeStruct((bsz, n_ctx, NA_WIDTH), F32),
        grid=(bsz, n_pairs),
        in_specs=[
            pl.BlockSpec((1, n_ctx, LANES), lambda b, p: (b, 0, p)),
            pl.BlockSpec((1, n_ctx, LANES), lambda b, p: (b, 0, n_pairs + p)),
            pl.BlockSpec((1, n_ctx, LANES), lambda b, p: (b, 0, 2 * n_pairs + p)),
            pl.BlockSpec((1, LANES), lambda b, p: (0, 0)),
            pl.BlockSpec((1, LANES), lambda b, p: (0, 0)),
        ],
        out_specs=pl.BlockSpec((1, n_ctx, LANES), lambda b, p: (b, 0, p)),
        compiler_params=_cparams(("parallel", "parallel")),
        name="context_attention",
    )(ctx3, ctx3, ctx3, g2(qg), g2(kg))
    return out.reshape(bsz * n_ctx, NA_WIDTH)


def _s5_kernel(x_ref, w_ref, m_ref, v_ref, a_ref, y_ref, s_scr, hf_scr, hr_scr, *, n_ctx_chunks):
    rows = x_ref.shape[1]
    n_chunks = rows // SUBLANES
    n_blk = 8
    rb = rows // n_blk
    half = 2 * LANES
    first_group = (lax.broadcasted_iota(jnp.int32, (rb, half), 0) & (SUBLANES // 2)) == 0

    for p in range(PAIRS_PER_STEP):
        for b in range(n_blk):
            sl = slice(b * rb, (b + 1) * rb)
            r = _dot(x_ref[p, sl, :], w_ref[p])
            s_scr[p, sl, :] = jnp.where(first_group, r[:, :half], r[:, half:])

    is_fwd = lax.broadcasted_iota(jnp.int32, (SUBLANES, LANES), 1) < S5_STATE
    a_re = [a_ref[p, :, 0:LANES] for p in range(PAIRS_PER_STEP)]
    a_im = [a_ref[p, :, LANES:half] for p in range(PAIRS_PER_STEP)]

    def body(k, carry):
        kr = jnp.where(k < n_ctx_chunks, n_ctx_chunks - 1 - k, n_chunks + n_ctx_chunks - 1 - k)
        rf = pl.multiple_of(k * SUBLANES, SUBLANES)
        rr = pl.multiple_of(kr * SUBLANES, SUBLANES)
        new = []
        for p in range(PAIRS_PER_STEP):
            h_re, h_im = carry[2 * p], carry[2 * p + 1]
            hf_scr[p, pl.ds(rf, SUBLANES), 0:LANES] = h_re
            hf_scr[p, pl.ds(rf, SUBLANES), LANES:half] = h_im
            hr_scr[p, pl.ds(rr, SUBLANES), 0:LANES] = h_re
            hr_scr[p, pl.ds(rr, SUBLANES), LANES:half] = h_im
            sf = s_scr[p, pl.ds(rf, SUBLANES), :]
            sr = s_scr[p, pl.ds(rr, SUBLANES), :]
            s_re = jnp.where(is_fwd, sf[:, 0:LANES], sr[:, 0:LANES])
            s_im = jnp.where(is_fwd, sf[:, LANES:half], sr[:, LANES:half])
            new.append(a_re[p] * h_re - a_im[p] * h_im + s_re)
            new.append(a_re[p] * h_im + a_im[p] * h_re + s_im)
        return tuple(new)

    zero = jnp.zeros((SUBLANES, LANES), F32)
    lax.fori_loop(0, n_chunks, body, (zero,) * (2 * PAIRS_PER_STEP))

    fwd_cols = (lax.broadcasted_iota(jnp.int32, (rb, half), 1) & (LANES - 1)) < S5_STATE
    for p in range(PAIRS_PER_STEP):
        for b in range(n_blk):
            sl = slice(b * rb, (b + 1) * rb)
            h_in = jnp.where(fwd_cols, hf_scr[p, sl, :], hr_scr[p, sl, :]).astype(BF16)
            r = _dot(x_ref[p, sl, :], m_ref[p]) + _dot(h_in, v_ref[p])
            y_ref[p, sl, :] = jnp.where(first_group, r[:, :half], r[:, half:])


def _s5_scan(x_pairs, w_c, m_c, v_c, a_c, n_ctx_chunks):
    n_pairs, rows, width = x_pairs.shape
    pb = PAIRS_PER_STEP
    wspec = pl.BlockSpec((pb, width, 2 * width), lambda i: (i, 0, 0))
    return pl.pallas_call(
        functools.partial(_s5_kernel, n_ctx_chunks=n_ctx_chunks),
        out_shape=jax.ShapeDtypeStruct((n_pairs, rows, width), F32),
        grid=(n_pairs // pb,),
        in_specs=[
            pl.BlockSpec((pb, rows, width), lambda i: (i, 0, 0)),
            wspec, wspec, wspec,
            pl.BlockSpec((pb, SUBLANES, width), lambda i: (i, 0, 0)),
        ],
        out_specs=pl.BlockSpec((pb, rows, width), lambda i: (i, 0, 0)),
        scratch_shapes=[pltpu.VMEM((pb, rows, width), F32)] * 3,
        compiler_params=_cparams(("parallel",)),
        name="s5_scan",
    )(x_pairs, w_c, m_c, v_c, a_c)


def _s5_matrices(lam_re, lam_im, log_dt, b_re, b_im, c_re, c_im):
    t = CHUNK
    g, p, hc = S5_GROUPS, S5_STATE, S5_CH
    lr = lam_re.astype(F32)
    li = lam_im.astype(F32)
    dt = jnp.exp(log_dt.astype(F32))[..., None]
    mag = jnp.exp(lr * dt)
    ab_re = mag * jnp.cos(li * dt)
    ab_im = mag * jnp.sin(li * dt)
    den = lr * lr + li * li
    nr = ab_re - 1.0
    z_re = (nr * lr + ab_im * li) / den
    z_im = (ab_im * lr - nr * li) / den
    br = b_re.astype(F32)
    bi = b_im.astype(F32)
    bb_re = z_re[..., None] * br - z_im[..., None] * bi
    bb_im = z_re[..., None] * bi + z_im[..., None] * br
    n = jnp.arange(t + 1, dtype=F32)[:, None, None, None]
    pmag = jnp.exp(n * (lr * dt)[None])
    pw_re = pmag * jnp.cos(n * (li * dt)[None])
    pw_im = pmag * jnp.sin(n * (li * dt)[None])
    cr = c_re.astype(F32)
    ci = c_im.astype(F32)
    ca_re = cr[None] * pw_re[:, :, :, None, :] - ci[None] * pw_im[:, :, :, None, :]
    ca_im = cr[None] * pw_im[:, :, :, None, :] + ci[None] * pw_re[:, :, :, None, :]
    kern = (jnp.einsum('ndgop,dgpi->ndgoi', ca_re[:t], bb_re, precision=HIGHEST)
            - jnp.einsum('ndgop,dgpi->ndgoi', ca_im[:t], bb_im, precision=HIGHEST))
    idx = jnp.arange(t)
    lag_f = jax.nn.one_hot(idx[None, :] - idx[:, None], t, dtype=F32)
    lag_r = jax.nn.one_hot(idx[:, None] - idx[None, :], t, dtype=F32)
    m = (jnp.einsum('stn,ngoi->gsito', lag_f, kern[:, 0], precision=HIGHEST)
         + jnp.einsum('stn,ngoi->gsito', lag_r, kern[:, 1], precision=HIGHEST))
    m = m.reshape(g, t * hc, t * hc)
    pf_re, pf_im = pw_re[:t, 0][::-1], pw_im[:t, 0][::-1]
    pr_re, pr_im = pw_re[:t, 1], pw_im[:t, 1]

    def state_w(p_re, p_im, d):
        w_re = p_re[..., None] * bb_re[d][None] - p_im[..., None] * bb_im[d][None]
        w_im = p_re[..., None] * bb_im[d][None] + p_im[..., None] * bb_re[d][None]
        to_cols = lambda w: jnp.transpose(w, (1, 0, 3, 2)).reshape(g, t * hc, p)
        return to_cols(w_re), to_cols(w_im)

    wf_re, wf_im = state_w(pf_re, pf_im, 0)
    wr_re, wr_im = state_w(pr_re, pr_im, 1)
    w = jnp.concatenate([wf_re, wr_re, wf_im, wr_im], axis=-1)
    to_rows = lambda c: jnp.transpose(c, (1, 3, 0, 2)).reshape(g, p, t * hc)
    vf_re, vf_im = to_rows(ca_re[1:, 0]), to_rows(-ca_im[1:, 0])
    vr_re, vr_im = to_rows(ca_re[1:, 1][::-1]), to_rows(-ca_im[1:, 1][::-1])
    v = jnp.concatenate([vf_re, vr_re, vf_im, vr_im], axis=1)
    a16 = jnp.concatenate([pw_re[t, 0], pw_re[t, 1], pw_im[t, 0], pw_im[t, 1]], axis=-1)

    def pair_cols(x):
        x = x.reshape(g // 2, 2, t * hc, t * hc)
        return jnp.concatenate([x[:, 0], x[:, 1]], axis=-1).astype(BF16)

    a_rows = jnp.repeat(a16.reshape(g // 2, 2, 1, 4 * p), SUBLANES // 2, axis=2).reshape(g // 2, SUBLANES, 4 * p)
    return pair_cols(w), pair_cols(m), pair_cols(v), a_rows


def _s5_mixer(qkvu_lat, qkvu_ctx, mats, bsz):
    n_lat = qkvu_lat.shape[0] // bsz
    n_ctx = qkvu_ctx.shape[0] // bsz
    u_off = 3 * NA_WIDTH
    u = jnp.concatenate([qkvu_ctx[:, u_off:].reshape(bsz, n_ctx, S5_WIDTH),
                         qkvu_lat[:, u_off:].reshape(bsz, n_lat, S5_WIDTH)], axis=1)
    n_seq = n_ctx + n_lat
    n_chunks = n_seq // CHUNK
    gp = S5_GROUPS // 2
    x = u.reshape(bsz, n_chunks, CHUNK, gp, 2, S5_CH)
    x = jnp.transpose(x, (3, 1, 4, 0, 2, 5)).reshape(gp, n_chunks * 2 * bsz, CHUNK * S5_CH).astype(BF16)
    w_c, m_c, v_c, a_c = mats
    y = _s5_scan(x, w_c, m_c, v_c, a_c, n_ctx // CHUNK)
    y = y.reshape(gp, n_chunks, 2, bsz, CHUNK, S5_CH)
    return jnp.transpose(y, (3, 1, 4, 0, 2, 5)).reshape(bsz, n_seq, S5_WIDTH)


def _outproj_kernel(na_ref, y_ref, u_ref, h_ref, mod_ref, d_ref, wglu_ref, bglu_ref, wout_ref, g2_ref,
                    ho_ref, f_ref):
    z = jax.nn.gelu(y_ref[0] + d_ref[...] * u_ref[...])
    s5 = z * jax.nn.sigmoid(_dot(z.astype(BF16), wglu_ref[...]) + bglu_ref[...])
    mix = (_dot(na_ref[...].astype(BF16), wout_ref[0:NA_WIDTH, :])
           + _dot(s5.astype(BF16), wout_ref[NA_WIDTH:NA_WIDTH + S5_WIDTH, :]))
    d = D_MODEL
    gate = mod_ref[0, :, 2 * d:3 * d]
    h = h_ref[...] + gate * mix
    ho_ref[...] = h
    ms = jnp.mean(h * h, axis=-1, keepdims=True)
    y2 = h * lax.rsqrt(ms + EPS) * g2_ref[...]
    f_ref[...] = y2 * (1.0 + mod_ref[0, :, 4 * d:5 * d]) + mod_ref[0, :, 3 * d:4 * d]


def _out_projection(na, y_all, qkvu, h2d, mod3, d_skip, wglu_bf16, b_glu, wout_bf16, g2,
                    mod_map, rows_per_batch, y_blocks_per_batch, y_block0):
    r, d = h2d.shape
    tiles_per_batch = rows_per_batch // TM_PROJ
    n_seq = y_all.shape[1]
    y3 = y_all.reshape(y_all.shape[0] * n_seq // TM_PROJ, TM_PROJ, S5_WIDTH)
    u_blk = 3 * NA_WIDTH // S5_WIDTH

    def y_map(i):
        return ((i // tiles_per_batch) * y_blocks_per_batch + y_block0 + i % tiles_per_batch, 0, 0)

    const = lambda i: (0, 0)
    return pl.pallas_call(
        _outproj_kernel,
        out_shape=(jax.ShapeDtypeStruct((r, d), F32), jax.ShapeDtypeStruct((r, d), F32)),
        grid=(r // TM_PROJ,),
        in_specs=[
            pl.BlockSpec((TM_PROJ, NA_WIDTH), lambda i: (i, 0)),
            pl.BlockSpec((1, TM_PROJ, S5_WIDTH), y_map),
            pl.BlockSpec((TM_PROJ, S5_WIDTH), lambda i: (i, u_blk)),
            pl.BlockSpec((TM_PROJ, d), lambda i: (i, 0)),
            pl.BlockSpec((1, 1, N_MOD * d), mod_map),
            pl.BlockSpec((1, S5_WIDTH), const),
            pl.BlockSpec((S5_WIDTH, S5_WIDTH), const),
            pl.BlockSpec((1, S5_WIDTH), const),
            pl.BlockSpec((NA_WIDTH + S5_WIDTH, d), const),
            pl.BlockSpec((1, d), const),
        ],
        out_specs=(pl.BlockSpec((TM_PROJ, d), lambda i: (i, 0)), pl.BlockSpec((TM_PROJ, d), lambda i: (i, 0))),
        compiler_params=_cparams(("parallel",)),
        name="out_projection",
    )(na, y3, qkvu, h2d, mod3, d_skip.reshape(1, -1), wglu_bf16, b_glu.reshape(1, -1), wout_bf16,
      g2.reshape(1, d))


def _top2(vals):
    best = vals[0]
    bi = jnp.zeros(best.shape, jnp.int32)
    for i in range(1, len(vals)):
        gt = vals[i] > best
        best = jnp.where(gt, vals[i], best)
        bi = jnp.where(gt, i, bi)
    second = jnp.full(best.shape, -jnp.inf, F32)
    si = jnp.zeros(best.shape, jnp.int32)
    for i in range(len(vals)):
        cand = jnp.where(bi == i, -jnp.inf, vals[i])
        gt = cand > second
        second = jnp.where(gt, cand, second)
        si = jnp.where(gt, i, si)
    return best, bi, second, si


def _router_kernel(f_ref, rwt_ref, rb_ref, idx_ref, gate_ref):
    logits = lax.dot_general(rwt_ref[...], f_ref[...], (((1,), (1,)), ((), ())),
                             precision=HIGHEST, preferred_element_type=F32)
    m = jnp.max(logits, axis=0, keepdims=True)
    e = jnp.exp(logits - m)
    probs = e / jnp.sum(e, axis=0, keepdims=True)
    sel = probs + rb_ref[...]
    sel_rows = [sel[i:i + 1, :] for i in range(N_EXPERTS)]
    prob_rows = [probs[i:i + 1, :] for i in range(N_EXPERTS)]
    scores = []
    for g in range(N_GROUPS):
        b, _, s, _ = _top2(sel_rows[g * EPG:(g + 1) * EPG])
        scores.append(b + s)
    grp = jnp.zeros(scores[0].shape, jnp.int32)
    gbest = scores[0]
    for g in range(1, N_GROUPS):
        gt = scores[g] > gbest
        gbest = jnp.where(gt, scores[g], gbest)
        grp = jnp.where(gt, g, grp)
    in_rows = []
    for j in range(EPG):
        v = sel_rows[j]
        for g in range(1, N_GROUPS):
            v = jnp.where(grp == g, sel_rows[g * EPG + j], v)
        in_rows.append(v)
    _, l1, _, l2 = _top2(in_rows)
    i1 = grp * EPG + l1
    i2 = grp * EPG + l2
    w1 = jnp.zeros(gbest.shape, F32)
    w2 = jnp.zeros(gbest.shape, F32)
    for i in range(N_EXPERTS):
        w1 = jnp.where(i1 == i, prob_rows[i], w1)
        w2 = jnp.where(i2 == i, prob_rows[i], w2)
    tot = w1 + w2
    idx_ref[0:1, :] = i1
    idx_ref[1:2, :] = i2
    gate_ref[0:1, :] = w1 / tot
    gate_ref[1:2, :] = w2 / tot


def _router(f_all, router_w, router_bias):
    n, d = f_all.shape
    return pl.pallas_call(
        _router_kernel,
        out_shape=(jax.ShapeDtypeStruct((2, n), jnp.int32), jax.ShapeDtypeStruct((2, n), F32)),
        grid=(n // TM_ROUTE,),
        in_specs=[
            pl.BlockSpec((TM_ROUTE, d), lambda i: (i, 0)),
            pl.BlockSpec((N_EXPERTS, d), lambda i: (0, 0)),
            pl.BlockSpec((N_EXPERTS, 1), lambda i: (0, 0)),
        ],
        out_specs=(pl.BlockSpec((2, TM_ROUTE), lambda i: (0, i)), pl.BlockSpec((2, TM_ROUTE), lambda i: (0, i))),
        compiler_params=_cparams(("parallel",)),
        name="router",
    )(f_all, router_w.T.astype(F32), router_bias.reshape(N_EXPERTS, 1).astype(F32))


def _scatter_rows_kernel(dest_ref, pad_ref, end_ref, *refs, n_tok, seg_tiles):
    f_refs = refs[:len(seg_tiles)]
    xs_ref, zero_scr, sem = refs[len(seg_tiles):]
    i = pl.program_id(0)
    tm = f_refs[0].shape[0]
    n_rows = xs_ref.shape[0]
    slab = TM_EXP + SUBLANES

    def slab_copy(start, rows):
        return pltpu.make_async_copy(zero_scr.at[pl.ds(0, rows), :], xs_ref.at[pl.ds(start, rows), :], sem)

    @pl.when(i == 0)
    def _():
        zero_scr[...] = jnp.zeros(zero_scr.shape, zero_scr.dtype)
        for e in range(N_EXPERTS):
            start = jnp.minimum((pad_ref[e] // SUBLANES) * SUBLANES, n_rows - slab)
            slab_copy(pl.multiple_of(start, SUBLANES), slab).start()
        for e in range(N_EXPERTS):
            slab_copy(0, slab).wait()
        for k in range(N_EXPERTS):
            start = end_ref[0] + k * TM_EXP

            @pl.when(start < n_rows)
            def _():
                cp = slab_copy(pl.multiple_of(start, TM_EXP), TM_EXP)
                cp.start()
                cp.wait()

    def scatter_tile(f_ref):
        base = i * tm

        def row_copy(r, d):
            return pltpu.make_async_copy(f_ref.at[pl.ds(r, 1), :], xs_ref.at[pl.ds(d, 1), :], sem)

        def body(r, carry):
            row_copy(r, dest_ref[base + r]).start()
            row_copy(r, dest_ref[n_tok + base + r]).start()
            return carry
        lax.fori_loop(0, tm, body, 0, unroll=8)
        for _ in range(2):
            pltpu.make_async_copy(f_ref, xs_ref.at[pl.ds(0, tm), :], sem).wait()

    tile0 = 0
    for f_ref, n_t in zip(f_refs, seg_tiles):
        pl.when((i >= tile0) & (i < tile0 + n_t))(functools.partial(scatter_tile, f_ref))
        tile0 += n_t


def _scatter_rows(segments, dest_flat, pad_start, total_end):
    d = segments[0].shape[1]
    seg_tiles = tuple(s.shape[0] // TM_PROJ for s in segments)
    n_tok = sum(s.shape[0] for s in segments)
    r_max = 2 * n_tok + N_EXPERTS * TM_EXP
    in_specs = []
    tile0 = 0
    for n_t in seg_tiles:
        in_specs.append(pl.BlockSpec(
            (TM_PROJ, d), lambda i, *_, t0=tile0, nt=n_t: (jnp.clip(i - t0, 0, nt - 1), 0)))
        tile0 += n_t
    grid_spec = pltpu.PrefetchScalarGridSpec(
        num_scalar_prefetch=3,
        grid=(tile0,),
        in_specs=in_specs,
        out_specs=pl.BlockSpec(memory_space=pl.ANY),
        scratch_shapes=[pltpu.VMEM((TM_EXP + SUBLANES, d), F32), pltpu.SemaphoreType.DMA(())],
    )
    return pl.pallas_call(
        functools.partial(_scatter_rows_kernel, n_tok=n_tok, seg_tiles=seg_tiles),
        out_shape=jax.ShapeDtypeStruct((r_max, d), F32),
        grid_spec=grid_spec,
        compiler_params=_cparams(("arbitrary",)),
        name="moe_scatter_rows",
    )(dest_flat, pad_start, total_end, *segments)


def _experts_kernel(te_ref, nv_ref, x_ref, wg_ref, wu_ref, wd_ref, o_ref, wg_scr, wu_scr, wd_scr):
    i = pl.program_id(0)
    e = te_ref[i]
    prev = te_ref[jnp.maximum(i - 1, 0)]
    rows = 128

    @pl.when((i == 0) | (e != prev))
    def _():
        def body(r, carry):
            sl = pl.ds(pl.multiple_of(r * rows, rows), rows)
            wg_scr[sl, :] = wg_ref[0, sl, :].astype(BF16)
            wu_scr[sl, :] = wu_ref[0, sl, :].astype(BF16)
            wd_scr[sl, :] = wd_ref[0, sl, :].astype(BF16)
            return carry
        lax.fori_loop(0, wg_scr.shape[0] // rows, body, 0)

    @pl.when(i < nv_ref[0])
    def _():
        x = x_ref[...].astype(BF16)
        g = _dot(x, wg_scr[...])
        u = _dot(x, wu_scr[...])
        a = (g * jax.nn.sigmoid(g)) * u
        o_ref[...] = _dot(a.astype(BF16), wd_scr[...]).astype(BF16)

    @pl.when(i >= nv_ref[0])
    def _():
        o_ref[...] = jnp.zeros(o_ref.shape, BF16)


def _experts(xs, tile_expert, n_valid, w_gate, w_up, w_down):
    r, d = xs.shape
    de = w_gate.shape[2]
    n_tiles = r // TM_EXP
    x_map = lambda i, te, nv: (jnp.minimum(i, nv[0] - 1), 0)
    grid_spec = pltpu.PrefetchScalarGridSpec(
        num_scalar_prefetch=2,
        grid=(n_tiles,),
        in_specs=[
            pl.BlockSpec((TM_EXP, d), x_map),
            pl.BlockSpec((1, d, de), lambda i, te, nv: (te[i], 0, 0)),
            pl.BlockSpec((1, d, de), lambda i, te, nv: (te[i], 0, 0)),
            pl.BlockSpec((1, de, d), lambda i, te, nv: (te[i], 0, 0)),
        ],
        out_specs=pl.BlockSpec((TM_EXP, d), lambda i, te, nv: (i, 0)),
        scratch_shapes=[pltpu.VMEM((d, de), BF16), pltpu.VMEM((d, de), BF16), pltpu.VMEM((de, d), BF16)],
    )
    return pl.pallas_call(
        _experts_kernel,
        out_shape=jax.ShapeDtypeStruct((r, d), BF16),
        grid_spec=grid_spec,
        compiler_params=_cparams(("arbitrary",)),
        name="experts",
    )(tile_expert, n_valid, xs, w_gate, w_up, w_down)


def _dispatch(idx):
    n = idx.shape[1]
    e_flat = idx.reshape(-1)
    onehot = (e_flat[:, None] == jnp.arange(N_EXPERTS, dtype=jnp.int32)[None, :]).astype(jnp.int32)
    csum = jnp.cumsum(onehot, axis=0)
    rank = jnp.sum(csum * onehot, axis=1) - 1
    counts = csum[-1]
    padded = ((counts + TM_EXP - 1) // TM_EXP) * TM_EXP
    ends = jnp.cumsum(padded)
    starts = ends - padded
    dest = (jnp.sum(onehot * starts[None, :], axis=1) + rank).astype(jnp.int32)
    r_max = 2 * n + N_EXPERTS * TM_EXP
    tile_start = jnp.arange(r_max // TM_EXP, dtype=jnp.int32) * TM_EXP
    tile_expert = jnp.minimum(jnp.sum((tile_start[:, None] >= ends[None, :]).astype(jnp.int32), axis=1),
                              N_EXPERTS - 1).astype(jnp.int32)
    n_valid = (ends[-1] // TM_EXP).astype(jnp.int32).reshape(1)
    pad_start = (starts + counts).astype(jnp.int32)
    total_end = ends[-1].astype(jnp.int32).reshape(1)
    return dest, pad_start, total_end, tile_expert, n_valid


def _combine_kernel(h_ref, y1_ref, y2_ref, gate_ref, mod_ref, o_ref):
    d = D_MODEL
    g = gate_ref[...]
    y = g[:, 0:1] * y1_ref[...].astype(F32) + g[:, 1:2] * y2_ref[...].astype(F32)
    o_ref[...] = h_ref[...] + mod_ref[0, :, 5 * d:6 * d] * y


def _combine(h2d, y1, y2, gates_t, mod3, mod_map, row0):
    r, d = h2d.shape
    blk0 = row0 // TM_PROJ
    row = lambda i: (i, 0)
    seg = lambda i: (blk0 + i, 0)
    return pl.pallas_call(
        _combine_kernel,
        out_shape=jax.ShapeDtypeStruct((r, d), F32),
        grid=(r // TM_PROJ,),
        in_specs=[
            pl.BlockSpec((TM_PROJ, d), row),
            pl.BlockSpec((TM_PROJ, d), seg),
            pl.BlockSpec((TM_PROJ, d), seg),
            pl.BlockSpec((TM_PROJ, 2), seg),
            pl.BlockSpec((1, 1, N_MOD * d), mod_map),
        ],
        out_specs=pl.BlockSpec((TM_PROJ, d), row),
        compiler_params=_cparams(("parallel",)),
        name="moe_combine",
    )(h2d, y1, y2, gates_t, mod3)


def kernel(x, c, ctx, c_ctx, w_mod, b_mod, norm1_g, norm2_g, w_in, w_out, q_norm_g, k_norm_g, na_rpb,
           s5_lam_re, s5_lam_im, s5_log_dt, s5_b_re, s5_b_im, s5_c_re, s5_c_im, s5_d, s5_w_glu, s5_b_glu,
           router_w, router_bias, moe_w_gate, moe_w_up, moe_w_down):
    bsz, n_lat, d = x.shape
    n_ctx = ctx.shape[1]
    depth = w_mod.shape[0]
    ctx_row = bsz
    c_rows = jnp.concatenate([c.astype(F32), c_ctx.astype(F32)[None],
                              jnp.zeros((SUBLANES - bsz - 1, d), F32)], axis=0)
    mod_all = _modulation(c_rows, w_mod.astype(F32), b_mod.astype(F32))

    h_lat = x.reshape(bsz * n_lat, d).astype(F32)
    h_ctx = ctx.reshape(bsz * n_ctx, d).astype(F32)
    y_blocks = (n_ctx + n_lat) // TM_PROJ
    lat_map = _mod_row_map(n_lat, 0, True)
    ctx_map = _mod_row_map(n_ctx, ctx_row, False)

    for layer in range(depth):
        ctx_out = layer < depth - 1
        mod3 = mod_all[layer].reshape(SUBLANES, 1, N_MOD * d)
        w_in_b = w_in[layer].astype(BF16)
        qkvu_lat = _in_projection(h_lat, norm1_g[layer], mod3, w_in_b, lat_map)
        qkvu_ctx = _in_projection(h_ctx, norm1_g[layer], mod3, w_in_b, ctx_map)
        bias_tab = _na_bias_table(na_rpb[layer])
        na_lat = _neighborhood_attention(qkvu_lat, qkvu_ctx, bias_tab, q_norm_g[layer], k_norm_g[layer], bsz)
        mats = _s5_matrices(s5_lam_re[layer], s5_lam_im[layer], s5_log_dt[layer], s5_b_re[layer],
                            s5_b_im[layer], s5_c_re[layer], s5_c_im[layer])
        y_all = _s5_mixer(qkvu_lat, qkvu_ctx, mats, bsz)
        wglu_b = s5_w_glu[layer].astype(BF16)
        wout_b = w_out[layer].astype(BF16)
        h_lat, f_lat = _out_projection(na_lat, y_all, qkvu_lat, h_lat, mod3, s5_d[layer], wglu_b,
                                       s5_b_glu[layer], wout_b, norm2_g[layer],
                                       lat_map, n_lat, y_blocks, n_ctx // TM_PROJ)
        if ctx_out:
            na_ctx = _context_attention(qkvu_ctx, q_norm_g[layer], k_norm_g[layer], bsz)
            h_ctx, f_ctx = _out_projection(na_ctx, y_all, qkvu_ctx, h_ctx, mod3, s5_d[layer], wglu_b,
                                           s5_b_glu[layer], wout_b, norm2_g[layer],
                                           ctx_map, n_ctx, y_blocks, 0)
        n_l = bsz * n_lat
        idx, gates = _router(f_lat, router_w, router_bias)
        if ctx_out:
            idx_c, gates_c = _router(f_ctx, router_w, router_bias)
            idx = jnp.concatenate([idx, idx_c], axis=1)
            gates = jnp.concatenate([gates, gates_c], axis=1)
        n_tok = idx.shape[1]
        dest, pad_start, total_end, tile_expert, n_valid = _dispatch(idx)
        xs = _scatter_rows([f_lat, f_ctx] if ctx_out else [f_lat], dest, pad_start, total_end)
        ys = _experts(xs, tile_expert, n_valid, moe_w_gate[layer], moe_w_up[layer], moe_w_down[layer])
        y1 = jnp.take(ys, dest[:n_tok], axis=0)
        y2 = jnp.take(ys, dest[n_tok:], axis=0)
        gates_t = gates.T
        h_lat = _combine(h_lat, y1, y2, gates_t, mod3, lat_map, 0)
        if ctx_out:
            h_ctx = _combine(h_ctx, y1, y2, gates_t, mod3, ctx_map, n_l)
    return h_lat.reshape(bsz, n_lat, d).astype(x.dtype)
```

```python
import functools
import math

import jax
import jax.numpy as jnp
from jax import lax
from jax.experimental import pallas as pl
from jax.experimental.pallas import tpu as pltpu

F32 = jnp.float32
BF16 = jnp.bfloat16
HIGHEST = lax.Precision.HIGHEST

D_MODEL = 1024
GRID_W = 64
HEAD_DIM = 64
NA_WIDTH = 512
S5_WIDTH = 512
S5_CH = 16
S5_GROUPS = 32
S5_STATE = 64
WIN_ROWS = 8
WIN_COLS = 16
N_EXPERTS = 16
N_GROUPS = 4
EPG = 4
N_MOD = 6
EPS = 1e-6

LANES = 128
SUBLANES = 8
VMEM_LIMIT = 56 * 1024 * 1024

TM_PROJ = 256
Q_ROWS = 8
Q_COLS = 16
K_ROWS = 16
K_COLS = 32
CHUNK = 16
PAIRS_PER_STEP = 2
TM_EXP = 256
TM_ROUTE = 512
MASK_VALUE = -1e30


def _cparams(sem):
    return pltpu.CompilerParams(dimension_semantics=sem, vmem_limit_bytes=VMEM_LIMIT)


def _dot(a, b):
    return jnp.dot(a, b, preferred_element_type=F32)


def _dot_nt(a, b):
    return lax.dot_general(a, b, (((1,), (1,)), ((), ())), preferred_element_type=F32)


def _mod_kernel(c_ref, w_ref, b_ref, o_ref):
    a = c_ref[...]
    a = a * jax.nn.sigmoid(a)
    o_ref[0] = jnp.dot(a, w_ref[0], precision=HIGHEST, preferred_element_type=F32) + b_ref[0]


def _modulation(c_rows, w_mod, b_mod):
    depth, d, n = w_mod.shape
    tn = 1536
    return pl.pallas_call(
        _mod_kernel,
        out_shape=jax.ShapeDtypeStruct((depth, SUBLANES, n), F32),
        grid=(depth, n // tn),
        in_specs=[
            pl.BlockSpec((SUBLANES, d), lambda l, j: (0, 0)),
            pl.BlockSpec((1, d, tn), lambda l, j: (l, 0, j)),
            pl.BlockSpec((1, 1, tn), lambda l, j: (l, 0, j)),
        ],
        out_specs=pl.BlockSpec((1, SUBLANES, tn), lambda l, j: (l, 0, j)),
        compiler_params=_cparams(("arbitrary", "arbitrary")),
        name="modulation",
    )(c_rows, w_mod, b_mod.reshape(depth, 1, n))


def _inproj_kernel(x_ref, g_ref, mod_ref, w_ref, o_ref):
    x = x_ref[...]
    ms = jnp.mean(x * x, axis=-1, keepdims=True)
    y = x * lax.rsqrt(ms + EPS) * g_ref[...]
    shift = mod_ref[0, :, 0:D_MODEL]
    scale = mod_ref[0, :, D_MODEL:2 * D_MODEL]
    a = y * (1.0 + scale) + shift
    o_ref[...] = _dot(a.astype(BF16), w_ref[...])


def _mod_row_map(rows_per_batch, mod_row0, per_batch):
    tiles_per_batch = rows_per_batch // TM_PROJ
    if per_batch:
        return lambda i: (mod_row0 + i // tiles_per_batch, 0, 0)
    return lambda i: (mod_row0, 0, 0)


def _in_projection(x2d, g, mod3, w_bf16, mod_map):
    r, d = x2d.shape
    n = w_bf16.shape[1]
    return pl.pallas_call(
        _inproj_kernel,
        out_shape=jax.ShapeDtypeStruct((r, n), F32),
        grid=(r // TM_PROJ,),
        in_specs=[
            pl.BlockSpec((TM_PROJ, d), lambda i: (i, 0)),
            pl.BlockSpec((1, d), lambda i: (0, 0)),
            pl.BlockSpec((1, 1, N_MOD * d), mod_map),
            pl.BlockSpec((d, n), lambda i: (0, 0)),
        ],
        out_specs=pl.BlockSpec((TM_PROJ, n), lambda i: (i, 0)),
        compiler_params=_cparams(("parallel",)),
        name="in_projection",
    )(x2d, g.reshape(1, d), mod3, w_bf16)


def _pair_rms(x, g, lo):
    ss = x * x
    sa = jnp.sum(jnp.where(lo, ss, 0.0), axis=-1, keepdims=True)
    sb = jnp.sum(jnp.where(lo, 0.0, ss), axis=-1, keepdims=True)
    ms = jnp.where(lo, sa, sb) * (1.0 / HEAD_DIM)
    return x * lax.rsqrt(ms + EPS) * g


def _softmax_pv(qm, kw, kcb, vw, vcb, bias):
    s_nb = _dot_nt(qm, kw) + bias
    s_cx = _dot_nt(qm, kcb)
    m = jnp.maximum(jnp.max(s_nb, axis=-1, keepdims=True), jnp.max(s_cx, axis=-1, keepdims=True))
    p_nb = jnp.exp(s_nb - m)
    p_cx = jnp.exp(s_cx - m)
    l = jnp.sum(p_nb, axis=-1, keepdims=True) + jnp.sum(p_cx, axis=-1, keepdims=True)
    o = _dot(p_nb.astype(BF16), vw) + _dot(p_cx.astype(BF16), vcb)
    return o / l


def _na_kernel(q_ref, k_ref, v_ref, kc_ref, vc_ref, bias_ref, qg_ref, kg_ref, o_ref, kn_scr, kcn_scr):
    t = pl.program_id(2)
    lo = lax.broadcasted_iota(jnp.int32, (1, LANES), 1) < HEAD_DIM
    n_rows = k_ref.shape[1]
    col_tiles = GRID_W // Q_COLS

    @pl.when(t == 0)
    def _():
        def body(r, carry):
            kn_scr[r] = _pair_rms(k_ref[0, r], kg_ref[...], lo)
            return carry
        lax.fori_loop(0, n_rows, body, 0)
        kcn_scr[...] = _pair_rms(kc_ref[0], kg_ref[...], lo).astype(BF16)

    i = t // col_tiles
    j = t % col_tiles
    kr0 = jnp.clip(Q_ROWS * i - WIN_ROWS // 2, 0, n_rows - K_ROWS)
    kc0 = pl.multiple_of(jnp.clip(Q_COLS * j - WIN_COLS // 2, 0, GRID_W - K_COLS), SUBLANES)
    rt = jnp.where(i == 0, 0, jnp.where(i == n_rows // Q_ROWS - 1, 2, 1))
    ct = jnp.where(j == 0, 0, jnp.where(j == col_tiles - 1, 2, 1))
    typ = rt * 3 + ct

    nq = Q_ROWS * Q_COLS
    nk = K_ROWS * K_COLS
    q = q_ref[0].reshape(nq, LANES)
    qn = _pair_rms(q, qg_ref[...], lo) * (HEAD_DIM ** -0.5)
    kw = kn_scr[pl.ds(kr0, K_ROWS), pl.ds(kc0, K_COLS), :].reshape(nk, LANES).astype(BF16)
    vw = v_ref[0, pl.ds(kr0, K_ROWS), pl.ds(kc0, K_COLS), :].reshape(nk, LANES).astype(BF16)
    kcb = kcn_scr[...]
    vcb = vc_ref[0].astype(BF16)
    o_a = _softmax_pv(jnp.where(lo, qn, 0.0).astype(BF16), kw, kcb, vw, vcb, bias_ref[0, typ, 0])
    o_b = _softmax_pv(jnp.where(lo, 0.0, qn).astype(BF16), kw, kcb, vw, vcb, bias_ref[0, typ, 1])
    o_ref[0] = jnp.where(lo, o_a, o_b).reshape(Q_ROWS, Q_COLS, LANES)


def _na_bias_table(rpb):
    h = rpb.shape[0]
    rpb = rpb.astype(F32)

    def axis_tables(q_n, k_n, win, d_opts, origin_opts, extent):
        onehots, valids = [], []
        for d_rel, origin in zip(d_opts, origin_opts):
            qa = origin + jnp.arange(q_n)[:, None]
            ka = origin + d_rel + jnp.arange(k_n)[None, :]
            start = jnp.clip(qa - win // 2, 0, extent - win)
            valid = (ka >= start) & (ka < start + win)
            off = jnp.clip(ka - qa + win - 1, 0, 2 * win - 2)
            onehots.append(jax.nn.one_hot(off, 2 * win - 1, dtype=F32))
            valids.append(valid)
        return jnp.stack(onehots), jnp.stack(valids)

    rows = GRID_W
    oh_r, va_r = axis_tables(Q_ROWS, K_ROWS, WIN_ROWS, (0, -4, -8), (0, Q_ROWS, rows - Q_ROWS), rows)
    oh_c, va_c = axis_tables(Q_COLS, K_COLS, WIN_COLS, (0, -8, -16), (0, Q_COLS, GRID_W - Q_COLS), GRID_W)
    t1 = jnp.einsum('raku,huv->hrakv', oh_r, rpb, precision=HIGHEST)
    bias = jnp.einsum('hrakv,cbjv->hrcabkj', t1, oh_c, precision=HIGHEST)
    valid = va_r[:, None, :, None, :, None] & va_c[None, :, None, :, None, :]
    bias = jnp.where(valid[None], bias, MASK_VALUE)
    bias = bias.reshape(h // 2, 2, 9, Q_ROWS * Q_COLS, K_ROWS * K_COLS)
    return jnp.transpose(bias, (0, 2, 1, 3, 4))


def _neighborhood_attention(qkvu_lat, qkvu_ctx, bias_tab, qg, kg, bsz):
    n_lat = qkvu_lat.shape[0] // bsz
    n_ctx = qkvu_ctx.shape[0] // bsz
    rows = n_lat // GRID_W
    n_cols = qkvu_lat.shape[1]
    lat4 = qkvu_lat.reshape(bsz, rows, GRID_W, n_cols)
    ctx3 = qkvu_ctx.reshape(bsz, n_ctx, n_cols)
    n_pairs = NA_WIDTH // LANES
    col_tiles = GRID_W // Q_COLS
    n_tiles = (rows // Q_ROWS) * col_tiles
    g2 = lambda g: jnp.concatenate([g, g]).reshape(1, LANES).astype(F32)
    out = pl.pallas_call(
        _na_kernel,
        out_shape=jax.ShapeDtypeStruct((bsz, rows, GRID_W, NA_WIDTH), F32),
        grid=(bsz, n_pairs, n_tiles),
        in_specs=[
            pl.BlockSpec((1, Q_ROWS, Q_COLS, LANES), lambda b, p, t: (b, t // col_tiles, t % col_tiles, p)),
            pl.BlockSpec((1, rows, GRID_W, LANES), lambda b, p, t: (b, 0, 0, n_pairs + p)),
            pl.BlockSpec((1, rows, GRID_W, LANES), lambda b, p, t: (b, 0, 0, 2 * n_pairs + p)),
            pl.BlockSpec((1, n_ctx, LANES), lambda b, p, t: (b, 0, n_pairs + p)),
            pl.BlockSpec((1, n_ctx, LANES), lambda b, p, t: (b, 0, 2 * n_pairs + p)),
            pl.BlockSpec((1, 9, 2, Q_ROWS * Q_COLS, K_ROWS * K_COLS), lambda b, p, t: (p, 0, 0, 0, 0)),
            pl.BlockSpec((1, LANES), lambda b, p, t: (0, 0)),
            pl.BlockSpec((1, LANES), lambda b, p, t: (0, 0)),
        ],
        out_specs=pl.BlockSpec((1, Q_ROWS, Q_COLS, LANES),
                               lambda b, p, t: (b, t // col_tiles, t % col_tiles, p)),
        scratch_shapes=[pltpu.VMEM((rows, GRID_W, LANES), F32), pltpu.VMEM((n_ctx, LANES), BF16)],
        compiler_params=_cparams(("parallel", "parallel", "arbitrary")),
        name="neighborhood_attention",
    )(lat4, lat4, lat4, ctx3, ctx3, bias_tab, g2(qg), g2(kg))
    return out.reshape(bsz * n_lat, NA_WIDTH)


def _ctx_attn_kernel(q_ref, k_ref, v_ref, qg_ref, kg_ref, o_ref):
    lo = lax.broadcasted_iota(jnp.int32, (1, LANES), 1) < HEAD_DIM
    qn = _pair_rms(q_ref[0], qg_ref[...], lo) * (HEAD_DIM ** -0.5)
    kn = _pair_rms(k_ref[0], kg_ref[...], lo).astype(BF16)
    vb = v_ref[0].astype(BF16)

    def one(qm):
        s = _dot_nt(qm, kn)
        m = jnp.max(s, axis=-1, keepdims=True)
        p = jnp.exp(s - m)
        l = jnp.sum(p, axis=-1, keepdims=True)
        return _dot(p.astype(BF16), vb) / l

    o_a = one(jnp.where(lo, qn, 0.0).astype(BF16))
    o_b = one(jnp.where(lo, 0.0, qn).astype(BF16))
    o_ref[0] = jnp.where(lo, o_a, o_b)


def _context_attention(qkvu_ctx, qg, kg, bsz):
    n_ctx = qkvu_ctx.shape[0] // bsz
    ctx3 = qkvu_ctx.reshape(bsz, n_ctx, qkvu_ctx.shape[1])
    n_pairs = NA_WIDTH // LANES
    g2 = lambda g: jnp.concatenate([g, g]).reshape(1, LANES).astype(F32)
    out = pl.pallas_call(
        _ctx_attn_kernel,
        out_shape=jax.ShapeDtypeStruct((bsz, n_ctx, NA_WIDTH), F32),
        grid=(bsz, n_pairs),
        in_specs=[
            pl.BlockSpec((1, n_ctx, LANES), lambda b, p: (b, 0, p)),
            pl.BlockSpec((1, n_ctx, LANES), lambda b, p: (b, 0, n_pairs + p)),
            pl.BlockSpec((1, n_ctx, LANES), lambda b, p: (b, 0, 2 * n_pairs + p)),
            pl.BlockSpec((1, LANES), lambda b, p: (0, 0)),
            pl.BlockSpec((1, LANES), lambda b, p: (0, 0)),
        ],
        out_specs=pl.BlockSpec((1, n_ctx, LANES), lambda b, p: (b, 0, p)),
        compiler_params=_cparams(("parallel", "parallel")),
        name="context_attention",
    )(ctx3, ctx3, ctx3, g2(qg), g2(kg))
    return out.reshape(bsz * n_ctx, NA_WIDTH)


def _s5_kernel(uc_ref, ul_ref, w_ref, m_ref, v_ref, a_ref, y_ref, x_scr, s_scr, hf_scr, hr_scr, *, n_ctx_chunks):
    bsz = ul_ref.shape[0]
    n_lat_chunks = ul_ref.shape[1] // CHUNK
    n_chunks = n_ctx_chunks + n_lat_chunks
    rows = n_chunks * SUBLANES
    n_pairs = w_ref.shape[0]
    gpb = 2 * n_pairs
    half = 2 * LANES
    tile_chunks = TM_PROJ // CHUNK
    lane_blk = lax.broadcasted_iota(jnp.int32, (n_chunks, LANES), 1) // S5_CH

    for b in range(bsz):
        for q in range(2):
            rolled = []
            for j in range(SUBLANES):
                s = SUBLANES * q + j
                u_s = jnp.concatenate([uc_ref[b, pl.ds(s, n_ctx_chunks, stride=CHUNK), :],
                                       ul_ref[b, pl.ds(s, n_lat_chunks, stride=CHUNK), :]], axis=0)
                rolled.append(u_s if j == 0 else pltpu.roll(u_s, j * S5_CH, 1))
            for g in range(gpb):
                xg = rolled[0]
                for j in range(1, SUBLANES):
                    xg = jnp.where(lane_blk == (g + j) % SUBLANES, rolled[j], xg)
                x_scr[g // 2, q, pl.ds((g % 2) * bsz + b, n_chunks, stride=SUBLANES), :] = xg

    n_blk = 8
    rb = rows // n_blk
    first_group = (lax.broadcasted_iota(jnp.int32, (rb, half), 0) & (SUBLANES // 2)) == 0
    fwd_cols = (lax.broadcasted_iota(jnp.int32, (rb, half), 1) & (LANES - 1)) < S5_STATE
    is_fwd = lax.broadcasted_iota(jnp.int32, (SUBLANES, LANES), 1) < S5_STATE
    zero = jnp.zeros((SUBLANES, LANES), F32)

    def x_rows(p, sl):
        return jnp.concatenate([x_scr[p, 0, sl, :], x_scr[p, 1, sl, :]], axis=1).astype(BF16)

    def put_cols(scr, i, sl, val):
        scr[i, 0, sl, :] = val[:, 0:LANES]
        scr[i, 1, sl, :] = val[:, LANES:half]

    for hh in range(n_pairs // PAIRS_PER_STEP):
        pairs = [hh * PAIRS_PER_STEP + i for i in range(PAIRS_PER_STEP)]
        for i, p in enumerate(pairs):
            for blk in range(n_blk):
                sl = slice(blk * rb, (blk + 1) * rb)
                r = _dot(x_rows(p, sl), w_ref[p])
                put_cols(s_scr, i, sl, jnp.where(first_group, r[:, :half], r[:, half:]))

        a_re = [a_ref[p, :, 0:LANES] for p in pairs]
        a_im = [a_ref[p, :, LANES:half] for p in pairs]

        def body(k, carry):
            kr = jnp.where(k < n_ctx_chunks, n_ctx_chunks - 1 - k, n_chunks + n_ctx_chunks - 1 - k)
            rf = pl.ds(pl.multiple_of(k * SUBLANES, SUBLANES), SUBLANES)
            rr = pl.ds(pl.multiple_of(kr * SUBLANES, SUBLANES), SUBLANES)
            new = []
            for i in range(PAIRS_PER_STEP):
                h_re, h_im = carry[2 * i], carry[2 * i + 1]
                hf_scr[i, 0, rf, :] = h_re
                hf_scr[i, 1, rf, :] = h_im
                hr_scr[i, 0, rr, :] = h_re
                hr_scr[i, 1, rr, :] = h_im
                s_re = jnp.where(is_fwd, s_scr[i, 0, rf, :], s_scr[i, 0, rr, :])
                s_im = jnp.where(is_fwd, s_scr[i, 1, rf, :], s_scr[i, 1, rr, :])
                new.append(a_re[i] * h_re - a_im[i] * h_im + s_re)
                new.append(a_re[i] * h_im + a_im[i] * h_re + s_im)
            return tuple(new)

        lax.fori_loop(0, n_chunks, body, (zero,) * (2 * PAIRS_PER_STEP))

        for i, p in enumerate(pairs):
            for blk in range(n_blk):
                sl = slice(blk * rb, (blk + 1) * rb)
                hf = jnp.concatenate([hf_scr[i, 0, sl, :], hf_scr[i, 1, sl, :]], axis=1)
                hr = jnp.concatenate([hr_scr[i, 0, sl, :], hr_scr[i, 1, sl, :]], axis=1)
                h_in = jnp.where(fwd_cols, hf, hr).astype(BF16)
                r = _dot(x_rows(p, sl), m_ref[p]) + _dot(h_in, v_ref[p])
                put_cols(s_scr, i, sl, jnp.where(first_group, r[:, :half], r[:, half:]))

        g_lo = 2 * pairs[0]
        n_g = 2 * PAIRS_PER_STEP
        out_blk = lax.broadcasted_iota(jnp.int32, (tile_chunks, LANES), 1) // S5_CH
        lanes_out = (out_blk >= g_lo) & (out_blk < g_lo + n_g)
        for b in range(bsz):
            for q in range(2):
                y_g = [s_scr[(g - g_lo) // 2, q, pl.ds((g % 2) * bsz + b, n_chunks, stride=SUBLANES), :]
                       for g in range(g_lo, g_lo + n_g)]
                for j in range(SUBLANES):
                    z = y_g[0]
                    for gi in range(1, n_g):
                        z = jnp.where(lane_blk == (g_lo + gi + j) % SUBLANES, y_g[gi], z)
                    if j:
                        z = pltpu.roll(z, LANES - j * S5_CH, 1)
                    t = SUBLANES * q + j
                    for ct in range(n_chunks // tile_chunks):
                        r0 = ct * TM_PROJ + t * tile_chunks
                        pltpu.store(y_ref.at[b, 0, r0:r0 + tile_chunks, :],
                                    z[ct * tile_chunks:(ct + 1) * tile_chunks, :], mask=lanes_out)


def _s5_scan(qkvu_ctx3, qkvu_lat3, w_c, m_c, v_c, a_c):
    bsz, n_ctx, _ = qkvu_ctx3.shape
    n_lat = qkvu_lat3.shape[1]
    n_seq = n_ctx + n_lat
    rows = n_seq // CHUNK * SUBLANES
    n_blocks = S5_WIDTH // LANES
    ppb = w_c.shape[0] // n_blocks
    u_blk0 = 3 * NA_WIDTH // LANES
    wspec = pl.BlockSpec((ppb, 2 * LANES, 4 * LANES), lambda i: (i, 0, 0))
    one = pl.Buffered(1)
    state = pltpu.VMEM((PAIRS_PER_STEP, 2, rows, LANES), F32)
    return pl.pallas_call(
        functools.partial(_s5_kernel, n_ctx_chunks=n_ctx // CHUNK),
        out_shape=jax.ShapeDtypeStruct((bsz, n_blocks, n_seq, LANES), F32),
        grid=(n_blocks,),
        in_specs=[
            pl.BlockSpec((bsz, n_ctx, LANES), lambda i: (0, 0, u_blk0 + i)),
            pl.BlockSpec((bsz, n_lat, LANES), lambda i: (0, 0, u_blk0 + i), pipeline_mode=one),
            wspec, wspec, wspec,
            pl.BlockSpec((ppb, SUBLANES, 2 * LANES), lambda i: (i, 0, 0)),
        ],
        out_specs=pl.BlockSpec((bsz, 1, n_seq, LANES), lambda i: (0, i, 0, 0), pipeline_mode=one),
        scratch_shapes=[pltpu.VMEM((ppb, 2, rows, LANES), F32), state, state, state],
        compiler_params=_cparams(("parallel",)),
        name="s5_scan",
    )(qkvu_ctx3, qkvu_lat3, w_c, m_c, v_c, a_c)


def _s5_matrices(lam_re, lam_im, log_dt, b_re, b_im, c_re, c_im):
    t = CHUNK
    g, p, hc = S5_GROUPS, S5_STATE, S5_CH
    lr = lam_re.astype(F32)
    li = lam_im.astype(F32)
    dt = jnp.exp(log_dt.astype(F32))[..., None]
    mag = jnp.exp(lr * dt)
    ab_re = mag * jnp.cos(li * dt)
    ab_im = mag * jnp.sin(li * dt)
    den = lr * lr + li * li
    nr = ab_re - 1.0
    z_re = (nr * lr + ab_im * li) / den
    z_im = (ab_im * lr - nr * li) / den
    br = b_re.astype(F32)
    bi = b_im.astype(F32)
    bb_re = z_re[..., None] * br - z_im[..., None] * bi
    bb_im = z_re[..., None] * bi + z_im[..., None] * br
    n = jnp.arange(t + 1, dtype=F32)[:, None, None, None]
    pmag = jnp.exp(n * (lr * dt)[None])
    pw_re = pmag * jnp.cos(n * (li * dt)[None])
    pw_im = pmag * jnp.sin(n * (li * dt)[None])
    cr = c_re.astype(F32)
    ci = c_im.astype(F32)
    ca_re = cr[None] * pw_re[:, :, :, None, :] - ci[None] * pw_im[:, :, :, None, :]
    ca_im = cr[None] * pw_im[:, :, :, None, :] + ci[None] * pw_re[:, :, :, None, :]
    kern = (jnp.einsum('ndgop,dgpi->ndgoi', ca_re[:t], bb_re, precision=HIGHEST)
            - jnp.einsum('ndgop,dgpi->ndgoi', ca_im[:t], bb_im, precision=HIGHEST))
    idx = jnp.arange(t)
    lag_f = jax.nn.one_hot(idx[None, :] - idx[:, None], t, dtype=F32)
    lag_r = jax.nn.one_hot(idx[:, None] - idx[None, :], t, dtype=F32)
    pos = jnp.arange(t)[None, :]
    step_at = SUBLANES * (pos // SUBLANES) + (pos % SUBLANES - jnp.arange(g)[:, None]) % SUBLANES
    perm = jax.nn.one_hot(step_at, t, dtype=F32)
    m = (jnp.einsum('stn,ngoi->gsito', lag_f, kern[:, 0], precision=HIGHEST)
         + jnp.einsum('stn,ngoi->gsito', lag_r, kern[:, 1], precision=HIGHEST))
    m = jnp.einsum('gas,gsito->gaito', perm, m, precision=HIGHEST)
    m = jnp.einsum('gbt,gaito->gaibo', perm, m, precision=HIGHEST)
    m = m.reshape(g, t * hc, t * hc)
    pf_re, pf_im = pw_re[:t, 0][::-1], pw_im[:t, 0][::-1]
    pr_re, pr_im = pw_re[:t, 1], pw_im[:t, 1]

    def state_w(p_re, p_im, d):
        w_re = p_re[..., None] * bb_re[d][None] - p_im[..., None] * bb_im[d][None]
        w_im = p_re[..., None] * bb_im[d][None] + p_im[..., None] * bb_re[d][None]
        to_cols = lambda w: jnp.einsum('gas,sgpi->gaip', perm, w, precision=HIGHEST).reshape(g, t * hc, p)
        return to_cols(w_re), to_cols(w_im)

    wf_re, wf_im = state_w(pf_re, pf_im, 0)
    wr_re, wr_im = state_w(pr_re, pr_im, 1)
    w = jnp.concatenate([wf_re, wr_re, wf_im, wr_im], axis=-1)
    to_rows = lambda c: jnp.einsum('gbt,tgop->gpbo', perm, c, precision=HIGHEST).reshape(g, p, t * hc)
    vf_re, vf_im = to_rows(ca_re[1:, 0]), to_rows(-ca_im[1:, 0])
    vr_re, vr_im = to_rows(ca_re[1:, 1][::-1]), to_rows(-ca_im[1:, 1][::-1])
    v = jnp.concatenate([vf_re, vr_re, vf_im, vr_im], axis=1)
    a16 = jnp.concatenate([pw_re[t, 0], pw_re[t, 1], pw_im[t, 0], pw_im[t, 1]], axis=-1)

    def pair_cols(x):
        x = x.reshape(g // 2, 2, t * hc, t * hc)
        return jnp.concatenate([x[:, 0], x[:, 1]], axis=-1).astype(BF16)

    a_rows = jnp.repeat(a16.reshape(g // 2, 2, 1, 4 * p), SUBLANES // 2, axis=2).reshape(g // 2, SUBLANES, 4 * p)
    return pair_cols(w), pair_cols(m), pair_cols(v), a_rows


def _s5_mixer(qkvu_lat, qkvu_ctx, mats, bsz):
    n_cols = qkvu_lat.shape[1]
    w_c, m_c, v_c, a_c = mats
    return _s5_scan(qkvu_ctx.reshape(bsz, -1, n_cols), qkvu_lat.reshape(bsz, -1, n_cols), w_c, m_c, v_c, a_c)


def _outproj_kernel(na_ref, y_ref, u_ref, h_ref, mod_ref, d_ref, wglu_ref, bglu_ref, wout_ref, g2_ref,
                    ho_ref, f_ref):
    tile_chunks = TM_PROJ // CHUNK
    y = jnp.concatenate(
        [jnp.concatenate([y_ref[0, blk, pl.ds(c, CHUNK, stride=tile_chunks), :] for c in range(tile_chunks)], axis=0)
         for blk in range(S5_WIDTH // LANES)], axis=1)
    z = jax.nn.gelu(y + d_ref[...] * u_ref[...])
    s5 = z * jax.nn.sigmoid(_dot(z.astype(BF16), wglu_ref[...]) + bglu_ref[...])
    mix = (_dot(na_ref[...].astype(BF16), wout_ref[0:NA_WIDTH, :])
           + _dot(s5.astype(BF16), wout_ref[NA_WIDTH:NA_WIDTH + S5_WIDTH, :]))
    d = D_MODEL
    gate = mod_ref[0, :, 2 * d:3 * d]
    h = h_ref[...] + gate * mix
    ho_ref[...] = h
    ms = jnp.mean(h * h, axis=-1, keepdims=True)
    y2 = h * lax.rsqrt(ms + EPS) * g2_ref[...]
    f_ref[...] = y2 * (1.0 + mod_ref[0, :, 4 * d:5 * d]) + mod_ref[0, :, 3 * d:4 * d]


def _out_projection(na, y_all, qkvu, h2d, mod3, d_skip, wglu_bf16, b_glu, wout_bf16, g2,
                    mod_map, rows_per_batch, y_block0):
    r, d = h2d.shape
    tiles_per_batch = rows_per_batch // TM_PROJ
    u_blk = 3 * NA_WIDTH // S5_WIDTH

    def y_map(i):
        return (i // tiles_per_batch, 0, y_block0 + i % tiles_per_batch, 0)

    const = lambda i: (0, 0)
    return pl.pallas_call(
        _outproj_kernel,
        out_shape=(jax.ShapeDtypeStruct((r, d), F32), jax.ShapeDtypeStruct((r, d), F32)),
        grid=(r // TM_PROJ,),
        in_specs=[
            pl.BlockSpec((TM_PROJ, NA_WIDTH), lambda i: (i, 0)),
            pl.BlockSpec((1, S5_WIDTH // LANES, TM_PROJ, LANES), y_map),
            pl.BlockSpec((TM_PROJ, S5_WIDTH), lambda i: (i, u_blk)),
            pl.BlockSpec((TM_PROJ, d), lambda i: (i, 0)),
            pl.BlockSpec((1, 1, N_MOD * d), mod_map),
            pl.BlockSpec((1, S5_WIDTH), const),
            pl.BlockSpec((S5_WIDTH, S5_WIDTH), const),
            pl.BlockSpec((1, S5_WIDTH), const),
            pl.BlockSpec((NA_WIDTH + S5_WIDTH, d), const),
            pl.BlockSpec((1, d), const),
        ],
        out_specs=(pl.BlockSpec((TM_PROJ, d), lambda i: (i, 0)), pl.BlockSpec((TM_PROJ, d), lambda i: (i, 0))),
        compiler_params=_cparams(("parallel",)),
        name="out_projection",
    )(na, y_all, qkvu, h2d, mod3, d_skip.reshape(1, -1), wglu_bf16, b_glu.reshape(1, -1), wout_bf16,
      g2.reshape(1, d))


def _top2(vals):
    best = vals[0]
    bi = jnp.zeros(best.shape, jnp.int32)
    for i in range(1, len(vals)):
        gt = vals[i] > best
        best = jnp.where(gt, vals[i], best)
        bi = jnp.where(gt, i, bi)
    second = jnp.full(best.shape, -jnp.inf, F32)
    si = jnp.zeros(best.shape, jnp.int32)
    for i in range(len(vals)):
        cand = jnp.where(bi == i, -jnp.inf, vals[i])
        gt = cand > second
        second = jnp.where(gt, cand, second)
        si = jnp.where(gt, i, si)
    return best, bi, second, si


def _router_kernel(f_ref, rwt_ref, rb_ref, idx_ref, gate_ref):
    logits = lax.dot_general(rwt_ref[...], f_ref[...], (((1,), (1,)), ((), ())),
                             precision=HIGHEST, preferred_element_type=F32)
    m = jnp.max(logits, axis=0, keepdims=True)
    e = jnp.exp(logits - m)
    probs = e / jnp.sum(e, axis=0, keepdims=True)
    sel = probs + rb_ref[...]
    sel_rows = [sel[i:i + 1, :] for i in range(N_EXPERTS)]
    prob_rows = [probs[i:i + 1, :] for i in range(N_EXPERTS)]
    scores = []
    for g in range(N_GROUPS):
        b, _, s, _ = _top2(sel_rows[g * EPG:(g + 1) * EPG])
        scores.append(b + s)
    grp = jnp.zeros(scores[0].shape, jnp.int32)
    gbest = scores[0]
    for g in range(1, N_GROUPS):
        gt = scores[g] > gbest
        gbest = jnp.where(gt, scores[g], gbest)
        grp = jnp.where(gt, g, grp)
    in_rows = []
    for j in range(EPG):
        v = sel_rows[j]
        for g in range(1, N_GROUPS):
            v = jnp.where(grp == g, sel_rows[g * EPG + j], v)
        in_rows.append(v)
    _, l1, _, l2 = _top2(in_rows)
    i1 = grp * EPG + l1
    i2 = grp * EPG + l2
    w1 = jnp.zeros(gbest.shape, F32)
    w2 = jnp.zeros(gbest.shape, F32)
    for i in range(N_EXPERTS):
        w1 = jnp.where(i1 == i, prob_rows[i], w1)
        w2 = jnp.where(i2 == i, prob_rows[i], w2)
    tot = w1 + w2
    idx_ref[0:1, :] = i1
    idx_ref[1:2, :] = i2
    gate_ref[0:1, :] = w1 / tot
    gate_ref[1:2, :] = w2 / tot


def _router(f_all, router_w, router_bias):
    n, d = f_all.shape
    return pl.pallas_call(
        _router_kernel,
        out_shape=(jax.ShapeDtypeStruct((2, n), jnp.int32), jax.ShapeDtypeStruct((2, n), F32)),
        grid=(n // TM_ROUTE,),
        in_specs=[
            pl.BlockSpec((TM_ROUTE, d), lambda i: (i, 0)),
            pl.BlockSpec((N_EXPERTS, d), lambda i: (0, 0)),
            pl.BlockSpec((N_EXPERTS, 1), lambda i: (0, 0)),
        ],
        out_specs=(pl.BlockSpec((2, TM_ROUTE), lambda i: (0, i)), pl.BlockSpec((2, TM_ROUTE), lambda i: (0, i))),
        compiler_params=_cparams(("parallel",)),
        name="router",
    )(f_all, router_w.T.astype(F32), router_bias.reshape(N_EXPERTS, 1).astype(F32))


def _scatter_rows_kernel(dest_ref, pad_ref, end_ref, *refs, n_tok, seg_tiles):
    f_refs = refs[:len(seg_tiles)]
    xs_ref, zero_scr, sem = refs[len(seg_tiles):]
    i = pl.program_id(0)
    tm = f_refs[0].shape[0]
    n_rows = xs_ref.shape[0]
    slab = TM_EXP + SUBLANES

    def slab_copy(start, rows):
        return pltpu.make_async_copy(zero_scr.at[pl.ds(0, rows), :], xs_ref.at[pl.ds(start, rows), :], sem)

    @pl.when(i == 0)
    def _():
        zero_scr[...] = jnp.zeros(zero_scr.shape, zero_scr.dtype)
        for e in range(N_EXPERTS):
            start = jnp.minimum((pad_ref[e] // SUBLANES) * SUBLANES, n_rows - slab)
            slab_copy(pl.multiple_of(start, SUBLANES), slab).start()
        for e in range(N_EXPERTS):
            slab_copy(0, slab).wait()
        for k in range(N_EXPERTS):
            start = end_ref[0] + k * TM_EXP

            @pl.when(start < n_rows)
            def _():
                cp = slab_copy(pl.multiple_of(start, TM_EXP), TM_EXP)
                cp.start()
                cp.wait()

    def scatter_tile(f_ref):
        base = i * tm

        def row_copy(r, d):
            return pltpu.make_async_copy(f_ref.at[pl.ds(r, 1), :], xs_ref.at[pl.ds(d, 1), :], sem)

        def body(r, carry):
            row_copy(r, dest_ref[base + r]).start()
            row_copy(r, dest_ref[n_tok + base + r]).start()
            return carry
        lax.fori_loop(0, tm, body, 0, unroll=8)
        for _ in range(2):
            pltpu.make_async_copy(f_ref, xs_ref.at[pl.ds(0, tm), :], sem).wait()

    tile0 = 0
    for f_ref, n_t in zip(f_refs, seg_tiles):
        pl.when((i >= tile0) & (i < tile0 + n_t))(functools.partial(scatter_tile, f_ref))
        tile0 += n_t


def _scatter_rows(segments, dest_flat, pad_start, total_end):
    d = segments[0].shape[1]
    seg_tiles = tuple(s.shape[0] // TM_PROJ for s in segments)
    n_tok = sum(s.shape[0] for s in segments)
    r_max = 2 * n_tok + N_EXPERTS * TM_EXP
    in_specs = []
    tile0 = 0
    for n_t in seg_tiles:
        in_specs.append(pl.BlockSpec(
            (TM_PROJ, d), lambda i, *_, t0=tile0, nt=n_t: (jnp.clip(i - t0, 0, nt - 1), 0)))
        tile0 += n_t
    grid_spec = pltpu.PrefetchScalarGridSpec(
        num_scalar_prefetch=3,
        grid=(tile0,),
        in_specs=in_specs,
        out_specs=pl.BlockSpec(memory_space=pl.ANY),
        scratch_shapes=[pltpu.VMEM((TM_EXP + SUBLANES, d), F32), pltpu.SemaphoreType.DMA(())],
    )
    return pl.pallas_call(
        functools.partial(_scatter_rows_kernel, n_tok=n_tok, seg_tiles=seg_tiles),
        out_shape=jax.ShapeDtypeStruct((r_max, d), F32),
        grid_spec=grid_spec,
        compiler_params=_cparams(("arbitrary",)),
        name="moe_scatter_rows",
    )(dest_flat, pad_start, total_end, *segments)


def _experts_kernel(te_ref, nv_ref, x_ref, wg_ref, wu_ref, wd_ref, o_ref, wg_scr, wu_scr, wd_scr):
    i = pl.program_id(0)
    e = te_ref[i]
    prev = te_ref[jnp.maximum(i - 1, 0)]
    rows = 128

    @pl.when((i == 0) | (e != prev))
    def _():
        def body(r, carry):
            sl = pl.ds(pl.multiple_of(r * rows, rows), rows)
            wg_scr[sl, :] = wg_ref[0, sl, :].astype(BF16)
            wu_scr[sl, :] = wu_ref[0, sl, :].astype(BF16)
            wd_scr[sl, :] = wd_ref[0, sl, :].astype(BF16)
            return carry
        lax.fori_loop(0, wg_scr.shape[0] // rows, body, 0)

    @pl.when(i < nv_ref[0])
    def _():
        x = x_ref[...].astype(BF16)
        g = _dot(x, wg_scr[...])
        u = _dot(x, wu_scr[...])
        a = (g * jax.nn.sigmoid(g)) * u
        o_ref[...] = _dot(a.astype(BF16), wd_scr[...]).astype(BF16)

    @pl.when(i >= nv_ref[0])
    def _():
        o_ref[...] = jnp.zeros(o_ref.shape, BF16)


def _experts(xs, tile_expert, n_valid, w_gate, w_up, w_down):
    r, d = xs.shape
    de = w_gate.shape[2]
    n_tiles = r // TM_EXP
    x_map = lambda i, te, nv: (jnp.minimum(i, nv[0] - 1), 0)
    grid_spec = pltpu.PrefetchScalarGridSpec(
        num_scalar_prefetch=2,
        grid=(n_tiles,),
        in_specs=[
            pl.BlockSpec((TM_EXP, d), x_map),
            pl.BlockSpec((1, d, de), lambda i, te, nv: (te[i], 0, 0)),
            pl.BlockSpec((1, d, de), lambda i, te, nv: (te[i], 0, 0)),
            pl.BlockSpec((1, de, d), lambda i, te, nv: (te[i], 0, 0)),
        ],
        out_specs=pl.BlockSpec((TM_EXP, d), lambda i, te, nv: (i, 0)),
        scratch_shapes=[pltpu.VMEM((d, de), BF16), pltpu.VMEM((d, de), BF16), pltpu.VMEM((de, d), BF16)],
    )
    return pl.pallas_call(
        _experts_kernel,
        out_shape=jax.ShapeDtypeStruct((r, d), BF16),
        grid_spec=grid_spec,
        compiler_params=_cparams(("arbitrary",)),
        name="experts",
    )(tile_expert, n_valid, xs, w_gate, w_up, w_down)


def _dispatch(idx):
    n = idx.shape[1]
    e_flat = idx.reshape(-1)
    onehot = (e_flat[:, None] == jnp.arange(N_EXPERTS, dtype=jnp.int32)[None, :]).astype(jnp.int32)
    csum = jnp.cumsum(onehot, axis=0)
    rank = jnp.sum(csum * onehot, axis=1) - 1
    counts = csum[-1]
    padded = ((counts + TM_EXP - 1) // TM_EXP) * TM_EXP
    ends = jnp.cumsum(padded)
    starts = ends - padded
    dest = (jnp.sum(onehot * starts[None, :], axis=1) + rank).astype(jnp.int32)
    r_max = 2 * n + N_EXPERTS * TM_EXP
    tile_start = jnp.arange(r_max // TM_EXP, dtype=jnp.int32) * TM_EXP
    tile_expert = jnp.minimum(jnp.sum((tile_start[:, None] >= ends[None, :]).astype(jnp.int32), axis=1),
                              N_EXPERTS - 1).astype(jnp.int32)
    n_valid = (ends[-1] // TM_EXP).astype(jnp.int32).reshape(1)
    pad_start = (starts + counts).astype(jnp.int32)
    total_end = ends[-1].astype(jnp.int32).reshape(1)
    return dest, pad_start, total_end, tile_expert, n_valid


def _combine_kernel(h_ref, y1_ref, y2_ref, gate_ref, mod_ref, o_ref):
    d = D_MODEL
    g = gate_ref[...]
    y = g[:, 0:1] * y1_ref[...].astype(F32) + g[:, 1:2] * y2_ref[...].astype(F32)
    o_ref[...] = h_ref[...] + mod_ref[0, :, 5 * d:6 * d] * y


def _combine(h2d, y1, y2, gates_t, mod3, mod_map, row0):
    r, d = h2d.shape
    blk0 = row0 // TM_PROJ
    row = lambda i: (i, 0)
    seg = lambda i: (blk0 + i, 0)
    return pl.pallas_call(
        _combine_kernel,
        out_shape=jax.ShapeDtypeStruct((r, d), F32),
        grid=(r // TM_PROJ,),
        in_specs=[
            pl.BlockSpec((TM_PROJ, d), row),
            pl.BlockSpec((TM_PROJ, d), seg),
            pl.BlockSpec((TM_PROJ, d), seg),
            pl.BlockSpec((TM_PROJ, 2), seg),
            pl.BlockSpec((1, 1, N_MOD * d), mod_map),
        ],
        out_specs=pl.BlockSpec((TM_PROJ, d), row),
        compiler_params=_cparams(("parallel",)),
        name="moe_combine",
    )(h2d, y1, y2, gates_t, mod3)


def kernel(x, c, ctx, c_ctx, w_mod, b_mod, norm1_g, norm2_g, w_in, w_out, q_norm_g, k_norm_g, na_rpb,
           s5_lam_re, s5_lam_im, s5_log_dt, s5_b_re, s5_b_im, s5_c_re, s5_c_im, s5_d, s5_w_glu, s5_b_glu,
           router_w, router_bias, moe_w_gate, moe_w_up, moe_w_down):
    bsz, n_lat, d = x.shape
    n_ctx = ctx.shape[1]
    depth = w_mod.shape[0]
    ctx_row = bsz
    c_rows = jnp.concatenate([c.astype(F32), c_ctx.astype(F32)[None],
                              jnp.zeros((SUBLANES - bsz - 1, d), F32)], axis=0)
    mod_all = _modulation(c_rows, w_mod.astype(F32), b_mod.astype(F32))

    h_lat = x.reshape(bsz * n_lat, d).astype(F32)
    h_ctx = ctx.reshape(bsz * n_ctx, d).astype(F32)
    lat_map = _mod_row_map(n_lat, 0, True)
    ctx_map = _mod_row_map(n_ctx, ctx_row, False)

    for layer in range(depth):
        ctx_out = layer < depth - 1
        mod3 = mod_all[layer].reshape(SUBLANES, 1, N_MOD * d)
        w_in_b = w_in[layer].astype(BF16)
        qkvu_lat = _in_projection(h_lat, norm1_g[layer], mod3, w_in_b, lat_map)
        qkvu_ctx = _in_projection(h_ctx, norm1_g[layer], mod3, w_in_b, ctx_map)
        bias_tab = _na_bias_table(na_rpb[layer])
        na_lat = _neighborhood_attention(qkvu_lat, qkvu_ctx, bias_tab, q_norm_g[layer], k_norm_g[layer], bsz)
        mats = _s5_matrices(s5_lam_re[layer], s5_lam_im[layer], s5_log_dt[layer], s5_b_re[layer],
                            s5_b_im[layer], s5_c_re[layer], s5_c_im[layer])
        y_all = _s5_mixer(qkvu_lat, qkvu_ctx, mats, bsz)
        wglu_b = s5_w_glu[layer].astype(BF16)
        wout_b = w_out[layer].astype(BF16)
        h_lat, f_lat = _out_projection(na_lat, y_all, qkvu_lat, h_lat, mod3, s5_d[layer], wglu_b,
                                       s5_b_glu[layer], wout_b, norm2_g[layer],
                                       lat_map, n_lat, n_ctx // TM_PROJ)
        if ctx_out:
            na_ctx = _context_attention(qkvu_ctx, q_norm_g[layer], k_norm_g[layer], bsz)
            h_ctx, f_ctx = _out_projection(na_ctx, y_all, qkvu_ctx, h_ctx, mod3, s5_d[layer], wglu_b,
                                           s5_b_glu[layer], wout_b, norm2_g[layer],
                                           ctx_map, n_ctx, 0)
        n_l = bsz * n_lat
        idx, gates = _router(f_lat, router_w, router_bias)
        if ctx_out:
            idx_c, gates_c = _router(f_ctx, router_w, router_bias)
            idx = jnp.concatenate([idx, idx_c], axis=1)
            gates = jnp.concatenate([gates, gates_c], axis=1)
        n_tok = idx.shape[1]
        dest, pad_start, total_end, tile_expert, n_valid = _dispatch(idx)
        xs = _scatter_rows([f_lat, f_ctx] if ctx_out else [f_lat], dest, pad_start, total_end)
        ys = _experts(xs, tile_expert, n_valid, moe_w_gate[layer], moe_w_up[layer], moe_w_down[layer])
        y1 = jnp.take(ys, dest[:n_tok], axis=0)
        y2 = jnp.take(ys, dest[n_tok:], axis=0)
        gates_t = gates.T
        h_lat = _combine(h_lat, y1, y2, gates_t, mod3, lat_map, 0)
        if ctx_out:
            h_ctx = _combine(h_ctx, y1, y2, gates_t, mod3, ctx_map, n_l)
    return h_lat.reshape(bsz, n_lat, d).astype(x.dtype)
```

```python
import functools
import math

import jax
import jax.numpy as jnp
from jax import lax
from jax.experimental import pallas as pl
from jax.experimental.pallas import tpu as pltpu

F32 = jnp.float32
BF16 = jnp.bfloat16
HIGHEST = lax.Precision.HIGHEST

D_MODEL = 1024
GRID_W = 64
HEAD_DIM = 64
NA_WIDTH = 512
S5_WIDTH = 512
S5_CH = 16
S5_GROUPS = 32
S5_STATE = 64
WIN_ROWS = 8
WIN_COLS = 16
N_EXPERTS = 16
N_GROUPS = 4
EPG = 4
N_MOD = 6
EPS = 1e-6

LANES = 128
SUBLANES = 8
VMEM_LIMIT = 56 * 1024 * 1024

TM_PROJ = 256
Q_ROWS = 8
Q_COLS = 16
K_ROWS = 16
K_COLS = 32
CHUNK = 16
PAIRS_PER_STEP = 2
TM_EXP = 256
TM_ROUTE = 512
MASK_VALUE = -1e30


def _cparams(sem):
    return pltpu.CompilerParams(dimension_semantics=sem, vmem_limit_bytes=VMEM_LIMIT)


def _dot(a, b):
    return jnp.dot(a, b, preferred_element_type=F32)


def _dot_nt(a, b):
    return lax.dot_general(a, b, (((1,), (1,)), ((), ())), preferred_element_type=F32)


def _mod_kernel(c_ref, w_ref, b_ref, o_ref):
    a = c_ref[...]
    a = a * jax.nn.sigmoid(a)
    o_ref[0] = jnp.dot(a, w_ref[0], precision=HIGHEST, preferred_element_type=F32) + b_ref[0]


def _modulation(c_rows, w_mod, b_mod):
    depth, d, n = w_mod.shape
    tn = 1536
    return pl.pallas_call(
        _mod_kernel,
        out_shape=jax.ShapeDtypeStruct((depth, SUBLANES, n), F32),
        grid=(depth, n // tn),
        in_specs=[
            pl.BlockSpec((SUBLANES, d), lambda l, j: (0, 0)),
            pl.BlockSpec((1, d, tn), lambda l, j: (l, 0, j)),
            pl.BlockSpec((1, 1, tn), lambda l, j: (l, 0, j)),
        ],
        out_specs=pl.BlockSpec((1, SUBLANES, tn), lambda l, j: (l, 0, j)),
        compiler_params=_cparams(("arbitrary", "arbitrary")),
        name="modulation",
    )(c_rows, w_mod, b_mod.reshape(depth, 1, n))


def _inproj_kernel(x_ref, g_ref, mod_ref, w_ref, o_ref):
    x = x_ref[...]
    ms = jnp.mean(x * x, axis=-1, keepdims=True)
    y = x * lax.rsqrt(ms + EPS) * g_ref[...]
    shift = mod_ref[0, :, 0:D_MODEL]
    scale = mod_ref[0, :, D_MODEL:2 * D_MODEL]
    a = y * (1.0 + scale) + shift
    o_ref[...] = _dot(a.astype(BF16), w_ref[...])


def _mod_row_map(rows_per_batch, mod_row0, per_batch):
    tiles_per_batch = rows_per_batch // TM_PROJ
    if per_batch:
        return lambda i: (mod_row0 + i // tiles_per_batch, 0, 0)
    return lambda i: (mod_row0, 0, 0)


def _in_projection(x2d, g, mod3, w_bf16, mod_map):
    r, d = x2d.shape
    n = w_bf16.shape[1]
    return pl.pallas_call(
        _inproj_kernel,
        out_shape=jax.ShapeDtypeStruct((r, n), F32),
        grid=(r // TM_PROJ,),
        in_specs=[
            pl.BlockSpec((TM_PROJ, d), lambda i: (i, 0)),
            pl.BlockSpec((1, d), lambda i: (0, 0)),
            pl.BlockSpec((1, 1, N_MOD * d), mod_map),
            pl.BlockSpec((d, n), lambda i: (0, 0)),
        ],
        out_specs=pl.BlockSpec((TM_PROJ, n), lambda i: (i, 0)),
        compiler_params=_cparams(("parallel",)),
        name="in_projection",
    )(x2d, g.reshape(1, d), mod3, w_bf16)


def _pair_rms(x, g, lo):
    ss = x * x
    sa = jnp.sum(jnp.where(lo, ss, 0.0), axis=-1, keepdims=True)
    sb = jnp.sum(jnp.where(lo, 0.0, ss), axis=-1, keepdims=True)
    ms = jnp.where(lo, sa, sb) * (1.0 / HEAD_DIM)
    return x * lax.rsqrt(ms + EPS) * g


def _softmax_pv(qm, kw, kcb, vw, vcb, bias):
    s_nb = _dot_nt(qm, kw) + bias
    s_cx = _dot_nt(qm, kcb)
    m = jnp.maximum(jnp.max(s_nb, axis=-1, keepdims=True), jnp.max(s_cx, axis=-1, keepdims=True))
    p_nb = jnp.exp(s_nb - m)
    p_cx = jnp.exp(s_cx - m)
    l = jnp.sum(p_nb, axis=-1, keepdims=True) + jnp.sum(p_cx, axis=-1, keepdims=True)
    o = _dot(p_nb.astype(BF16), vw) + _dot(p_cx.astype(BF16), vcb)
    return o / l


def _na_kernel(q_ref, k_ref, v_ref, kc_ref, vc_ref, bias_ref, qg_ref, kg_ref, o_ref, kn_scr, kcn_scr):
    t = pl.program_id(2)
    lo = lax.broadcasted_iota(jnp.int32, (1, LANES), 1) < HEAD_DIM
    n_rows = k_ref.shape[1]
    col_tiles = GRID_W // Q_COLS

    @pl.when(t == 0)
    def _():
        def body(r, carry):
            kn_scr[r] = _pair_rms(k_ref[0, r], kg_ref[...], lo)
            return carry
        lax.fori_loop(0, n_rows, body, 0)
        kcn_scr[...] = _pair_rms(kc_ref[0], kg_ref[...], lo).astype(BF16)

    i = t
    kr0 = jnp.clip(Q_ROWS * i - WIN_ROWS // 2, 0, n_rows - K_ROWS)
    rt = jnp.where(i == 0, 0, jnp.where(i == n_rows // Q_ROWS - 1, 2, 1))
    nq = Q_ROWS * Q_COLS
    nk = K_ROWS * K_COLS
    kcb = kcn_scr[...]
    vcb = vc_ref[0].astype(BF16)
    for j in range(col_tiles):
        kc0 = min(max(Q_COLS * j - WIN_COLS // 2, 0), GRID_W - K_COLS)
        ct = 0 if j == 0 else (2 if j == col_tiles - 1 else 1)
        typ = rt * 3 + ct
        cols = slice(j * Q_COLS, (j + 1) * Q_COLS)
        q = q_ref[0, :, cols, :].reshape(nq, LANES)
        qn = _pair_rms(q, qg_ref[...], lo) * (HEAD_DIM ** -0.5)
        kw = kn_scr[pl.ds(kr0, K_ROWS), kc0:kc0 + K_COLS, :].reshape(nk, LANES).astype(BF16)
        vw = v_ref[0, pl.ds(kr0, K_ROWS), kc0:kc0 + K_COLS, :].reshape(nk, LANES).astype(BF16)
        o_a = _softmax_pv(jnp.where(lo, qn, 0.0).astype(BF16), kw, kcb, vw, vcb, bias_ref[0, typ, 0])
        o_b = _softmax_pv(jnp.where(lo, 0.0, qn).astype(BF16), kw, kcb, vw, vcb, bias_ref[0, typ, 1])
        o_ref[0, :, cols, :] = jnp.where(lo, o_a, o_b).reshape(Q_ROWS, Q_COLS, LANES)


def _na_bias_table(rpb):
    h = rpb.shape[0]
    rpb = rpb.astype(F32)

    def axis_tables(q_n, k_n, win, d_opts, origin_opts, extent):
        onehots, valids = [], []
        for d_rel, origin in zip(d_opts, origin_opts):
            qa = origin + jnp.arange(q_n)[:, None]
            ka = origin + d_rel + jnp.arange(k_n)[None, :]
            start = jnp.clip(qa - win // 2, 0, extent - win)
            valid = (ka >= start) & (ka < start + win)
            off = jnp.clip(ka - qa + win - 1, 0, 2 * win - 2)
            onehots.append(jax.nn.one_hot(off, 2 * win - 1, dtype=F32))
            valids.append(valid)
        return jnp.stack(onehots), jnp.stack(valids)

    rows = GRID_W
    oh_r, va_r = axis_tables(Q_ROWS, K_ROWS, WIN_ROWS, (0, -4, -8), (0, Q_ROWS, rows - Q_ROWS), rows)
    oh_c, va_c = axis_tables(Q_COLS, K_COLS, WIN_COLS, (0, -8, -16), (0, Q_COLS, GRID_W - Q_COLS), GRID_W)
    t1 = jnp.einsum('raku,huv->hrakv', oh_r, rpb, precision=HIGHEST)
    bias = jnp.einsum('hrakv,cbjv->hrcabkj', t1, oh_c, precision=HIGHEST)
    valid = va_r[:, None, :, None, :, None] & va_c[None, :, None, :, None, :]
    bias = jnp.where(valid[None], bias, MASK_VALUE)
    bias = bias.reshape(h // 2, 2, 9, Q_ROWS * Q_COLS, K_ROWS * K_COLS)
    return jnp.transpose(bias, (0, 2, 1, 3, 4))


def _neighborhood_attention(qkvu_lat, qkvu_ctx, bias_tab, qg, kg, bsz):
    n_lat = qkvu_lat.shape[0] // bsz
    n_ctx = qkvu_ctx.shape[0] // bsz
    rows = n_lat // GRID_W
    n_cols = qkvu_lat.shape[1]
    lat4 = qkvu_lat.reshape(bsz, rows, GRID_W, n_cols)
    ctx3 = qkvu_ctx.reshape(bsz, n_ctx, n_cols)
    n_pairs = NA_WIDTH // LANES
    n_tiles = rows // Q_ROWS
    g2 = lambda g: jnp.concatenate([g, g]).reshape(1, LANES).astype(F32)
    out = pl.pallas_call(
        _na_kernel,
        out_shape=jax.ShapeDtypeStruct((bsz, rows, GRID_W, NA_WIDTH), F32),
        grid=(bsz, n_pairs, n_tiles),
        in_specs=[
            pl.BlockSpec((1, Q_ROWS, GRID_W, LANES), lambda b, p, t: (b, t, 0, p)),
            pl.BlockSpec((1, rows, GRID_W, LANES), lambda b, p, t: (b, 0, 0, n_pairs + p)),
            pl.BlockSpec((1, rows, GRID_W, LANES), lambda b, p, t: (b, 0, 0, 2 * n_pairs + p)),
            pl.BlockSpec((1, n_ctx, LANES), lambda b, p, t: (b, 0, n_pairs + p)),
            pl.BlockSpec((1, n_ctx, LANES), lambda b, p, t: (b, 0, 2 * n_pairs + p)),
            pl.BlockSpec((1, 9, 2, Q_ROWS * Q_COLS, K_ROWS * K_COLS), lambda b, p, t: (p, 0, 0, 0, 0)),
            pl.BlockSpec((1, LANES), lambda b, p, t: (0, 0)),
            pl.BlockSpec((1, LANES), lambda b, p, t: (0, 0)),
        ],
        out_specs=pl.BlockSpec((1, Q_ROWS, GRID_W, LANES), lambda b, p, t: (b, t, 0, p)),
        scratch_shapes=[pltpu.VMEM((rows, GRID_W, LANES), F32), pltpu.VMEM((n_ctx, LANES), BF16)],
        compiler_params=_cparams(("parallel", "parallel", "arbitrary")),
        name="neighborhood_attention",
    )(lat4, lat4, lat4, ctx3, ctx3, bias_tab, g2(qg), g2(kg))
    return out.reshape(bsz * n_lat, NA_WIDTH)


def _ctx_attn_kernel(q_ref, k_ref, v_ref, qg_ref, kg_ref, o_ref):
    lo = lax.broadcasted_iota(jnp.int32, (1, LANES), 1) < HEAD_DIM
    qn = _pair_rms(q_ref[0], qg_ref[...], lo) * (HEAD_DIM ** -0.5)
    kn = _pair_rms(k_ref[0], kg_ref[...], lo).astype(BF16)
    vb = v_ref[0].astype(BF16)

    def one(qm):
        s = _dot_nt(qm, kn)
        m = jnp.max(s, axis=-1, keepdims=True)
        p = jnp.exp(s - m)
        l = jnp.sum(p, axis=-1, keepdims=True)
        return _dot(p.astype(BF16), vb) / l

    o_a = one(jnp.where(lo, qn, 0.0).astype(BF16))
    o_b = one(jnp.where(lo, 0.0, qn).astype(BF16))
    o_ref[0] = jnp.where(lo, o_a, o_b)


def _context_attention(qkvu_ctx, qg, kg, bsz):
    n_ctx = qkvu_ctx.shape[0] // bsz
    ctx3 = qkvu_ctx.reshape(bsz, n_ctx, qkvu_ctx.shape[1])
    n_pairs = NA_WIDTH // LANES
    g2 = lambda g: jnp.concatenate([g, g]).reshape(1, LANES).astype(F32)
    out = pl.pallas_call(
        _ctx_attn_kernel,
        out_shape=jax.ShapeDtypeStruct((bsz, n_ctx, NA_WIDTH), F32),
        grid=(bsz, n_pairs),
        in_specs=[
            pl.BlockSpec((1, n_ctx, LANES), lambda b, p: (b, 0, p)),
            pl.BlockSpec((1, n_ctx, LANES), lambda b, p: (b, 0, n_pairs + p)),
            pl.BlockSpec((1, n_ctx, LANES), lambda b, p: (b, 0, 2 * n_pairs + p)),
            pl.BlockSpec((1, LANES), lambda b, p: (0, 0)),
            pl.BlockSpec((1, LANES), lambda b, p: (0, 0)),
        ],
        out_specs=pl.BlockSpec((1, n_ctx, LANES), lambda b, p: (b, 0, p)),
        compiler_params=_cparams(("parallel", "parallel")),
        name="context_attention",
    )(ctx3, ctx3, ctx3, g2(qg), g2(kg))
    return out.reshape(bsz * n_ctx, NA_WIDTH)


def _s5_kernel(uc_ref, ul_ref, w_ref, m_ref, v_ref, a_ref, y_ref, x_scr, s_scr, hf_scr, hr_scr, *, n_ctx_chunks):
    bsz = ul_ref.shape[0]
    n_lat_chunks = ul_ref.shape[1] // CHUNK
    n_chunks = n_ctx_chunks + n_lat_chunks
    rows = n_chunks * SUBLANES
    n_pairs = w_ref.shape[0]
    gpb = 2 * n_pairs
    half = 2 * LANES
    tile_chunks = TM_PROJ // CHUNK
    lane_blk = lax.broadcasted_iota(jnp.int32, (n_chunks, LANES), 1) // S5_CH

    for b in range(bsz):
        for q in range(2):
            rolled = []
            for j in range(SUBLANES):
                s = SUBLANES * q + j
                u_s = jnp.concatenate([uc_ref[b, pl.ds(s, n_ctx_chunks, stride=CHUNK), :],
                                       ul_ref[b, pl.ds(s, n_lat_chunks, stride=CHUNK), :]], axis=0)
                rolled.append(u_s if j == 0 else pltpu.roll(u_s, j * S5_CH, 1))
            for g in range(gpb):
                xg = rolled[0]
                for j in range(1, SUBLANES):
                    xg = jnp.where(lane_blk == (g + j) % SUBLANES, rolled[j], xg)
                x_scr[g // 2, q, pl.ds((g % 2) * bsz + b, n_chunks, stride=SUBLANES), :] = xg

    n_blk = 8
    rb = rows // n_blk
    first_group = (lax.broadcasted_iota(jnp.int32, (rb, half), 0) & (SUBLANES // 2)) == 0
    fwd_cols = (lax.broadcasted_iota(jnp.int32, (rb, half), 1) & (LANES - 1)) < S5_STATE
    is_fwd = lax.broadcasted_iota(jnp.int32, (SUBLANES, LANES), 1) < S5_STATE
    zero = jnp.zeros((SUBLANES, LANES), F32)

    def x_rows(p, sl):
        return jnp.concatenate([x_scr[p, 0, sl, :], x_scr[p, 1, sl, :]], axis=1).astype(BF16)

    def put_cols(scr, i, sl, val):
        scr[i, 0, sl, :] = val[:, 0:LANES]
        scr[i, 1, sl, :] = val[:, LANES:half]

    for hh in range(n_pairs // PAIRS_PER_STEP):
        pairs = [hh * PAIRS_PER_STEP + i for i in range(PAIRS_PER_STEP)]
        for i, p in enumerate(pairs):
            for blk in range(n_blk):
                sl = slice(blk * rb, (blk + 1) * rb)
                r = _dot(x_rows(p, sl), w_ref[p])
                put_cols(s_scr, i, sl, jnp.where(first_group, r[:, :half], r[:, half:]))

        a_re = [a_ref[p, :, 0:LANES] for p in pairs]
        a_im = [a_ref[p, :, LANES:half] for p in pairs]

        def body(k, carry):
            kr = jnp.where(k < n_ctx_chunks, n_ctx_chunks - 1 - k, n_chunks + n_ctx_chunks - 1 - k)
            rf = pl.ds(pl.multiple_of(k * SUBLANES, SUBLANES), SUBLANES)
            rr = pl.ds(pl.multiple_of(kr * SUBLANES, SUBLANES), SUBLANES)
            new = []
            for i in range(PAIRS_PER_STEP):
                h_re, h_im = carry[2 * i], carry[2 * i + 1]
                hf_scr[i, 0, rf, :] = h_re
                hf_scr[i, 1, rf, :] = h_im
                hr_scr[i, 0, rr, :] = h_re
                hr_scr[i, 1, rr, :] = h_im
                s_re = jnp.where(is_fwd, s_scr[i, 0, rf, :], s_scr[i, 0, rr, :])
                s_im = jnp.where(is_fwd, s_scr[i, 1, rf, :], s_scr[i, 1, rr, :])
                new.append(a_re[i] * h_re - a_im[i] * h_im + s_re)
                new.append(a_re[i] * h_im + a_im[i] * h_re + s_im)
            return tuple(new)

        lax.fori_loop(0, n_chunks, body, (zero,) * (2 * PAIRS_PER_STEP))

        for i, p in enumerate(pairs):
            for blk in range(n_blk):
                sl = slice(blk * rb, (blk + 1) * rb)
                hf = jnp.concatenate([hf_scr[i, 0, sl, :], hf_scr[i, 1, sl, :]], axis=1)
                hr = jnp.concatenate([hr_scr[i, 0, sl, :], hr_scr[i, 1, sl, :]], axis=1)
                h_in = jnp.where(fwd_cols, hf, hr).astype(BF16)
                r = _dot(x_rows(p, sl), m_ref[p]) + _dot(h_in, v_ref[p])
                put_cols(s_scr, i, sl, jnp.where(first_group, r[:, :half], r[:, half:]))

        g_lo = 2 * pairs[0]
        n_g = 2 * PAIRS_PER_STEP
        out_blk = lax.broadcasted_iota(jnp.int32, (tile_chunks, LANES), 1) // S5_CH
        lanes_out = (out_blk >= g_lo) & (out_blk < g_lo + n_g)
        for b in range(bsz):
            for q in range(2):
                y_g = [s_scr[(g - g_lo) // 2, q, pl.ds((g % 2) * bsz + b, n_chunks, stride=SUBLANES), :]
                       for g in range(g_lo, g_lo + n_g)]
                for j in range(SUBLANES):
                    z = y_g[0]
                    for gi in range(1, n_g):
                        z = jnp.where(lane_blk == (g_lo + gi + j) % SUBLANES, y_g[gi], z)
                    if j:
                        z = pltpu.roll(z, LANES - j * S5_CH, 1)
                    t = SUBLANES * q + j
                    for ct in range(n_chunks // tile_chunks):
                        r0 = ct * TM_PROJ + t * tile_chunks
                        pltpu.store(y_ref.at[b, 0, r0:r0 + tile_chunks, :],
                                    z[ct * tile_chunks:(ct + 1) * tile_chunks, :], mask=lanes_out)


def _s5_scan(qkvu_ctx3, qkvu_lat3, w_c, m_c, v_c, a_c):
    bsz, n_ctx, _ = qkvu_ctx3.shape
    n_lat = qkvu_lat3.shape[1]
    n_seq = n_ctx + n_lat
    rows = n_seq // CHUNK * SUBLANES
    n_blocks = S5_WIDTH // LANES
    ppb = w_c.shape[0] // n_blocks
    u_blk0 = 3 * NA_WIDTH // LANES
    wspec = pl.BlockSpec((ppb, 2 * LANES, 4 * LANES), lambda i: (i, 0, 0))
    one = pl.Buffered(1)
    state = pltpu.VMEM((PAIRS_PER_STEP, 2, rows, LANES), F32)
    return pl.pallas_call(
        functools.partial(_s5_kernel, n_ctx_chunks=n_ctx // CHUNK),
        out_shape=jax.ShapeDtypeStruct((bsz, n_blocks, n_seq, LANES), F32),
        grid=(n_blocks,),
        in_specs=[
            pl.BlockSpec((bsz, n_ctx, LANES), lambda i: (0, 0, u_blk0 + i)),
            pl.BlockSpec((bsz, n_lat, LANES), lambda i: (0, 0, u_blk0 + i), pipeline_mode=one),
            wspec, wspec, wspec,
            pl.BlockSpec((ppb, SUBLANES, 2 * LANES), lambda i: (i, 0, 0)),
        ],
        out_specs=pl.BlockSpec((bsz, 1, n_seq, LANES), lambda i: (0, i, 0, 0), pipeline_mode=one),
        scratch_shapes=[pltpu.VMEM((ppb, 2, rows, LANES), F32), state, state, state],
        compiler_params=_cparams(("parallel",)),
        name="s5_scan",
    )(qkvu_ctx3, qkvu_lat3, w_c, m_c, v_c, a_c)


def _s5_matrices(lam_re, lam_im, log_dt, b_re, b_im, c_re, c_im):
    t = CHUNK
    g, p, hc = S5_GROUPS, S5_STATE, S5_CH
    lr = lam_re.astype(F32)
    li = lam_im.astype(F32)
    dt = jnp.exp(log_dt.astype(F32))[..., None]
    mag = jnp.exp(lr * dt)
    ab_re = mag * jnp.cos(li * dt)
    ab_im = mag * jnp.sin(li * dt)
    den = lr * lr + li * li
    nr = ab_re - 1.0
    z_re = (nr * lr + ab_im * li) / den
    z_im = (ab_im * lr - nr * li) / den
    br = b_re.astype(F32)
    bi = b_im.astype(F32)
    bb_re = z_re[..., None] * br - z_im[..., None] * bi
    bb_im = z_re[..., None] * bi + z_im[..., None] * br
    n = jnp.arange(t + 1, dtype=F32)[:, None, None, None]
    pmag = jnp.exp(n * (lr * dt)[None])
    pw_re = pmag * jnp.cos(n * (li * dt)[None])
    pw_im = pmag * jnp.sin(n * (li * dt)[None])
    cr = c_re.astype(F32)
    ci = c_im.astype(F32)
    ca_re = cr[None] * pw_re[:, :, :, None, :] - ci[None] * pw_im[:, :, :, None, :]
    ca_im = cr[None] * pw_im[:, :, :, None, :] + ci[None] * pw_re[:, :, :, None, :]
    kern = (jnp.einsum('ndgop,dgpi->ndgoi', ca_re[:t], bb_re, precision=HIGHEST)
            - jnp.einsum('ndgop,dgpi->ndgoi', ca_im[:t], bb_im, precision=HIGHEST))
    idx = jnp.arange(t)
    lag_f = jax.nn.one_hot(idx[None, :] - idx[:, None], t, dtype=F32)
    lag_r = jax.nn.one_hot(idx[:, None] - idx[None, :], t, dtype=F32)
    pos = jnp.arange(t)[None, :]
    step_at = SUBLANES * (pos // SUBLANES) + (pos % SUBLANES - jnp.arange(g)[:, None]) % SUBLANES
    perm = jax.nn.one_hot(step_at, t, dtype=F32)
    m = (jnp.einsum('stn,ngoi->gsito', lag_f, kern[:, 0], precision=HIGHEST)
         + jnp.einsum('stn,ngoi->gsito', lag_r, kern[:, 1], precision=HIGHEST))
    m = jnp.einsum('gas,gsito->gaito', perm, m, precision=HIGHEST)
    m = jnp.einsum('gbt,gaito->gaibo', perm, m, precision=HIGHEST)
    m = m.reshape(g, t * hc, t * hc)
    pf_re, pf_im = pw_re[:t, 0][::-1], pw_im[:t, 0][::-1]
    pr_re, pr_im = pw_re[:t, 1], pw_im[:t, 1]

    def state_w(p_re, p_im, d):
        w_re = p_re[..., None] * bb_re[d][None] - p_im[..., None] * bb_im[d][None]
        w_im = p_re[..., None] * bb_im[d][None] + p_im[..., None] * bb_re[d][None]
        to_cols = lambda w: jnp.einsum('gas,sgpi->gaip', perm, w, precision=HIGHEST).reshape(g, t * hc, p)
        return to_cols(w_re), to_cols(w_im)

    wf_re, wf_im = state_w(pf_re, pf_im, 0)
    wr_re, wr_im = state_w(pr_re, pr_im, 1)
    w = jnp.concatenate([wf_re, wr_re, wf_im, wr_im], axis=-1)
    to_rows = lambda c: jnp.einsum('gbt,tgop->gpbo', perm, c, precision=HIGHEST).reshape(g, p, t * hc)
    vf_re, vf_im = to_rows(ca_re[1:, 0]), to_rows(-ca_im[1:, 0])
    vr_re, vr_im = to_rows(ca_re[1:, 1][::-1]), to_rows(-ca_im[1:, 1][::-1])
    v = jnp.concatenate([vf_re, vr_re, vf_im, vr_im], axis=1)
    a16 = jnp.concatenate([pw_re[t, 0], pw_re[t, 1], pw_im[t, 0], pw_im[t, 1]], axis=-1)

    def pair_cols(x):
        x = x.reshape(g // 2, 2, t * hc, t * hc)
        return jnp.concatenate([x[:, 0], x[:, 1]], axis=-1).astype(BF16)

    a_rows = jnp.repeat(a16.reshape(g // 2, 2, 1, 4 * p), SUBLANES // 2, axis=2).reshape(g // 2, SUBLANES, 4 * p)
    return pair_cols(w), pair_cols(m), pair_cols(v), a_rows


def _s5_mixer(qkvu_lat, qkvu_ctx, mats, bsz):
    n_cols = qkvu_lat.shape[1]
    w_c, m_c, v_c, a_c = mats
    return _s5_scan(qkvu_ctx.reshape(bsz, -1, n_cols), qkvu_lat.reshape(bsz, -1, n_cols), w_c, m_c, v_c, a_c)


def _outproj_kernel(na_ref, y_ref, u_ref, h_ref, mod_ref, d_ref, wglu_ref, bglu_ref, wout_ref, g2_ref,
                    ho_ref, f_ref):
    tile_chunks = TM_PROJ // CHUNK
    y = jnp.concatenate(
        [jnp.concatenate([y_ref[0, blk, pl.ds(c, CHUNK, stride=tile_chunks), :] for c in range(tile_chunks)], axis=0)
         for blk in range(S5_WIDTH // LANES)], axis=1)
    z = jax.nn.gelu(y + d_ref[...] * u_ref[...])
    s5 = z * jax.nn.sigmoid(_dot(z.astype(BF16), wglu_ref[...]) + bglu_ref[...])
    mix = (_dot(na_ref[...].astype(BF16), wout_ref[0:NA_WIDTH, :])
           + _dot(s5.astype(BF16), wout_ref[NA_WIDTH:NA_WIDTH + S5_WIDTH, :]))
    d = D_MODEL
    gate = mod_ref[0, :, 2 * d:3 * d]
    h = h_ref[...] + gate * mix
    ho_ref[...] = h
    ms = jnp.mean(h * h, axis=-1, keepdims=True)
    y2 = h * lax.rsqrt(ms + EPS) * g2_ref[...]
    f_ref[...] = y2 * (1.0 + mod_ref[0, :, 4 * d:5 * d]) + mod_ref[0, :, 3 * d:4 * d]


def _out_projection(na, y_all, qkvu, h2d, mod3, d_skip, wglu_bf16, b_glu, wout_bf16, g2,
                    mod_map, rows_per_batch, y_block0):
    r, d = h2d.shape
    tiles_per_batch = rows_per_batch // TM_PROJ
    u_blk = 3 * NA_WIDTH // S5_WIDTH

    def y_map(i):
        return (i // tiles_per_batch, 0, y_block0 + i % tiles_per_batch, 0)

    const = lambda i: (0, 0)
    return pl.pallas_call(
        _outproj_kernel,
        out_shape=(jax.ShapeDtypeStruct((r, d), F32), jax.ShapeDtypeStruct((r, d), F32)),
        grid=(r // TM_PROJ,),
        in_specs=[
            pl.BlockSpec((TM_PROJ, NA_WIDTH), lambda i: (i, 0)),
            pl.BlockSpec((1, S5_WIDTH // LANES, TM_PROJ, LANES), y_map),
            pl.BlockSpec((TM_PROJ, S5_WIDTH), lambda i: (i, u_blk)),
            pl.BlockSpec((TM_PROJ, d), lambda i: (i, 0)),
            pl.BlockSpec((1, 1, N_MOD * d), mod_map),
            pl.BlockSpec((1, S5_WIDTH), const),
            pl.BlockSpec((S5_WIDTH, S5_WIDTH), const),
            pl.BlockSpec((1, S5_WIDTH), const),
            pl.BlockSpec((NA_WIDTH + S5_WIDTH, d), const),
            pl.BlockSpec((1, d), const),
        ],
        out_specs=(pl.BlockSpec((TM_PROJ, d), lambda i: (i, 0)), pl.BlockSpec((TM_PROJ, d), lambda i: (i, 0))),
        compiler_params=_cparams(("parallel",)),
        name="out_projection",
    )(na, y_all, qkvu, h2d, mod3, d_skip.reshape(1, -1), wglu_bf16, b_glu.reshape(1, -1), wout_bf16,
      g2.reshape(1, d))


def _top2(vals):
    best = vals[0]
    bi = jnp.zeros(best.shape, jnp.int32)
    for i in range(1, len(vals)):
        gt = vals[i] > best
        best = jnp.where(gt, vals[i], best)
        bi = jnp.where(gt, i, bi)
    second = jnp.full(best.shape, -jnp.inf, F32)
    si = jnp.zeros(best.shape, jnp.int32)
    for i in range(len(vals)):
        cand = jnp.where(bi == i, -jnp.inf, vals[i])
        gt = cand > second
        second = jnp.where(gt, cand, second)
        si = jnp.where(gt, i, si)
    return best, bi, second, si


def _router_kernel(f_ref, rwt_ref, rb_ref, idx_ref, gate_ref):
    logits = lax.dot_general(rwt_ref[...], f_ref[...], (((1,), (1,)), ((), ())),
                             precision=HIGHEST, preferred_element_type=F32)
    m = jnp.max(logits, axis=0, keepdims=True)
    e = jnp.exp(logits - m)
    probs = e / jnp.sum(e, axis=0, keepdims=True)
    sel = probs + rb_ref[...]
    sel_rows = [sel[i:i + 1, :] for i in range(N_EXPERTS)]
    prob_rows = [probs[i:i + 1, :] for i in range(N_EXPERTS)]
    scores = []
    for g in range(N_GROUPS):
        b, _, s, _ = _top2(sel_rows[g * EPG:(g + 1) * EPG])
        scores.append(b + s)
    grp = jnp.zeros(scores[0].shape, jnp.int32)
    gbest = scores[0]
    for g in range(1, N_GROUPS):
        gt = scores[g] > gbest
        gbest = jnp.where(gt, scores[g], gbest)
        grp = jnp.where(gt, g, grp)
    in_rows = []
    for j in range(EPG):
        v = sel_rows[j]
        for g in range(1, N_GROUPS):
            v = jnp.where(grp == g, sel_rows[g * EPG + j], v)
        in_rows.append(v)
    _, l1, _, l2 = _top2(in_rows)
    i1 = grp * EPG + l1
    i2 = grp * EPG + l2
    w1 = jnp.zeros(gbest.shape, F32)
    w2 = jnp.zeros(gbest.shape, F32)
    for i in range(N_EXPERTS):
        w1 = jnp.where(i1 == i, prob_rows[i], w1)
        w2 = jnp.where(i2 == i, prob_rows[i], w2)
    tot = w1 + w2
    idx_ref[0:1, :] = i1
    idx_ref[1:2, :] = i2
    gate_ref[0:1, :] = w1 / tot
    gate_ref[1:2, :] = w2 / tot


def _router(f_all, router_w, router_bias):
    n, d = f_all.shape
    return pl.pallas_call(
        _router_kernel,
        out_shape=(jax.ShapeDtypeStruct((2, n), jnp.int32), jax.ShapeDtypeStruct((2, n), F32)),
        grid=(n // TM_ROUTE,),
        in_specs=[
            pl.BlockSpec((TM_ROUTE, d), lambda i: (i, 0)),
            pl.BlockSpec((N_EXPERTS, d), lambda i: (0, 0)),
            pl.BlockSpec((N_EXPERTS, 1), lambda i: (0, 0)),
        ],
        out_specs=(pl.BlockSpec((2, TM_ROUTE), lambda i: (0, i)), pl.BlockSpec((2, TM_ROUTE), lambda i: (0, i))),
        compiler_params=_cparams(("parallel",)),
        name="router",
    )(f_all, router_w.T.astype(F32), router_bias.reshape(N_EXPERTS, 1).astype(F32))


def _scatter_rows_kernel(dest_ref, pad_ref, end_ref, *refs, n_tok, seg_tiles):
    f_refs = refs[:len(seg_tiles)]
    xs_ref, zero_scr, sem = refs[len(seg_tiles):]
    i = pl.program_id(0)
    tm = f_refs[0].shape[0]
    n_rows = xs_ref.shape[0]
    slab = TM_EXP + SUBLANES

    def slab_copy(start, rows):
        return pltpu.make_async_copy(zero_scr.at[pl.ds(0, rows), :], xs_ref.at[pl.ds(start, rows), :], sem)

    @pl.when(i == 0)
    def _():
        zero_scr[...] = jnp.zeros(zero_scr.shape, zero_scr.dtype)
        for e in range(N_EXPERTS):
            start = jnp.minimum((pad_ref[e] // SUBLANES) * SUBLANES, n_rows - slab)
            slab_copy(pl.multiple_of(start, SUBLANES), slab).start()
        for e in range(N_EXPERTS):
            slab_copy(0, slab).wait()
        for k in range(N_EXPERTS):
            start = end_ref[0] + k * TM_EXP

            @pl.when(start < n_rows)
            def _():
                cp = slab_copy(pl.multiple_of(start, TM_EXP), TM_EXP)
                cp.start()
                cp.wait()

    def scatter_tile(f_ref):
        base = i * tm

        def row_copy(r, d):
            return pltpu.make_async_copy(f_ref.at[pl.ds(r, 1), :], xs_ref.at[pl.ds(d, 1), :], sem)

        def body(r, carry):
            row_copy(r, dest_ref[base + r]).start()
            row_copy(r, dest_ref[n_tok + base + r]).start()
            return carry
        lax.fori_loop(0, tm, body, 0, unroll=8)
        for _ in range(2):
            pltpu.make_async_copy(f_ref, xs_ref.at[pl.ds(0, tm), :], sem).wait()

    tile0 = 0
    for f_ref, n_t in zip(f_refs, seg_tiles):
        pl.when((i >= tile0) & (i < tile0 + n_t))(functools.partial(scatter_tile, f_ref))
        tile0 += n_t


def _scatter_rows(segments, dest_flat, pad_start, total_end):
    d = segments[0].shape[1]
    seg_tiles = tuple(s.shape[0] // TM_PROJ for s in segments)
    n_tok = sum(s.shape[0] for s in segments)
    r_max = 2 * n_tok + N_EXPERTS * TM_EXP
    in_specs = []
    tile0 = 0
    for n_t in seg_tiles:
        in_specs.append(pl.BlockSpec(
            (TM_PROJ, d), lambda i, *_, t0=tile0, nt=n_t: (jnp.clip(i - t0, 0, nt - 1), 0)))
        tile0 += n_t
    grid_spec = pltpu.PrefetchScalarGridSpec(
        num_scalar_prefetch=3,
        grid=(tile0,),
        in_specs=in_specs,
        out_specs=pl.BlockSpec(memory_space=pl.ANY),
        scratch_shapes=[pltpu.VMEM((TM_EXP + SUBLANES, d), F32), pltpu.SemaphoreType.DMA(())],
    )
    return pl.pallas_call(
        functools.partial(_scatter_rows_kernel, n_tok=n_tok, seg_tiles=seg_tiles),
        out_shape=jax.ShapeDtypeStruct((r_max, d), F32),
        grid_spec=grid_spec,
        compiler_params=_cparams(("arbitrary",)),
        name="moe_scatter_rows",
    )(dest_flat, pad_start, total_end, *segments)


def _experts_kernel(te_ref, nv_ref, x_ref, wg_ref, wu_ref, wd_ref, o_ref, wg_scr, wu_scr, wd_scr):
    i = pl.program_id(0)
    e = te_ref[i]
    prev = te_ref[jnp.maximum(i - 1, 0)]
    rows = 128

    @pl.when((i == 0) | (e != prev))
    def _():
        def body(r, carry):
            sl = pl.ds(pl.multiple_of(r * rows, rows), rows)
            wg_scr[sl, :] = wg_ref[0, 0, sl, :].astype(BF16)
            wu_scr[sl, :] = wu_ref[0, 0, sl, :].astype(BF16)
            wd_scr[sl, :] = wd_ref[0, 0, sl, :].astype(BF16)
            return carry
        lax.fori_loop(0, wg_scr.shape[0] // rows, body, 0)

    @pl.when(i < nv_ref[0])
    def _():
        x = x_ref[...].astype(BF16)
        g = _dot(x, wg_scr[...])
        u = _dot(x, wu_scr[...])
        a = (g * jax.nn.sigmoid(g)) * u
        o_ref[...] = _dot(a.astype(BF16), wd_scr[...]).astype(BF16)

    @pl.when(i >= nv_ref[0])
    def _():
        o_ref[...] = jnp.zeros(o_ref.shape, BF16)


def _experts(xs, tile_expert, n_valid, w_gate, w_up, w_down, layer):
    r, d = xs.shape
    de = w_gate.shape[3]
    n_tiles = r // TM_EXP
    x_map = lambda i, te, nv: (jnp.minimum(i, nv[0] - 1), 0)
    grid_spec = pltpu.PrefetchScalarGridSpec(
        num_scalar_prefetch=2,
        grid=(n_tiles,),
        in_specs=[
            pl.BlockSpec((TM_EXP, d), x_map),
            pl.BlockSpec((1, 1, d, de), lambda i, te, nv: (layer, te[i], 0, 0)),
            pl.BlockSpec((1, 1, d, de), lambda i, te, nv: (layer, te[i], 0, 0)),
            pl.BlockSpec((1, 1, de, d), lambda i, te, nv: (layer, te[i], 0, 0)),
        ],
        out_specs=pl.BlockSpec((TM_EXP, d), lambda i, te, nv: (i, 0)),
        scratch_shapes=[pltpu.VMEM((d, de), BF16), pltpu.VMEM((d, de), BF16), pltpu.VMEM((de, d), BF16)],
    )
    return pl.pallas_call(
        _experts_kernel,
        out_shape=jax.ShapeDtypeStruct((r, d), BF16),
        grid_spec=grid_spec,
        compiler_params=_cparams(("arbitrary",)),
        name="experts",
    )(tile_expert, n_valid, xs, w_gate, w_up, w_down)


def _dispatch(idx):
    n = idx.shape[1]
    e_flat = idx.reshape(-1)
    onehot = (e_flat[:, None] == jnp.arange(N_EXPERTS, dtype=jnp.int32)[None, :]).astype(jnp.int32)
    csum = jnp.cumsum(onehot, axis=0)
    rank = jnp.sum(csum * onehot, axis=1) - 1
    counts = csum[-1]
    padded = ((counts + TM_EXP - 1) // TM_EXP) * TM_EXP
    ends = jnp.cumsum(padded)
    starts = ends - padded
    dest = (jnp.sum(onehot * starts[None, :], axis=1) + rank).astype(jnp.int32)
    r_max = 2 * n + N_EXPERTS * TM_EXP
    tile_start = jnp.arange(r_max // TM_EXP, dtype=jnp.int32) * TM_EXP
    tile_expert = jnp.minimum(jnp.sum((tile_start[:, None] >= ends[None, :]).astype(jnp.int32), axis=1),
                              N_EXPERTS - 1).astype(jnp.int32)
    n_valid = (ends[-1] // TM_EXP).astype(jnp.int32).reshape(1)
    pad_start = (starts + counts).astype(jnp.int32)
    total_end = ends[-1].astype(jnp.int32).reshape(1)
    return dest, pad_start, total_end, tile_expert, n_valid


def _combine_kernel(h_ref, y1_ref, y2_ref, gate_ref, mod_ref, o_ref):
    d = D_MODEL
    g = gate_ref[...]
    y = g[:, 0:1] * y1_ref[...].astype(F32) + g[:, 1:2] * y2_ref[...].astype(F32)
    o_ref[...] = h_ref[...] + mod_ref[0, :, 5 * d:6 * d] * y


def _combine(h2d, y1, y2, gates_t, mod3, mod_map, row0):
    r, d = h2d.shape
    blk0 = row0 // TM_PROJ
    row = lambda i: (i, 0)
    seg = lambda i: (blk0 + i, 0)
    return pl.pallas_call(
        _combine_kernel,
        out_shape=jax.ShapeDtypeStruct((r, d), F32),
        grid=(r // TM_PROJ,),
        in_specs=[
            pl.BlockSpec((TM_PROJ, d), row),
            pl.BlockSpec((TM_PROJ, d), seg),
            pl.BlockSpec((TM_PROJ, d), seg),
            pl.BlockSpec((TM_PROJ, 2), seg),
            pl.BlockSpec((1, 1, N_MOD * d), mod_map),
        ],
        out_specs=pl.BlockSpec((TM_PROJ, d), row),
        compiler_params=_cparams(("parallel",)),
        name="moe_combine",
    )(h2d, y1, y2, gates_t, mod3)


def kernel(x, c, ctx, c_ctx, w_mod, b_mod, norm1_g, norm2_g, w_in, w_out, q_norm_g, k_norm_g, na_rpb,
           s5_lam_re, s5_lam_im, s5_log_dt, s5_b_re, s5_b_im, s5_c_re, s5_c_im, s5_d, s5_w_glu, s5_b_glu,
           router_w, router_bias, moe_w_gate, moe_w_up, moe_w_down):
    bsz, n_lat, d = x.shape
    n_ctx = ctx.shape[1]
    depth = w_mod.shape[0]
    ctx_row = bsz
    c_rows = jnp.concatenate([c.astype(F32), c_ctx.astype(F32)[None],
                              jnp.zeros((SUBLANES - bsz - 1, d), F32)], axis=0)
    mod_all = _modulation(c_rows, w_mod.astype(F32), b_mod.astype(F32))

    h_lat = x.reshape(bsz * n_lat, d).astype(F32)
    h_ctx = ctx.reshape(bsz * n_ctx, d).astype(F32)
    lat_map = _mod_row_map(n_lat, 0, True)
    ctx_map = _mod_row_map(n_ctx, ctx_row, False)

    for layer in range(depth):
        ctx_out = layer < depth - 1
        mod3 = mod_all[layer].reshape(SUBLANES, 1, N_MOD * d)
        w_in_b = w_in[layer].astype(BF16)
        qkvu_lat = _in_projection(h_lat, norm1_g[layer], mod3, w_in_b, lat_map)
        qkvu_ctx = _in_projection(h_ctx, norm1_g[layer], mod3, w_in_b, ctx_map)
        bias_tab = _na_bias_table(na_rpb[layer])
        na_lat = _neighborhood_attention(qkvu_lat, qkvu_ctx, bias_tab, q_norm_g[layer], k_norm_g[layer], bsz)
        mats = _s5_matrices(s5_lam_re[layer], s5_lam_im[layer], s5_log_dt[layer], s5_b_re[layer],
                            s5_b_im[layer], s5_c_re[layer], s5_c_im[layer])
        y_all = _s5_mixer(qkvu_lat, qkvu_ctx, mats, bsz)
        wglu_b = s5_w_glu[layer].astype(BF16)
        wout_b = w_out[layer].astype(BF16)
        h_lat, f_lat = _out_projection(na_lat, y_all, qkvu_lat, h_lat, mod3, s5_d[layer], wglu_b,
                                       s5_b_glu[layer], wout_b, norm2_g[layer],
                                       lat_map, n_lat, n_ctx // TM_PROJ)
        if ctx_out:
            na_ctx = _context_attention(qkvu_ctx, q_norm_g[layer], k_norm_g[layer], bsz)
            h_ctx, f_ctx = _out_projection(na_ctx, y_all, qkvu_ctx, h_ctx, mod3, s5_d[layer], wglu_b,
                                           s5_b_glu[layer], wout_b, norm2_g[layer],
                                           ctx_map, n_ctx, 0)
        n_l = bsz * n_lat
        idx, gates = _router(f_lat, router_w, router_bias)
        if ctx_out:
            idx_c, gates_c = _router(f_ctx, router_w, router_bias)
            idx = jnp.concatenate([idx, idx_c], axis=1)
            gates = jnp.concatenate([gates, gates_c], axis=1)
        n_tok = idx.shape[1]
        dest, pad_start, total_end, tile_expert, n_valid = _dispatch(idx)
        xs = _scatter_rows([f_lat, f_ctx] if ctx_out else [f_lat], dest, pad_start, total_end)
        ys = _experts(xs, tile_expert, n_valid, moe_w_gate, moe_w_up, moe_w_down, layer)
        y1 = jnp.take(ys, dest[:n_tok], axis=0, mode="clip")
        y2 = jnp.take(ys, dest[n_tok:], axis=0, mode="clip")
        gates_t = gates.T
        h_lat = _combine(h_lat, y1, y2, gates_t, mod3, lat_map, 0)
        if ctx_out:
            h_ctx = _combine(h_ctx, y1, y2, gates_t, mod3, ctx_map, n_l)
    return h_lat.reshape(bsz, n_lat, d).astype(x.dtype)
```

```python
import functools
import math

import jax
import jax.numpy as jnp
from jax import lax
from jax.experimental import pallas as pl
from jax.experimental.pallas import tpu as pltpu

F32 = jnp.float32
BF16 = jnp.bfloat16
HIGHEST = lax.Precision.HIGHEST

D_MODEL = 1024
GRID_W = 64
HEAD_DIM = 64
NA_WIDTH = 512
S5_WIDTH = 512
S5_CH = 16
S5_GROUPS = 32
S5_STATE = 64
WIN_ROWS = 8
WIN_COLS = 16
N_EXPERTS = 16
N_GROUPS = 4
EPG = 4
N_MOD = 6
EPS = 1e-6

LANES = 128
SUBLANES = 8
VMEM_LIMIT = 56 * 1024 * 1024

TM_PROJ = 256
Q_ROWS = 8
Q_COLS = 16
K_ROWS = 16
K_COLS = 32
CHUNK = 16
PAIRS_PER_STEP = 2
TM_EXP = 256
TM_ROUTE = 512
MASK_VALUE = -1e30
RPB_LANE0 = 48


def _cparams(sem):
    return pltpu.CompilerParams(dimension_semantics=sem, vmem_limit_bytes=VMEM_LIMIT)


def _dot(a, b):
    return jnp.dot(a, b, preferred_element_type=F32)


def _dot_nt(a, b):
    return lax.dot_general(a, b, (((1,), (1,)), ((), ())), preferred_element_type=F32)


def _mod_kernel(c_ref, w_ref, b_ref, o_ref):
    a = c_ref[...]
    a = a * jax.nn.sigmoid(a)
    o_ref[0] = jnp.dot(a, w_ref[0], precision=HIGHEST, preferred_element_type=F32) + b_ref[0]


def _modulation(c_rows, w_mod, b_mod):
    depth, d, n = w_mod.shape
    tn = 1536
    return pl.pallas_call(
        _mod_kernel,
        out_shape=jax.ShapeDtypeStruct((depth, SUBLANES, n), F32),
        grid=(depth, n // tn),
        in_specs=[
            pl.BlockSpec((SUBLANES, d), lambda l, j: (0, 0)),
            pl.BlockSpec((1, d, tn), lambda l, j: (l, 0, j)),
            pl.BlockSpec((1, 1, tn), lambda l, j: (l, 0, j)),
        ],
        out_specs=pl.BlockSpec((1, SUBLANES, tn), lambda l, j: (l, 0, j)),
        compiler_params=_cparams(("arbitrary", "arbitrary")),
        name="modulation",
    )(c_rows, w_mod, b_mod.reshape(depth, 1, n))


def _inproj_kernel(x_ref, g_ref, mod_ref, w_ref, o_ref):
    x = x_ref[...]
    ms = jnp.mean(x * x, axis=-1, keepdims=True)
    y = x * lax.rsqrt(ms + EPS) * g_ref[...]
    shift = mod_ref[0, :, 0:D_MODEL]
    scale = mod_ref[0, :, D_MODEL:2 * D_MODEL]
    a = y * (1.0 + scale) + shift
    o_ref[...] = _dot(a.astype(BF16), w_ref[...])


def _mod_row_map(rows_per_batch, mod_row0, per_batch):
    tiles_per_batch = rows_per_batch // TM_PROJ
    if per_batch:
        return lambda i: (mod_row0 + i // tiles_per_batch, 0, 0)
    return lambda i: (mod_row0, 0, 0)


def _in_projection(x2d, g, mod3, w_bf16, mod_map):
    r, d = x2d.shape
    n = w_bf16.shape[1]
    return pl.pallas_call(
        _inproj_kernel,
        out_shape=jax.ShapeDtypeStruct((r, n), F32),
        grid=(r // TM_PROJ,),
        in_specs=[
            pl.BlockSpec((TM_PROJ, d), lambda i: (i, 0)),
            pl.BlockSpec((1, d), lambda i: (0, 0)),
            pl.BlockSpec((1, 1, N_MOD * d), mod_map),
            pl.BlockSpec((d, n), lambda i: (0, 0)),
        ],
        out_specs=pl.BlockSpec((TM_PROJ, n), lambda i: (i, 0)),
        compiler_params=_cparams(("parallel",)),
        name="in_projection",
    )(x2d, g.reshape(1, d), mod3, w_bf16)


def _pair_rms(x, g, lo):
    ss = x * x
    sa = jnp.sum(jnp.where(lo, ss, 0.0), axis=-1, keepdims=True)
    sb = jnp.sum(jnp.where(lo, 0.0, ss), axis=-1, keepdims=True)
    ms = jnp.where(lo, sa, sb) * (1.0 / HEAD_DIM)
    return x * lax.rsqrt(ms + EPS) * g


def _softmax_pv(qm, kw, kcb, vw, vcb, bias):
    s_nb = _dot_nt(qm, kw) + bias
    s_cx = _dot_nt(qm, kcb)
    m = jnp.maximum(jnp.max(s_nb, axis=-1, keepdims=True), jnp.max(s_cx, axis=-1, keepdims=True))
    p_nb = jnp.exp(s_nb - m)
    p_cx = jnp.exp(s_cx - m)
    l = jnp.sum(p_nb, axis=-1, keepdims=True) + jnp.sum(p_cx, axis=-1, keepdims=True)
    o = _dot(p_nb.astype(BF16), vw) + _dot(p_cx.astype(BF16), vcb)
    return o / l


def _na_kernel(q_ref, k_ref, v_ref, kc_ref, vc_ref, bias_ref, qg_ref, kg_ref, o_ref, kn_scr, kcn_scr):
    t = pl.program_id(2)
    lo = lax.broadcasted_iota(jnp.int32, (1, LANES), 1) < HEAD_DIM
    n_rows = k_ref.shape[1]
    col_tiles = GRID_W // Q_COLS

    @pl.when(t == 0)
    def _():
        def body(r, carry):
            kn_scr[r] = _pair_rms(k_ref[0, r], kg_ref[...], lo)
            return carry
        lax.fori_loop(0, n_rows, body, 0)
        kcn_scr[...] = _pair_rms(kc_ref[0], kg_ref[...], lo).astype(BF16)

    i = t
    kr0 = jnp.clip(Q_ROWS * i - WIN_ROWS // 2, 0, n_rows - K_ROWS)
    rt = jnp.where(i == 0, 0, jnp.where(i == n_rows // Q_ROWS - 1, 2, 1))
    nq = Q_ROWS * Q_COLS
    nk = K_ROWS * K_COLS
    kcb = kcn_scr[...]
    vcb = vc_ref[0].astype(BF16)
    for j in range(col_tiles):
        kc0 = min(max(Q_COLS * j - WIN_COLS // 2, 0), GRID_W - K_COLS)
        ct = 0 if j == 0 else (2 if j == col_tiles - 1 else 1)
        typ = rt * 3 + ct
        cols = slice(j * Q_COLS, (j + 1) * Q_COLS)
        q = q_ref[0, :, cols, :].reshape(nq, LANES)
        qn = _pair_rms(q, qg_ref[...], lo) * (HEAD_DIM ** -0.5)
        kw = kn_scr[pl.ds(kr0, K_ROWS), kc0:kc0 + K_COLS, :].reshape(nk, LANES).astype(BF16)
        vw = v_ref[0, pl.ds(kr0, K_ROWS), kc0:kc0 + K_COLS, :].reshape(nk, LANES).astype(BF16)
        o_a = _softmax_pv(jnp.where(lo, qn, 0.0).astype(BF16), kw, kcb, vw, vcb, bias_ref[0, 0, typ, 0])
        o_b = _softmax_pv(jnp.where(lo, 0.0, qn).astype(BF16), kw, kcb, vw, vcb, bias_ref[0, 0, typ, 1])
        o_ref[0, :, cols, :] = jnp.where(lo, o_a, o_b).reshape(Q_ROWS, Q_COLS, LANES)


def _bias_table_kernel(rpb_ref, o_ref, tt_scr):
    n_off_r = 2 * WIN_ROWS - 1
    lane = lax.broadcasted_iota(jnp.int32, (Q_COLS, LANES), 1)
    qc = lax.broadcasted_iota(jnp.int32, (Q_COLS, LANES), 0)
    kc = lane % K_COLS
    lane_blk = lane // K_COLS
    per_vreg = LANES // K_COLS
    col_rel = (0, -WIN_COLS // 2, -WIN_COLS)
    col_origin = (0, Q_COLS, GRID_W - Q_COLS)
    row_rel = (0, -WIN_ROWS // 2, -WIN_ROWS)
    row_origin = (0, Q_ROWS, GRID_W - Q_ROWS)
    masked = jnp.full((Q_COLS, LANES), MASK_VALUE, F32)

    for ct in range(3):
        c_abs = col_origin[ct] + qc
        k_abs = col_origin[ct] + col_rel[ct] + kc
        start = jnp.clip(c_abs - WIN_COLS // 2, 0, GRID_W - WIN_COLS)
        valid_c = (k_abs >= start) & (k_abs < start + WIN_COLS)
        base = (1 - WIN_COLS - col_rel[ct] - RPB_LANE0) % LANES
        for ro in range(n_off_r):
            row = jnp.broadcast_to(rpb_ref[0, 0, ro:ro + 1, :], (Q_COLS, LANES))
            t = pltpu.roll(row, base, 1, stride=1, stride_axis=0)
            rep = t
            for m in range(1, per_vreg):
                rep = jnp.where(lane_blk == m, pltpu.roll(t, K_COLS * m, 1), rep)
            tt_scr[ct, ro] = jnp.where(valid_c, rep, MASK_VALUE)

    for rt in range(3):
        for ct in range(3):
            for qr in range(Q_ROWS):
                r_abs = row_origin[rt] + qr
                r_start = min(max(r_abs - WIN_ROWS // 2, 0), GRID_W - WIN_ROWS)
                for w in range(K_ROWS // per_vreg):
                    val = None
                    for m in range(per_vreg):
                        k_abs = row_origin[rt] + row_rel[rt] + per_vreg * w + m
                        ok = r_start <= k_abs < r_start + WIN_ROWS
                        src = tt_scr[ct, k_abs - r_abs + WIN_ROWS - 1] if ok else masked
                        val = src if val is None else jnp.where(lane_blk == m, src, val)
                    o_ref[0, 0, rt * 3 + ct, 0, qr * Q_COLS:(qr + 1) * Q_COLS, w * LANES:(w + 1) * LANES] = val


def _na_bias_tables(na_rpb):
    depth, h, n_r, n_c = na_rpb.shape
    rpb_pad = jnp.pad(na_rpb.astype(F32), ((0, 0), (0, 0), (0, 2 * SUBLANES - n_r),
                                            (RPB_LANE0, LANES - RPB_LANE0 - n_c)))
    nq, nk = Q_ROWS * Q_COLS, K_ROWS * K_COLS
    return pl.pallas_call(
        _bias_table_kernel,
        out_shape=jax.ShapeDtypeStruct((depth, h // 2, 9, 2, nq, nk), F32),
        grid=(depth, h),
        in_specs=[pl.BlockSpec((1, 1, 2 * SUBLANES, LANES), lambda l, i: (l, i, 0, 0))],
        out_specs=pl.BlockSpec((1, 1, 9, 1, nq, nk), lambda l, i: (l, i // 2, 0, i % 2, 0, 0)),
        scratch_shapes=[pltpu.VMEM((3, 2 * WIN_ROWS - 1, Q_COLS, LANES), F32)],
        compiler_params=_cparams(("parallel", "parallel")),
        name="na_bias_tables",
    )(rpb_pad)


def _neighborhood_attention(qkvu_lat, qkvu_ctx, bias_tabs, layer, qg, kg, bsz):
    n_lat = qkvu_lat.shape[0] // bsz
    n_ctx = qkvu_ctx.shape[0] // bsz
    rows = n_lat // GRID_W
    n_cols = qkvu_lat.shape[1]
    lat4 = qkvu_lat.reshape(bsz, rows, GRID_W, n_cols)
    ctx3 = qkvu_ctx.reshape(bsz, n_ctx, n_cols)
    n_pairs = NA_WIDTH // LANES
    n_tiles = rows // Q_ROWS
    g2 = lambda g: jnp.concatenate([g, g]).reshape(1, LANES).astype(F32)
    out = pl.pallas_call(
        _na_kernel,
        out_shape=jax.ShapeDtypeStruct((bsz, rows, GRID_W, NA_WIDTH), F32),
        grid=(bsz, n_pairs, n_tiles),
        in_specs=[
            pl.BlockSpec((1, Q_ROWS, GRID_W, LANES), lambda b, p, t: (b, t, 0, p)),
            pl.BlockSpec((1, rows, GRID_W, LANES), lambda b, p, t: (b, 0, 0, n_pairs + p)),
            pl.BlockSpec((1, rows, GRID_W, LANES), lambda b, p, t: (b, 0, 0, 2 * n_pairs + p)),
            pl.BlockSpec((1, n_ctx, LANES), lambda b, p, t: (b, 0, n_pairs + p)),
            pl.BlockSpec((1, n_ctx, LANES), lambda b, p, t: (b, 0, 2 * n_pairs + p)),
            pl.BlockSpec((1, 1, 9, 2, Q_ROWS * Q_COLS, K_ROWS * K_COLS), lambda b, p, t: (layer, p, 0, 0, 0, 0)),
            pl.BlockSpec((1, LANES), lambda b, p, t: (0, 0)),
            pl.BlockSpec((1, LANES), lambda b, p, t: (0, 0)),
        ],
        out_specs=pl.BlockSpec((1, Q_ROWS, GRID_W, LANES), lambda b, p, t: (b, t, 0, p)),
        scratch_shapes=[pltpu.VMEM((rows, GRID_W, LANES), F32), pltpu.VMEM((n_ctx, LANES), BF16)],
        compiler_params=_cparams(("parallel", "parallel", "arbitrary")),
        name="neighborhood_attention",
    )(lat4, lat4, lat4, ctx3, ctx3, bias_tabs, g2(qg), g2(kg))
    return out.reshape(bsz * n_lat, NA_WIDTH)


def _ctx_attn_kernel(q_ref, k_ref, v_ref, qg_ref, kg_ref, o_ref):
    lo = lax.broadcasted_iota(jnp.int32, (1, LANES), 1) < HEAD_DIM
    qn = _pair_rms(q_ref[0], qg_ref[...], lo) * (HEAD_DIM ** -0.5)
    kn = _pair_rms(k_ref[0], kg_ref[...], lo).astype(BF16)
    vb = v_ref[0].astype(BF16)

    def one(qm):
        s = _dot_nt(qm, kn)
        m = jnp.max(s, axis=-1, keepdims=True)
        p = jnp.exp(s - m)
        l = jnp.sum(p, axis=-1, keepdims=True)
        return _dot(p.astype(BF16), vb) / l

    o_a = one(jnp.where(lo, qn, 0.0).astype(BF16))
    o_b = one(jnp.where(lo, 0.0, qn).astype(BF16))
    o_ref[0] = jnp.where(lo, o_a, o_b)


def _context_attention(qkvu_ctx, qg, kg, bsz):
    n_ctx = qkvu_ctx.shape[0] // bsz
    ctx3 = qkvu_ctx.reshape(bsz, n_ctx, qkvu_ctx.shape[1])
    n_pairs = NA_WIDTH // LANES
    g2 = lambda g: jnp.concatenate([g, g]).reshape(1, LANES).astype(F32)
    out = pl.pallas_call(
        _ctx_attn_kernel,
        out_shape=jax.ShapeDtypeStruct((bsz, n_ctx, NA_WIDTH), F32),
        grid=(bsz, n_pairs),
        in_specs=[
            pl.BlockSpec((1, n_ctx, LANES), lambda b, p: (b, 0, p)),
            pl.BlockSpec((1, n_ctx, LANES), lambda b, p: (b, 0, n_pairs + p)),
            pl.BlockSpec((1, n_ctx, LANES), lambda b, p: (b, 0, 2 * n_pairs + p)),
            pl.BlockSpec((1, LANES), lambda b, p: (0, 0)),
            pl.BlockSpec((1, LANES), lambda b, p: (0, 0)),
        ],
        out_specs=pl.BlockSpec((1, n_ctx, LANES), lambda b, p: (b, 0, p)),
        compiler_params=_cparams(("parallel", "parallel")),
        name="context_attention",
    )(ctx3, ctx3, ctx3, g2(qg), g2(kg))
    return out.reshape(bsz * n_ctx, NA_WIDTH)


def _s5_kernel(uc_ref, ul_ref, w_ref, m_ref, v_ref, a_ref, y_ref, x_scr, s_scr, hf_scr, hr_scr, *, n_ctx_chunks):
    bsz = ul_ref.shape[0]
    n_lat_chunks = ul_ref.shape[1] // CHUNK
    n_chunks = n_ctx_chunks + n_lat_chunks
    rows = n_chunks * SUBLANES
    n_pairs = w_ref.shape[1]
    gpb = 2 * n_pairs
    half = 2 * LANES
    tile_chunks = TM_PROJ // CHUNK
    lane_blk = lax.broadcasted_iota(jnp.int32, (n_chunks, LANES), 1) // S5_CH

    for b in range(bsz):
        for q in range(2):
            rolled = []
            for j in range(SUBLANES):
                s = SUBLANES * q + j
                u_s = jnp.concatenate([uc_ref[b, pl.ds(s, n_ctx_chunks, stride=CHUNK), :],
                                       ul_ref[b, pl.ds(s, n_lat_chunks, stride=CHUNK), :]], axis=0)
                rolled.append(u_s if j == 0 else pltpu.roll(u_s, j * S5_CH, 1))
            for g in range(gpb):
                xg = rolled[0]
                for j in range(1, SUBLANES):
                    xg = jnp.where(lane_blk == (g + j) % SUBLANES, rolled[j], xg)
                x_scr[g // 2, q, pl.ds((g % 2) * bsz + b, n_chunks, stride=SUBLANES), :] = xg

    n_blk = 8
    rb = rows // n_blk
    first_group = (lax.broadcasted_iota(jnp.int32, (rb, half), 0) & (SUBLANES // 2)) == 0
    fwd_cols = (lax.broadcasted_iota(jnp.int32, (rb, half), 1) & (LANES - 1)) < S5_STATE
    is_fwd = lax.broadcasted_iota(jnp.int32, (SUBLANES, LANES), 1) < S5_STATE
    first_rows = lax.broadcasted_iota(jnp.int32, (SUBLANES, half), 0) < SUBLANES // 2
    zero = jnp.zeros((SUBLANES, LANES), F32)

    def x_rows(p, sl):
        return jnp.concatenate([x_scr[p, 0, sl, :], x_scr[p, 1, sl, :]], axis=1).astype(BF16)

    def put_cols(scr, i, sl, val):
        scr[i, 0, sl, :] = val[:, 0:LANES]
        scr[i, 1, sl, :] = val[:, LANES:half]

    for hh in range(n_pairs // PAIRS_PER_STEP):
        pairs = [hh * PAIRS_PER_STEP + i for i in range(PAIRS_PER_STEP)]
        for i, p in enumerate(pairs):
            for blk in range(n_blk):
                sl = slice(blk * rb, (blk + 1) * rb)
                r = _dot(x_rows(p, sl), w_ref[0, p])
                put_cols(s_scr, i, sl, jnp.where(first_group, r[:, :half], r[:, half:]))

        a_pair = [jnp.where(first_rows, a_ref[0, 2 * p], a_ref[0, 2 * p + 1]) for p in pairs]
        a_re = [a[:, 0:LANES] for a in a_pair]
        a_im = [a[:, LANES:half] for a in a_pair]

        def body(k, carry):
            kr = jnp.where(k < n_ctx_chunks, n_ctx_chunks - 1 - k, n_chunks + n_ctx_chunks - 1 - k)
            rf = pl.ds(pl.multiple_of(k * SUBLANES, SUBLANES), SUBLANES)
            rr = pl.ds(pl.multiple_of(kr * SUBLANES, SUBLANES), SUBLANES)
            new = []
            for i in range(PAIRS_PER_STEP):
                h_re, h_im = carry[2 * i], carry[2 * i + 1]
                hf_scr[i, 0, rf, :] = h_re
                hf_scr[i, 1, rf, :] = h_im
                hr_scr[i, 0, rr, :] = h_re
                hr_scr[i, 1, rr, :] = h_im
                s_re = jnp.where(is_fwd, s_scr[i, 0, rf, :], s_scr[i, 0, rr, :])
                s_im = jnp.where(is_fwd, s_scr[i, 1, rf, :], s_scr[i, 1, rr, :])
                new.append(a_re[i] * h_re - a_im[i] * h_im + s_re)
                new.append(a_re[i] * h_im + a_im[i] * h_re + s_im)
            return tuple(new)

        lax.fori_loop(0, n_chunks, body, (zero,) * (2 * PAIRS_PER_STEP))

        for i, p in enumerate(pairs):
            for blk in range(n_blk):
                sl = slice(blk * rb, (blk + 1) * rb)
                hf = jnp.concatenate([hf_scr[i, 0, sl, :], hf_scr[i, 1, sl, :]], axis=1)
                hr = jnp.concatenate([hr_scr[i, 0, sl, :], hr_scr[i, 1, sl, :]], axis=1)
                h_in = jnp.where(fwd_cols, hf, hr).astype(BF16)
                r = _dot(x_rows(p, sl), m_ref[0, p]) + _dot_nt(h_in, v_ref[0, p])
                put_cols(s_scr, i, sl, jnp.where(first_group, r[:, :half], r[:, half:]))

        g_lo = 2 * pairs[0]
        n_g = 2 * PAIRS_PER_STEP
        out_blk = lax.broadcasted_iota(jnp.int32, (tile_chunks, LANES), 1) // S5_CH
        lanes_out = (out_blk >= g_lo) & (out_blk < g_lo + n_g)
        for b in range(bsz):
            for q in range(2):
                y_g = [s_scr[(g - g_lo) // 2, q, pl.ds((g % 2) * bsz + b, n_chunks, stride=SUBLANES), :]
                       for g in range(g_lo, g_lo + n_g)]
                for j in range(SUBLANES):
                    z = y_g[0]
                    for gi in range(1, n_g):
                        z = jnp.where(lane_blk == (g_lo + gi + j) % SUBLANES, y_g[gi], z)
                    if j:
                        z = pltpu.roll(z, LANES - j * S5_CH, 1)
                    t = SUBLANES * q + j
                    for ct in range(n_chunks // tile_chunks):
                        r0 = ct * TM_PROJ + t * tile_chunks
                        pltpu.store(y_ref.at[b, 0, r0:r0 + tile_chunks, :],
                                    z[ct * tile_chunks:(ct + 1) * tile_chunks, :], mask=lanes_out)


def _s5_scan(qkvu_ctx3, qkvu_lat3, mats, layer):
    w_c, m_c, vt_c, a_c = mats
    bsz, n_ctx, _ = qkvu_ctx3.shape
    n_lat = qkvu_lat3.shape[1]
    n_seq = n_ctx + n_lat
    rows = n_seq // CHUNK * SUBLANES
    n_blocks = S5_WIDTH // LANES
    ppb = w_c.shape[1] // n_blocks
    u_blk0 = 3 * NA_WIDTH // LANES
    wspec = pl.BlockSpec((1, ppb, 2 * LANES, 4 * LANES), lambda i: (layer, i, 0, 0))
    one = pl.Buffered(1)
    state = pltpu.VMEM((PAIRS_PER_STEP, 2, rows, LANES), F32)
    return pl.pallas_call(
        functools.partial(_s5_kernel, n_ctx_chunks=n_ctx // CHUNK),
        out_shape=jax.ShapeDtypeStruct((bsz, n_blocks, n_seq, LANES), F32),
        grid=(n_blocks,),
        in_specs=[
            pl.BlockSpec((bsz, n_ctx, LANES), lambda i: (0, 0, u_blk0 + i)),
            pl.BlockSpec((bsz, n_lat, LANES), lambda i: (0, 0, u_blk0 + i), pipeline_mode=one),
            wspec, wspec,
            pl.BlockSpec((1, ppb, 4 * LANES, 2 * LANES), lambda i: (layer, i, 0, 0)),
            pl.BlockSpec((1, 2 * ppb, SUBLANES, 2 * LANES), lambda i: (layer, i, 0, 0)),
        ],
        out_specs=pl.BlockSpec((bsz, 1, n_seq, LANES), lambda i: (0, i, 0, 0), pipeline_mode=one),
        scratch_shapes=[pltpu.VMEM((ppb, 2, rows, LANES), F32), state, state, state],
        compiler_params=_cparams(("parallel",)),
        name="s5_scan",
    )(qkvu_ctx3, qkvu_lat3, w_c, m_c, vt_c, a_c)


def _s5_mats_kernel(prm_ref, btr_ref, bti_ref, cr_ref, ci_ref, w_ref, m_ref, vt_ref, a_ref):
    t = CHUNK
    gl = pl.program_id(1) % SUBLANES
    is_fwd = lax.broadcasted_iota(jnp.int32, (1, LANES), 1) < S5_STATE
    lr = prm_ref[0, 0, 0:1, :]
    li = prm_ref[0, 0, 1:2, :]
    dt = jnp.exp(prm_ref[0, 0, 2:3, :])
    n = lax.broadcasted_iota(jnp.int32, (3 * SUBLANES, LANES), 0).astype(F32)
    pmag = jnp.exp(n * (lr * dt))
    pw_re = pmag * jnp.cos(n * (li * dt))
    pw_im = pmag * jnp.sin(n * (li * dt))
    ab_re, ab_im = pw_re[1:2, :], pw_im[1:2, :]
    den = lr * lr + li * li
    nr = ab_re - 1.0
    z_re = (nr * lr + ab_im * li) / den
    z_im = (ab_im * lr - nr * li) / den
    bt_re, bt_im = btr_ref[0, 0], bti_ref[0, 0]
    bb_re = z_re * bt_re - z_im * bt_im
    bb_im = z_re * bt_im + z_im * bt_re
    c_re, c_im = cr_ref[0, 0], ci_ref[0, 0]

    def powers(n_fwd, n_rev):
        return (jnp.where(is_fwd, pw_re[n_fwd:n_fwd + 1, :], pw_re[n_rev:n_rev + 1, :]),
                jnp.where(is_fwd, pw_im[n_fwd:n_fwd + 1, :], pw_im[n_rev:n_rev + 1, :]))

    def block_rows(s):
        pos = SUBLANES * (s // SUBLANES) + (s % SUBLANES + gl) % SUBLANES
        return pl.ds(pl.multiple_of(pos * S5_CH, S5_CH), S5_CH)

    for s in range(t):
        rows = block_rows(s)
        p_re, p_im = powers(t - 1 - s, s)
        w_ref[0, 0, rows, 0:LANES] = (bb_re * p_re - bb_im * p_im).astype(BF16)
        w_ref[0, 0, rows, LANES:2 * LANES] = (bb_re * p_im + bb_im * p_re).astype(BF16)
        q_re, q_im = powers(s + 1, t - s)
        vt_ref[0, 0, rows, 0:LANES] = (c_re * q_re - c_im * q_im).astype(BF16)
        vt_ref[0, 0, rows, LANES:2 * LANES] = (-(c_re * q_im + c_im * q_re)).astype(BF16)

    ca_re, ca_im = [], []
    for lag in range(t):
        p_re, p_im = powers(lag, t - 1 - lag)
        ca_re.append(c_re * p_re - c_im * p_im)
        ca_im.append(c_re * p_im + c_im * p_re)
    stack = jnp.concatenate([jnp.concatenate(ca_re, axis=0), jnp.concatenate(ca_im, axis=0)], axis=1)
    zero = jnp.zeros_like(bb_re)
    lhs = jnp.concatenate([
        jnp.concatenate([jnp.where(is_fwd, bb_re, zero), jnp.where(is_fwd, -bb_im, zero)], axis=1),
        jnp.concatenate([jnp.where(is_fwd, zero, bb_re), jnp.where(is_fwd, zero, -bb_im)], axis=1)], axis=0)
    kt = lax.dot_general(lhs, stack, (((1,), (1,)), ((), ())), precision=HIGHEST, preferred_element_type=F32)
    kt_f, kt_r = kt[0:S5_CH], kt[S5_CH:2 * S5_CH]
    blk = lax.broadcasted_iota(jnp.int32, (S5_CH, 2 * LANES), 1) // S5_CH
    for s in range(t):
        strip = (jnp.where(blk >= s, pltpu.roll(kt_f, S5_CH * s, 1), 0.0)
                 + jnp.where(blk <= s, pltpu.roll(kt_r, (S5_CH * (s - t + 1)) % (2 * LANES), 1), 0.0))
        strip = jnp.concatenate([pltpu.roll(strip[:, 0:LANES], gl * S5_CH, 1),
                                 pltpu.roll(strip[:, LANES:2 * LANES], gl * S5_CH, 1)], axis=1)
        m_ref[0, 0, block_rows(s), :] = strip.astype(BF16)

    a_ref[0, 0, :, 0:LANES] = jnp.broadcast_to(pw_re[t:t + 1, :], (SUBLANES, LANES))
    a_ref[0, 0, :, LANES:2 * LANES] = jnp.broadcast_to(pw_im[t:t + 1, :], (SUBLANES, LANES))


def _s5_matrices(lam_re, lam_im, log_dt, b_re, b_im, c_re, c_im):
    depth, _, g, p = lam_re.shape
    hc = b_re.shape[-1]
    width = CHUNK * hc
    both = lambda x: jnp.transpose(x.astype(F32), (0, 2, 1, 3)).reshape(depth, g, 1, 2 * p)
    dt_rows = jnp.broadcast_to(jnp.transpose(log_dt.astype(F32), (0, 2, 1))[..., None], (depth, g, 2, p))
    prm = jnp.concatenate([both(lam_re), both(lam_im), dt_rows.reshape(depth, g, 1, 2 * p),
                           jnp.zeros((depth, g, SUBLANES - 3, 2 * p), F32)], axis=2)
    bt = lambda x: jnp.transpose(x.astype(F32), (0, 2, 4, 1, 3)).reshape(depth, g, hc, 2 * p)
    ct = lambda x: jnp.transpose(x.astype(F32), (0, 2, 3, 1, 4)).reshape(depth, g, hc, 2 * p)
    vec = lambda rows: pl.BlockSpec((1, 1, rows, 2 * p), lambda l, i: (l, i, 0, 0))
    return pl.pallas_call(
        _s5_mats_kernel,
        out_shape=(jax.ShapeDtypeStruct((depth, g // 2, width, 2 * width), BF16),
                   jax.ShapeDtypeStruct((depth, g // 2, width, 2 * width), BF16),
                   jax.ShapeDtypeStruct((depth, g // 2, 2 * width, width), BF16),
                   jax.ShapeDtypeStruct((depth, g, SUBLANES, 4 * p), F32)),
        grid=(depth, g),
        in_specs=[vec(SUBLANES), vec(hc), vec(hc), vec(hc), vec(hc)],
        out_specs=(pl.BlockSpec((1, 1, width, width), lambda l, i: (l, i // 2, 0, i % 2)),
                   pl.BlockSpec((1, 1, width, width), lambda l, i: (l, i // 2, 0, i % 2)),
                   pl.BlockSpec((1, 1, width, width), lambda l, i: (l, i // 2, i % 2, 0)),
                   pl.BlockSpec((1, 1, SUBLANES, 4 * p), lambda l, i: (l, i, 0, 0))),
        compiler_params=_cparams(("parallel", "parallel")),
        name="s5_matrices",
    )(prm, bt(b_re), bt(b_im), ct(c_re), ct(c_im))


def _s5_mixer(qkvu_lat, qkvu_ctx, mats, layer, bsz):
    n_cols = qkvu_lat.shape[1]
    return _s5_scan(qkvu_ctx.reshape(bsz, -1, n_cols), qkvu_lat.reshape(bsz, -1, n_cols), mats, layer)


def _outproj_kernel(na_ref, y_ref, u_ref, h_ref, mod_ref, d_ref, wglu_ref, bglu_ref, wout_ref, g2_ref,
                    ho_ref, f_ref):
    tile_chunks = TM_PROJ // CHUNK
    y = jnp.concatenate(
        [jnp.concatenate([y_ref[0, blk, pl.ds(c, CHUNK, stride=tile_chunks), :] for c in range(tile_chunks)], axis=0)
         for blk in range(S5_WIDTH // LANES)], axis=1)
    z = jax.nn.gelu(y + d_ref[...] * u_ref[...])
    s5 = z * jax.nn.sigmoid(_dot(z.astype(BF16), wglu_ref[...]) + bglu_ref[...])
    mix = (_dot(na_ref[...].astype(BF16), wout_ref[0:NA_WIDTH, :])
           + _dot(s5.astype(BF16), wout_ref[NA_WIDTH:NA_WIDTH + S5_WIDTH, :]))
    d = D_MODEL
    gate = mod_ref[0, :, 2 * d:3 * d]
    h = h_ref[...] + gate * mix
    ho_ref[...] = h
    ms = jnp.mean(h * h, axis=-1, keepdims=True)
    y2 = h * lax.rsqrt(ms + EPS) * g2_ref[...]
    f_ref[...] = y2 * (1.0 + mod_ref[0, :, 4 * d:5 * d]) + mod_ref[0, :, 3 * d:4 * d]


def _out_projection(na, y_all, qkvu, h2d, mod3, d_skip, wglu_bf16, b_glu, wout_bf16, g2,
                    mod_map, rows_per_batch, y_block0):
    r, d = h2d.shape
    tiles_per_batch = rows_per_batch // TM_PROJ
    u_blk = 3 * NA_WIDTH // S5_WIDTH

    def y_map(i):
        return (i // tiles_per_batch, 0, y_block0 + i % tiles_per_batch, 0)

    const = lambda i: (0, 0)
    return pl.pallas_call(
        _outproj_kernel,
        out_shape=(jax.ShapeDtypeStruct((r, d), F32), jax.ShapeDtypeStruct((r, d), F32)),
        grid=(r // TM_PROJ,),
        in_specs=[
            pl.BlockSpec((TM_PROJ, NA_WIDTH), lambda i: (i, 0)),
            pl.BlockSpec((1, S5_WIDTH // LANES, TM_PROJ, LANES), y_map),
            pl.BlockSpec((TM_PROJ, S5_WIDTH), lambda i: (i, u_blk)),
            pl.BlockSpec((TM_PROJ, d), lambda i: (i, 0)),
            pl.BlockSpec((1, 1, N_MOD * d), mod_map),
            pl.BlockSpec((1, S5_WIDTH), const),
            pl.BlockSpec((S5_WIDTH, S5_WIDTH), const),
            pl.BlockSpec((1, S5_WIDTH), const),
            pl.BlockSpec((NA_WIDTH + S5_WIDTH, d), const),
            pl.BlockSpec((1, d), const),
        ],
        out_specs=(pl.BlockSpec((TM_PROJ, d), lambda i: (i, 0)), pl.BlockSpec((TM_PROJ, d), lambda i: (i, 0))),
        compiler_params=_cparams(("parallel",)),
        name="out_projection",
    )(na, y_all, qkvu, h2d, mod3, d_skip.reshape(1, -1), wglu_bf16, b_glu.reshape(1, -1), wout_bf16,
      g2.reshape(1, d))


def _top2(vals):
    best = vals[0]
    bi = jnp.zeros(best.shape, jnp.int32)
    for i in range(1, len(vals)):
        gt = vals[i] > best
        best = jnp.where(gt, vals[i], best)
        bi = jnp.where(gt, i, bi)
    second = jnp.full(best.shape, -jnp.inf, F32)
    si = jnp.zeros(best.shape, jnp.int32)
    for i in range(len(vals)):
        cand = jnp.where(bi == i, -jnp.inf, vals[i])
        gt = cand > second
        second = jnp.where(gt, cand, second)
        si = jnp.where(gt, i, si)
    return best, bi, second, si


def _router_kernel(f_ref, rwt_ref, rb_ref, idx_ref, gate_ref):
    logits = lax.dot_general(rwt_ref[...], f_ref[...], (((1,), (1,)), ((), ())),
                             precision=HIGHEST, preferred_element_type=F32)
    m = jnp.max(logits, axis=0, keepdims=True)
    e = jnp.exp(logits - m)
    probs = e / jnp.sum(e, axis=0, keepdims=True)
    sel = probs + rb_ref[...]
    sel_rows = [sel[i:i + 1, :] for i in range(N_EXPERTS)]
    prob_rows = [probs[i:i + 1, :] for i in range(N_EXPERTS)]
    scores = []
    for g in range(N_GROUPS):
        b, _, s, _ = _top2(sel_rows[g * EPG:(g + 1) * EPG])
        scores.append(b + s)
    grp = jnp.zeros(scores[0].shape, jnp.int32)
    gbest = scores[0]
    for g in range(1, N_GROUPS):
        gt = scores[g] > gbest
        gbest = jnp.where(gt, scores[g], gbest)
        grp = jnp.where(gt, g, grp)
    in_rows = []
    for j in range(EPG):
        v = sel_rows[j]
        for g in range(1, N_GROUPS):
            v = jnp.where(grp == g, sel_rows[g * EPG + j], v)
        in_rows.append(v)
    _, l1, _, l2 = _top2(in_rows)
    i1 = grp * EPG + l1
    i2 = grp * EPG + l2
    w1 = jnp.zeros(gbest.shape, F32)
    w2 = jnp.zeros(gbest.shape, F32)
    for i in range(N_EXPERTS):
        w1 = jnp.where(i1 == i, prob_rows[i], w1)
        w2 = jnp.where(i2 == i, prob_rows[i], w2)
    tot = w1 + w2
    idx_ref[0:1, :] = i1
    idx_ref[1:2, :] = i2
    gate_ref[0:1, :] = w1 / tot
    gate_ref[1:2, :] = w2 / tot


def _router(f_all, router_w, router_bias):
    n, d = f_all.shape
    return pl.pallas_call(
        _router_kernel,
        out_shape=(jax.ShapeDtypeStruct((2, n), jnp.int32), jax.ShapeDtypeStruct((2, n), F32)),
        grid=(n // TM_ROUTE,),
        in_specs=[
            pl.BlockSpec((TM_ROUTE, d), lambda i: (i, 0)),
            pl.BlockSpec((N_EXPERTS, d), lambda i: (0, 0)),
            pl.BlockSpec((N_EXPERTS, 1), lambda i: (0, 0)),
        ],
        out_specs=(pl.BlockSpec((2, TM_ROUTE), lambda i: (0, i)), pl.BlockSpec((2, TM_ROUTE), lambda i: (0, i))),
        compiler_params=_cparams(("parallel",)),
        name="router",
    )(f_all, router_w.T.astype(F32), router_bias.reshape(N_EXPERTS, 1).astype(F32))


def _scatter_rows_kernel(dest_ref, pad_ref, end_ref, *refs, n_tok, seg_tiles):
    f_refs = refs[:len(seg_tiles)]
    xs_ref, zero_scr, sem = refs[len(seg_tiles):]
    i = pl.program_id(0)
    tm = f_refs[0].shape[0]
    n_rows = xs_ref.shape[0]
    slab = TM_EXP + SUBLANES

    def slab_copy(start, rows):
        return pltpu.make_async_copy(zero_scr.at[pl.ds(0, rows), :], xs_ref.at[pl.ds(start, rows), :], sem)

    @pl.when(i == 0)
    def _():
        zero_scr[...] = jnp.zeros(zero_scr.shape, zero_scr.dtype)
        for e in range(N_EXPERTS):
            start = jnp.minimum((pad_ref[e] // SUBLANES) * SUBLANES, n_rows - slab)
            slab_copy(pl.multiple_of(start, SUBLANES), slab).start()
        for e in range(N_EXPERTS):
            slab_copy(0, slab).wait()
        for k in range(N_EXPERTS):
            start = end_ref[0] + k * TM_EXP

            @pl.when(start < n_rows)
            def _():
                cp = slab_copy(pl.multiple_of(start, TM_EXP), TM_EXP)
                cp.start()
                cp.wait()

    def scatter_tile(f_ref):
        base = i * tm

        def row_copy(r, d):
            return pltpu.make_async_copy(f_ref.at[pl.ds(r, 1), :], xs_ref.at[pl.ds(d, 1), :], sem)

        def body(r, carry):
            row_copy(r, dest_ref[base + r]).start()
            row_copy(r, dest_ref[n_tok + base + r]).start()
            return carry
        lax.fori_loop(0, tm, body, 0, unroll=8)
        for _ in range(2):
            pltpu.make_async_copy(f_ref, xs_ref.at[pl.ds(0, tm), :], sem).wait()

    tile0 = 0
    for f_ref, n_t in zip(f_refs, seg_tiles):
        pl.when((i >= tile0) & (i < tile0 + n_t))(functools.partial(scatter_tile, f_ref))
        tile0 += n_t


def _scatter_rows(segments, dest_flat, pad_start, total_end):
    d = segments[0].shape[1]
    seg_tiles = tuple(s.shape[0] // TM_PROJ for s in segments)
    n_tok = sum(s.shape[0] for s in segments)
    r_max = 2 * n_tok + N_EXPERTS * TM_EXP
    in_specs = []
    tile0 = 0
    for n_t in seg_tiles:
        in_specs.append(pl.BlockSpec(
            (TM_PROJ, d), lambda i, *_, t0=tile0, nt=n_t: (jnp.clip(i - t0, 0, nt - 1), 0)))
        tile0 += n_t
    grid_spec = pltpu.PrefetchScalarGridSpec(
        num_scalar_prefetch=3,
        grid=(tile0,),
        in_specs=in_specs,
        out_specs=pl.BlockSpec(memory_space=pl.ANY),
        scratch_shapes=[pltpu.VMEM((TM_EXP + SUBLANES, d), F32), pltpu.SemaphoreType.DMA(())],
    )
    return pl.pallas_call(
        functools.partial(_scatter_rows_kernel, n_tok=n_tok, seg_tiles=seg_tiles),
        out_shape=jax.ShapeDtypeStruct((r_max, d), F32),
        grid_spec=grid_spec,
        compiler_params=_cparams(("arbitrary",)),
        name="moe_scatter_rows",
    )(dest_flat, pad_start, total_end, *segments)


def _experts_kernel(te_ref, nv_ref, x_ref, wg_ref, wu_ref, wd_ref, o_ref, wg_scr, wu_scr, wd_scr):
    i = pl.program_id(0)
    e = te_ref[i]
    prev = te_ref[jnp.maximum(i - 1, 0)]
    rows = 128

    @pl.when((i == 0) | (e != prev))
    def _():
        def body(r, carry):
            sl = pl.ds(pl.multiple_of(r * rows, rows), rows)
            wg_scr[sl, :] = wg_ref[0, 0, sl, :].astype(BF16)
            wu_scr[sl, :] = wu_ref[0, 0, sl, :].astype(BF16)
            wd_scr[sl, :] = wd_ref[0, 0, sl, :].astype(BF16)
            return carry
        lax.fori_loop(0, wg_scr.shape[0] // rows, body, 0)

    @pl.when(i < nv_ref[0])
    def _():
        x = x_ref[...].astype(BF16)
        g = _dot(x, wg_scr[...])
        u = _dot(x, wu_scr[...])
        a = (g * jax.nn.sigmoid(g)) * u
        o_ref[...] = _dot(a.astype(BF16), wd_scr[...]).astype(BF16)

    @pl.when(i >= nv_ref[0])
    def _():
        o_ref[...] = jnp.zeros(o_ref.shape, BF16)


def _experts(xs, tile_expert, n_valid, w_gate, w_up, w_down, layer):
    r, d = xs.shape
    de = w_gate.shape[3]
    n_tiles = r // TM_EXP
    x_map = lambda i, te, nv: (jnp.minimum(i, nv[0] - 1), 0)
    grid_spec = pltpu.PrefetchScalarGridSpec(
        num_scalar_prefetch=2,
        grid=(n_tiles,),
        in_specs=[
            pl.BlockSpec((TM_EXP, d), x_map),
            pl.BlockSpec((1, 1, d, de), lambda i, te, nv: (layer, te[i], 0, 0)),
            pl.BlockSpec((1, 1, d, de), lambda i, te, nv: (layer, te[i], 0, 0)),
            pl.BlockSpec((1, 1, de, d), lambda i, te, nv: (layer, te[i], 0, 0)),
        ],
        out_specs=pl.BlockSpec((TM_EXP, d), lambda i, te, nv: (i, 0)),
        scratch_shapes=[pltpu.VMEM((d, de), BF16), pltpu.VMEM((d, de), BF16), pltpu.VMEM((de, d), BF16)],
    )
    return pl.pallas_call(
        _experts_kernel,
        out_shape=jax.ShapeDtypeStruct((r, d), BF16),
        grid_spec=grid_spec,
        compiler_params=_cparams(("arbitrary",)),
        name="experts",
    )(tile_expert, n_valid, xs, w_gate, w_up, w_down)


def _dispatch(idx):
    n = idx.shape[1]
    e_flat = idx.reshape(-1)
    onehot = (e_flat[:, None] == jnp.arange(N_EXPERTS, dtype=jnp.int32)[None, :]).astype(jnp.int32)
    csum = jnp.cumsum(onehot, axis=0)
    rank = jnp.sum(csum * onehot, axis=1) - 1
    counts = csum[-1]
    padded = ((counts + TM_EXP - 1) // TM_EXP) * TM_EXP
    ends = jnp.cumsum(padded)
    starts = ends - padded
    dest = (jnp.sum(onehot * starts[None, :], axis=1) + rank).astype(jnp.int32)
    r_max = 2 * n + N_EXPERTS * TM_EXP
    tile_start = jnp.arange(r_max // TM_EXP, dtype=jnp.int32) * TM_EXP
    tile_expert = jnp.minimum(jnp.sum((tile_start[:, None] >= ends[None, :]).astype(jnp.int32), axis=1),
                              N_EXPERTS - 1).astype(jnp.int32)
    n_valid = (ends[-1] // TM_EXP).astype(jnp.int32).reshape(1)
    pad_start = (starts + counts).astype(jnp.int32)
    total_end = ends[-1].astype(jnp.int32).reshape(1)
    return dest, pad_start, total_end, tile_expert, n_valid


def _combine_kernel(h_ref, y1_ref, y2_ref, gate_ref, mod_ref, o_ref):
    d = D_MODEL
    g = gate_ref[...]
    y = g[:, 0:1] * y1_ref[...].astype(F32) + g[:, 1:2] * y2_ref[...].astype(F32)
    o_ref[...] = h_ref[...] + mod_ref[0, :, 5 * d:6 * d] * y


def _combine(h2d, y1, y2, gates_t, mod3, mod_map, row0):
    r, d = h2d.shape
    blk0 = row0 // TM_PROJ
    row = lambda i: (i, 0)
    seg = lambda i: (blk0 + i, 0)
    return pl.pallas_call(
        _combine_kernel,
        out_shape=jax.ShapeDtypeStruct((r, d), F32),
        grid=(r // TM_PROJ,),
        in_specs=[
            pl.BlockSpec((TM_PROJ, d), row),
            pl.BlockSpec((TM_PROJ, d), seg),
            pl.BlockSpec((TM_PROJ, d), seg),
            pl.BlockSpec((TM_PROJ, 2), seg),
            pl.BlockSpec((1, 1, N_MOD * d), mod_map),
        ],
        out_specs=pl.BlockSpec((TM_PROJ, d), row),
        compiler_params=_cparams(("parallel",)),
        name="moe_combine",
    )(h2d, y1, y2, gates_t, mod3)


def kernel(x, c, ctx, c_ctx, w_mod, b_mod, norm1_g, norm2_g, w_in, w_out, q_norm_g, k_norm_g, na_rpb,
           s5_lam_re, s5_lam_im, s5_log_dt, s5_b_re, s5_b_im, s5_c_re, s5_c_im, s5_d, s5_w_glu, s5_b_glu,
           router_w, router_bias, moe_w_gate, moe_w_up, moe_w_down):
    bsz, n_lat, d = x.shape
    n_ctx = ctx.shape[1]
    depth = w_mod.shape[0]
    ctx_row = bsz
    c_rows = jnp.concatenate([c.astype(F32), c_ctx.astype(F32)[None],
                              jnp.zeros((SUBLANES - bsz - 1, d), F32)], axis=0)
    mod_all = _modulation(c_rows, w_mod.astype(F32), b_mod.astype(F32))

    h_lat = x.reshape(bsz * n_lat, d).astype(F32)
    h_ctx = ctx.reshape(bsz * n_ctx, d).astype(F32)
    lat_map = _mod_row_map(n_lat, 0, True)
    ctx_map = _mod_row_map(n_ctx, ctx_row, False)
    bias_tabs = _na_bias_tables(na_rpb)
    s5_mats = _s5_matrices(s5_lam_re, s5_lam_im, s5_log_dt, s5_b_re, s5_b_im, s5_c_re, s5_c_im)

    for layer in range(depth):
        ctx_out = layer < depth - 1
        mod3 = mod_all[layer].reshape(SUBLANES, 1, N_MOD * d)
        w_in_b = w_in[layer].astype(BF16)
        qkvu_lat = _in_projection(h_lat, norm1_g[layer], mod3, w_in_b, lat_map)
        qkvu_ctx = _in_projection(h_ctx, norm1_g[layer], mod3, w_in_b, ctx_map)
        na_lat = _neighborhood_attention(qkvu_lat, qkvu_ctx, bias_tabs, layer, q_norm_g[layer],
                                         k_norm_g[layer], bsz)
        y_all = _s5_mixer(qkvu_lat, qkvu_ctx, s5_mats, layer, bsz)
        wglu_b = s5_w_glu[layer].astype(BF16)
        wout_b = w_out[layer].astype(BF16)
        h_lat, f_lat = _out_projection(na_lat, y_all, qkvu_lat, h_lat, mod3, s5_d[layer], wglu_b,
                                       s5_b_glu[layer], wout_b, norm2_g[layer],
                                       lat_map, n_lat, n_ctx // TM_PROJ)
        if ctx_out:
            na_ctx = _context_attention(qkvu_ctx, q_norm_g[layer], k_norm_g[layer], bsz)
            h_ctx, f_ctx = _out_projection(na_ctx, y_all, qkvu_ctx, h_ctx, mod3, s5_d[layer], wglu_b,
                                           s5_b_glu[layer], wout_b, norm2_g[layer],
                                           ctx_map, n_ctx, 0)
        n_l = bsz * n_lat
        idx, gates = _router(f_lat, router_w, router_bias)
        if ctx_out:
            idx_c, gates_c = _router(f_ctx, router_w, router_bias)
            idx = jnp.concatenate([idx, idx_c], axis=1)
            gates = jnp.concatenate([gates, gates_c], axis=1)
        n_tok = idx.shape[1]
        dest, pad_start, total_end, tile_expert, n_valid = _dispatch(idx)
        xs = _scatter_rows([f_lat, f_ctx] if ctx_out else [f_lat], dest, pad_start, total_end)
        ys = _experts(xs, tile_expert, n_valid, moe_w_gate, moe_w_up, moe_w_down, layer)
        y1 = jnp.take(ys, dest[:n_tok], axis=0, mode="clip")
        y2 = jnp.take(ys, dest[n_tok:], axis=0, mode="clip")
        gates_t = gates.T
        h_lat = _combine(h_lat, y1, y2, gates_t, mod3, lat_map, 0)
        if ctx_out:
            h_ctx = _combine(h_ctx, y1, y2, gates_t, mod3, ctx_map, n_l)
    return h_lat.reshape(bsz, n_lat, d).astype(x.dtype)
```

```python
import functools
import math

import jax
import jax.numpy as jnp
from jax import lax
from jax.experimental import pallas as pl
from jax.experimental.pallas import tpu as pltpu

F32 = jnp.float32
BF16 = jnp.bfloat16
HIGHEST = lax.Precision.HIGHEST

D_MODEL = 1024
GRID_W = 64
HEAD_DIM = 64
NA_WIDTH = 512
S5_WIDTH = 512
S5_CH = 16
S5_GROUPS = 32
S5_STATE = 64
WIN_ROWS = 8
WIN_COLS = 16
N_EXPERTS = 16
N_GROUPS = 4
EPG = 4
N_MOD = 6
EPS = 1e-6

LANES = 128
SUBLANES = 8
VMEM_LIMIT = 56 * 1024 * 1024

TM_PROJ = 256
Q_ROWS = 8
Q_COLS = 16
K_ROWS = 16
K_COLS = 32
CHUNK = 16
PAIRS_PER_STEP = 2
TM_EXP = 256
TM_ROUTE = 512
MASK_VALUE = -1e30
RPB_LANE0 = 48


def _cparams(sem):
    return pltpu.CompilerParams(dimension_semantics=sem, vmem_limit_bytes=VMEM_LIMIT)


def _dot(a, b):
    return jnp.dot(a, b, preferred_element_type=F32)


def _dot_nt(a, b):
    return lax.dot_general(a, b, (((1,), (1,)), ((), ())), preferred_element_type=F32)


def _mod_kernel(c_ref, w_ref, b_ref, o_ref):
    a = c_ref[...]
    a = a * jax.nn.sigmoid(a)
    o_ref[0] = jnp.dot(a, w_ref[0], precision=HIGHEST, preferred_element_type=F32) + b_ref[0]


def _modulation(c_rows, w_mod, b_mod):
    depth, d, n = w_mod.shape
    tn = 1536
    return pl.pallas_call(
        _mod_kernel,
        out_shape=jax.ShapeDtypeStruct((depth, SUBLANES, n), F32),
        grid=(depth, n // tn),
        in_specs=[
            pl.BlockSpec((SUBLANES, d), lambda l, j: (0, 0)),
            pl.BlockSpec((1, d, tn), lambda l, j: (l, 0, j)),
            pl.BlockSpec((1, 1, tn), lambda l, j: (l, 0, j)),
        ],
        out_specs=pl.BlockSpec((1, SUBLANES, tn), lambda l, j: (l, 0, j)),
        compiler_params=_cparams(("arbitrary", "arbitrary")),
        name="modulation",
    )(c_rows, w_mod, b_mod.reshape(depth, 1, n))


def _inproj_kernel(x_ref, g_ref, mod_ref, w_ref, qg_ref, kg_ref, o_ref):
    x = x_ref[...]
    ms = jnp.mean(x * x, axis=-1, keepdims=True)
    y = x * lax.rsqrt(ms + EPS) * g_ref[...]
    shift = mod_ref[0, :, 0:D_MODEL]
    scale = mod_ref[0, :, D_MODEL:2 * D_MODEL]
    a = y * (1.0 + scale) + shift
    acc = _dot(a.astype(BF16), w_ref[...])
    lo = lax.broadcasted_iota(jnp.int32, (1, LANES), 1) < HEAD_DIM
    n_pairs = NA_WIDTH // LANES
    for blk in range(2 * n_pairs):
        cols = slice(blk * LANES, (blk + 1) * LANES)
        if blk < n_pairs:
            o_ref[:, cols] = _pair_rms(acc[:, cols], qg_ref[...], lo) * (HEAD_DIM ** -0.5)
        else:
            o_ref[:, cols] = _pair_rms(acc[:, cols], kg_ref[...], lo)
    o_ref[:, 2 * NA_WIDTH:] = acc[:, 2 * NA_WIDTH:]


def _mod_row_map(rows_per_batch, mod_row0, per_batch):
    tiles_per_batch = rows_per_batch // TM_PROJ
    if per_batch:
        return lambda i: (mod_row0 + i // tiles_per_batch, 0, 0)
    return lambda i: (mod_row0, 0, 0)


def _in_projection(x2d, g, mod3, w_bf16, mod_map, qg, kg):
    r, d = x2d.shape
    n = w_bf16.shape[1]
    g2 = lambda v: jnp.concatenate([v, v]).reshape(1, LANES).astype(F32)
    return pl.pallas_call(
        _inproj_kernel,
        out_shape=jax.ShapeDtypeStruct((r, n), F32),
        grid=(r // TM_PROJ,),
        in_specs=[
            pl.BlockSpec((TM_PROJ, d), lambda i: (i, 0)),
            pl.BlockSpec((1, d), lambda i: (0, 0)),
            pl.BlockSpec((1, 1, N_MOD * d), mod_map),
            pl.BlockSpec((d, n), lambda i: (0, 0)),
            pl.BlockSpec((1, LANES), lambda i: (0, 0)),
            pl.BlockSpec((1, LANES), lambda i: (0, 0)),
        ],
        out_specs=pl.BlockSpec((TM_PROJ, n), lambda i: (i, 0)),
        compiler_params=_cparams(("parallel",)),
        name="in_projection",
    )(x2d, g.reshape(1, d), mod3, w_bf16, g2(qg), g2(kg))


def _pair_rms(x, g, lo):
    ss = x * x
    sa = jnp.sum(jnp.where(lo, ss, 0.0), axis=-1, keepdims=True)
    sb = jnp.sum(jnp.where(lo, 0.0, ss), axis=-1, keepdims=True)
    ms = jnp.where(lo, sa, sb) * (1.0 / HEAD_DIM)
    return x * lax.rsqrt(ms + EPS) * g


def _softmax_pv(qm, kw, kcb, vw, vcb, bias):
    s_nb = _dot_nt(qm, kw) + bias
    s_cx = _dot_nt(qm, kcb)
    m = jnp.maximum(jnp.max(s_nb, axis=-1, keepdims=True), jnp.max(s_cx, axis=-1, keepdims=True))
    p_nb = jnp.exp(s_nb - m)
    p_cx = jnp.exp(s_cx - m)
    l = jnp.sum(p_nb, axis=-1, keepdims=True) + jnp.sum(p_cx, axis=-1, keepdims=True)
    o = _dot(p_nb.astype(BF16), vw) + _dot(p_cx.astype(BF16), vcb)
    return o / l


def _na_kernel(q_ref, k_ref, v_ref, kc_ref, vc_ref, bias_ref, o_ref):
    i = pl.program_id(2)
    lo = lax.broadcasted_iota(jnp.int32, (1, LANES), 1) < HEAD_DIM
    n_rows = k_ref.shape[1]
    col_tiles = GRID_W // Q_COLS
    kr0 = jnp.clip(Q_ROWS * i - WIN_ROWS // 2, 0, n_rows - K_ROWS)
    rt = jnp.where(i == 0, 0, jnp.where(i == n_rows // Q_ROWS - 1, 2, 1))
    nq = Q_ROWS * Q_COLS
    nk = K_ROWS * K_COLS
    kcb = kc_ref[0].astype(BF16)
    vcb = vc_ref[0].astype(BF16)
    for j in range(col_tiles):
        kc0 = min(max(Q_COLS * j - WIN_COLS // 2, 0), GRID_W - K_COLS)
        ct = 0 if j == 0 else (2 if j == col_tiles - 1 else 1)
        typ = rt * 3 + ct
        cols = slice(j * Q_COLS, (j + 1) * Q_COLS)
        qn = q_ref[0, :, cols, :].reshape(nq, LANES)
        kw = k_ref[0, pl.ds(kr0, K_ROWS), kc0:kc0 + K_COLS, :].reshape(nk, LANES).astype(BF16)
        vw = v_ref[0, pl.ds(kr0, K_ROWS), kc0:kc0 + K_COLS, :].reshape(nk, LANES).astype(BF16)
        o_a = _softmax_pv(jnp.where(lo, qn, 0.0).astype(BF16), kw, kcb, vw, vcb, bias_ref[0, 0, typ, 0])
        o_b = _softmax_pv(jnp.where(lo, 0.0, qn).astype(BF16), kw, kcb, vw, vcb, bias_ref[0, 0, typ, 1])
        o_ref[0, :, cols, :] = jnp.where(lo, o_a, o_b).reshape(Q_ROWS, Q_COLS, LANES)


def _bias_table_kernel(rpb_ref, o_ref, tt_scr):
    n_off_r = 2 * WIN_ROWS - 1
    lane = lax.broadcasted_iota(jnp.int32, (Q_COLS, LANES), 1)
    qc = lax.broadcasted_iota(jnp.int32, (Q_COLS, LANES), 0)
    kc = lane % K_COLS
    lane_blk = lane // K_COLS
    per_vreg = LANES // K_COLS
    col_rel = (0, -WIN_COLS // 2, -WIN_COLS)
    col_origin = (0, Q_COLS, GRID_W - Q_COLS)
    row_rel = (0, -WIN_ROWS // 2, -WIN_ROWS)
    row_origin = (0, Q_ROWS, GRID_W - Q_ROWS)
    masked = jnp.full((Q_COLS, LANES), MASK_VALUE, F32)

    for ct in range(3):
        c_abs = col_origin[ct] + qc
        k_abs = col_origin[ct] + col_rel[ct] + kc
        start = jnp.clip(c_abs - WIN_COLS // 2, 0, GRID_W - WIN_COLS)
        valid_c = (k_abs >= start) & (k_abs < start + WIN_COLS)
        base = (1 - WIN_COLS - col_rel[ct] - RPB_LANE0) % LANES
        for ro in range(n_off_r):
            row = jnp.broadcast_to(rpb_ref[0, 0, ro:ro + 1, :], (Q_COLS, LANES))
            t = pltpu.roll(row, base, 1, stride=1, stride_axis=0)
            rep = t
            for m in range(1, per_vreg):
                rep = jnp.where(lane_blk == m, pltpu.roll(t, K_COLS * m, 1), rep)
            tt_scr[ct, ro] = jnp.where(valid_c, rep, MASK_VALUE)

    for rt in range(3):
        for ct in range(3):
            for qr in range(Q_ROWS):
                r_abs = row_origin[rt] + qr
                r_start = min(max(r_abs - WIN_ROWS // 2, 0), GRID_W - WIN_ROWS)
                for w in range(K_ROWS // per_vreg):
                    val = None
                    for m in range(per_vreg):
                        k_abs = row_origin[rt] + row_rel[rt] + per_vreg * w + m
                        ok = r_start <= k_abs < r_start + WIN_ROWS
                        src = tt_scr[ct, k_abs - r_abs + WIN_ROWS - 1] if ok else masked
                        val = src if val is None else jnp.where(lane_blk == m, src, val)
                    o_ref[0, 0, rt * 3 + ct, 0, qr * Q_COLS:(qr + 1) * Q_COLS, w * LANES:(w + 1) * LANES] = val


def _na_bias_tables(na_rpb):
    depth, h, n_r, n_c = na_rpb.shape
    rpb_pad = jnp.pad(na_rpb.astype(F32), ((0, 0), (0, 0), (0, 2 * SUBLANES - n_r),
                                            (RPB_LANE0, LANES - RPB_LANE0 - n_c)))
    nq, nk = Q_ROWS * Q_COLS, K_ROWS * K_COLS
    return pl.pallas_call(
        _bias_table_kernel,
        out_shape=jax.ShapeDtypeStruct((depth, h // 2, 9, 2, nq, nk), F32),
        grid=(depth, h),
        in_specs=[pl.BlockSpec((1, 1, 2 * SUBLANES, LANES), lambda l, i: (l, i, 0, 0))],
        out_specs=pl.BlockSpec((1, 1, 9, 1, nq, nk), lambda l, i: (l, i // 2, 0, i % 2, 0, 0)),
        scratch_shapes=[pltpu.VMEM((3, 2 * WIN_ROWS - 1, Q_COLS, LANES), F32)],
        compiler_params=_cparams(("parallel", "parallel")),
        name="na_bias_tables",
    )(rpb_pad)


def _neighborhood_attention(qkvu_lat, qkvu_ctx, bias_tabs, layer, bsz):
    n_lat = qkvu_lat.shape[0] // bsz
    n_ctx = qkvu_ctx.shape[0] // bsz
    rows = n_lat // GRID_W
    n_cols = qkvu_lat.shape[1]
    lat4 = qkvu_lat.reshape(bsz, rows, GRID_W, n_cols)
    ctx3 = qkvu_ctx.reshape(bsz, n_ctx, n_cols)
    n_pairs = NA_WIDTH // LANES
    n_tiles = rows // Q_ROWS
    out = pl.pallas_call(
        _na_kernel,
        out_shape=jax.ShapeDtypeStruct((bsz, rows, GRID_W, NA_WIDTH), F32),
        grid=(n_pairs, bsz, n_tiles),
        in_specs=[
            pl.BlockSpec((1, Q_ROWS, GRID_W, LANES), lambda p, b, t: (b, t, 0, p)),
            pl.BlockSpec((1, rows, GRID_W, LANES), lambda p, b, t: (b, 0, 0, n_pairs + p)),
            pl.BlockSpec((1, rows, GRID_W, LANES), lambda p, b, t: (b, 0, 0, 2 * n_pairs + p)),
            pl.BlockSpec((1, n_ctx, LANES), lambda p, b, t: (b, 0, n_pairs + p)),
            pl.BlockSpec((1, n_ctx, LANES), lambda p, b, t: (b, 0, 2 * n_pairs + p)),
            pl.BlockSpec((1, 1, 9, 2, Q_ROWS * Q_COLS, K_ROWS * K_COLS), lambda p, b, t: (layer, p, 0, 0, 0, 0)),
        ],
        out_specs=pl.BlockSpec((1, Q_ROWS, GRID_W, LANES), lambda p, b, t: (b, t, 0, p)),
        compiler_params=_cparams(("parallel", "parallel", "parallel")),
        name="neighborhood_attention",
    )(lat4, lat4, lat4, ctx3, ctx3, bias_tabs)
    return out.reshape(bsz * n_lat, NA_WIDTH)


def _ctx_attn_kernel(q_ref, k_ref, v_ref, o_ref):
    lo = lax.broadcasted_iota(jnp.int32, (1, LANES), 1) < HEAD_DIM
    qn = q_ref[0]
    kn = k_ref[0].astype(BF16)
    vb = v_ref[0].astype(BF16)

    def one(qm):
        s = _dot_nt(qm, kn)
        m = jnp.max(s, axis=-1, keepdims=True)
        p = jnp.exp(s - m)
        l = jnp.sum(p, axis=-1, keepdims=True)
        return _dot(p.astype(BF16), vb) / l

    o_a = one(jnp.where(lo, qn, 0.0).astype(BF16))
    o_b = one(jnp.where(lo, 0.0, qn).astype(BF16))
    o_ref[0] = jnp.where(lo, o_a, o_b)


def _context_attention(qkvu_ctx, bsz):
    n_ctx = qkvu_ctx.shape[0] // bsz
    ctx3 = qkvu_ctx.reshape(bsz, n_ctx, qkvu_ctx.shape[1])
    n_pairs = NA_WIDTH // LANES
    out = pl.pallas_call(
        _ctx_attn_kernel,
        out_shape=jax.ShapeDtypeStruct((bsz, n_ctx, NA_WIDTH), F32),
        grid=(bsz, n_pairs),
        in_specs=[
            pl.BlockSpec((1, n_ctx, LANES), lambda b, p: (b, 0, p)),
            pl.BlockSpec((1, n_ctx, LANES), lambda b, p: (b, 0, n_pairs + p)),
            pl.BlockSpec((1, n_ctx, LANES), lambda b, p: (b, 0, 2 * n_pairs + p)),
        ],
        out_specs=pl.BlockSpec((1, n_ctx, LANES), lambda b, p: (b, 0, p)),
        compiler_params=_cparams(("parallel", "parallel")),
        name="context_attention",
    )(ctx3, ctx3, ctx3)
    return out.reshape(bsz * n_ctx, NA_WIDTH)


def _s5_kernel(uc_ref, ul_ref, w_ref, m_ref, v_ref, a_ref, y_ref, x_scr, s_scr, hf_scr, hr_scr, *, n_ctx_chunks):
    bsz = ul_ref.shape[0]
    n_lat_chunks = ul_ref.shape[1] // CHUNK
    n_chunks = n_ctx_chunks + n_lat_chunks
    rows = n_chunks * SUBLANES
    n_pairs = w_ref.shape[1]
    gpb = 2 * n_pairs
    half = 2 * LANES
    tile_chunks = TM_PROJ // CHUNK
    lane_blk = lax.broadcasted_iota(jnp.int32, (n_chunks, LANES), 1) // S5_CH

    for b in range(bsz):
        for q in range(2):
            rolled = []
            for j in range(SUBLANES):
                s = SUBLANES * q + j
                u_s = jnp.concatenate([uc_ref[b, pl.ds(s, n_ctx_chunks, stride=CHUNK), :],
                                       ul_ref[b, pl.ds(s, n_lat_chunks, stride=CHUNK), :]], axis=0)
                rolled.append(u_s if j == 0 else pltpu.roll(u_s, j * S5_CH, 1))
            for g in range(gpb):
                xg = rolled[0]
                for j in range(1, SUBLANES):
                    xg = jnp.where(lane_blk == (g + j) % SUBLANES, rolled[j], xg)
                x_scr[g // 2, q, pl.ds((g % 2) * bsz + b, n_chunks, stride=SUBLANES), :] = xg

    n_blk = 8
    rb = rows // n_blk
    first_group = (lax.broadcasted_iota(jnp.int32, (rb, half), 0) & (SUBLANES // 2)) == 0
    fwd_cols = (lax.broadcasted_iota(jnp.int32, (rb, half), 1) & (LANES - 1)) < S5_STATE
    is_fwd = lax.broadcasted_iota(jnp.int32, (SUBLANES, LANES), 1) < S5_STATE
    first_rows = lax.broadcasted_iota(jnp.int32, (SUBLANES, half), 0) < SUBLANES // 2
    zero = jnp.zeros((SUBLANES, LANES), F32)

    def x_rows(p, sl):
        return jnp.concatenate([x_scr[p, 0, sl, :], x_scr[p, 1, sl, :]], axis=1).astype(BF16)

    def put_cols(scr, i, sl, val):
        scr[i, 0, sl, :] = val[:, 0:LANES]
        scr[i, 1, sl, :] = val[:, LANES:half]

    for hh in range(n_pairs // PAIRS_PER_STEP):
        pairs = [hh * PAIRS_PER_STEP + i for i in range(PAIRS_PER_STEP)]
        for i, p in enumerate(pairs):
            for blk in range(n_blk):
                sl = slice(blk * rb, (blk + 1) * rb)
                r = _dot(x_rows(p, sl), w_ref[0, p])
                put_cols(s_scr, i, sl, jnp.where(first_group, r[:, :half], r[:, half:]))

        a_pair = [jnp.where(first_rows, a_ref[0, 2 * p], a_ref[0, 2 * p + 1]) for p in pairs]
        a_re = [a[:, 0:LANES] for a in a_pair]
        a_im = [a[:, LANES:half] for a in a_pair]

        def body(k, carry):
            kr = jnp.where(k < n_ctx_chunks, n_ctx_chunks - 1 - k, n_chunks + n_ctx_chunks - 1 - k)
            rf = pl.ds(pl.multiple_of(k * SUBLANES, SUBLANES), SUBLANES)
            rr = pl.ds(pl.multiple_of(kr * SUBLANES, SUBLANES), SUBLANES)
            new = []
            for i in range(PAIRS_PER_STEP):
                h_re, h_im = carry[2 * i], carry[2 * i + 1]
                hf_scr[i, 0, rf, :] = h_re
                hf_scr[i, 1, rf, :] = h_im
                hr_scr[i, 0, rr, :] = h_re
                hr_scr[i, 1, rr, :] = h_im
                s_re = jnp.where(is_fwd, s_scr[i, 0, rf, :], s_scr[i, 0, rr, :])
                s_im = jnp.where(is_fwd, s_scr[i, 1, rf, :], s_scr[i, 1, rr, :])
                new.append(a_re[i] * h_re - a_im[i] * h_im + s_re)
                new.append(a_re[i] * h_im + a_im[i] * h_re + s_im)
            return tuple(new)

        lax.fori_loop(0, n_chunks, body, (zero,) * (2 * PAIRS_PER_STEP))

        for i, p in enumerate(pairs):
            for blk in range(n_blk):
                sl = slice(blk * rb, (blk + 1) * rb)
                hf = jnp.concatenate([hf_scr[i, 0, sl, :], hf_scr[i, 1, sl, :]], axis=1)
                hr = jnp.concatenate([hr_scr[i, 0, sl, :], hr_scr[i, 1, sl, :]], axis=1)
                h_in = jnp.where(fwd_cols, hf, hr).astype(BF16)
                r = _dot(x_rows(p, sl), m_ref[0, p]) + _dot_nt(h_in, v_ref[0, p])
                put_cols(s_scr, i, sl, jnp.where(first_group, r[:, :half], r[:, half:]))

        g_lo = 2 * pairs[0]
        n_g = 2 * PAIRS_PER_STEP
        out_blk = lax.broadcasted_iota(jnp.int32, (tile_chunks, LANES), 1) // S5_CH
        lanes_out = (out_blk >= g_lo) & (out_blk < g_lo + n_g)
        for b in range(bsz):
            for q in range(2):
                y_g = [s_scr[(g - g_lo) // 2, q, pl.ds((g % 2) * bsz + b, n_chunks, stride=SUBLANES), :]
                       for g in range(g_lo, g_lo + n_g)]
                for j in range(SUBLANES):
                    z = y_g[0]
                    for gi in range(1, n_g):
                        z = jnp.where(lane_blk == (g_lo + gi + j) % SUBLANES, y_g[gi], z)
                    if j:
                        z = pltpu.roll(z, LANES - j * S5_CH, 1)
                    t = SUBLANES * q + j
                    for ct in range(n_chunks // tile_chunks):
                        r0 = ct * TM_PROJ + t * tile_chunks
                        pltpu.store(y_ref.at[b, 0, r0:r0 + tile_chunks, :],
                                    z[ct * tile_chunks:(ct + 1) * tile_chunks, :], mask=lanes_out)


def _s5_scan(qkvu_ctx3, qkvu_lat3, mats, layer):
    w_c, m_c, vt_c, a_c = mats
    bsz, n_ctx, _ = qkvu_ctx3.shape
    n_lat = qkvu_lat3.shape[1]
    n_seq = n_ctx + n_lat
    rows = n_seq // CHUNK * SUBLANES
    n_blocks = S5_WIDTH // LANES
    ppb = w_c.shape[1] // n_blocks
    u_blk0 = 3 * NA_WIDTH // LANES
    wspec = pl.BlockSpec((1, ppb, 2 * LANES, 4 * LANES), lambda i: (layer, i, 0, 0))
    one = pl.Buffered(1)
    state = pltpu.VMEM((PAIRS_PER_STEP, 2, rows, LANES), F32)
    return pl.pallas_call(
        functools.partial(_s5_kernel, n_ctx_chunks=n_ctx // CHUNK),
        out_shape=jax.ShapeDtypeStruct((bsz, n_blocks, n_seq, LANES), F32),
        grid=(n_blocks,),
        in_specs=[
            pl.BlockSpec((bsz, n_ctx, LANES), lambda i: (0, 0, u_blk0 + i)),
            pl.BlockSpec((bsz, n_lat, LANES), lambda i: (0, 0, u_blk0 + i), pipeline_mode=one),
            wspec, wspec,
            pl.BlockSpec((1, ppb, 4 * LANES, 2 * LANES), lambda i: (layer, i, 0, 0)),
            pl.BlockSpec((1, 2 * ppb, SUBLANES, 2 * LANES), lambda i: (layer, i, 0, 0)),
        ],
        out_specs=pl.BlockSpec((bsz, 1, n_seq, LANES), lambda i: (0, i, 0, 0), pipeline_mode=one),
        scratch_shapes=[pltpu.VMEM((ppb, 2, rows, LANES), F32), state, state, state],
        compiler_params=_cparams(("parallel",)),
        name="s5_scan",
    )(qkvu_ctx3, qkvu_lat3, w_c, m_c, vt_c, a_c)


def _s5_mats_kernel(prm_ref, btr_ref, bti_ref, cr_ref, ci_ref, w_ref, m_ref, vt_ref, a_ref):
    t = CHUNK
    gl = pl.program_id(1) % SUBLANES
    is_fwd = lax.broadcasted_iota(jnp.int32, (1, LANES), 1) < S5_STATE
    lr = prm_ref[0, 0, 0:1, :]
    li = prm_ref[0, 0, 1:2, :]
    dt = jnp.exp(prm_ref[0, 0, 2:3, :])
    n = lax.broadcasted_iota(jnp.int32, (3 * SUBLANES, LANES), 0).astype(F32)
    pmag = jnp.exp(n * (lr * dt))
    pw_re = pmag * jnp.cos(n * (li * dt))
    pw_im = pmag * jnp.sin(n * (li * dt))
    ab_re, ab_im = pw_re[1:2, :], pw_im[1:2, :]
    den = lr * lr + li * li
    nr = ab_re - 1.0
    z_re = (nr * lr + ab_im * li) / den
    z_im = (ab_im * lr - nr * li) / den
    bt_re, bt_im = btr_ref[0, 0], bti_ref[0, 0]
    bb_re = z_re * bt_re - z_im * bt_im
    bb_im = z_re * bt_im + z_im * bt_re
    c_re, c_im = cr_ref[0, 0], ci_ref[0, 0]

    def powers(n_fwd, n_rev):
        return (jnp.where(is_fwd, pw_re[n_fwd:n_fwd + 1, :], pw_re[n_rev:n_rev + 1, :]),
                jnp.where(is_fwd, pw_im[n_fwd:n_fwd + 1, :], pw_im[n_rev:n_rev + 1, :]))

    def block_rows(s):
        pos = SUBLANES * (s // SUBLANES) + (s % SUBLANES + gl) % SUBLANES
        return pl.ds(pl.multiple_of(pos * S5_CH, S5_CH), S5_CH)

    for s in range(t):
        rows = block_rows(s)
        p_re, p_im = powers(t - 1 - s, s)
        w_ref[0, 0, rows, 0:LANES] = (bb_re * p_re - bb_im * p_im).astype(BF16)
        w_ref[0, 0, rows, LANES:2 * LANES] = (bb_re * p_im + bb_im * p_re).astype(BF16)
        q_re, q_im = powers(s + 1, t - s)
        vt_ref[0, 0, rows, 0:LANES] = (c_re * q_re - c_im * q_im).astype(BF16)
        vt_ref[0, 0, rows, LANES:2 * LANES] = (-(c_re * q_im + c_im * q_re)).astype(BF16)

    ca_re, ca_im = [], []
    for lag in range(t):
        p_re, p_im = powers(lag, t - 1 - lag)
        ca_re.append(c_re * p_re - c_im * p_im)
        ca_im.append(c_re * p_im + c_im * p_re)
    stack = jnp.concatenate([jnp.concatenate(ca_re, axis=0), jnp.concatenate(ca_im, axis=0)], axis=1)
    zero = jnp.zeros_like(bb_re)
    lhs = jnp.concatenate([
        jnp.concatenate([jnp.where(is_fwd, bb_re, zero), jnp.where(is_fwd, -bb_im, zero)], axis=1),
        jnp.concatenate([jnp.where(is_fwd, zero, bb_re), jnp.where(is_fwd, zero, -bb_im)], axis=1)], axis=0)
    kt = lax.dot_general(lhs, stack, (((1,), (1,)), ((), ())), precision=HIGHEST, preferred_element_type=F32)
    kt_f, kt_r = kt[0:S5_CH], kt[S5_CH:2 * S5_CH]
    blk = lax.broadcasted_iota(jnp.int32, (S5_CH, 2 * LANES), 1) // S5_CH
    for s in range(t):
        strip = (jnp.where(blk >= s, pltpu.roll(kt_f, S5_CH * s, 1), 0.0)
                 + jnp.where(blk <= s, pltpu.roll(kt_r, (S5_CH * (s - t + 1)) % (2 * LANES), 1), 0.0))
        strip = jnp.concatenate([pltpu.roll(strip[:, 0:LANES], gl * S5_CH, 1),
                                 pltpu.roll(strip[:, LANES:2 * LANES], gl * S5_CH, 1)], axis=1)
        m_ref[0, 0, block_rows(s), :] = strip.astype(BF16)

    a_ref[0, 0, :, 0:LANES] = jnp.broadcast_to(pw_re[t:t + 1, :], (SUBLANES, LANES))
    a_ref[0, 0, :, LANES:2 * LANES] = jnp.broadcast_to(pw_im[t:t + 1, :], (SUBLANES, LANES))


def _s5_matrices(lam_re, lam_im, log_dt, b_re, b_im, c_re, c_im):
    depth, _, g, p = lam_re.shape
    hc = b_re.shape[-1]
    width = CHUNK * hc
    both = lambda x: jnp.transpose(x.astype(F32), (0, 2, 1, 3)).reshape(depth, g, 1, 2 * p)
    dt_rows = jnp.broadcast_to(jnp.transpose(log_dt.astype(F32), (0, 2, 1))[..., None], (depth, g, 2, p))
    prm = jnp.concatenate([both(lam_re), both(lam_im), dt_rows.reshape(depth, g, 1, 2 * p),
                           jnp.zeros((depth, g, SUBLANES - 3, 2 * p), F32)], axis=2)
    bt = lambda x: jnp.transpose(x.astype(F32), (0, 2, 4, 1, 3)).reshape(depth, g, hc, 2 * p)
    ct = lambda x: jnp.transpose(x.astype(F32), (0, 2, 3, 1, 4)).reshape(depth, g, hc, 2 * p)
    vec = lambda rows: pl.BlockSpec((1, 1, rows, 2 * p), lambda l, i: (l, i, 0, 0))
    return pl.pallas_call(
        _s5_mats_kernel,
        out_shape=(jax.ShapeDtypeStruct((depth, g // 2, width, 2 * width), BF16),
                   jax.ShapeDtypeStruct((depth, g // 2, width, 2 * width), BF16),
                   jax.ShapeDtypeStruct((depth, g // 2, 2 * width, width), BF16),
                   jax.ShapeDtypeStruct((depth, g, SUBLANES, 4 * p), F32)),
        grid=(depth, g),
        in_specs=[vec(SUBLANES), vec(hc), vec(hc), vec(hc), vec(hc)],
        out_specs=(pl.BlockSpec((1, 1, width, width), lambda l, i: (l, i // 2, 0, i % 2)),
                   pl.BlockSpec((1, 1, width, width), lambda l, i: (l, i // 2, 0, i % 2)),
                   pl.BlockSpec((1, 1, width, width), lambda l, i: (l, i // 2, i % 2, 0)),
                   pl.BlockSpec((1, 1, SUBLANES, 4 * p), lambda l, i: (l, i, 0, 0))),
        compiler_params=_cparams(("parallel", "parallel")),
        name="s5_matrices",
    )(prm, bt(b_re), bt(b_im), ct(c_re), ct(c_im))


def _s5_mixer(qkvu_lat, qkvu_ctx, mats, layer, bsz):
    n_cols = qkvu_lat.shape[1]
    return _s5_scan(qkvu_ctx.reshape(bsz, -1, n_cols), qkvu_lat.reshape(bsz, -1, n_cols), mats, layer)


def _outproj_kernel(na_ref, y_ref, u_ref, h_ref, mod_ref, d_ref, wglu_ref, bglu_ref, wout_ref, g2_ref,
                    ho_ref, f_ref):
    tile_chunks = TM_PROJ // CHUNK
    y = jnp.concatenate(
        [jnp.concatenate([y_ref[0, blk, pl.ds(c, CHUNK, stride=tile_chunks), :] for c in range(tile_chunks)], axis=0)
         for blk in range(S5_WIDTH // LANES)], axis=1)
    z = jax.nn.gelu(y + d_ref[...] * u_ref[...])
    s5 = z * jax.nn.sigmoid(_dot(z.astype(BF16), wglu_ref[...]) + bglu_ref[...])
    mix = (_dot(na_ref[...].astype(BF16), wout_ref[0:NA_WIDTH, :])
           + _dot(s5.astype(BF16), wout_ref[NA_WIDTH:NA_WIDTH + S5_WIDTH, :]))
    d = D_MODEL
    gate = mod_ref[0, :, 2 * d:3 * d]
    h = h_ref[...] + gate * mix
    ho_ref[...] = h
    ms = jnp.mean(h * h, axis=-1, keepdims=True)
    y2 = h * lax.rsqrt(ms + EPS) * g2_ref[...]
    f_ref[...] = y2 * (1.0 + mod_ref[0, :, 4 * d:5 * d]) + mod_ref[0, :, 3 * d:4 * d]


def _out_projection(na, y_all, qkvu, h2d, mod3, d_skip, wglu_bf16, b_glu, wout_bf16, g2,
                    mod_map, rows_per_batch, y_block0):
    r, d = h2d.shape
    tiles_per_batch = rows_per_batch // TM_PROJ
    u_blk = 3 * NA_WIDTH // S5_WIDTH

    def y_map(i):
        return (i // tiles_per_batch, 0, y_block0 + i % tiles_per_batch, 0)

    const = lambda i: (0, 0)
    return pl.pallas_call(
        _outproj_kernel,
        out_shape=(jax.ShapeDtypeStruct((r, d), F32), jax.ShapeDtypeStruct((r, d), F32)),
        grid=(r // TM_PROJ,),
        in_specs=[
            pl.BlockSpec((TM_PROJ, NA_WIDTH), lambda i: (i, 0)),
            pl.BlockSpec((1, S5_WIDTH // LANES, TM_PROJ, LANES), y_map),
            pl.BlockSpec((TM_PROJ, S5_WIDTH), lambda i: (i, u_blk)),
            pl.BlockSpec((TM_PROJ, d), lambda i: (i, 0)),
            pl.BlockSpec((1, 1, N_MOD * d), mod_map),
            pl.BlockSpec((1, S5_WIDTH), const),
            pl.BlockSpec((S5_WIDTH, S5_WIDTH), const),
            pl.BlockSpec((1, S5_WIDTH), const),
            pl.BlockSpec((NA_WIDTH + S5_WIDTH, d), const),
            pl.BlockSpec((1, d), const),
        ],
        out_specs=(pl.BlockSpec((TM_PROJ, d), lambda i: (i, 0)), pl.BlockSpec((TM_PROJ, d), lambda i: (i, 0))),
        compiler_params=_cparams(("parallel",)),
        name="out_projection",
    )(na, y_all, qkvu, h2d, mod3, d_skip.reshape(1, -1), wglu_bf16, b_glu.reshape(1, -1), wout_bf16,
      g2.reshape(1, d))


def _top2(vals):
    best = vals[0]
    bi = jnp.zeros(best.shape, jnp.int32)
    for i in range(1, len(vals)):
        gt = vals[i] > best
        best = jnp.where(gt, vals[i], best)
        bi = jnp.where(gt, i, bi)
    second = jnp.full(best.shape, -jnp.inf, F32)
    si = jnp.zeros(best.shape, jnp.int32)
    for i in range(len(vals)):
        cand = jnp.where(bi == i, -jnp.inf, vals[i])
        gt = cand > second
        second = jnp.where(gt, cand, second)
        si = jnp.where(gt, i, si)
    return best, bi, second, si


def _router_kernel(f_ref, rwt_ref, rb_ref, idx_ref, gate_ref):
    logits = lax.dot_general(rwt_ref[...], f_ref[...], (((1,), (1,)), ((), ())),
                             precision=HIGHEST, preferred_element_type=F32)
    m = jnp.max(logits, axis=0, keepdims=True)
    e = jnp.exp(logits - m)
    probs = e / jnp.sum(e, axis=0, keepdims=True)
    sel = probs + rb_ref[...]
    sel_rows = [sel[i:i + 1, :] for i in range(N_EXPERTS)]
    prob_rows = [probs[i:i + 1, :] for i in range(N_EXPERTS)]
    scores = []
    for g in range(N_GROUPS):
        b, _, s, _ = _top2(sel_rows[g * EPG:(g + 1) * EPG])
        scores.append(b + s)
    grp = jnp.zeros(scores[0].shape, jnp.int32)
    gbest = scores[0]
    for g in range(1, N_GROUPS):
        gt = scores[g] > gbest
        gbest = jnp.where(gt, scores[g], gbest)
        grp = jnp.where(gt, g, grp)
    in_rows = []
    for j in range(EPG):
        v = sel_rows[j]
        for g in range(1, N_GROUPS):
            v = jnp.where(grp == g, sel_rows[g * EPG + j], v)
        in_rows.append(v)
    _, l1, _, l2 = _top2(in_rows)
    i1 = grp * EPG + l1
    i2 = grp * EPG + l2
    w1 = jnp.zeros(gbest.shape, F32)
    w2 = jnp.zeros(gbest.shape, F32)
    for i in range(N_EXPERTS):
        w1 = jnp.where(i1 == i, prob_rows[i], w1)
        w2 = jnp.where(i2 == i, prob_rows[i], w2)
    tot = w1 + w2
    idx_ref[0:1, :] = i1
    idx_ref[1:2, :] = i2
    gate_ref[0:1, :] = w1 / tot
    gate_ref[1:2, :] = w2 / tot


def _router(f_all, router_w, router_bias):
    n, d = f_all.shape
    return pl.pallas_call(
        _router_kernel,
        out_shape=(jax.ShapeDtypeStruct((2, n), jnp.int32), jax.ShapeDtypeStruct((2, n), F32)),
        grid=(n // TM_ROUTE,),
        in_specs=[
            pl.BlockSpec((TM_ROUTE, d), lambda i: (i, 0)),
            pl.BlockSpec((N_EXPERTS, d), lambda i: (0, 0)),
            pl.BlockSpec((N_EXPERTS, 1), lambda i: (0, 0)),
        ],
        out_specs=(pl.BlockSpec((2, TM_ROUTE), lambda i: (0, i)), pl.BlockSpec((2, TM_ROUTE), lambda i: (0, i))),
        compiler_params=_cparams(("parallel",)),
        name="router",
    )(f_all, router_w.T.astype(F32), router_bias.reshape(N_EXPERTS, 1).astype(F32))


def _scatter_rows_kernel(dest_ref, pad_ref, end_ref, *refs, n_tok, seg_tiles):
    f_refs = refs[:len(seg_tiles)]
    xs_ref, zero_scr, sem = refs[len(seg_tiles):]
    i = pl.program_id(0)
    tm = f_refs[0].shape[0]
    n_rows = xs_ref.shape[0]
    slab = TM_EXP + SUBLANES

    def slab_copy(start, rows):
        return pltpu.make_async_copy(zero_scr.at[pl.ds(0, rows), :], xs_ref.at[pl.ds(start, rows), :], sem)

    @pl.when(i == 0)
    def _():
        zero_scr[...] = jnp.zeros(zero_scr.shape, zero_scr.dtype)
        for e in range(N_EXPERTS):
            start = jnp.minimum((pad_ref[e] // SUBLANES) * SUBLANES, n_rows - slab)
            slab_copy(pl.multiple_of(start, SUBLANES), slab).start()
        for e in range(N_EXPERTS):
            slab_copy(0, slab).wait()
        for k in range(N_EXPERTS):
            start = end_ref[0] + k * TM_EXP

            @pl.when(start < n_rows)
            def _():
                cp = slab_copy(pl.multiple_of(start, TM_EXP), TM_EXP)
                cp.start()
                cp.wait()

    def scatter_tile(f_ref):
        base = i * tm

        def row_copy(r, d):
            return pltpu.make_async_copy(f_ref.at[pl.ds(r, 1), :], xs_ref.at[pl.ds(d, 1), :], sem)

        def body(r, carry):
            row_copy(r, dest_ref[base + r]).start(priority=0)
            row_copy(r, dest_ref[n_tok + base + r]).start(priority=1)
            return carry
        lax.fori_loop(0, tm, body, 0, unroll=8)
        for _ in range(2):
            pltpu.make_async_copy(f_ref, xs_ref.at[pl.ds(0, tm), :], sem).wait()

    tile0 = 0
    for f_ref, n_t in zip(f_refs, seg_tiles):
        pl.when((i >= tile0) & (i < tile0 + n_t))(functools.partial(scatter_tile, f_ref))
        tile0 += n_t


def _scatter_rows(segments, dest_flat, pad_start, total_end):
    d = segments[0].shape[1]
    seg_tiles = tuple(s.shape[0] // TM_PROJ for s in segments)
    n_tok = sum(s.shape[0] for s in segments)
    r_max = 2 * n_tok + N_EXPERTS * TM_EXP
    in_specs = []
    tile0 = 0
    for n_t in seg_tiles:
        in_specs.append(pl.BlockSpec(
            (TM_PROJ, d), lambda i, *_, t0=tile0, nt=n_t: (jnp.clip(i - t0, 0, nt - 1), 0)))
        tile0 += n_t
    grid_spec = pltpu.PrefetchScalarGridSpec(
        num_scalar_prefetch=3,
        grid=(tile0,),
        in_specs=in_specs,
        out_specs=pl.BlockSpec(memory_space=pl.ANY),
        scratch_shapes=[pltpu.VMEM((TM_EXP + SUBLANES, d), F32), pltpu.SemaphoreType.DMA(())],
    )
    return pl.pallas_call(
        functools.partial(_scatter_rows_kernel, n_tok=n_tok, seg_tiles=seg_tiles),
        out_shape=jax.ShapeDtypeStruct((r_max, d), F32),
        grid_spec=grid_spec,
        compiler_params=_cparams(("arbitrary",)),
        name="moe_scatter_rows",
    )(dest_flat, pad_start, total_end, *segments)


def _experts_kernel(te_ref, nv_ref, x_ref, wg_ref, wu_ref, wd_ref, o_ref, wg_scr, wu_scr, wd_scr):
    i = pl.program_id(0)
    e = te_ref[i]
    prev = te_ref[jnp.maximum(i - 1, 0)]
    rows = 128

    @pl.when((i == 0) | (e != prev))
    def _():
        def body(r, carry):
            sl = pl.ds(pl.multiple_of(r * rows, rows), rows)
            wg_scr[sl, :] = wg_ref[0, 0, sl, :].astype(BF16)
            wu_scr[sl, :] = wu_ref[0, 0, sl, :].astype(BF16)
            wd_scr[sl, :] = wd_ref[0, 0, sl, :].astype(BF16)
            return carry
        lax.fori_loop(0, wg_scr.shape[0] // rows, body, 0)

    @pl.when(i < nv_ref[0])
    def _():
        x = x_ref[...].astype(BF16)
        g = _dot(x, wg_scr[...])
        u = _dot(x, wu_scr[...])
        a = (g * jax.nn.sigmoid(g)) * u
        o_ref[...] = _dot(a.astype(BF16), wd_scr[...]).astype(BF16)

    @pl.when(i >= nv_ref[0])
    def _():
        o_ref[...] = jnp.zeros(o_ref.shape, BF16)


def _experts(xs, tile_expert, n_valid, w_gate, w_up, w_down, layer):
    r, d = xs.shape
    de = w_gate.shape[3]
    n_tiles = r // TM_EXP
    x_map = lambda i, te, nv: (jnp.minimum(i, nv[0] - 1), 0)
    grid_spec = pltpu.PrefetchScalarGridSpec(
        num_scalar_prefetch=2,
        grid=(n_tiles,),
        in_specs=[
            pl.BlockSpec((TM_EXP, d), x_map),
            pl.BlockSpec((1, 1, d, de), lambda i, te, nv: (layer, te[i], 0, 0)),
            pl.BlockSpec((1, 1, d, de), lambda i, te, nv: (layer, te[i], 0, 0)),
            pl.BlockSpec((1, 1, de, d), lambda i, te, nv: (layer, te[i], 0, 0)),
        ],
        out_specs=pl.BlockSpec((TM_EXP, d), lambda i, te, nv: (i, 0)),
        scratch_shapes=[pltpu.VMEM((d, de), BF16), pltpu.VMEM((d, de), BF16), pltpu.VMEM((de, d), BF16)],
    )
    return pl.pallas_call(
        _experts_kernel,
        out_shape=jax.ShapeDtypeStruct((r, d), BF16),
        grid_spec=grid_spec,
        compiler_params=_cparams(("arbitrary",)),
        name="experts",
    )(tile_expert, n_valid, xs, w_gate, w_up, w_down)


def _dispatch(idx):
    n = idx.shape[1]
    e_flat = idx.reshape(-1)
    onehot = (e_flat[:, None] == jnp.arange(N_EXPERTS, dtype=jnp.int32)[None, :]).astype(jnp.int32)
    csum = jnp.cumsum(onehot, axis=0)
    rank = jnp.sum(csum * onehot, axis=1) - 1
    counts = csum[-1]
    padded = ((counts + TM_EXP - 1) // TM_EXP) * TM_EXP
    ends = jnp.cumsum(padded)
    starts = ends - padded
    dest = (jnp.sum(onehot * starts[None, :], axis=1) + rank).astype(jnp.int32)
    r_max = 2 * n + N_EXPERTS * TM_EXP
    tile_start = jnp.arange(r_max // TM_EXP, dtype=jnp.int32) * TM_EXP
    tile_expert = jnp.minimum(jnp.sum((tile_start[:, None] >= ends[None, :]).astype(jnp.int32), axis=1),
                              N_EXPERTS - 1).astype(jnp.int32)
    n_valid = (ends[-1] // TM_EXP).astype(jnp.int32).reshape(1)
    pad_start = (starts + counts).astype(jnp.int32)
    total_end = ends[-1].astype(jnp.int32).reshape(1)
    return dest, pad_start, total_end, tile_expert, n_valid


def _combine_kernel(h_ref, y1_ref, y2_ref, gate_ref, mod_ref, o_ref):
    d = D_MODEL
    g = gate_ref[...]
    y = g[:, 0:1] * y1_ref[...].astype(F32) + g[:, 1:2] * y2_ref[...].astype(F32)
    o_ref[...] = h_ref[...] + mod_ref[0, :, 5 * d:6 * d] * y


def _combine(h2d, y1, y2, gates_t, mod3, mod_map, row0):
    r, d = h2d.shape
    blk0 = row0 // TM_PROJ
    row = lambda i: (i, 0)
    seg = lambda i: (blk0 + i, 0)
    return pl.pallas_call(
        _combine_kernel,
        out_shape=jax.ShapeDtypeStruct((r, d), F32),
        grid=(r // TM_PROJ,),
        in_specs=[
            pl.BlockSpec((TM_PROJ, d), row),
            pl.BlockSpec((TM_PROJ, d), seg),
            pl.BlockSpec((TM_PROJ, d), seg),
            pl.BlockSpec((TM_PROJ, 2), seg),
            pl.BlockSpec((1, 1, N_MOD * d), mod_map),
        ],
        out_specs=pl.BlockSpec((TM_PROJ, d), row),
        compiler_params=_cparams(("parallel",)),
        name="moe_combine",
    )(h2d, y1, y2, gates_t, mod3)


def kernel(x, c, ctx, c_ctx, w_mod, b_mod, norm1_g, norm2_g, w_in, w_out, q_norm_g, k_norm_g, na_rpb,
           s5_lam_re, s5_lam_im, s5_log_dt, s5_b_re, s5_b_im, s5_c_re, s5_c_im, s5_d, s5_w_glu, s5_b_glu,
           router_w, router_bias, moe_w_gate, moe_w_up, moe_w_down):
    bsz, n_lat, d = x.shape
    n_ctx = ctx.shape[1]
    depth = w_mod.shape[0]
    ctx_row = bsz
    c_rows = jnp.concatenate([c.astype(F32), c_ctx.astype(F32)[None],
                              jnp.zeros((SUBLANES - bsz - 1, d), F32)], axis=0)
    mod_all = _modulation(c_rows, w_mod.astype(F32), b_mod.astype(F32))

    h_lat = x.reshape(bsz * n_lat, d).astype(F32)
    h_ctx = ctx.reshape(bsz * n_ctx, d).astype(F32)
    lat_map = _mod_row_map(n_lat, 0, True)
    ctx_map = _mod_row_map(n_ctx, ctx_row, False)
    bias_tabs = _na_bias_tables(na_rpb)
    s5_mats = _s5_matrices(s5_lam_re, s5_lam_im, s5_log_dt, s5_b_re, s5_b_im, s5_c_re, s5_c_im)

    for layer in range(depth):
        ctx_out = layer < depth - 1
        mod3 = mod_all[layer].reshape(SUBLANES, 1, N_MOD * d)
        w_in_b = w_in[layer].astype(BF16)
        qkvu_lat = _in_projection(h_lat, norm1_g[layer], mod3, w_in_b, lat_map, q_norm_g[layer], k_norm_g[layer])
        qkvu_ctx = _in_projection(h_ctx, norm1_g[layer], mod3, w_in_b, ctx_map, q_norm_g[layer], k_norm_g[layer])
        na_lat = _neighborhood_attention(qkvu_lat, qkvu_ctx, bias_tabs, layer, bsz)
        y_all = _s5_mixer(qkvu_lat, qkvu_ctx, s5_mats, layer, bsz)
        wglu_b = s5_w_glu[layer].astype(BF16)
        wout_b = w_out[layer].astype(BF16)
        h_lat, f_lat = _out_projection(na_lat, y_all, qkvu_lat, h_lat, mod3, s5_d[layer], wglu_b,
                                       s5_b_glu[layer], wout_b, norm2_g[layer],
                                       lat_map, n_lat, n_ctx // TM_PROJ)
        if ctx_out:
            na_ctx = _context_attention(qkvu_ctx, bsz)
            h_ctx, f_ctx = _out_projection(na_ctx, y_all, qkvu_ctx, h_ctx, mod3, s5_d[layer], wglu_b,
                                           s5_b_glu[layer], wout_b, norm2_g[layer],
                                           ctx_map, n_ctx, 0)
        n_l = bsz * n_lat
        idx, gates = _router(f_lat, router_w, router_bias)
        if ctx_out:
            idx_c, gates_c = _router(f_ctx, router_w, router_bias)
            idx = jnp.concatenate([idx, idx_c], axis=1)
            gates = jnp.concatenate([gates, gates_c], axis=1)
        n_tok = idx.shape[1]
        dest, pad_start, total_end, tile_expert, n_valid = _dispatch(idx)
        xs = _scatter_rows([f_lat, f_ctx] if ctx_out else [f_lat], dest, pad_start, total_end)
        ys = _experts(xs, tile_expert, n_valid, moe_w_gate, moe_w_up, moe_w_down, layer)
        y1 = jnp.take(ys, dest[:n_tok], axis=0, mode="clip")
        y2 = jnp.take(ys, dest[n_tok:], axis=0, mode="clip")
        gates_t = gates.T
        h_lat = _combine(h_lat, y1, y2, gates_t, mod3, lat_map, 0)
        if ctx_out:
            h_ctx = _combine(h_ctx, y1, y2, gates_t, mod3, ctx_map, n_l)
    return h_lat.reshape(bsz, n_lat, d).astype(x.dtype)
```

```python
import functools
import math

import jax
import jax.numpy as jnp
from jax import lax
from jax.experimental import pallas as pl
from jax.experimental.pallas import tpu as pltpu

F32 = jnp.float32
BF16 = jnp.bfloat16
HIGHEST = lax.Precision.HIGHEST

D_MODEL = 1024
GRID_W = 64
HEAD_DIM = 64
NA_WIDTH = 512
S5_WIDTH = 512
S5_CH = 16
S5_GROUPS = 32
S5_STATE = 64
WIN_ROWS = 8
WIN_COLS = 16
N_EXPERTS = 16
N_GROUPS = 4
EPG = 4
N_MOD = 6
EPS = 1e-6

LANES = 128
SUBLANES = 8
VMEM_LIMIT = 56 * 1024 * 1024

TM_PROJ = 256
Q_ROWS = 8
NA_ROW_TILES = 2
Q_COLS = 16
K_ROWS = 16
K_COLS = 32
CHUNK = 16
PAIRS_PER_STEP = 2
TM_EXP = 512
TM_ROUTE = 512
MASK_VALUE = -1e30
RPB_LANE0 = 48


def _cparams(sem):
    return pltpu.CompilerParams(dimension_semantics=sem, vmem_limit_bytes=VMEM_LIMIT)


def _dot(a, b):
    return jnp.dot(a, b, preferred_element_type=F32)


def _dot_nt(a, b):
    return lax.dot_general(a, b, (((1,), (1,)), ((), ())), preferred_element_type=F32)


def _mod_kernel(c_ref, w_ref, b_ref, o_ref):
    a = c_ref[...]
    a = a * jax.nn.sigmoid(a)
    o_ref[0] = jnp.dot(a, w_ref[0], precision=HIGHEST, preferred_element_type=F32) + b_ref[0]


def _modulation(c_rows, w_mod, b_mod):
    depth, d, n = w_mod.shape
    tn = 1536
    return pl.pallas_call(
        _mod_kernel,
        out_shape=jax.ShapeDtypeStruct((depth, SUBLANES, n), F32),
        grid=(depth, n // tn),
        in_specs=[
            pl.BlockSpec((SUBLANES, d), lambda l, j: (0, 0)),
            pl.BlockSpec((1, d, tn), lambda l, j: (l, 0, j)),
            pl.BlockSpec((1, 1, tn), lambda l, j: (l, 0, j)),
        ],
        out_specs=pl.BlockSpec((1, SUBLANES, tn), lambda l, j: (l, 0, j)),
        compiler_params=_cparams(("arbitrary", "arbitrary")),
        name="modulation",
    )(c_rows, w_mod, b_mod.reshape(depth, 1, n))


def _inproj_kernel(x_ref, g_ref, mod_ref, w_ref, qg_ref, kg_ref, o_ref):
    x = x_ref[...]
    ms = jnp.mean(x * x, axis=-1, keepdims=True)
    y = x * lax.rsqrt(ms + EPS) * g_ref[...]
    shift = mod_ref[0, :, 0:D_MODEL]
    scale = mod_ref[0, :, D_MODEL:2 * D_MODEL]
    a = y * (1.0 + scale) + shift
    acc = _dot(a.astype(BF16), w_ref[...])
    lo = lax.broadcasted_iota(jnp.int32, (1, LANES), 1) < HEAD_DIM
    n_pairs = NA_WIDTH // LANES
    for blk in range(2 * n_pairs):
        cols = slice(blk * LANES, (blk + 1) * LANES)
        if blk < n_pairs:
            o_ref[:, cols] = _pair_rms(acc[:, cols], qg_ref[...], lo) * (HEAD_DIM ** -0.5)
        else:
            o_ref[:, cols] = _pair_rms(acc[:, cols], kg_ref[...], lo)
    o_ref[:, 2 * NA_WIDTH:] = acc[:, 2 * NA_WIDTH:]


def _mod_row_map(rows_per_batch, mod_row0, per_batch):
    tiles_per_batch = rows_per_batch // TM_PROJ
    if per_batch:
        return lambda i: (mod_row0 + i // tiles_per_batch, 0, 0)
    return lambda i: (mod_row0, 0, 0)


def _in_projection(x2d, g, mod3, w_bf16, mod_map, qg, kg):
    r, d = x2d.shape
    n = w_bf16.shape[1]
    g2 = lambda v: jnp.concatenate([v, v]).reshape(1, LANES).astype(F32)
    return pl.pallas_call(
        _inproj_kernel,
        out_shape=jax.ShapeDtypeStruct((r, n), F32),
        grid=(r // TM_PROJ,),
        in_specs=[
            pl.BlockSpec((TM_PROJ, d), lambda i: (i, 0)),
            pl.BlockSpec((1, d), lambda i: (0, 0)),
            pl.BlockSpec((1, 1, N_MOD * d), mod_map),
            pl.BlockSpec((d, n), lambda i: (0, 0)),
            pl.BlockSpec((1, LANES), lambda i: (0, 0)),
            pl.BlockSpec((1, LANES), lambda i: (0, 0)),
        ],
        out_specs=pl.BlockSpec((TM_PROJ, n), lambda i: (i, 0)),
        compiler_params=_cparams(("parallel",)),
        name="in_projection",
    )(x2d, g.reshape(1, d), mod3, w_bf16, g2(qg), g2(kg))


def _pair_rms(x, g, lo):
    ss = x * x
    sa = jnp.sum(jnp.where(lo, ss, 0.0), axis=-1, keepdims=True)
    sb = jnp.sum(jnp.where(lo, 0.0, ss), axis=-1, keepdims=True)
    ms = jnp.where(lo, sa, sb) * (1.0 / HEAD_DIM)
    return x * lax.rsqrt(ms + EPS) * g


def _softmax_pv(qm, kw, kcb, vw, vcb, bias):
    s_nb = _dot_nt(qm, kw) + bias
    s_cx = _dot_nt(qm, kcb)
    m = jnp.maximum(jnp.max(s_nb, axis=-1, keepdims=True), jnp.max(s_cx, axis=-1, keepdims=True))
    p_nb = jnp.exp(s_nb - m)
    p_cx = jnp.exp(s_cx - m)
    l = jnp.sum(p_nb, axis=-1, keepdims=True) + jnp.sum(p_cx, axis=-1, keepdims=True)
    o = _dot(p_nb.astype(BF16), vw) + _dot(p_cx.astype(BF16), vcb)
    return o / l


def _na_kernel(q_ref, k_ref, v_ref, kc_ref, vc_ref, bias_ref, o_ref):
    lo = lax.broadcasted_iota(jnp.int32, (1, LANES), 1) < HEAD_DIM
    n_rows = k_ref.shape[1]
    col_tiles = GRID_W // Q_COLS
    nq = Q_ROWS * Q_COLS
    nk = K_ROWS * K_COLS
    kcb = kc_ref[0].astype(BF16)
    vcb = vc_ref[0].astype(BF16)
    for rr in range(NA_ROW_TILES):
        i = pl.program_id(2) * NA_ROW_TILES + rr
        kr0 = jnp.clip(Q_ROWS * i - WIN_ROWS // 2, 0, n_rows - K_ROWS)
        rt = jnp.where(i == 0, 0, jnp.where(i == n_rows // Q_ROWS - 1, 2, 1))
        q_rows = slice(rr * Q_ROWS, (rr + 1) * Q_ROWS)
        for j in range(col_tiles):
            kc0 = min(max(Q_COLS * j - WIN_COLS // 2, 0), GRID_W - K_COLS)
            ct = 0 if j == 0 else (2 if j == col_tiles - 1 else 1)
            typ = rt * 3 + ct
            cols = slice(j * Q_COLS, (j + 1) * Q_COLS)
            qn = q_ref[0, q_rows, cols, :].reshape(nq, LANES)
            kw = k_ref[0, pl.ds(kr0, K_ROWS), kc0:kc0 + K_COLS, :].reshape(nk, LANES).astype(BF16)
            vw = v_ref[0, pl.ds(kr0, K_ROWS), kc0:kc0 + K_COLS, :].reshape(nk, LANES).astype(BF16)
            o_a = _softmax_pv(jnp.where(lo, qn, 0.0).astype(BF16), kw, kcb, vw, vcb, bias_ref[0, 0, typ, 0])
            o_b = _softmax_pv(jnp.where(lo, 0.0, qn).astype(BF16), kw, kcb, vw, vcb, bias_ref[0, 0, typ, 1])
            o_ref[0, q_rows, cols, :] = jnp.where(lo, o_a, o_b).reshape(Q_ROWS, Q_COLS, LANES)


def _bias_table_kernel(rpb_ref, o_ref, tt_scr):
    n_off_r = 2 * WIN_ROWS - 1
    lane = lax.broadcasted_iota(jnp.int32, (Q_COLS, LANES), 1)
    qc = lax.broadcasted_iota(jnp.int32, (Q_COLS, LANES), 0)
    kc = lane % K_COLS
    lane_blk = lane // K_COLS
    per_vreg = LANES // K_COLS
    col_rel = (0, -WIN_COLS // 2, -WIN_COLS)
    col_origin = (0, Q_COLS, GRID_W - Q_COLS)
    row_rel = (0, -WIN_ROWS // 2, -WIN_ROWS)
    row_origin = (0, Q_ROWS, GRID_W - Q_ROWS)
    masked = jnp.full((Q_COLS, LANES), MASK_VALUE, F32)

    for ct in range(3):
        c_abs = col_origin[ct] + qc
        k_abs = col_origin[ct] + col_rel[ct] + kc
        start = jnp.clip(c_abs - WIN_COLS // 2, 0, GRID_W - WIN_COLS)
        valid_c = (k_abs >= start) & (k_abs < start + WIN_COLS)
        base = (1 - WIN_COLS - col_rel[ct] - RPB_LANE0) % LANES
        for ro in range(n_off_r):
            row = jnp.broadcast_to(rpb_ref[0, 0, ro:ro + 1, :], (Q_COLS, LANES))
            t = pltpu.roll(row, base, 1, stride=1, stride_axis=0)
            rep = t
            for m in range(1, per_vreg):
                rep = jnp.where(lane_blk == m, pltpu.roll(t, K_COLS * m, 1), rep)
            tt_scr[ct, ro] = jnp.where(valid_c, rep, MASK_VALUE)

    for rt in range(3):
        for ct in range(3):
            for qr in range(Q_ROWS):
                r_abs = row_origin[rt] + qr
                r_start = min(max(r_abs - WIN_ROWS // 2, 0), GRID_W - WIN_ROWS)
                for w in range(K_ROWS // per_vreg):
                    val = None
                    for m in range(per_vreg):
                        k_abs = row_origin[rt] + row_rel[rt] + per_vreg * w + m
                        ok = r_start <= k_abs < r_start + WIN_ROWS
                        src = tt_scr[ct, k_abs - r_abs + WIN_ROWS - 1] if ok else masked
                        val = src if val is None else jnp.where(lane_blk == m, src, val)
                    o_ref[0, 0, rt * 3 + ct, 0, qr * Q_COLS:(qr + 1) * Q_COLS, w * LANES:(w + 1) * LANES] = val


def _na_bias_tables(na_rpb):
    depth, h, n_r, n_c = na_rpb.shape
    rpb_pad = jnp.pad(na_rpb.astype(F32), ((0, 0), (0, 0), (0, 2 * SUBLANES - n_r),
                                            (RPB_LANE0, LANES - RPB_LANE0 - n_c)))
    nq, nk = Q_ROWS * Q_COLS, K_ROWS * K_COLS
    return pl.pallas_call(
        _bias_table_kernel,
        out_shape=jax.ShapeDtypeStruct((depth, h // 2, 9, 2, nq, nk), F32),
        grid=(depth, h),
        in_specs=[pl.BlockSpec((1, 1, 2 * SUBLANES, LANES), lambda l, i: (l, i, 0, 0))],
        out_specs=pl.BlockSpec((1, 1, 9, 1, nq, nk), lambda l, i: (l, i // 2, 0, i % 2, 0, 0)),
        scratch_shapes=[pltpu.VMEM((3, 2 * WIN_ROWS - 1, Q_COLS, LANES), F32)],
        compiler_params=_cparams(("parallel", "parallel")),
        name="na_bias_tables",
    )(rpb_pad)


def _neighborhood_attention(qkvu_lat, qkvu_ctx, bias_tabs, layer, bsz):
    n_lat = qkvu_lat.shape[0] // bsz
    n_ctx = qkvu_ctx.shape[0] // bsz
    rows = n_lat // GRID_W
    n_cols = qkvu_lat.shape[1]
    lat4 = qkvu_lat.reshape(bsz, rows, GRID_W, n_cols)
    ctx3 = qkvu_ctx.reshape(bsz, n_ctx, n_cols)
    n_pairs = NA_WIDTH // LANES
    step_rows = NA_ROW_TILES * Q_ROWS
    n_tiles = rows // step_rows
    out = pl.pallas_call(
        _na_kernel,
        out_shape=jax.ShapeDtypeStruct((bsz, rows, GRID_W, NA_WIDTH), F32),
        grid=(n_pairs, bsz, n_tiles),
        in_specs=[
            pl.BlockSpec((1, step_rows, GRID_W, LANES), lambda p, b, t: (b, t, 0, p)),
            pl.BlockSpec((1, rows, GRID_W, LANES), lambda p, b, t: (b, 0, 0, n_pairs + p)),
            pl.BlockSpec((1, rows, GRID_W, LANES), lambda p, b, t: (b, 0, 0, 2 * n_pairs + p)),
            pl.BlockSpec((1, n_ctx, LANES), lambda p, b, t: (b, 0, n_pairs + p)),
            pl.BlockSpec((1, n_ctx, LANES), lambda p, b, t: (b, 0, 2 * n_pairs + p)),
            pl.BlockSpec((1, 1, 9, 2, Q_ROWS * Q_COLS, K_ROWS * K_COLS), lambda p, b, t: (layer, p, 0, 0, 0, 0)),
        ],
        out_specs=pl.BlockSpec((1, step_rows, GRID_W, LANES), lambda p, b, t: (b, t, 0, p)),
        compiler_params=_cparams(("parallel", "parallel", "parallel")),
        name="neighborhood_attention",
    )(lat4, lat4, lat4, ctx3, ctx3, bias_tabs)
    return out.reshape(bsz * n_lat, NA_WIDTH)


def _ctx_attn_kernel(q_ref, k_ref, v_ref, o_ref):
    lo = lax.broadcasted_iota(jnp.int32, (1, LANES), 1) < HEAD_DIM
    qn = q_ref[0]
    kn = k_ref[0].astype(BF16)
    vb = v_ref[0].astype(BF16)

    def one(qm):
        s = _dot_nt(qm, kn)
        m = jnp.max(s, axis=-1, keepdims=True)
        p = jnp.exp(s - m)
        l = jnp.sum(p, axis=-1, keepdims=True)
        return _dot(p.astype(BF16), vb) / l

    o_a = one(jnp.where(lo, qn, 0.0).astype(BF16))
    o_b = one(jnp.where(lo, 0.0, qn).astype(BF16))
    o_ref[0] = jnp.where(lo, o_a, o_b)


def _context_attention(qkvu_ctx, bsz):
    n_ctx = qkvu_ctx.shape[0] // bsz
    ctx3 = qkvu_ctx.reshape(bsz, n_ctx, qkvu_ctx.shape[1])
    n_pairs = NA_WIDTH // LANES
    out = pl.pallas_call(
        _ctx_attn_kernel,
        out_shape=jax.ShapeDtypeStruct((bsz, n_ctx, NA_WIDTH), F32),
        grid=(bsz, n_pairs),
        in_specs=[
            pl.BlockSpec((1, n_ctx, LANES), lambda b, p: (b, 0, p)),
            pl.BlockSpec((1, n_ctx, LANES), lambda b, p: (b, 0, n_pairs + p)),
            pl.BlockSpec((1, n_ctx, LANES), lambda b, p: (b, 0, 2 * n_pairs + p)),
        ],
        out_specs=pl.BlockSpec((1, n_ctx, LANES), lambda b, p: (b, 0, p)),
        compiler_params=_cparams(("parallel", "parallel")),
        name="context_attention",
    )(ctx3, ctx3, ctx3)
    return out.reshape(bsz * n_ctx, NA_WIDTH)


def _s5_kernel(uc_ref, ul_ref, w_ref, m_ref, v_ref, a_ref, y_ref, x_scr, s_scr, hf_scr, hr_scr, *, n_ctx_chunks):
    bsz = ul_ref.shape[0]
    n_lat_chunks = ul_ref.shape[1] // CHUNK
    n_chunks = n_ctx_chunks + n_lat_chunks
    rows = n_chunks * SUBLANES
    n_pairs = w_ref.shape[1]
    gpb = 2 * n_pairs
    half = 2 * LANES
    tile_chunks = TM_PROJ // CHUNK
    lane_blk = lax.broadcasted_iota(jnp.int32, (n_chunks, LANES), 1) // S5_CH

    for b in range(bsz):
        for q in range(2):
            rolled = []
            for j in range(SUBLANES):
                s = SUBLANES * q + j
                u_s = jnp.concatenate([uc_ref[b, pl.ds(s, n_ctx_chunks, stride=CHUNK), :],
                                       ul_ref[b, pl.ds(s, n_lat_chunks, stride=CHUNK), :]], axis=0)
                rolled.append(u_s if j == 0 else pltpu.roll(u_s, j * S5_CH, 1))
            for g in range(gpb):
                xg = rolled[0]
                for j in range(1, SUBLANES):
                    xg = jnp.where(lane_blk == (g + j) % SUBLANES, rolled[j], xg)
                x_scr[g // 2, q, pl.ds((g % 2) * bsz + b, n_chunks, stride=SUBLANES), :] = xg

    n_blk = 8
    rb = rows // n_blk
    first_group = (lax.broadcasted_iota(jnp.int32, (rb, half), 0) & (SUBLANES // 2)) == 0
    fwd_cols = (lax.broadcasted_iota(jnp.int32, (rb, half), 1) & (LANES - 1)) < S5_STATE
    is_fwd = lax.broadcasted_iota(jnp.int32, (SUBLANES, LANES), 1) < S5_STATE
    first_rows = lax.broadcasted_iota(jnp.int32, (SUBLANES, half), 0) < SUBLANES // 2
    zero = jnp.zeros((SUBLANES, LANES), F32)

    def x_rows(p, sl):
        return jnp.concatenate([x_scr[p, 0, sl, :], x_scr[p, 1, sl, :]], axis=1).astype(BF16)

    def put_cols(scr, i, sl, val):
        scr[i, 0, sl, :] = val[:, 0:LANES]
        scr[i, 1, sl, :] = val[:, LANES:half]

    for hh in range(n_pairs // PAIRS_PER_STEP):
        pairs = [hh * PAIRS_PER_STEP + i for i in range(PAIRS_PER_STEP)]
        for i, p in enumerate(pairs):
            for blk in range(n_blk):
                sl = slice(blk * rb, (blk + 1) * rb)
                r = _dot(x_rows(p, sl), w_ref[0, p])
                put_cols(s_scr, i, sl, jnp.where(first_group, r[:, :half], r[:, half:]))

        a_pair = [jnp.where(first_rows, a_ref[0, 2 * p], a_ref[0, 2 * p + 1]) for p in pairs]
        a_re = [a[:, 0:LANES] for a in a_pair]
        a_im = [a[:, LANES:half] for a in a_pair]

        def body(k, carry):
            kr = jnp.where(k < n_ctx_chunks, n_ctx_chunks - 1 - k, n_chunks + n_ctx_chunks - 1 - k)
            rf = pl.ds(pl.multiple_of(k * SUBLANES, SUBLANES), SUBLANES)
            rr = pl.ds(pl.multiple_of(kr * SUBLANES, SUBLANES), SUBLANES)
            new = []
            for i in range(PAIRS_PER_STEP):
                h_re, h_im = carry[2 * i], carry[2 * i + 1]
                hf_scr[i, 0, rf, :] = h_re
                hf_scr[i, 1, rf, :] = h_im
                hr_scr[i, 0, rr, :] = h_re
                hr_scr[i, 1, rr, :] = h_im
                s_re = jnp.where(is_fwd, s_scr[i, 0, rf, :], s_scr[i, 0, rr, :])
                s_im = jnp.where(is_fwd, s_scr[i, 1, rf, :], s_scr[i, 1, rr, :])
                new.append(a_re[i] * h_re - a_im[i] * h_im + s_re)
                new.append(a_re[i] * h_im + a_im[i] * h_re + s_im)
            return tuple(new)

        lax.fori_loop(0, n_chunks, body, (zero,) * (2 * PAIRS_PER_STEP))

        for i, p in enumerate(pairs):
            for blk in range(n_blk):
                sl = slice(blk * rb, (blk + 1) * rb)
                hf = jnp.concatenate([hf_scr[i, 0, sl, :], hf_scr[i, 1, sl, :]], axis=1)
                hr = jnp.concatenate([hr_scr[i, 0, sl, :], hr_scr[i, 1, sl, :]], axis=1)
                h_in = jnp.where(fwd_cols, hf, hr).astype(BF16)
                r = _dot(x_rows(p, sl), m_ref[0, p]) + _dot_nt(h_in, v_ref[0, p])
                put_cols(s_scr, i, sl, jnp.where(first_group, r[:, :half], r[:, half:]))

        g_lo = 2 * pairs[0]
        n_g = 2 * PAIRS_PER_STEP
        out_blk = lax.broadcasted_iota(jnp.int32, (tile_chunks, LANES), 1) // S5_CH
        lanes_out = (out_blk >= g_lo) & (out_blk < g_lo + n_g)
        for b in range(bsz):
            for q in range(2):
                y_g = [s_scr[(g - g_lo) // 2, q, pl.ds((g % 2) * bsz + b, n_chunks, stride=SUBLANES), :]
                       for g in range(g_lo, g_lo + n_g)]
                for j in range(SUBLANES):
                    z = y_g[0]
                    for gi in range(1, n_g):
                        z = jnp.where(lane_blk == (g_lo + gi + j) % SUBLANES, y_g[gi], z)
                    if j:
                        z = pltpu.roll(z, LANES - j * S5_CH, 1)
                    t = SUBLANES * q + j
                    for ct in range(n_chunks // tile_chunks):
                        r0 = ct * TM_PROJ + t * tile_chunks
                        pltpu.store(y_ref.at[b, 0, r0:r0 + tile_chunks, :],
                                    z[ct * tile_chunks:(ct + 1) * tile_chunks, :], mask=lanes_out)


def _s5_scan(qkvu_ctx3, qkvu_lat3, mats, layer):
    w_c, m_c, vt_c, a_c = mats
    bsz, n_ctx, _ = qkvu_ctx3.shape
    n_lat = qkvu_lat3.shape[1]
    n_seq = n_ctx + n_lat
    rows = n_seq // CHUNK * SUBLANES
    n_blocks = S5_WIDTH // LANES
    ppb = w_c.shape[1] // n_blocks
    u_blk0 = 3 * NA_WIDTH // LANES
    wspec = pl.BlockSpec((1, ppb, 2 * LANES, 4 * LANES), lambda i: (layer, i, 0, 0))
    one = pl.Buffered(1)
    state = pltpu.VMEM((PAIRS_PER_STEP, 2, rows, LANES), F32)
    return pl.pallas_call(
        functools.partial(_s5_kernel, n_ctx_chunks=n_ctx // CHUNK),
        out_shape=jax.ShapeDtypeStruct((bsz, n_blocks, n_seq, LANES), F32),
        grid=(n_blocks,),
        in_specs=[
            pl.BlockSpec((bsz, n_ctx, LANES), lambda i: (0, 0, u_blk0 + i)),
            pl.BlockSpec((bsz, n_lat, LANES), lambda i: (0, 0, u_blk0 + i), pipeline_mode=one),
            wspec, wspec,
            pl.BlockSpec((1, ppb, 4 * LANES, 2 * LANES), lambda i: (layer, i, 0, 0)),
            pl.BlockSpec((1, 2 * ppb, SUBLANES, 2 * LANES), lambda i: (layer, i, 0, 0)),
        ],
        out_specs=pl.BlockSpec((bsz, 1, n_seq, LANES), lambda i: (0, i, 0, 0), pipeline_mode=one),
        scratch_shapes=[pltpu.VMEM((ppb, 2, rows, LANES), F32), state, state, state],
        compiler_params=_cparams(("parallel",)),
        name="s5_scan",
    )(qkvu_ctx3, qkvu_lat3, w_c, m_c, vt_c, a_c)


def _s5_mats_kernel(prm_ref, btr_ref, bti_ref, cr_ref, ci_ref, w_ref, m_ref, vt_ref, a_ref):
    t = CHUNK
    gl = pl.program_id(1) % SUBLANES
    is_fwd = lax.broadcasted_iota(jnp.int32, (1, LANES), 1) < S5_STATE
    lr = prm_ref[0, 0, 0:1, :]
    li = prm_ref[0, 0, 1:2, :]
    dt = jnp.exp(prm_ref[0, 0, 2:3, :])
    n = lax.broadcasted_iota(jnp.int32, (3 * SUBLANES, LANES), 0).astype(F32)
    pmag = jnp.exp(n * (lr * dt))
    pw_re = pmag * jnp.cos(n * (li * dt))
    pw_im = pmag * jnp.sin(n * (li * dt))
    ab_re, ab_im = pw_re[1:2, :], pw_im[1:2, :]
    den = lr * lr + li * li
    nr = ab_re - 1.0
    z_re = (nr * lr + ab_im * li) / den
    z_im = (ab_im * lr - nr * li) / den
    bt_re, bt_im = btr_ref[0, 0], bti_ref[0, 0]
    bb_re = z_re * bt_re - z_im * bt_im
    bb_im = z_re * bt_im + z_im * bt_re
    c_re, c_im = cr_ref[0, 0], ci_ref[0, 0]

    def powers(n_fwd, n_rev):
        return (jnp.where(is_fwd, pw_re[n_fwd:n_fwd + 1, :], pw_re[n_rev:n_rev + 1, :]),
                jnp.where(is_fwd, pw_im[n_fwd:n_fwd + 1, :], pw_im[n_rev:n_rev + 1, :]))

    def block_rows(s):
        pos = SUBLANES * (s // SUBLANES) + (s % SUBLANES + gl) % SUBLANES
        return pl.ds(pl.multiple_of(pos * S5_CH, S5_CH), S5_CH)

    for s in range(t):
        rows = block_rows(s)
        p_re, p_im = powers(t - 1 - s, s)
        w_ref[0, 0, rows, 0:LANES] = (bb_re * p_re - bb_im * p_im).astype(BF16)
        w_ref[0, 0, rows, LANES:2 * LANES] = (bb_re * p_im + bb_im * p_re).astype(BF16)
        q_re, q_im = powers(s + 1, t - s)
        vt_ref[0, 0, rows, 0:LANES] = (c_re * q_re - c_im * q_im).astype(BF16)
        vt_ref[0, 0, rows, LANES:2 * LANES] = (-(c_re * q_im + c_im * q_re)).astype(BF16)

    ca_re, ca_im = [], []
    for lag in range(t):
        p_re, p_im = powers(lag, t - 1 - lag)
        ca_re.append(c_re * p_re - c_im * p_im)
        ca_im.append(c_re * p_im + c_im * p_re)
    stack = jnp.concatenate([jnp.concatenate(ca_re, axis=0), jnp.concatenate(ca_im, axis=0)], axis=1)
    zero = jnp.zeros_like(bb_re)
    lhs = jnp.concatenate([
        jnp.concatenate([jnp.where(is_fwd, bb_re, zero), jnp.where(is_fwd, -bb_im, zero)], axis=1),
        jnp.concatenate([jnp.where(is_fwd, zero, bb_re), jnp.where(is_fwd, zero, -bb_im)], axis=1)], axis=0)
    kt = lax.dot_general(lhs, stack, (((1,), (1,)), ((), ())), precision=HIGHEST, preferred_element_type=F32)
    kt_f, kt_r = kt[0:S5_CH], kt[S5_CH:2 * S5_CH]
    blk = lax.broadcasted_iota(jnp.int32, (S5_CH, 2 * LANES), 1) // S5_CH
    for s in range(t):
        strip = (jnp.where(blk >= s, pltpu.roll(kt_f, S5_CH * s, 1), 0.0)
                 + jnp.where(blk <= s, pltpu.roll(kt_r, (S5_CH * (s - t + 1)) % (2 * LANES), 1), 0.0))
        strip = jnp.concatenate([pltpu.roll(strip[:, 0:LANES], gl * S5_CH, 1),
                                 pltpu.roll(strip[:, LANES:2 * LANES], gl * S5_CH, 1)], axis=1)
        m_ref[0, 0, block_rows(s), :] = strip.astype(BF16)

    a_ref[0, 0, :, 0:LANES] = jnp.broadcast_to(pw_re[t:t + 1, :], (SUBLANES, LANES))
    a_ref[0, 0, :, LANES:2 * LANES] = jnp.broadcast_to(pw_im[t:t + 1, :], (SUBLANES, LANES))


def _s5_matrices(lam_re, lam_im, log_dt, b_re, b_im, c_re, c_im):
    depth, _, g, p = lam_re.shape
    hc = b_re.shape[-1]
    width = CHUNK * hc
    both = lambda x: jnp.transpose(x.astype(F32), (0, 2, 1, 3)).reshape(depth, g, 1, 2 * p)
    dt_rows = jnp.broadcast_to(jnp.transpose(log_dt.astype(F32), (0, 2, 1))[..., None], (depth, g, 2, p))
    prm = jnp.concatenate([both(lam_re), both(lam_im), dt_rows.reshape(depth, g, 1, 2 * p),
                           jnp.zeros((depth, g, SUBLANES - 3, 2 * p), F32)], axis=2)
    bt = lambda x: jnp.transpose(x.astype(F32), (0, 2, 4, 1, 3)).reshape(depth, g, hc, 2 * p)
    ct = lambda x: jnp.transpose(x.astype(F32), (0, 2, 3, 1, 4)).reshape(depth, g, hc, 2 * p)
    vec = lambda rows: pl.BlockSpec((1, 1, rows, 2 * p), lambda l, i: (l, i, 0, 0))
    return pl.pallas_call(
        _s5_mats_kernel,
        out_shape=(jax.ShapeDtypeStruct((depth, g // 2, width, 2 * width), BF16),
                   jax.ShapeDtypeStruct((depth, g // 2, width, 2 * width), BF16),
                   jax.ShapeDtypeStruct((depth, g // 2, 2 * width, width), BF16),
                   jax.ShapeDtypeStruct((depth, g, SUBLANES, 4 * p), F32)),
        grid=(depth, g),
        in_specs=[vec(SUBLANES), vec(hc), vec(hc), vec(hc), vec(hc)],
        out_specs=(pl.BlockSpec((1, 1, width, width), lambda l, i: (l, i // 2, 0, i % 2)),
                   pl.BlockSpec((1, 1, width, width), lambda l, i: (l, i // 2, 0, i % 2)),
                   pl.BlockSpec((1, 1, width, width), lambda l, i: (l, i // 2, i % 2, 0)),
                   pl.BlockSpec((1, 1, SUBLANES, 4 * p), lambda l, i: (l, i, 0, 0))),
        compiler_params=_cparams(("parallel", "parallel")),
        name="s5_matrices",
    )(prm, bt(b_re), bt(b_im), ct(c_re), ct(c_im))


def _s5_mixer(qkvu_lat, qkvu_ctx, mats, layer, bsz):
    n_cols = qkvu_lat.shape[1]
    return _s5_scan(qkvu_ctx.reshape(bsz, -1, n_cols), qkvu_lat.reshape(bsz, -1, n_cols), mats, layer)


def _outproj_kernel(na_ref, y_ref, u_ref, h_ref, mod_ref, d_ref, wglu_ref, bglu_ref, wout_ref, g2_ref,
                    ho_ref, f_ref):
    tile_chunks = TM_PROJ // CHUNK
    y = jnp.concatenate(
        [jnp.concatenate([y_ref[0, blk, pl.ds(c, CHUNK, stride=tile_chunks), :] for c in range(tile_chunks)], axis=0)
         for blk in range(S5_WIDTH // LANES)], axis=1)
    z = jax.nn.gelu(y + d_ref[...] * u_ref[...])
    s5 = z * jax.nn.sigmoid(_dot(z.astype(BF16), wglu_ref[...]) + bglu_ref[...])
    mix = (_dot(na_ref[...].astype(BF16), wout_ref[0:NA_WIDTH, :])
           + _dot(s5.astype(BF16), wout_ref[NA_WIDTH:NA_WIDTH + S5_WIDTH, :]))
    d = D_MODEL
    gate = mod_ref[0, :, 2 * d:3 * d]
    h = h_ref[...] + gate * mix
    ho_ref[...] = h
    ms = jnp.mean(h * h, axis=-1, keepdims=True)
    y2 = h * lax.rsqrt(ms + EPS) * g2_ref[...]
    f_ref[...] = y2 * (1.0 + mod_ref[0, :, 4 * d:5 * d]) + mod_ref[0, :, 3 * d:4 * d]


def _out_projection(na, y_all, qkvu, h2d, mod3, d_skip, wglu_bf16, b_glu, wout_bf16, g2,
                    mod_map, rows_per_batch, y_block0):
    r, d = h2d.shape
    tiles_per_batch = rows_per_batch // TM_PROJ
    u_blk = 3 * NA_WIDTH // S5_WIDTH

    def y_map(i):
        return (i // tiles_per_batch, 0, y_block0 + i % tiles_per_batch, 0)

    const = lambda i: (0, 0)
    return pl.pallas_call(
        _outproj_kernel,
        out_shape=(jax.ShapeDtypeStruct((r, d), F32), jax.ShapeDtypeStruct((r, d), F32)),
        grid=(r // TM_PROJ,),
        in_specs=[
            pl.BlockSpec((TM_PROJ, NA_WIDTH), lambda i: (i, 0)),
            pl.BlockSpec((1, S5_WIDTH // LANES, TM_PROJ, LANES), y_map),
            pl.BlockSpec((TM_PROJ, S5_WIDTH), lambda i: (i, u_blk)),
            pl.BlockSpec((TM_PROJ, d), lambda i: (i, 0)),
            pl.BlockSpec((1, 1, N_MOD * d), mod_map),
            pl.BlockSpec((1, S5_WIDTH), const),
            pl.BlockSpec((S5_WIDTH, S5_WIDTH), const),
            pl.BlockSpec((1, S5_WIDTH), const),
            pl.BlockSpec((NA_WIDTH + S5_WIDTH, d), const),
            pl.BlockSpec((1, d), const),
        ],
        out_specs=(pl.BlockSpec((TM_PROJ, d), lambda i: (i, 0)), pl.BlockSpec((TM_PROJ, d), lambda i: (i, 0))),
        compiler_params=_cparams(("parallel",)),
        name="out_projection",
    )(na, y_all, qkvu, h2d, mod3, d_skip.reshape(1, -1), wglu_bf16, b_glu.reshape(1, -1), wout_bf16,
      g2.reshape(1, d))


def _top2(vals):
    best = vals[0]
    bi = jnp.zeros(best.shape, jnp.int32)
    for i in range(1, len(vals)):
        gt = vals[i] > best
        best = jnp.where(gt, vals[i], best)
        bi = jnp.where(gt, i, bi)
    second = jnp.full(best.shape, -jnp.inf, F32)
    si = jnp.zeros(best.shape, jnp.int32)
    for i in range(len(vals)):
        cand = jnp.where(bi == i, -jnp.inf, vals[i])
        gt = cand > second
        second = jnp.where(gt, cand, second)
        si = jnp.where(gt, i, si)
    return best, bi, second, si


def _router_kernel(f_ref, rwt_ref, rb_ref, idx_ref, gate_ref):
    logits = lax.dot_general(rwt_ref[...], f_ref[...], (((1,), (1,)), ((), ())),
                             precision=HIGHEST, preferred_element_type=F32)
    m = jnp.max(logits, axis=0, keepdims=True)
    e = jnp.exp(logits - m)
    probs = e / jnp.sum(e, axis=0, keepdims=True)
    sel = probs + rb_ref[...]
    sel_rows = [sel[i:i + 1, :] for i in range(N_EXPERTS)]
    prob_rows = [probs[i:i + 1, :] for i in range(N_EXPERTS)]
    scores = []
    for g in range(N_GROUPS):
        b, _, s, _ = _top2(sel_rows[g * EPG:(g + 1) * EPG])
        scores.append(b + s)
    grp = jnp.zeros(scores[0].shape, jnp.int32)
    gbest = scores[0]
    for g in range(1, N_GROUPS):
        gt = scores[g] > gbest
        gbest = jnp.where(gt, scores[g], gbest)
        grp = jnp.where(gt, g, grp)
    in_rows = []
    for j in range(EPG):
        v = sel_rows[j]
        for g in range(1, N_GROUPS):
            v = jnp.where(grp == g, sel_rows[g * EPG + j], v)
        in_rows.append(v)
    _, l1, _, l2 = _top2(in_rows)
    i1 = grp * EPG + l1
    i2 = grp * EPG + l2
    w1 = jnp.zeros(gbest.shape, F32)
    w2 = jnp.zeros(gbest.shape, F32)
    for i in range(N_EXPERTS):
        w1 = jnp.where(i1 == i, prob_rows[i], w1)
        w2 = jnp.where(i2 == i, prob_rows[i], w2)
    tot = w1 + w2
    idx_ref[0:1, :] = i1
    idx_ref[1:2, :] = i2
    gate_ref[0:1, :] = w1 / tot
    gate_ref[1:2, :] = w2 / tot


def _router(f_all, router_w, router_bias):
    n, d = f_all.shape
    return pl.pallas_call(
        _router_kernel,
        out_shape=(jax.ShapeDtypeStruct((2, n), jnp.int32), jax.ShapeDtypeStruct((2, n), F32)),
        grid=(n // TM_ROUTE,),
        in_specs=[
            pl.BlockSpec((TM_ROUTE, d), lambda i: (i, 0)),
            pl.BlockSpec((N_EXPERTS, d), lambda i: (0, 0)),
            pl.BlockSpec((N_EXPERTS, 1), lambda i: (0, 0)),
        ],
        out_specs=(pl.BlockSpec((2, TM_ROUTE), lambda i: (0, i)), pl.BlockSpec((2, TM_ROUTE), lambda i: (0, i))),
        compiler_params=_cparams(("parallel",)),
        name="router",
    )(f_all, router_w.T.astype(F32), router_bias.reshape(N_EXPERTS, 1).astype(F32))


def _scatter_rows_kernel(dest_ref, pad_ref, end_ref, *refs, n_tok, seg_tiles):
    f_refs = refs[:len(seg_tiles)]
    xs_ref, zero_scr, stage, sems = refs[len(seg_tiles):]
    i = pl.program_id(0)
    tm = f_refs[0].shape[0]
    n_rows = xs_ref.shape[0]
    slab = TM_EXP + SUBLANES
    fill_sem = sems.at[2]

    def slab_copy(start, rows):
        return pltpu.make_async_copy(zero_scr.at[pl.ds(0, rows), :], xs_ref.at[pl.ds(start, rows), :], fill_sem)

    @pl.when(i == 0)
    def _():
        zero_scr[...] = jnp.zeros(zero_scr.shape, zero_scr.dtype)
        for e in range(N_EXPERTS):
            start = jnp.minimum((pad_ref[e] // SUBLANES) * SUBLANES, n_rows - slab)
            slab_copy(pl.multiple_of(start, SUBLANES), slab).start()
        for e in range(N_EXPERTS):
            slab_copy(0, slab).wait()
        for k in range(N_EXPERTS):
            start = end_ref[0] + k * TM_EXP

            @pl.when(start < n_rows)
            def _():
                cp = slab_copy(pl.multiple_of(start, TM_EXP), TM_EXP)
                cp.start()
                cp.wait()

    slot = i % 2

    def wait_tile(s):
        for _ in range(2):
            pltpu.make_async_copy(stage.at[s], xs_ref.at[pl.ds(0, tm), :], sems.at[s]).wait()

    def scatter_tile(f_ref):
        base = i * tm
        stage[slot] = f_ref[...]

        def row_copy(r, d):
            return pltpu.make_async_copy(stage.at[slot, pl.ds(r, 1), :], xs_ref.at[pl.ds(d, 1), :], sems.at[slot])

        def body(r, carry):
            row_copy(r, dest_ref[base + r]).start()
            row_copy(r, dest_ref[n_tok + base + r]).start()
            return carry
        lax.fori_loop(0, tm, body, 0, unroll=8)

    tile0 = 0
    for f_ref, n_t in zip(f_refs, seg_tiles):
        pl.when((i >= tile0) & (i < tile0 + n_t))(functools.partial(scatter_tile, f_ref))
        tile0 += n_t

    pl.when(i > 0)(lambda: wait_tile(1 - slot))
    pl.when(i == pl.num_programs(0) - 1)(lambda: wait_tile(slot))


def _scatter_rows(segments, dest_flat, pad_start, total_end):
    d = segments[0].shape[1]
    seg_tiles = tuple(s.shape[0] // TM_PROJ for s in segments)
    n_tok = sum(s.shape[0] for s in segments)
    r_max = 2 * n_tok + N_EXPERTS * TM_EXP
    in_specs = []
    tile0 = 0
    for n_t in seg_tiles:
        in_specs.append(pl.BlockSpec(
            (TM_PROJ, d), lambda i, *_, t0=tile0, nt=n_t: (jnp.clip(i - t0, 0, nt - 1), 0)))
        tile0 += n_t
    grid_spec = pltpu.PrefetchScalarGridSpec(
        num_scalar_prefetch=3,
        grid=(tile0,),
        in_specs=in_specs,
        out_specs=pl.BlockSpec(memory_space=pl.ANY),
        scratch_shapes=[pltpu.VMEM((TM_EXP + SUBLANES, d), F32), pltpu.VMEM((2, TM_PROJ, d), F32),
                        pltpu.SemaphoreType.DMA((3,))],
    )
    return pl.pallas_call(
        functools.partial(_scatter_rows_kernel, n_tok=n_tok, seg_tiles=seg_tiles),
        out_shape=jax.ShapeDtypeStruct((r_max, d), F32),
        grid_spec=grid_spec,
        compiler_params=_cparams(("arbitrary",)),
        name="moe_scatter_rows",
    )(dest_flat, pad_start, total_end, *segments)


def _experts_kernel(te_ref, nv_ref, x_ref, wg_ref, wu_ref, wd_ref, o_ref, wg_scr, wu_scr, wd_scr):
    i = pl.program_id(0)
    e = te_ref[i]
    prev = te_ref[jnp.maximum(i - 1, 0)]
    rows = 128

    @pl.when((i == 0) | (e != prev))
    def _():
        def body(r, carry):
            sl = pl.ds(pl.multiple_of(r * rows, rows), rows)
            wg_scr[sl, :] = wg_ref[0, 0, sl, :].astype(BF16)
            wu_scr[sl, :] = wu_ref[0, 0, sl, :].astype(BF16)
            wd_scr[sl, :] = wd_ref[0, 0, sl, :].astype(BF16)
            return carry
        lax.fori_loop(0, wg_scr.shape[0] // rows, body, 0)

    @pl.when(i < nv_ref[0])
    def _():
        x = x_ref[...].astype(BF16)
        g = _dot(x, wg_scr[...])
        u = _dot(x, wu_scr[...])
        a = (g * jax.nn.sigmoid(g)) * u
        o_ref[...] = _dot(a.astype(BF16), wd_scr[...]).astype(BF16)

    @pl.when(i >= nv_ref[0])
    def _():
        o_ref[...] = jnp.zeros(o_ref.shape, BF16)


def _experts(xs, tile_expert, n_valid, w_gate, w_up, w_down, layer):
    r, d = xs.shape
    de = w_gate.shape[3]
    n_tiles = r // TM_EXP
    x_map = lambda i, te, nv: (jnp.minimum(i, nv[0] - 1), 0)
    grid_spec = pltpu.PrefetchScalarGridSpec(
        num_scalar_prefetch=2,
        grid=(n_tiles,),
        in_specs=[
            pl.BlockSpec((TM_EXP, d), x_map),
            pl.BlockSpec((1, 1, d, de), lambda i, te, nv: (layer, te[i], 0, 0)),
            pl.BlockSpec((1, 1, d, de), lambda i, te, nv: (layer, te[i], 0, 0)),
            pl.BlockSpec((1, 1, de, d), lambda i, te, nv: (layer, te[i], 0, 0)),
        ],
        out_specs=pl.BlockSpec((TM_EXP, d), lambda i, te, nv: (i, 0)),
        scratch_shapes=[pltpu.VMEM((d, de), BF16), pltpu.VMEM((d, de), BF16), pltpu.VMEM((de, d), BF16)],
    )
    return pl.pallas_call(
        _experts_kernel,
        out_shape=jax.ShapeDtypeStruct((r, d), BF16),
        grid_spec=grid_spec,
        compiler_params=_cparams(("arbitrary",)),
        name="experts",
    )(tile_expert, n_valid, xs, w_gate, w_up, w_down)


def _dispatch(idx):
    n = idx.shape[1]
    e_flat = idx.reshape(-1)
    onehot = (e_flat[:, None] == jnp.arange(N_EXPERTS, dtype=jnp.int32)[None, :]).astype(jnp.int32)
    csum = jnp.cumsum(onehot, axis=0)
    rank = jnp.sum(csum * onehot, axis=1) - 1
    counts = csum[-1]
    padded = ((counts + TM_EXP - 1) // TM_EXP) * TM_EXP
    ends = jnp.cumsum(padded)
    starts = ends - padded
    dest = (jnp.sum(onehot * starts[None, :], axis=1) + rank).astype(jnp.int32)
    r_max = 2 * n + N_EXPERTS * TM_EXP
    tile_start = jnp.arange(r_max // TM_EXP, dtype=jnp.int32) * TM_EXP
    tile_expert = jnp.minimum(jnp.sum((tile_start[:, None] >= ends[None, :]).astype(jnp.int32), axis=1),
                              N_EXPERTS - 1).astype(jnp.int32)
    n_valid = (ends[-1] // TM_EXP).astype(jnp.int32).reshape(1)
    pad_start = (starts + counts).astype(jnp.int32)
    total_end = ends[-1].astype(jnp.int32).reshape(1)
    return dest, pad_start, total_end, tile_expert, n_valid


def _combine_kernel(h_ref, y1_ref, y2_ref, gate_ref, mod_ref, o_ref):
    d = D_MODEL
    g = gate_ref[...]
    y = g[:, 0:1] * y1_ref[...].astype(F32) + g[:, 1:2] * y2_ref[...].astype(F32)
    o_ref[...] = h_ref[...] + mod_ref[0, :, 5 * d:6 * d] * y


def _combine(h2d, y1, y2, gates_t, mod3, mod_map, row0):
    r, d = h2d.shape
    blk0 = row0 // TM_PROJ
    row = lambda i: (i, 0)
    seg = lambda i: (blk0 + i, 0)
    return pl.pallas_call(
        _combine_kernel,
        out_shape=jax.ShapeDtypeStruct((r, d), F32),
        grid=(r // TM_PROJ,),
        in_specs=[
            pl.BlockSpec((TM_PROJ, d), row),
            pl.BlockSpec((TM_PROJ, d), seg),
            pl.BlockSpec((TM_PROJ, d), seg),
            pl.BlockSpec((TM_PROJ, 2), seg),
            pl.BlockSpec((1, 1, N_MOD * d), mod_map),
        ],
        out_specs=pl.BlockSpec((TM_PROJ, d), row),
        compiler_params=_cparams(("parallel",)),
        name="moe_combine",
    )(h2d, y1, y2, gates_t, mod3)


def kernel(x, c, ctx, c_ctx, w_mod, b_mod, norm1_g, norm2_g, w_in, w_out, q_norm_g, k_norm_g, na_rpb,
           s5_lam_re, s5_lam_im, s5_log_dt, s5_b_re, s5_b_im, s5_c_re, s5_c_im, s5_d, s5_w_glu, s5_b_glu,
           router_w, router_bias, moe_w_gate, moe_w_up, moe_w_down):
    bsz, n_lat, d = x.shape
    n_ctx = ctx.shape[1]
    depth = w_mod.shape[0]
    ctx_row = bsz
    c_rows = jnp.concatenate([c.astype(F32), c_ctx.astype(F32)[None],
                              jnp.zeros((SUBLANES - bsz - 1, d), F32)], axis=0)
    mod_all = _modulation(c_rows, w_mod.astype(F32), b_mod.astype(F32))

    h_lat = x.reshape(bsz * n_lat, d).astype(F32)
    h_ctx = ctx.reshape(bsz * n_ctx, d).astype(F32)
    lat_map = _mod_row_map(n_lat, 0, True)
    ctx_map = _mod_row_map(n_ctx, ctx_row, False)
    bias_tabs = _na_bias_tables(na_rpb)
    s5_mats = _s5_matrices(s5_lam_re, s5_lam_im, s5_log_dt, s5_b_re, s5_b_im, s5_c_re, s5_c_im)

    for layer in range(depth):
        ctx_out = layer < depth - 1
        mod3 = mod_all[layer].reshape(SUBLANES, 1, N_MOD * d)
        w_in_b = w_in[layer].astype(BF16)
        qkvu_lat = _in_projection(h_lat, norm1_g[layer], mod3, w_in_b, lat_map, q_norm_g[layer], k_norm_g[layer])
        qkvu_ctx = _in_projection(h_ctx, norm1_g[layer], mod3, w_in_b, ctx_map, q_norm_g[layer], k_norm_g[layer])
        na_lat = _neighborhood_attention(qkvu_lat, qkvu_ctx, bias_tabs, layer, bsz)
        y_all = _s5_mixer(qkvu_lat, qkvu_ctx, s5_mats, layer, bsz)
        wglu_b = s5_w_glu[layer].astype(BF16)
        wout_b = w_out[layer].astype(BF16)
        h_lat, f_lat = _out_projection(na_lat, y_all, qkvu_lat, h_lat, mod3, s5_d[layer], wglu_b,
                                       s5_b_glu[layer], wout_b, norm2_g[layer],
                                       lat_map, n_lat, n_ctx // TM_PROJ)
        if ctx_out:
            na_ctx = _context_attention(qkvu_ctx, bsz)
            h_ctx, f_ctx = _out_projection(na_ctx, y_all, qkvu_ctx, h_ctx, mod3, s5_d[layer], wglu_b,
                                           s5_b_glu[layer], wout_b, norm2_g[layer],
                                           ctx_map, n_ctx, 0)
        n_l = bsz * n_lat
        idx, gates = _router(f_lat, router_w, router_bias)
        if ctx_out:
            idx_c, gates_c = _router(f_ctx, router_w, router_bias)
            idx = jnp.concatenate([idx, idx_c], axis=1)
            gates = jnp.concatenate([gates, gates_c], axis=1)
        n_tok = idx.shape[1]
        dest, pad_start, total_end, tile_expert, n_valid = _dispatch(idx)
        xs = _scatter_rows([f_lat, f_ctx] if ctx_out else [f_lat], dest, pad_start, total_end)
        ys = _experts(xs, tile_expert, n_valid, moe_w_gate, moe_w_up, moe_w_down, layer)
        y1 = jnp.take(ys, dest[:n_tok], axis=0, mode="clip")
        y2 = jnp.take(ys, dest[n_tok:], axis=0, mode="clip")
        gates_t = gates.T
        h_lat = _combine(h_lat, y1, y2, gates_t, mod3, lat_map, 0)
        if ctx_out:
            h_ctx = _combine(h_ctx, y1, y2, gates_t, mod3, ctx_map, n_l)
    return h_lat.reshape(bsz, n_lat, d).astype(x.dtype)
```

```python
import functools
import math

import jax
import jax.numpy as jnp
from jax import lax
from jax.experimental import pallas as pl
from jax.experimental.pallas import tpu as pltpu

F32 = jnp.float32
BF16 = jnp.bfloat16
HIGHEST = lax.Precision.HIGHEST

D_MODEL = 1024
GRID_W = 64
HEAD_DIM = 64
NA_WIDTH = 512
S5_WIDTH = 512
S5_CH = 16
S5_GROUPS = 32
S5_STATE = 64
WIN_ROWS = 8
WIN_COLS = 16
N_EXPERTS = 16
N_GROUPS = 4
EPG = 4
N_MOD = 6
EPS = 1e-6

LANES = 128
SUBLANES = 8
VMEM_LIMIT = 56 * 1024 * 1024

TM_PROJ = 256
Q_ROWS = 8
NA_ROW_TILES = 2
Q_COLS = 16
K_ROWS = 16
K_COLS = 32
CHUNK = 16
PAIRS_PER_STEP = 2
TM_EXP = 512
TM_ROUTE = 1024
MASK_VALUE = -1e30
RPB_LANE0 = 48


def _cparams(sem):
    return pltpu.CompilerParams(dimension_semantics=sem, vmem_limit_bytes=VMEM_LIMIT)


def _dot(a, b):
    return jnp.dot(a, b, preferred_element_type=F32)


def _dot_nt(a, b):
    return lax.dot_general(a, b, (((1,), (1,)), ((), ())), preferred_element_type=F32)


def _mod_kernel(c_ref, w_ref, b_ref, o_ref):
    a = c_ref[...]
    a = a * jax.nn.sigmoid(a)
    o_ref[0] = jnp.dot(a, w_ref[0], precision=HIGHEST, preferred_element_type=F32) + b_ref[0]


def _modulation(c_rows, w_mod, b_mod):
    depth, d, n = w_mod.shape
    tn = 1536
    return pl.pallas_call(
        _mod_kernel,
        out_shape=jax.ShapeDtypeStruct((depth, SUBLANES, n), F32),
        grid=(depth, n // tn),
        in_specs=[
            pl.BlockSpec((SUBLANES, d), lambda l, j: (0, 0)),
            pl.BlockSpec((1, d, tn), lambda l, j: (l, 0, j)),
            pl.BlockSpec((1, 1, tn), lambda l, j: (l, 0, j)),
        ],
        out_specs=pl.BlockSpec((1, SUBLANES, tn), lambda l, j: (l, 0, j)),
        compiler_params=_cparams(("arbitrary", "arbitrary")),
        name="modulation",
    )(c_rows, w_mod, b_mod.reshape(depth, 1, n))


def _inproj_kernel(x_ref, g_ref, mod_ref, w_ref, qg_ref, kg_ref, o_ref):
    x = x_ref[...]
    ms = jnp.mean(x * x, axis=-1, keepdims=True)
    y = x * lax.rsqrt(ms + EPS) * g_ref[...]
    shift = mod_ref[0, :, 0:D_MODEL]
    scale = mod_ref[0, :, D_MODEL:2 * D_MODEL]
    a = y * (1.0 + scale) + shift
    acc = _dot(a.astype(BF16), w_ref[...])
    lo = lax.broadcasted_iota(jnp.int32, (1, LANES), 1) < HEAD_DIM
    n_pairs = NA_WIDTH // LANES
    for blk in range(2 * n_pairs):
        cols = slice(blk * LANES, (blk + 1) * LANES)
        if blk < n_pairs:
            o_ref[:, cols] = _pair_rms(acc[:, cols], qg_ref[...], lo) * (HEAD_DIM ** -0.5)
        else:
            o_ref[:, cols] = _pair_rms(acc[:, cols], kg_ref[...], lo)
    o_ref[:, 2 * NA_WIDTH:] = acc[:, 2 * NA_WIDTH:]


def _mod_row_map(rows_per_batch, mod_row0, per_batch):
    tiles_per_batch = rows_per_batch // TM_PROJ
    if per_batch:
        return lambda i: (mod_row0 + i // tiles_per_batch, 0, 0)
    return lambda i: (mod_row0, 0, 0)


def _in_projection(x2d, g, mod3, w_bf16, mod_map, qg, kg):
    r, d = x2d.shape
    n = w_bf16.shape[1]
    g2 = lambda v: jnp.concatenate([v, v]).reshape(1, LANES).astype(F32)
    return pl.pallas_call(
        _inproj_kernel,
        out_shape=jax.ShapeDtypeStruct((r, n), F32),
        grid=(r // TM_PROJ,),
        in_specs=[
            pl.BlockSpec((TM_PROJ, d), lambda i: (i, 0)),
            pl.BlockSpec((1, d), lambda i: (0, 0)),
            pl.BlockSpec((1, 1, N_MOD * d), mod_map),
            pl.BlockSpec((d, n), lambda i: (0, 0)),
            pl.BlockSpec((1, LANES), lambda i: (0, 0)),
            pl.BlockSpec((1, LANES), lambda i: (0, 0)),
        ],
        out_specs=pl.BlockSpec((TM_PROJ, n), lambda i: (i, 0)),
        compiler_params=_cparams(("parallel",)),
        name="in_projection",
    )(x2d, g.reshape(1, d), mod3, w_bf16, g2(qg), g2(kg))


def _pair_rms(x, g, lo):
    ss = x * x
    sa = jnp.sum(jnp.where(lo, ss, 0.0), axis=-1, keepdims=True)
    sb = jnp.sum(jnp.where(lo, 0.0, ss), axis=-1, keepdims=True)
    ms = jnp.where(lo, sa, sb) * (1.0 / HEAD_DIM)
    return x * lax.rsqrt(ms + EPS) * g


def _softmax_pv(qm, kw, kcb, vw, vcb, bias):
    s_nb = _dot_nt(qm, kw) + bias
    s_cx = _dot_nt(qm, kcb)
    m = jnp.maximum(jnp.max(s_nb, axis=-1, keepdims=True), jnp.max(s_cx, axis=-1, keepdims=True))
    p_nb = jnp.exp(s_nb - m)
    p_cx = jnp.exp(s_cx - m)
    l = jnp.sum(p_nb, axis=-1, keepdims=True) + jnp.sum(p_cx, axis=-1, keepdims=True)
    o = _dot(p_nb.astype(BF16), vw) + _dot(p_cx.astype(BF16), vcb)
    return o / l


def _na_kernel(q_ref, k_ref, v_ref, kc_ref, vc_ref, bias_ref, o_ref):
    lo = lax.broadcasted_iota(jnp.int32, (1, LANES), 1) < HEAD_DIM
    n_rows = k_ref.shape[1]
    col_tiles = GRID_W // Q_COLS
    nq = Q_ROWS * Q_COLS
    nk = K_ROWS * K_COLS
    kcb = kc_ref[0].astype(BF16)
    vcb = vc_ref[0].astype(BF16)
    for rr in range(NA_ROW_TILES):
        i = pl.program_id(2) * NA_ROW_TILES + rr
        kr0 = jnp.clip(Q_ROWS * i - WIN_ROWS // 2, 0, n_rows - K_ROWS)
        rt = jnp.where(i == 0, 0, jnp.where(i == n_rows // Q_ROWS - 1, 2, 1))
        q_rows = slice(rr * Q_ROWS, (rr + 1) * Q_ROWS)
        for j in range(col_tiles):
            kc0 = min(max(Q_COLS * j - WIN_COLS // 2, 0), GRID_W - K_COLS)
            ct = 0 if j == 0 else (2 if j == col_tiles - 1 else 1)
            typ = rt * 3 + ct
            cols = slice(j * Q_COLS, (j + 1) * Q_COLS)
            qn = q_ref[0, q_rows, cols, :].reshape(nq, LANES)
            kw = k_ref[0, pl.ds(kr0, K_ROWS), kc0:kc0 + K_COLS, :].reshape(nk, LANES).astype(BF16)
            vw = v_ref[0, pl.ds(kr0, K_ROWS), kc0:kc0 + K_COLS, :].reshape(nk, LANES).astype(BF16)
            q2 = jnp.concatenate([jnp.where(lo, qn, 0.0), jnp.where(lo, 0.0, qn)], axis=0).astype(BF16)
            o2 = _softmax_pv(q2, kw, kcb, vw, vcb, bias_ref[0, 0, typ].reshape(2 * nq, nk))
            o_ref[0, q_rows, cols, :] = jnp.where(lo, o2[0:nq], o2[nq:2 * nq]).reshape(Q_ROWS, Q_COLS, LANES)


def _bias_table_kernel(rpb_ref, o_ref, tt_scr):
    n_off_r = 2 * WIN_ROWS - 1
    lane = lax.broadcasted_iota(jnp.int32, (Q_COLS, LANES), 1)
    qc = lax.broadcasted_iota(jnp.int32, (Q_COLS, LANES), 0)
    kc = lane % K_COLS
    lane_blk = lane // K_COLS
    per_vreg = LANES // K_COLS
    col_rel = (0, -WIN_COLS // 2, -WIN_COLS)
    col_origin = (0, Q_COLS, GRID_W - Q_COLS)
    row_rel = (0, -WIN_ROWS // 2, -WIN_ROWS)
    row_origin = (0, Q_ROWS, GRID_W - Q_ROWS)
    masked = jnp.full((Q_COLS, LANES), MASK_VALUE, F32)

    for ct in range(3):
        c_abs = col_origin[ct] + qc
        k_abs = col_origin[ct] + col_rel[ct] + kc
        start = jnp.clip(c_abs - WIN_COLS // 2, 0, GRID_W - WIN_COLS)
        valid_c = (k_abs >= start) & (k_abs < start + WIN_COLS)
        base = (1 - WIN_COLS - col_rel[ct] - RPB_LANE0) % LANES
        for ro in range(n_off_r):
            row = jnp.broadcast_to(rpb_ref[0, 0, ro:ro + 1, :], (Q_COLS, LANES))
            t = pltpu.roll(row, base, 1, stride=1, stride_axis=0)
            rep = t
            for m in range(1, per_vreg):
                rep = jnp.where(lane_blk == m, pltpu.roll(t, K_COLS * m, 1), rep)
            tt_scr[ct, ro] = jnp.where(valid_c, rep, MASK_VALUE)

    for rt in range(3):
        for ct in range(3):
            for qr in range(Q_ROWS):
                r_abs = row_origin[rt] + qr
                r_start = min(max(r_abs - WIN_ROWS // 2, 0), GRID_W - WIN_ROWS)
                for w in range(K_ROWS // per_vreg):
                    val = None
                    for m in range(per_vreg):
                        k_abs = row_origin[rt] + row_rel[rt] + per_vreg * w + m
                        ok = r_start <= k_abs < r_start + WIN_ROWS
                        src = tt_scr[ct, k_abs - r_abs + WIN_ROWS - 1] if ok else masked
                        val = src if val is None else jnp.where(lane_blk == m, src, val)
                    o_ref[0, 0, rt * 3 + ct, 0, qr * Q_COLS:(qr + 1) * Q_COLS, w * LANES:(w + 1) * LANES] = val


def _na_bias_tables(na_rpb):
    depth, h, n_r, n_c = na_rpb.shape
    rpb_pad = jnp.pad(na_rpb.astype(F32), ((0, 0), (0, 0), (0, 2 * SUBLANES - n_r),
                                            (RPB_LANE0, LANES - RPB_LANE0 - n_c)))
    nq, nk = Q_ROWS * Q_COLS, K_ROWS * K_COLS
    return pl.pallas_call(
        _bias_table_kernel,
        out_shape=jax.ShapeDtypeStruct((depth, h // 2, 9, 2, nq, nk), F32),
        grid=(depth, h),
        in_specs=[pl.BlockSpec((1, 1, 2 * SUBLANES, LANES), lambda l, i: (l, i, 0, 0))],
        out_specs=pl.BlockSpec((1, 1, 9, 1, nq, nk), lambda l, i: (l, i // 2, 0, i % 2, 0, 0)),
        scratch_shapes=[pltpu.VMEM((3, 2 * WIN_ROWS - 1, Q_COLS, LANES), F32)],
        compiler_params=_cparams(("parallel", "parallel")),
        name="na_bias_tables",
    )(rpb_pad)


def _neighborhood_attention(qkvu_lat, qkvu_ctx, bias_tabs, layer, bsz):
    n_lat = qkvu_lat.shape[0] // bsz
    n_ctx = qkvu_ctx.shape[0] // bsz
    rows = n_lat // GRID_W
    n_cols = qkvu_lat.shape[1]
    lat4 = qkvu_lat.reshape(bsz, rows, GRID_W, n_cols)
    ctx3 = qkvu_ctx.reshape(bsz, n_ctx, n_cols)
    n_pairs = NA_WIDTH // LANES
    step_rows = NA_ROW_TILES * Q_ROWS
    n_tiles = rows // step_rows
    out = pl.pallas_call(
        _na_kernel,
        out_shape=jax.ShapeDtypeStruct((bsz, rows, GRID_W, NA_WIDTH), F32),
        grid=(n_pairs, bsz, n_tiles),
        in_specs=[
            pl.BlockSpec((1, step_rows, GRID_W, LANES), lambda p, b, t: (b, t, 0, p)),
            pl.BlockSpec((1, rows, GRID_W, LANES), lambda p, b, t: (b, 0, 0, n_pairs + p)),
            pl.BlockSpec((1, rows, GRID_W, LANES), lambda p, b, t: (b, 0, 0, 2 * n_pairs + p)),
            pl.BlockSpec((1, n_ctx, LANES), lambda p, b, t: (b, 0, n_pairs + p)),
            pl.BlockSpec((1, n_ctx, LANES), lambda p, b, t: (b, 0, 2 * n_pairs + p)),
            pl.BlockSpec((1, 1, 9, 2, Q_ROWS * Q_COLS, K_ROWS * K_COLS), lambda p, b, t: (layer, p, 0, 0, 0, 0)),
        ],
        out_specs=pl.BlockSpec((1, step_rows, GRID_W, LANES), lambda p, b, t: (b, t, 0, p)),
        compiler_params=_cparams(("parallel", "parallel", "parallel")),
        name="neighborhood_attention",
    )(lat4, lat4, lat4, ctx3, ctx3, bias_tabs)
    return out.reshape(bsz * n_lat, NA_WIDTH)


def _ctx_attn_kernel(q_ref, k_ref, v_ref, o_ref):
    lo = lax.broadcasted_iota(jnp.int32, (1, LANES), 1) < HEAD_DIM
    qn = q_ref[0]
    kn = k_ref[0].astype(BF16)
    vb = v_ref[0].astype(BF16)

    def one(qm):
        s = _dot_nt(qm, kn)
        m = jnp.max(s, axis=-1, keepdims=True)
        p = jnp.exp(s - m)
        l = jnp.sum(p, axis=-1, keepdims=True)
        return _dot(p.astype(BF16), vb) / l

    o_a = one(jnp.where(lo, qn, 0.0).astype(BF16))
    o_b = one(jnp.where(lo, 0.0, qn).astype(BF16))
    o_ref[0] = jnp.where(lo, o_a, o_b)


def _context_attention(qkvu_ctx, bsz):
    n_ctx = qkvu_ctx.shape[0] // bsz
    ctx3 = qkvu_ctx.reshape(bsz, n_ctx, qkvu_ctx.shape[1])
    n_pairs = NA_WIDTH // LANES
    out = pl.pallas_call(
        _ctx_attn_kernel,
        out_shape=jax.ShapeDtypeStruct((bsz, n_ctx, NA_WIDTH), F32),
        grid=(bsz, n_pairs),
        in_specs=[
            pl.BlockSpec((1, n_ctx, LANES), lambda b, p: (b, 0, p)),
            pl.BlockSpec((1, n_ctx, LANES), lambda b, p: (b, 0, n_pairs + p)),
            pl.BlockSpec((1, n_ctx, LANES), lambda b, p: (b, 0, 2 * n_pairs + p)),
        ],
        out_specs=pl.BlockSpec((1, n_ctx, LANES), lambda b, p: (b, 0, p)),
        compiler_params=_cparams(("parallel", "parallel")),
        name="context_attention",
    )(ctx3, ctx3, ctx3)
    return out.reshape(bsz * n_ctx, NA_WIDTH)


def _s5_kernel(uc_ref, ul_ref, w_ref, m_ref, v_ref, a_ref, y_ref, x_scr, s_scr, hf_scr, hr_scr, *, n_ctx_chunks):
    bsz = ul_ref.shape[0]
    n_lat_chunks = ul_ref.shape[1] // CHUNK
    n_chunks = n_ctx_chunks + n_lat_chunks
    rows = n_chunks * SUBLANES
    n_pairs = w_ref.shape[1]
    gpb = 2 * n_pairs
    half = 2 * LANES
    tile_chunks = TM_PROJ // CHUNK
    lane_blk = lax.broadcasted_iota(jnp.int32, (n_chunks, LANES), 1) // S5_CH

    for b in range(bsz):
        for q in range(2):
            rolled = []
            for j in range(SUBLANES):
                s = SUBLANES * q + j
                u_s = jnp.concatenate([uc_ref[b, pl.ds(s, n_ctx_chunks, stride=CHUNK), :],
                                       ul_ref[b, pl.ds(s, n_lat_chunks, stride=CHUNK), :]], axis=0)
                rolled.append(u_s if j == 0 else pltpu.roll(u_s, j * S5_CH, 1))
            for g in range(gpb):
                xg = rolled[0]
                for j in range(1, SUBLANES):
                    xg = jnp.where(lane_blk == (g + j) % SUBLANES, rolled[j], xg)
                x_scr[g // 2, q, pl.ds((g % 2) * bsz + b, n_chunks, stride=SUBLANES), :] = xg

    n_blk = 8
    rb = rows // n_blk
    first_group = (lax.broadcasted_iota(jnp.int32, (rb, half), 0) & (SUBLANES // 2)) == 0
    fwd_cols = (lax.broadcasted_iota(jnp.int32, (rb, half), 1) & (LANES - 1)) < S5_STATE
    is_fwd = lax.broadcasted_iota(jnp.int32, (SUBLANES, LANES), 1) < S5_STATE
    first_rows = lax.broadcasted_iota(jnp.int32, (SUBLANES, half), 0) < SUBLANES // 2
    zero = jnp.zeros((SUBLANES, LANES), F32)

    def x_rows(p, sl):
        return jnp.concatenate([x_scr[p, 0, sl, :], x_scr[p, 1, sl, :]], axis=1).astype(BF16)

    def put_cols(scr, i, sl, val):
        scr[i, 0, sl, :] = val[:, 0:LANES]
        scr[i, 1, sl, :] = val[:, LANES:half]

    for hh in range(n_pairs // PAIRS_PER_STEP):
        pairs = [hh * PAIRS_PER_STEP + i for i in range(PAIRS_PER_STEP)]
        for i, p in enumerate(pairs):
            for blk in range(n_blk):
                sl = slice(blk * rb, (blk + 1) * rb)
                r = _dot(x_rows(p, sl), w_ref[0, p])
                put_cols(s_scr, i, sl, jnp.where(first_group, r[:, :half], r[:, half:]))

        a_pair = [jnp.where(first_rows, a_ref[0, 2 * p], a_ref[0, 2 * p + 1]) for p in pairs]
        a_re = [a[:, 0:LANES] for a in a_pair]
        a_im = [a[:, LANES:half] for a in a_pair]

        def body(k, carry):
            kr = jnp.where(k < n_ctx_chunks, n_ctx_chunks - 1 - k, n_chunks + n_ctx_chunks - 1 - k)
            rf = pl.ds(pl.multiple_of(k * SUBLANES, SUBLANES), SUBLANES)
            rr = pl.ds(pl.multiple_of(kr * SUBLANES, SUBLANES), SUBLANES)
            new = []
            for i in range(PAIRS_PER_STEP):
                h_re, h_im = carry[2 * i], carry[2 * i + 1]
                hf_scr[i, 0, rf, :] = h_re
                hf_scr[i, 1, rf, :] = h_im
                hr_scr[i, 0, rr, :] = h_re
                hr_scr[i, 1, rr, :] = h_im
                s_re = jnp.where(is_fwd, s_scr[i, 0, rf, :], s_scr[i, 0, rr, :])
                s_im = jnp.where(is_fwd, s_scr[i, 1, rf, :], s_scr[i, 1, rr, :])
                new.append(a_re[i] * h_re - a_im[i] * h_im + s_re)
                new.append(a_re[i] * h_im + a_im[i] * h_re + s_im)
            return tuple(new)

        lax.fori_loop(0, n_chunks, body, (zero,) * (2 * PAIRS_PER_STEP))

        for i, p in enumerate(pairs):
            for blk in range(n_blk):
                sl = slice(blk * rb, (blk + 1) * rb)
                hf = jnp.concatenate([hf_scr[i, 0, sl, :], hf_scr[i, 1, sl, :]], axis=1)
                hr = jnp.concatenate([hr_scr[i, 0, sl, :], hr_scr[i, 1, sl, :]], axis=1)
                h_in = jnp.where(fwd_cols, hf, hr).astype(BF16)
                r = _dot(x_rows(p, sl), m_ref[0, p]) + _dot_nt(h_in, v_ref[0, p])
                put_cols(s_scr, i, sl, jnp.where(first_group, r[:, :half], r[:, half:]))

        g_lo = 2 * pairs[0]
        n_g = 2 * PAIRS_PER_STEP
        out_blk = lax.broadcasted_iota(jnp.int32, (tile_chunks, LANES), 1) // S5_CH
        lanes_out = (out_blk >= g_lo) & (out_blk < g_lo + n_g)
        for b in range(bsz):
            for q in range(2):
                y_g = [s_scr[(g - g_lo) // 2, q, pl.ds((g % 2) * bsz + b, n_chunks, stride=SUBLANES), :]
                       for g in range(g_lo, g_lo + n_g)]
                for j in range(SUBLANES):
                    z = y_g[0]
                    for gi in range(1, n_g):
                        z = jnp.where(lane_blk == (g_lo + gi + j) % SUBLANES, y_g[gi], z)
                    if j:
                        z = pltpu.roll(z, LANES - j * S5_CH, 1)
                    t = SUBLANES * q + j
                    for ct in range(n_chunks // tile_chunks):
                        r0 = ct * TM_PROJ + t * tile_chunks
                        pltpu.store(y_ref.at[b, 0, r0:r0 + tile_chunks, :],
                                    z[ct * tile_chunks:(ct + 1) * tile_chunks, :], mask=lanes_out)


def _s5_scan(qkvu_ctx3, qkvu_lat3, mats, layer):
    w_c, m_c, vt_c, a_c = mats
    bsz, n_ctx, _ = qkvu_ctx3.shape
    n_lat = qkvu_lat3.shape[1]
    n_seq = n_ctx + n_lat
    rows = n_seq // CHUNK * SUBLANES
    n_blocks = S5_WIDTH // LANES
    ppb = w_c.shape[1] // n_blocks
    u_blk0 = 3 * NA_WIDTH // LANES
    wspec = pl.BlockSpec((1, ppb, 2 * LANES, 4 * LANES), lambda i: (layer, i, 0, 0))
    one = pl.Buffered(1)
    state = pltpu.VMEM((PAIRS_PER_STEP, 2, rows, LANES), F32)
    return pl.pallas_call(
        functools.partial(_s5_kernel, n_ctx_chunks=n_ctx // CHUNK),
        out_shape=jax.ShapeDtypeStruct((bsz, n_blocks, n_seq, LANES), F32),
        grid=(n_blocks,),
        in_specs=[
            pl.BlockSpec((bsz, n_ctx, LANES), lambda i: (0, 0, u_blk0 + i)),
            pl.BlockSpec((bsz, n_lat, LANES), lambda i: (0, 0, u_blk0 + i), pipeline_mode=one),
            wspec, wspec,
            pl.BlockSpec((1, ppb, 4 * LANES, 2 * LANES), lambda i: (layer, i, 0, 0)),
            pl.BlockSpec((1, 2 * ppb, SUBLANES, 2 * LANES), lambda i: (layer, i, 0, 0)),
        ],
        out_specs=pl.BlockSpec((bsz, 1, n_seq, LANES), lambda i: (0, i, 0, 0), pipeline_mode=one),
        scratch_shapes=[pltpu.VMEM((ppb, 2, rows, LANES), F32), state, state, state],
        compiler_params=_cparams(("parallel",)),
        name="s5_scan",
    )(qkvu_ctx3, qkvu_lat3, w_c, m_c, vt_c, a_c)


def _s5_mats_kernel(prm_ref, btr_ref, bti_ref, cr_ref, ci_ref, w_ref, m_ref, vt_ref, a_ref):
    t = CHUNK
    gl = pl.program_id(1) % SUBLANES
    is_fwd = lax.broadcasted_iota(jnp.int32, (1, LANES), 1) < S5_STATE
    lr = prm_ref[0, 0, 0:1, :]
    li = prm_ref[0, 0, 1:2, :]
    dt = jnp.exp(prm_ref[0, 0, 2:3, :])
    n = lax.broadcasted_iota(jnp.int32, (3 * SUBLANES, LANES), 0).astype(F32)
    pmag = jnp.exp(n * (lr * dt))
    pw_re = pmag * jnp.cos(n * (li * dt))
    pw_im = pmag * jnp.sin(n * (li * dt))
    ab_re, ab_im = pw_re[1:2, :], pw_im[1:2, :]
    den = lr * lr + li * li
    nr = ab_re - 1.0
    z_re = (nr * lr + ab_im * li) / den
    z_im = (ab_im * lr - nr * li) / den
    bt_re, bt_im = btr_ref[0, 0], bti_ref[0, 0]
    bb_re = z_re * bt_re - z_im * bt_im
    bb_im = z_re * bt_im + z_im * bt_re
    c_re, c_im = cr_ref[0, 0], ci_ref[0, 0]

    def powers(n_fwd, n_rev):
        return (jnp.where(is_fwd, pw_re[n_fwd:n_fwd + 1, :], pw_re[n_rev:n_rev + 1, :]),
                jnp.where(is_fwd, pw_im[n_fwd:n_fwd + 1, :], pw_im[n_rev:n_rev + 1, :]))

    def block_rows(s):
        pos = SUBLANES * (s // SUBLANES) + (s % SUBLANES + gl) % SUBLANES
        return pl.ds(pl.multiple_of(pos * S5_CH, S5_CH), S5_CH)

    for s in range(t):
        rows = block_rows(s)
        p_re, p_im = powers(t - 1 - s, s)
        w_ref[0, 0, rows, 0:LANES] = (bb_re * p_re - bb_im * p_im).astype(BF16)
        w_ref[0, 0, rows, LANES:2 * LANES] = (bb_re * p_im + bb_im * p_re).astype(BF16)
        q_re, q_im = powers(s + 1, t - s)
        vt_ref[0, 0, rows, 0:LANES] = (c_re * q_re - c_im * q_im).astype(BF16)
        vt_ref[0, 0, rows, LANES:2 * LANES] = (-(c_re * q_im + c_im * q_re)).astype(BF16)

    ca_re, ca_im = [], []
    for lag in range(t):
        p_re, p_im = powers(lag, t - 1 - lag)
        ca_re.append(c_re * p_re - c_im * p_im)
        ca_im.append(c_re * p_im + c_im * p_re)
    stack = jnp.concatenate([jnp.concatenate(ca_re, axis=0), jnp.concatenate(ca_im, axis=0)], axis=1)
    zero = jnp.zeros_like(bb_re)
    lhs = jnp.concatenate([
        jnp.concatenate([jnp.where(is_fwd, bb_re, zero), jnp.where(is_fwd, -bb_im, zero)], axis=1),
        jnp.concatenate([jnp.where(is_fwd, zero, bb_re), jnp.where(is_fwd, zero, -bb_im)], axis=1)], axis=0)
    kt = lax.dot_general(lhs, stack, (((1,), (1,)), ((), ())), precision=HIGHEST, preferred_element_type=F32)
    kt_f, kt_r = kt[0:S5_CH], kt[S5_CH:2 * S5_CH]
    blk = lax.broadcasted_iota(jnp.int32, (S5_CH, 2 * LANES), 1) // S5_CH
    for s in range(t):
        strip = (jnp.where(blk >= s, pltpu.roll(kt_f, S5_CH * s, 1), 0.0)
                 + jnp.where(blk <= s, pltpu.roll(kt_r, (S5_CH * (s - t + 1)) % (2 * LANES), 1), 0.0))
        strip = jnp.concatenate([pltpu.roll(strip[:, 0:LANES], gl * S5_CH, 1),
                                 pltpu.roll(strip[:, LANES:2 * LANES], gl * S5_CH, 1)], axis=1)
        m_ref[0, 0, block_rows(s), :] = strip.astype(BF16)

    a_ref[0, 0, :, 0:LANES] = jnp.broadcast_to(pw_re[t:t + 1, :], (SUBLANES, LANES))
    a_ref[0, 0, :, LANES:2 * LANES] = jnp.broadcast_to(pw_im[t:t + 1, :], (SUBLANES, LANES))


def _s5_matrices(lam_re, lam_im, log_dt, b_re, b_im, c_re, c_im):
    depth, _, g, p = lam_re.shape
    hc = b_re.shape[-1]
    width = CHUNK * hc
    both = lambda x: jnp.transpose(x.astype(F32), (0, 2, 1, 3)).reshape(depth, g, 1, 2 * p)
    dt_rows = jnp.broadcast_to(jnp.transpose(log_dt.astype(F32), (0, 2, 1))[..., None], (depth, g, 2, p))
    prm = jnp.concatenate([both(lam_re), both(lam_im), dt_rows.reshape(depth, g, 1, 2 * p),
                           jnp.zeros((depth, g, SUBLANES - 3, 2 * p), F32)], axis=2)
    bt = lambda x: jnp.transpose(x.astype(F32), (0, 2, 4, 1, 3)).reshape(depth, g, hc, 2 * p)
    ct = lambda x: jnp.transpose(x.astype(F32), (0, 2, 3, 1, 4)).reshape(depth, g, hc, 2 * p)
    vec = lambda rows: pl.BlockSpec((1, 1, rows, 2 * p), lambda l, i: (l, i, 0, 0))
    return pl.pallas_call(
        _s5_mats_kernel,
        out_shape=(jax.ShapeDtypeStruct((depth, g // 2, width, 2 * width), BF16),
                   jax.ShapeDtypeStruct((depth, g // 2, width, 2 * width), BF16),
                   jax.ShapeDtypeStruct((depth, g // 2, 2 * width, width), BF16),
                   jax.ShapeDtypeStruct((depth, g, SUBLANES, 4 * p), F32)),
        grid=(depth, g),
        in_specs=[vec(SUBLANES), vec(hc), vec(hc), vec(hc), vec(hc)],
        out_specs=(pl.BlockSpec((1, 1, width, width), lambda l, i: (l, i // 2, 0, i % 2)),
                   pl.BlockSpec((1, 1, width, width), lambda l, i: (l, i // 2, 0, i % 2)),
                   pl.BlockSpec((1, 1, width, width), lambda l, i: (l, i // 2, i % 2, 0)),
                   pl.BlockSpec((1, 1, SUBLANES, 4 * p), lambda l, i: (l, i, 0, 0))),
        compiler_params=_cparams(("parallel", "parallel")),
        name="s5_matrices",
    )(prm, bt(b_re), bt(b_im), ct(c_re), ct(c_im))


def _s5_mixer(qkvu_lat, qkvu_ctx, mats, layer, bsz):
    n_cols = qkvu_lat.shape[1]
    return _s5_scan(qkvu_ctx.reshape(bsz, -1, n_cols), qkvu_lat.reshape(bsz, -1, n_cols), mats, layer)


def _outproj_kernel(na_ref, y_ref, u_ref, h_ref, mod_ref, d_ref, wglu_ref, bglu_ref, wout_ref, g2_ref,
                    ho_ref, f_ref):
    tile_chunks = TM_PROJ // CHUNK
    y = jnp.concatenate(
        [jnp.concatenate([y_ref[0, blk, pl.ds(c, CHUNK, stride=tile_chunks), :] for c in range(tile_chunks)], axis=0)
         for blk in range(S5_WIDTH // LANES)], axis=1)
    z = jax.nn.gelu(y + d_ref[...] * u_ref[...])
    s5 = z * jax.nn.sigmoid(_dot(z.astype(BF16), wglu_ref[...]) + bglu_ref[...])
    mix = (_dot(na_ref[...].astype(BF16), wout_ref[0:NA_WIDTH, :])
           + _dot(s5.astype(BF16), wout_ref[NA_WIDTH:NA_WIDTH + S5_WIDTH, :]))
    d = D_MODEL
    gate = mod_ref[0, :, 2 * d:3 * d]
    h = h_ref[...] + gate * mix
    ho_ref[...] = h
    ms = jnp.mean(h * h, axis=-1, keepdims=True)
    y2 = h * lax.rsqrt(ms + EPS) * g2_ref[...]
    f_ref[...] = y2 * (1.0 + mod_ref[0, :, 4 * d:5 * d]) + mod_ref[0, :, 3 * d:4 * d]


def _out_projection(na, y_all, qkvu, h2d, mod3, d_skip, wglu_bf16, b_glu, wout_bf16, g2,
                    mod_map, rows_per_batch, y_block0):
    r, d = h2d.shape
    tiles_per_batch = rows_per_batch // TM_PROJ
    u_blk = 3 * NA_WIDTH // S5_WIDTH

    def y_map(i):
        return (i // tiles_per_batch, 0, y_block0 + i % tiles_per_batch, 0)

    const = lambda i: (0, 0)
    return pl.pallas_call(
        _outproj_kernel,
        out_shape=(jax.ShapeDtypeStruct((r, d), F32), jax.ShapeDtypeStruct((r, d), F32)),
        grid=(r // TM_PROJ,),
        in_specs=[
            pl.BlockSpec((TM_PROJ, NA_WIDTH), lambda i: (i, 0)),
            pl.BlockSpec((1, S5_WIDTH // LANES, TM_PROJ, LANES), y_map),
            pl.BlockSpec((TM_PROJ, S5_WIDTH), lambda i: (i, u_blk)),
            pl.BlockSpec((TM_PROJ, d), lambda i: (i, 0)),
            pl.BlockSpec((1, 1, N_MOD * d), mod_map),
            pl.BlockSpec((1, S5_WIDTH), const),
            pl.BlockSpec((S5_WIDTH, S5_WIDTH), const),
            pl.BlockSpec((1, S5_WIDTH), const),
            pl.BlockSpec((NA_WIDTH + S5_WIDTH, d), const),
            pl.BlockSpec((1, d), const),
        ],
        out_specs=(pl.BlockSpec((TM_PROJ, d), lambda i: (i, 0)), pl.BlockSpec((TM_PROJ, d), lambda i: (i, 0))),
        compiler_params=_cparams(("parallel",)),
        name="out_projection",
    )(na, y_all, qkvu, h2d, mod3, d_skip.reshape(1, -1), wglu_bf16, b_glu.reshape(1, -1), wout_bf16,
      g2.reshape(1, d))


def _top2(vals):
    best = vals[0]
    bi = jnp.zeros(best.shape, jnp.int32)
    for i in range(1, len(vals)):
        gt = vals[i] > best
        best = jnp.where(gt, vals[i], best)
        bi = jnp.where(gt, i, bi)
    second = jnp.full(best.shape, -jnp.inf, F32)
    si = jnp.zeros(best.shape, jnp.int32)
    for i in range(len(vals)):
        cand = jnp.where(bi == i, -jnp.inf, vals[i])
        gt = cand > second
        second = jnp.where(gt, cand, second)
        si = jnp.where(gt, i, si)
    return best, bi, second, si


def _route(f, rwt, rb):
    logits = lax.dot_general(rwt, f, (((1,), (1,)), ((), ())),
                             precision=HIGHEST, preferred_element_type=F32)
    m = jnp.max(logits, axis=0, keepdims=True)
    e = jnp.exp(logits - m)
    probs = e / jnp.sum(e, axis=0, keepdims=True)
    sel = probs + rb
    sel_rows = [sel[i:i + 1, :] for i in range(N_EXPERTS)]
    prob_rows = [probs[i:i + 1, :] for i in range(N_EXPERTS)]
    scores = []
    for g in range(N_GROUPS):
        b, _, s, _ = _top2(sel_rows[g * EPG:(g + 1) * EPG])
        scores.append(b + s)
    grp = jnp.zeros(scores[0].shape, jnp.int32)
    gbest = scores[0]
    for g in range(1, N_GROUPS):
        gt = scores[g] > gbest
        gbest = jnp.where(gt, scores[g], gbest)
        grp = jnp.where(gt, g, grp)
    in_rows = []
    for j in range(EPG):
        v = sel_rows[j]
        for g in range(1, N_GROUPS):
            v = jnp.where(grp == g, sel_rows[g * EPG + j], v)
        in_rows.append(v)
    _, l1, _, l2 = _top2(in_rows)
    i1 = grp * EPG + l1
    i2 = grp * EPG + l2
    w1 = jnp.zeros(gbest.shape, F32)
    w2 = jnp.zeros(gbest.shape, F32)
    for i in range(N_EXPERTS):
        w1 = jnp.where(i1 == i, prob_rows[i], w1)
        w2 = jnp.where(i2 == i, prob_rows[i], w2)
    tot = w1 + w2
    return i1, i2, w1 / tot, w2 / tot


def _router_kernel(f_ref, rwt_ref, rb_ref, idx_ref, gate_ref):
    i1, i2, g1, g2 = _route(f_ref[...], rwt_ref[...], rb_ref[...])
    idx_ref[0:1, :] = i1
    idx_ref[1:2, :] = i2
    gate_ref[0:1, :] = g1
    gate_ref[1:2, :] = g2


def _router(f_all, router_wt, router_b):
    n, d = f_all.shape
    return pl.pallas_call(
        _router_kernel,
        out_shape=(jax.ShapeDtypeStruct((2, n), jnp.int32), jax.ShapeDtypeStruct((2, n), F32)),
        grid=(n // TM_ROUTE,),
        in_specs=[
            pl.BlockSpec((TM_ROUTE, d), lambda i: (i, 0)),
            pl.BlockSpec((N_EXPERTS, d), lambda i: (0, 0)),
            pl.BlockSpec((N_EXPERTS, 1), lambda i: (0, 0)),
        ],
        out_specs=(pl.BlockSpec((2, TM_ROUTE), lambda i: (0, i)), pl.BlockSpec((2, TM_ROUTE), lambda i: (0, i))),
        compiler_params=_cparams(("parallel",)),
        name="router",
    )(f_all, router_wt, router_b)


def _scatter_rows_kernel(dest_ref, pad_ref, end_ref, *refs, n_tok, seg_tiles):
    f_refs = refs[:len(seg_tiles)]
    xs_ref, zero_scr, stage, sems = refs[len(seg_tiles):]
    i = pl.program_id(0)
    tm = f_refs[0].shape[0]
    n_rows = xs_ref.shape[0]
    slab = TM_EXP + SUBLANES
    fill_sem = sems.at[2]

    def slab_copy(start, rows):
        return pltpu.make_async_copy(zero_scr.at[pl.ds(0, rows), :], xs_ref.at[pl.ds(start, rows), :], fill_sem)

    @pl.when(i == 0)
    def _():
        zero_scr[...] = jnp.zeros(zero_scr.shape, zero_scr.dtype)
        for e in range(N_EXPERTS):
            start = jnp.minimum((pad_ref[e] // SUBLANES) * SUBLANES, n_rows - slab)
            slab_copy(pl.multiple_of(start, SUBLANES), slab).start()
        for e in range(N_EXPERTS):
            slab_copy(0, slab).wait()
        for k in range(N_EXPERTS):
            start = end_ref[0] + k * TM_EXP

            @pl.when(start < n_rows)
            def _():
                cp = slab_copy(pl.multiple_of(start, TM_EXP), TM_EXP)
                cp.start()
                cp.wait()

    slot = i % 2

    def wait_tile(s):
        for _ in range(2):
            pltpu.make_async_copy(stage.at[s], xs_ref.at[pl.ds(0, tm), :], sems.at[s]).wait()

    def scatter_tile(f_ref):
        base = i * tm
        stage[slot] = f_ref[...]

        def row_copy(r, d):
            return pltpu.make_async_copy(stage.at[slot, pl.ds(r, 1), :], xs_ref.at[pl.ds(d, 1), :], sems.at[slot])

        for r in range(tm):
            row_copy(r, dest_ref[base + r]).start()
            row_copy(r, dest_ref[n_tok + base + r]).start()

    tile0 = 0
    for f_ref, n_t in zip(f_refs, seg_tiles):
        pl.when((i >= tile0) & (i < tile0 + n_t))(functools.partial(scatter_tile, f_ref))
        tile0 += n_t

    pl.when(i > 0)(lambda: wait_tile(1 - slot))
    pl.when(i == pl.num_programs(0) - 1)(lambda: wait_tile(slot))


def _scatter_rows(segments, dest_flat, pad_start, total_end):
    d = segments[0].shape[1]
    seg_tiles = tuple(s.shape[0] // TM_PROJ for s in segments)
    n_tok = sum(s.shape[0] for s in segments)
    r_max = 2 * n_tok + N_EXPERTS * TM_EXP
    in_specs = []
    tile0 = 0
    for n_t in seg_tiles:
        in_specs.append(pl.BlockSpec(
            (TM_PROJ, d), lambda i, *_, t0=tile0, nt=n_t: (jnp.clip(i - t0, 0, nt - 1), 0)))
        tile0 += n_t
    grid_spec = pltpu.PrefetchScalarGridSpec(
        num_scalar_prefetch=3,
        grid=(tile0,),
        in_specs=in_specs,
        out_specs=pl.BlockSpec(memory_space=pl.ANY),
        scratch_shapes=[pltpu.VMEM((TM_EXP + SUBLANES, d), F32), pltpu.VMEM((2, TM_PROJ, d), F32),
                        pltpu.SemaphoreType.DMA((3,))],
    )
    return pl.pallas_call(
        functools.partial(_scatter_rows_kernel, n_tok=n_tok, seg_tiles=seg_tiles),
        out_shape=jax.ShapeDtypeStruct((r_max, d), F32),
        grid_spec=grid_spec,
        compiler_params=_cparams(("arbitrary",)),
        name="moe_scatter_rows",
    )(dest_flat, pad_start, total_end, *segments)


def _experts_kernel(te_ref, nv_ref, x_ref, wg_ref, wu_ref, wd_ref, o_ref, wg_scr, wu_scr, wd_scr):
    i = pl.program_id(0)
    e = te_ref[i]
    prev = te_ref[jnp.maximum(i - 1, 0)]
    rows = 128

    @pl.when((i == 0) | (e != prev))
    def _():
        def body(r, carry):
            sl = pl.ds(pl.multiple_of(r * rows, rows), rows)
            wg_scr[sl, :] = wg_ref[0, 0, sl, :].astype(BF16)
            wu_scr[sl, :] = wu_ref[0, 0, sl, :].astype(BF16)
            wd_scr[sl, :] = wd_ref[0, 0, sl, :].astype(BF16)
            return carry
        lax.fori_loop(0, wg_scr.shape[0] // rows, body, 0)

    @pl.when(i < nv_ref[0])
    def _():
        x = x_ref[...].astype(BF16)
        g = _dot(x, wg_scr[...])
        u = _dot(x, wu_scr[...])
        a = (g * jax.nn.sigmoid(g)) * u
        o_ref[...] = _dot(a.astype(BF16), wd_scr[...]).astype(BF16)

    @pl.when(i >= nv_ref[0])
    def _():
        o_ref[...] = jnp.zeros(o_ref.shape, BF16)


def _experts(xs, tile_expert, n_valid, w_gate, w_up, w_down, layer):
    r, d = xs.shape
    de = w_gate.shape[3]
    n_tiles = r // TM_EXP
    x_map = lambda i, te, nv: (jnp.minimum(i, nv[0] - 1), 0)
    grid_spec = pltpu.PrefetchScalarGridSpec(
        num_scalar_prefetch=2,
        grid=(n_tiles,),
        in_specs=[
            pl.BlockSpec((TM_EXP, d), x_map),
            pl.BlockSpec((1, 1, d, de), lambda i, te, nv: (layer, te[i], 0, 0)),
            pl.BlockSpec((1, 1, d, de), lambda i, te, nv: (layer, te[i], 0, 0)),
            pl.BlockSpec((1, 1, de, d), lambda i, te, nv: (layer, te[i], 0, 0)),
        ],
        out_specs=pl.BlockSpec((TM_EXP, d), lambda i, te, nv: (i, 0)),
        scratch_shapes=[pltpu.VMEM((d, de), BF16), pltpu.VMEM((d, de), BF16), pltpu.VMEM((de, d), BF16)],
    )
    return pl.pallas_call(
        _experts_kernel,
        out_shape=jax.ShapeDtypeStruct((r, d), BF16),
        grid_spec=grid_spec,
        compiler_params=_cparams(("arbitrary",)),
        name="experts",
    )(tile_expert, n_valid, xs, w_gate, w_up, w_down)


def _dispatch(idx):
    n = idx.shape[1]
    e_flat = idx.reshape(-1)
    onehot = (e_flat[:, None] == jnp.arange(N_EXPERTS, dtype=jnp.int32)[None, :]).astype(jnp.int32)
    csum = jnp.cumsum(onehot, axis=0)
    rank = jnp.sum(csum * onehot, axis=1) - 1
    counts = csum[-1]
    padded = ((counts + TM_EXP - 1) // TM_EXP) * TM_EXP
    ends = jnp.cumsum(padded)
    starts = ends - padded
    dest = (jnp.sum(onehot * starts[None, :], axis=1) + rank).astype(jnp.int32)
    r_max = 2 * n + N_EXPERTS * TM_EXP
    tile_start = jnp.arange(r_max // TM_EXP, dtype=jnp.int32) * TM_EXP
    tile_expert = jnp.minimum(jnp.sum((tile_start[:, None] >= ends[None, :]).astype(jnp.int32), axis=1),
                              N_EXPERTS - 1).astype(jnp.int32)
    n_valid = (ends[-1] // TM_EXP).astype(jnp.int32).reshape(1)
    pad_start = (starts + counts).astype(jnp.int32)
    total_end = ends[-1].astype(jnp.int32).reshape(1)
    return dest, pad_start, total_end, tile_expert, n_valid


def _combine_kernel(h_ref, y1_ref, y2_ref, gate_ref, mod_ref, o_ref):
    d = D_MODEL
    g = gate_ref[...]
    y = g[:, 0:1] * y1_ref[...].astype(F32) + g[:, 1:2] * y2_ref[...].astype(F32)
    o_ref[...] = h_ref[...] + mod_ref[0, :, 5 * d:6 * d] * y


def _combine(h2d, y1, y2, gates_t, mod3, mod_map, row0):
    r, d = h2d.shape
    blk0 = row0 // TM_PROJ
    row = lambda i: (i, 0)
    seg = lambda i: (blk0 + i, 0)
    return pl.pallas_call(
        _combine_kernel,
        out_shape=jax.ShapeDtypeStruct((r, d), F32),
        grid=(r // TM_PROJ,),
        in_specs=[
            pl.BlockSpec((TM_PROJ, d), row),
            pl.BlockSpec((TM_PROJ, d), seg),
            pl.BlockSpec((TM_PROJ, d), seg),
            pl.BlockSpec((TM_PROJ, 2), seg),
            pl.BlockSpec((1, 1, N_MOD * d), mod_map),
        ],
        out_specs=pl.BlockSpec((TM_PROJ, d), row),
        compiler_params=_cparams(("parallel",)),
        name="moe_combine",
    )(h2d, y1, y2, gates_t, mod3)


def kernel(x, c, ctx, c_ctx, w_mod, b_mod, norm1_g, norm2_g, w_in, w_out, q_norm_g, k_norm_g, na_rpb,
           s5_lam_re, s5_lam_im, s5_log_dt, s5_b_re, s5_b_im, s5_c_re, s5_c_im, s5_d, s5_w_glu, s5_b_glu,
           router_w, router_bias, moe_w_gate, moe_w_up, moe_w_down):
    bsz, n_lat, d = x.shape
    n_ctx = ctx.shape[1]
    depth = w_mod.shape[0]
    ctx_row = bsz
    c_rows = jnp.concatenate([c.astype(F32), c_ctx.astype(F32)[None],
                              jnp.zeros((SUBLANES - bsz - 1, d), F32)], axis=0)
    mod_all = _modulation(c_rows, w_mod.astype(F32), b_mod.astype(F32))

    h_lat = x.reshape(bsz * n_lat, d).astype(F32)
    h_ctx = ctx.reshape(bsz * n_ctx, d).astype(F32)
    lat_map = _mod_row_map(n_lat, 0, True)
    ctx_map = _mod_row_map(n_ctx, ctx_row, False)
    bias_tabs = _na_bias_tables(na_rpb)
    router_wt = router_w.T.astype(F32)
    router_b = router_bias.reshape(N_EXPERTS, 1).astype(F32)
    s5_mats = _s5_matrices(s5_lam_re, s5_lam_im, s5_log_dt, s5_b_re, s5_b_im, s5_c_re, s5_c_im)

    for layer in range(depth):
        ctx_out = layer < depth - 1
        mod3 = mod_all[layer].reshape(SUBLANES, 1, N_MOD * d)
        w_in_b = w_in[layer].astype(BF16)
        qkvu_lat = _in_projection(h_lat, norm1_g[layer], mod3, w_in_b, lat_map, q_norm_g[layer], k_norm_g[layer])
        qkvu_ctx = _in_projection(h_ctx, norm1_g[layer], mod3, w_in_b, ctx_map, q_norm_g[layer], k_norm_g[layer])
        na_lat = _neighborhood_attention(qkvu_lat, qkvu_ctx, bias_tabs, layer, bsz)
        y_all = _s5_mixer(qkvu_lat, qkvu_ctx, s5_mats, layer, bsz)
        wglu_b = s5_w_glu[layer].astype(BF16)
        wout_b = w_out[layer].astype(BF16)
        h_lat, f_lat = _out_projection(na_lat, y_all, qkvu_lat, h_lat, mod3, s5_d[layer], wglu_b,
                                       s5_b_glu[layer], wout_b, norm2_g[layer],
                                       lat_map, n_lat, n_ctx // TM_PROJ)
        n_l = bsz * n_lat
        idx, gates = _router(f_lat, router_wt, router_b)
        if ctx_out:
            na_ctx = _context_attention(qkvu_ctx, bsz)
            h_ctx, f_ctx = _out_projection(na_ctx, y_all, qkvu_ctx, h_ctx, mod3, s5_d[layer], wglu_b,
                                           s5_b_glu[layer], wout_b, norm2_g[layer],
                                           ctx_map, n_ctx, 0)
            idx_c, gates_c = _router(f_ctx, router_wt, router_b)
            idx = jnp.concatenate([idx, idx_c], axis=1)
            gates = jnp.concatenate([gates, gates_c], axis=1)
        n_tok = idx.shape[1]
        dest, pad_start, total_end, tile_expert, n_valid = _dispatch(idx)
        xs = _scatter_rows([f_lat, f_ctx] if ctx_out else [f_lat], dest, pad_start, total_end)
        ys = _experts(xs, tile_expert, n_valid, moe_w_gate, moe_w_up, moe_w_down, layer)
        y1 = jnp.take(ys, dest[:n_tok], axis=0, mode="clip")
        y2 = jnp.take(ys, dest[n_tok:], axis=0, mode="clip")
        gates_t = gates.T
        h_lat = _combine(h_lat, y1, y2, gates_t, mod3, lat_map, 0)
        if ctx_out:
            h_ctx = _combine(h_ctx, y1, y2, gates_t, mod3, ctx_map, n_l)
    return h_lat.reshape(bsz, n_lat, d).astype(x.dtype)
```

```python
import functools
import math

import jax
import jax.numpy as jnp
from jax import lax
from jax.experimental import pallas as pl
from jax.experimental.pallas import tpu as pltpu

F32 = jnp.float32
BF16 = jnp.bfloat16
HIGHEST = lax.Precision.HIGHEST

D_MODEL = 1024
GRID_W = 64
HEAD_DIM = 64
NA_WIDTH = 512
S5_WIDTH = 512
S5_CH = 16
S5_GROUPS = 32
S5_STATE = 64
WIN_ROWS = 8
WIN_COLS = 16
N_EXPERTS = 16
N_GROUPS = 4
EPG = 4
N_MOD = 6
EPS = 1e-6

LANES = 128
SUBLANES = 8
VMEM_LIMIT = 56 * 1024 * 1024

TM_PROJ = 256
Q_ROWS = 8
NA_ROW_TILES = 2
Q_COLS = 16
K_ROWS = 16
K_COLS = 32
CHUNK = 16
PAIRS_PER_STEP = 2
S5_PIECE = 32
TM_EXP = 512
TM_ROUTE = 1024
MASK_VALUE = -1e30
RPB_LANE0 = 48


def _cparams(sem):
    return pltpu.CompilerParams(dimension_semantics=sem, vmem_limit_bytes=VMEM_LIMIT)


def _dot(a, b):
    return jnp.dot(a, b, preferred_element_type=F32)


def _dot_nt(a, b):
    return lax.dot_general(a, b, (((1,), (1,)), ((), ())), preferred_element_type=F32)


def _mod_kernel(c_ref, w_ref, b_ref, o_ref):
    a = c_ref[...]
    a = a * jax.nn.sigmoid(a)
    o_ref[0] = jnp.dot(a, w_ref[0], precision=HIGHEST, preferred_element_type=F32) + b_ref[0]


def _modulation(c_rows, w_mod, b_mod):
    depth, d, n = w_mod.shape
    tn = 1536
    return pl.pallas_call(
        _mod_kernel,
        out_shape=jax.ShapeDtypeStruct((depth, SUBLANES, n), F32),
        grid=(depth, n // tn),
        in_specs=[
            pl.BlockSpec((SUBLANES, d), lambda l, j: (0, 0)),
            pl.BlockSpec((1, d, tn), lambda l, j: (l, 0, j)),
            pl.BlockSpec((1, 1, tn), lambda l, j: (l, 0, j)),
        ],
        out_specs=pl.BlockSpec((1, SUBLANES, tn), lambda l, j: (l, 0, j)),
        compiler_params=_cparams(("arbitrary", "arbitrary")),
        name="modulation",
    )(c_rows, w_mod, b_mod.reshape(depth, 1, n))


def _inproj_kernel(x_ref, g_ref, mod_ref, w_ref, qg_ref, kg_ref, *rest, moe_pending):
    x = x_ref[...]
    if moe_pending:
        y1_ref, y2_ref, gate_ref, modp_ref, o_ref, xo_ref = rest
        gw = gate_ref[...]
        y = gw[:, 0:1] * y1_ref[...].astype(F32) + gw[:, 1:2] * y2_ref[...].astype(F32)
        x = x + modp_ref[0, :, 5 * D_MODEL:6 * D_MODEL] * y
        xo_ref[...] = x
    else:
        (o_ref,) = rest
    ms = jnp.mean(x * x, axis=-1, keepdims=True)
    y = x * lax.rsqrt(ms + EPS) * g_ref[...]
    shift = mod_ref[0, :, 0:D_MODEL]
    scale = mod_ref[0, :, D_MODEL:2 * D_MODEL]
    a = y * (1.0 + scale) + shift
    acc = _dot(a.astype(BF16), w_ref[...])
    lo = lax.broadcasted_iota(jnp.int32, (1, LANES), 1) < HEAD_DIM
    n_pairs = NA_WIDTH // LANES
    for blk in range(2 * n_pairs):
        cols = slice(blk * LANES, (blk + 1) * LANES)
        if blk < n_pairs:
            o_ref[:, cols] = _pair_rms(acc[:, cols], qg_ref[...], lo) * (HEAD_DIM ** -0.5)
        else:
            o_ref[:, cols] = _pair_rms(acc[:, cols], kg_ref[...], lo)
    o_ref[:, 2 * NA_WIDTH:] = acc[:, 2 * NA_WIDTH:]


def _mod_row_map(rows_per_batch, mod_row0, per_batch):
    tiles_per_batch = rows_per_batch // TM_PROJ
    if per_batch:
        return lambda i: (mod_row0 + i // tiles_per_batch, 0, 0)
    return lambda i: (mod_row0, 0, 0)


def _in_projection(x2d, g, mod3, w_bf16, mod_map, qg, kg, pending=None, row0=0):
    r, d = x2d.shape
    n = w_bf16.shape[1]
    g2 = lambda v: jnp.concatenate([v, v]).reshape(1, LANES).astype(F32)
    row = lambda i: (i, 0)
    in_specs = [
        pl.BlockSpec((TM_PROJ, d), row),
        pl.BlockSpec((1, d), lambda i: (0, 0)),
        pl.BlockSpec((1, 1, N_MOD * d), mod_map),
        pl.BlockSpec((d, n), lambda i: (0, 0)),
        pl.BlockSpec((1, LANES), lambda i: (0, 0)),
        pl.BlockSpec((1, LANES), lambda i: (0, 0)),
    ]
    args = [x2d, g.reshape(1, d), mod3, w_bf16, g2(qg), g2(kg)]
    out_shape = jax.ShapeDtypeStruct((r, n), F32)
    out_specs = pl.BlockSpec((TM_PROJ, n), row)
    if pending is not None:
        blk0 = row0 // TM_PROJ
        seg = lambda i: (blk0 + i, 0)
        in_specs += [pl.BlockSpec((TM_PROJ, d), seg), pl.BlockSpec((TM_PROJ, d), seg),
                     pl.BlockSpec((TM_PROJ, 2), seg), pl.BlockSpec((1, 1, N_MOD * d), mod_map)]
        args += list(pending)
        out_shape = (out_shape, jax.ShapeDtypeStruct((r, d), F32))
        out_specs = (out_specs, pl.BlockSpec((TM_PROJ, d), row))
    return pl.pallas_call(
        functools.partial(_inproj_kernel, moe_pending=pending is not None),
        out_shape=out_shape,
        grid=(r // TM_PROJ,),
        in_specs=in_specs,
        out_specs=out_specs,
        compiler_params=_cparams(("parallel",)),
        name="in_projection",
    )(*args)


def _pair_rms(x, g, lo):
    ss = x * x
    sa = jnp.sum(jnp.where(lo, ss, 0.0), axis=-1, keepdims=True)
    sb = jnp.sum(jnp.where(lo, 0.0, ss), axis=-1, keepdims=True)
    ms = jnp.where(lo, sa, sb) * (1.0 / HEAD_DIM)
    return x * lax.rsqrt(ms + EPS) * g


def _softmax_pv(qm, kw, kcb, vw, vcb, bias):
    s_nb = _dot_nt(qm, kw) + bias
    s_cx = _dot_nt(qm, kcb)
    m = jnp.maximum(jnp.max(s_nb, axis=-1, keepdims=True), jnp.max(s_cx, axis=-1, keepdims=True))
    p_nb = jnp.exp(s_nb - m)
    p_cx = jnp.exp(s_cx - m)
    l = jnp.sum(p_nb, axis=-1, keepdims=True) + jnp.sum(p_cx, axis=-1, keepdims=True)
    o = _dot(p_nb.astype(BF16), vw) + _dot(p_cx.astype(BF16), vcb)
    return o / l


def _na_kernel(q_ref, k_ref, v_ref, kc_ref, vc_ref, bias_ref, o_ref):
    lo = lax.broadcasted_iota(jnp.int32, (1, LANES), 1) < HEAD_DIM
    n_rows = k_ref.shape[1]
    col_tiles = GRID_W // Q_COLS
    nq = Q_ROWS * Q_COLS
    nk = K_ROWS * K_COLS
    kcb = kc_ref[0].astype(BF16)
    vcb = vc_ref[0].astype(BF16)
    for rr in range(NA_ROW_TILES):
        i = pl.program_id(2) * NA_ROW_TILES + rr
        kr0 = jnp.clip(Q_ROWS * i - WIN_ROWS // 2, 0, n_rows - K_ROWS)
        rt = jnp.where(i == 0, 0, jnp.where(i == n_rows // Q_ROWS - 1, 2, 1))
        q_rows = slice(rr * Q_ROWS, (rr + 1) * Q_ROWS)
        for j in range(col_tiles):
            kc0 = min(max(Q_COLS * j - WIN_COLS // 2, 0), GRID_W - K_COLS)
            ct = 0 if j == 0 else (2 if j == col_tiles - 1 else 1)
            typ = rt * 3 + ct
            cols = slice(j * Q_COLS, (j + 1) * Q_COLS)
            qn = q_ref[0, q_rows, cols, :].reshape(nq, LANES)
            kw = k_ref[0, pl.ds(kr0, K_ROWS), kc0:kc0 + K_COLS, :].reshape(nk, LANES).astype(BF16)
            vw = v_ref[0, pl.ds(kr0, K_ROWS), kc0:kc0 + K_COLS, :].reshape(nk, LANES).astype(BF16)
            q2 = jnp.concatenate([jnp.where(lo, qn, 0.0), jnp.where(lo, 0.0, qn)], axis=0).astype(BF16)
            o2 = _softmax_pv(q2, kw, kcb, vw, vcb, bias_ref[0, 0, typ].reshape(2 * nq, nk))
            o_ref[0, q_rows, cols, :] = jnp.where(lo, o2[0:nq], o2[nq:2 * nq]).reshape(Q_ROWS, Q_COLS, LANES)


def _bias_table_kernel(rpb_ref, o_ref, tt_scr):
    n_off_r = 2 * WIN_ROWS - 1
    lane = lax.broadcasted_iota(jnp.int32, (Q_COLS, LANES), 1)
    qc = lax.broadcasted_iota(jnp.int32, (Q_COLS, LANES), 0)
    kc = lane % K_COLS
    lane_blk = lane // K_COLS
    per_vreg = LANES // K_COLS
    col_rel = (0, -WIN_COLS // 2, -WIN_COLS)
    col_origin = (0, Q_COLS, GRID_W - Q_COLS)
    row_rel = (0, -WIN_ROWS // 2, -WIN_ROWS)
    row_origin = (0, Q_ROWS, GRID_W - Q_ROWS)
    masked = jnp.full((Q_COLS, LANES), MASK_VALUE, F32)

    for ct in range(3):
        c_abs = col_origin[ct] + qc
        k_abs = col_origin[ct] + col_rel[ct] + kc
        start = jnp.clip(c_abs - WIN_COLS // 2, 0, GRID_W - WIN_COLS)
        valid_c = (k_abs >= start) & (k_abs < start + WIN_COLS)
        base = (1 - WIN_COLS - col_rel[ct] - RPB_LANE0) % LANES
        for ro in range(n_off_r):
            row = jnp.broadcast_to(rpb_ref[0, 0, ro:ro + 1, :], (Q_COLS, LANES))
            t = pltpu.roll(row, base, 1, stride=1, stride_axis=0)
            rep = t
            for m in range(1, per_vreg):
                rep = jnp.where(lane_blk == m, pltpu.roll(t, K_COLS * m, 1), rep)
            tt_scr[ct, ro] = jnp.where(valid_c, rep, MASK_VALUE)

    for rt in range(3):
        for ct in range(3):
            for qr in range(Q_ROWS):
                r_abs = row_origin[rt] + qr
                r_start = min(max(r_abs - WIN_ROWS // 2, 0), GRID_W - WIN_ROWS)
                for w in range(K_ROWS // per_vreg):
                    val = None
                    for m in range(per_vreg):
                        k_abs = row_origin[rt] + row_rel[rt] + per_vreg * w + m
                        ok = r_start <= k_abs < r_start + WIN_ROWS
                        src = tt_scr[ct, k_abs - r_abs + WIN_ROWS - 1] if ok else masked
                        val = src if val is None else jnp.where(lane_blk == m, src, val)
                    o_ref[0, 0, rt * 3 + ct, 0, qr * Q_COLS:(qr + 1) * Q_COLS, w * LANES:(w + 1) * LANES] = val


def _na_bias_tables(na_rpb):
    depth, h, n_r, n_c = na_rpb.shape
    rpb_pad = jnp.pad(na_rpb.astype(F32), ((0, 0), (0, 0), (0, 2 * SUBLANES - n_r),
                                            (RPB_LANE0, LANES - RPB_LANE0 - n_c)))
    nq, nk = Q_ROWS * Q_COLS, K_ROWS * K_COLS
    return pl.pallas_call(
        _bias_table_kernel,
        out_shape=jax.ShapeDtypeStruct((depth, h // 2, 9, 2, nq, nk), F32),
        grid=(depth, h),
        in_specs=[pl.BlockSpec((1, 1, 2 * SUBLANES, LANES), lambda l, i: (l, i, 0, 0))],
        out_specs=pl.BlockSpec((1, 1, 9, 1, nq, nk), lambda l, i: (l, i // 2, 0, i % 2, 0, 0)),
        scratch_shapes=[pltpu.VMEM((3, 2 * WIN_ROWS - 1, Q_COLS, LANES), F32)],
        compiler_params=_cparams(("parallel", "parallel")),
        name="na_bias_tables",
    )(rpb_pad)


def _neighborhood_attention(qkvu_lat, qkvu_ctx, bias_tabs, layer, bsz):
    n_lat = qkvu_lat.shape[0] // bsz
    n_ctx = qkvu_ctx.shape[0] // bsz
    rows = n_lat // GRID_W
    n_cols = qkvu_lat.shape[1]
    lat4 = qkvu_lat.reshape(bsz, rows, GRID_W, n_cols)
    ctx3 = qkvu_ctx.reshape(bsz, n_ctx, n_cols)
    n_pairs = NA_WIDTH // LANES
    step_rows = NA_ROW_TILES * Q_ROWS
    n_tiles = rows // step_rows
    out = pl.pallas_call(
        _na_kernel,
        out_shape=jax.ShapeDtypeStruct((bsz, rows, GRID_W, NA_WIDTH), F32),
        grid=(n_pairs, bsz, n_tiles),
        in_specs=[
            pl.BlockSpec((1, step_rows, GRID_W, LANES), lambda p, b, t: (b, t, 0, p)),
            pl.BlockSpec((1, rows, GRID_W, LANES), lambda p, b, t: (b, 0, 0, n_pairs + p)),
            pl.BlockSpec((1, rows, GRID_W, LANES), lambda p, b, t: (b, 0, 0, 2 * n_pairs + p)),
            pl.BlockSpec((1, n_ctx, LANES), lambda p, b, t: (b, 0, n_pairs + p)),
            pl.BlockSpec((1, n_ctx, LANES), lambda p, b, t: (b, 0, 2 * n_pairs + p)),
            pl.BlockSpec((1, 1, 9, 2, Q_ROWS * Q_COLS, K_ROWS * K_COLS), lambda p, b, t: (layer, p, 0, 0, 0, 0)),
        ],
        out_specs=pl.BlockSpec((1, step_rows, GRID_W, LANES), lambda p, b, t: (b, t, 0, p)),
        compiler_params=_cparams(("parallel", "parallel", "parallel")),
        name="neighborhood_attention",
    )(lat4, lat4, lat4, ctx3, ctx3, bias_tabs)
    return out.reshape(bsz * n_lat, NA_WIDTH)


def _ctx_attn_kernel(q_ref, k_ref, v_ref, o_ref):
    lo = lax.broadcasted_iota(jnp.int32, (1, LANES), 1) < HEAD_DIM
    qn = q_ref[0]
    kn = k_ref[0].astype(BF16)
    vb = v_ref[0].astype(BF16)

    def one(qm):
        s = _dot_nt(qm, kn)
        m = jnp.max(s, axis=-1, keepdims=True)
        p = jnp.exp(s - m)
        l = jnp.sum(p, axis=-1, keepdims=True)
        return _dot(p.astype(BF16), vb) / l

    o_a = one(jnp.where(lo, qn, 0.0).astype(BF16))
    o_b = one(jnp.where(lo, 0.0, qn).astype(BF16))
    o_ref[0] = jnp.where(lo, o_a, o_b)


def _context_attention(qkvu_ctx, bsz):
    n_ctx = qkvu_ctx.shape[0] // bsz
    ctx3 = qkvu_ctx.reshape(bsz, n_ctx, qkvu_ctx.shape[1])
    n_pairs = NA_WIDTH // LANES
    out = pl.pallas_call(
        _ctx_attn_kernel,
        out_shape=jax.ShapeDtypeStruct((bsz, n_ctx, NA_WIDTH), F32),
        grid=(bsz, n_pairs),
        in_specs=[
            pl.BlockSpec((1, n_ctx, LANES), lambda b, p: (b, 0, p)),
            pl.BlockSpec((1, n_ctx, LANES), lambda b, p: (b, 0, n_pairs + p)),
            pl.BlockSpec((1, n_ctx, LANES), lambda b, p: (b, 0, 2 * n_pairs + p)),
        ],
        out_specs=pl.BlockSpec((1, n_ctx, LANES), lambda b, p: (b, 0, p)),
        compiler_params=_cparams(("parallel", "parallel")),
        name="context_attention",
    )(ctx3, ctx3, ctx3)
    return out.reshape(bsz * n_ctx, NA_WIDTH)


def _s5_kernel(uc_ref, ul_ref, w_ref, m_ref, v_ref, a_ref, y_ref, x_scr, s_scr, hf_scr, hr_scr, *, n_ctx_chunks):
    bsz = ul_ref.shape[0]
    n_lat_chunks = ul_ref.shape[1] // CHUNK
    n_chunks = n_ctx_chunks + n_lat_chunks
    rows = n_chunks * SUBLANES
    n_pairs = w_ref.shape[1]
    gpb = 2 * n_pairs
    half = 2 * LANES
    tile_chunks = TM_PROJ // CHUNK
    pieces = [(uc_ref, 0, n_ctx_chunks, 0)] if n_ctx_chunks else []
    pieces += [(ul_ref, c0, min(S5_PIECE, n_lat_chunks - c0), n_ctx_chunks + c0)
               for c0 in range(0, n_lat_chunks, S5_PIECE)]

    def lane_block_ids(n):
        return lax.broadcasted_iota(jnp.int32, (n, LANES), 1) // S5_CH

    for src_ref, c_src, n_c, c_dst in pieces:
        lane_blk = lane_block_ids(n_c)
        for b in range(bsz):
            for q in range(2):
                rolled = []
                for j in range(SUBLANES):
                    s = SUBLANES * q + j
                    u_s = src_ref[b, pl.ds(c_src * CHUNK + s, n_c, stride=CHUNK), :]
                    rolled.append(u_s if j == 0 else pltpu.roll(u_s, j * S5_CH, 1))
                for g in range(gpb):
                    xg = rolled[0]
                    for j in range(1, SUBLANES):
                        xg = jnp.where(lane_blk == (g + j) % SUBLANES, rolled[j], xg)
                    x_scr[g // 2, q, pl.ds(c_dst * SUBLANES + (g % 2) * bsz + b, n_c, stride=SUBLANES), :] = xg

    n_blk = 8
    rb = rows // n_blk
    first_group = (lax.broadcasted_iota(jnp.int32, (rb, half), 0) & (SUBLANES // 2)) == 0
    fwd_cols = (lax.broadcasted_iota(jnp.int32, (rb, half), 1) & (LANES - 1)) < S5_STATE
    is_fwd = lax.broadcasted_iota(jnp.int32, (SUBLANES, LANES), 1) < S5_STATE
    first_rows = lax.broadcasted_iota(jnp.int32, (SUBLANES, half), 0) < SUBLANES // 2
    zero = jnp.zeros((SUBLANES, LANES), F32)

    def x_rows(p, sl):
        return jnp.concatenate([x_scr[p, 0, sl, :], x_scr[p, 1, sl, :]], axis=1).astype(BF16)

    def put_cols(scr, i, sl, val):
        scr[i, 0, sl, :] = val[:, 0:LANES]
        scr[i, 1, sl, :] = val[:, LANES:half]

    for hh in range(n_pairs // PAIRS_PER_STEP):
        pairs = [hh * PAIRS_PER_STEP + i for i in range(PAIRS_PER_STEP)]
        for i, p in enumerate(pairs):
            for blk in range(n_blk):
                sl = slice(blk * rb, (blk + 1) * rb)
                r = _dot(x_rows(p, sl), w_ref[0, p])
                put_cols(s_scr, i, sl, jnp.where(first_group, r[:, :half], r[:, half:]))

        a_pair = [jnp.where(first_rows, a_ref[0, 2 * p], a_ref[0, 2 * p + 1]) for p in pairs]
        a_re = [a[:, 0:LANES] for a in a_pair]
        a_im = [a[:, LANES:half] for a in a_pair]

        def body(k, carry):
            kr = jnp.where(k < n_ctx_chunks, n_ctx_chunks - 1 - k, n_chunks + n_ctx_chunks - 1 - k)
            rf = pl.ds(pl.multiple_of(k * SUBLANES, SUBLANES), SUBLANES)
            rr = pl.ds(pl.multiple_of(kr * SUBLANES, SUBLANES), SUBLANES)
            new = []
            for i in range(PAIRS_PER_STEP):
                h_re, h_im = carry[2 * i], carry[2 * i + 1]
                hf_scr[i, 0, rf, :] = h_re
                hf_scr[i, 1, rf, :] = h_im
                hr_scr[i, 0, rr, :] = h_re
                hr_scr[i, 1, rr, :] = h_im
                s_re = jnp.where(is_fwd, s_scr[i, 0, rf, :], s_scr[i, 0, rr, :])
                s_im = jnp.where(is_fwd, s_scr[i, 1, rf, :], s_scr[i, 1, rr, :])
                new.append(a_re[i] * h_re - a_im[i] * h_im + s_re)
                new.append(a_re[i] * h_im + a_im[i] * h_re + s_im)
            return tuple(new)

        lax.fori_loop(0, n_chunks, body, (zero,) * (2 * PAIRS_PER_STEP))

        for i, p in enumerate(pairs):
            for blk in range(n_blk):
                sl = slice(blk * rb, (blk + 1) * rb)
                hf = jnp.concatenate([hf_scr[i, 0, sl, :], hf_scr[i, 1, sl, :]], axis=1)
                hr = jnp.concatenate([hr_scr[i, 0, sl, :], hr_scr[i, 1, sl, :]], axis=1)
                h_in = jnp.where(fwd_cols, hf, hr).astype(BF16)
                r = _dot(x_rows(p, sl), m_ref[0, p]) + _dot_nt(h_in, v_ref[0, p])
                put_cols(s_scr, i, sl, jnp.where(first_group, r[:, :half], r[:, half:]))

        g_lo = 2 * pairs[0]
        n_g = 2 * PAIRS_PER_STEP
        out_blk = lax.broadcasted_iota(jnp.int32, (tile_chunks, LANES), 1) // S5_CH
        lanes_out = (out_blk >= g_lo) & (out_blk < g_lo + n_g)
        for _, _, n_c, c_dst in pieces:
            lane_blk = lane_block_ids(n_c)
            for b in range(bsz):
                for q in range(2):
                    y_g = [s_scr[(g - g_lo) // 2, q,
                                 pl.ds(c_dst * SUBLANES + (g % 2) * bsz + b, n_c, stride=SUBLANES), :]
                           for g in range(g_lo, g_lo + n_g)]
                    for j in range(SUBLANES):
                        z = y_g[0]
                        for gi in range(1, n_g):
                            z = jnp.where(lane_blk == (g_lo + gi + j) % SUBLANES, y_g[gi], z)
                        if j:
                            z = pltpu.roll(z, LANES - j * S5_CH, 1)
                        t = SUBLANES * q + j
                        for ct in range(n_c // tile_chunks):
                            r0 = (c_dst // tile_chunks + ct) * TM_PROJ + t * tile_chunks
                            pltpu.store(y_ref.at[b, 0, r0:r0 + tile_chunks, :],
                                        z[ct * tile_chunks:(ct + 1) * tile_chunks, :], mask=lanes_out)


def _s5_scan(qkvu_ctx3, qkvu_lat3, mats, layer):
    w_c, m_c, vt_c, a_c = mats
    bsz, n_ctx, _ = qkvu_ctx3.shape
    n_lat = qkvu_lat3.shape[1]
    n_seq = n_ctx + n_lat
    rows = n_seq // CHUNK * SUBLANES
    n_blocks = S5_WIDTH // LANES
    ppb = w_c.shape[1] // n_blocks
    u_blk0 = 3 * NA_WIDTH // LANES
    wspec = pl.BlockSpec((1, ppb, 2 * LANES, 4 * LANES), lambda i: (layer, i, 0, 0))
    one = pl.Buffered(1)
    state = pltpu.VMEM((PAIRS_PER_STEP, 2, rows, LANES), F32)
    return pl.pallas_call(
        functools.partial(_s5_kernel, n_ctx_chunks=n_ctx // CHUNK),
        out_shape=jax.ShapeDtypeStruct((bsz, n_blocks, n_seq, LANES), F32),
        grid=(n_blocks,),
        in_specs=[
            pl.BlockSpec((bsz, n_ctx, LANES), lambda i: (0, 0, u_blk0 + i)),
            pl.BlockSpec((bsz, n_lat, LANES), lambda i: (0, 0, u_blk0 + i), pipeline_mode=one),
            wspec, wspec,
            pl.BlockSpec((1, ppb, 4 * LANES, 2 * LANES), lambda i: (layer, i, 0, 0)),
            pl.BlockSpec((1, 2 * ppb, SUBLANES, 2 * LANES), lambda i: (layer, i, 0, 0)),
        ],
        out_specs=pl.BlockSpec((bsz, 1, n_seq, LANES), lambda i: (0, i, 0, 0), pipeline_mode=one),
        scratch_shapes=[pltpu.VMEM((ppb, 2, rows, LANES), F32), state, state, state],
        compiler_params=_cparams(("parallel",)),
        name="s5_scan",
    )(qkvu_ctx3, qkvu_lat3, w_c, m_c, vt_c, a_c)


def _s5_mats_kernel(prm_ref, btr_ref, bti_ref, cr_ref, ci_ref, w_ref, m_ref, vt_ref, a_ref):
    t = CHUNK
    gl = pl.program_id(1) % SUBLANES
    is_fwd = lax.broadcasted_iota(jnp.int32, (1, LANES), 1) < S5_STATE
    lr = prm_ref[0, 0, 0:1, :]
    li = prm_ref[0, 0, 1:2, :]
    dt = jnp.exp(prm_ref[0, 0, 2:3, :])
    n = lax.broadcasted_iota(jnp.int32, (3 * SUBLANES, LANES), 0).astype(F32)
    pmag = jnp.exp(n * (lr * dt))
    pw_re = pmag * jnp.cos(n * (li * dt))
    pw_im = pmag * jnp.sin(n * (li * dt))
    ab_re, ab_im = pw_re[1:2, :], pw_im[1:2, :]
    den = lr * lr + li * li
    nr = ab_re - 1.0
    z_re = (nr * lr + ab_im * li) / den
    z_im = (ab_im * lr - nr * li) / den
    bt_re, bt_im = btr_ref[0, 0], bti_ref[0, 0]
    bb_re = z_re * bt_re - z_im * bt_im
    bb_im = z_re * bt_im + z_im * bt_re
    c_re, c_im = cr_ref[0, 0], ci_ref[0, 0]

    def powers(n_fwd, n_rev):
        return (jnp.where(is_fwd, pw_re[n_fwd:n_fwd + 1, :], pw_re[n_rev:n_rev + 1, :]),
                jnp.where(is_fwd, pw_im[n_fwd:n_fwd + 1, :], pw_im[n_rev:n_rev + 1, :]))

    def block_rows(s):
        pos = SUBLANES * (s // SUBLANES) + (s % SUBLANES + gl) % SUBLANES
        return pl.ds(pl.multiple_of(pos * S5_CH, S5_CH), S5_CH)

    for s in range(t):
        rows = block_rows(s)
        p_re, p_im = powers(t - 1 - s, s)
        w_ref[0, 0, rows, 0:LANES] = (bb_re * p_re - bb_im * p_im).astype(BF16)
        w_ref[0, 0, rows, LANES:2 * LANES] = (bb_re * p_im + bb_im * p_re).astype(BF16)
        q_re, q_im = powers(s + 1, t - s)
        vt_ref[0, 0, rows, 0:LANES] = (c_re * q_re - c_im * q_im).astype(BF16)
        vt_ref[0, 0, rows, LANES:2 * LANES] = (-(c_re * q_im + c_im * q_re)).astype(BF16)

    ca_re, ca_im = [], []
    for lag in range(t):
        p_re, p_im = powers(lag, t - 1 - lag)
        ca_re.append(c_re * p_re - c_im * p_im)
        ca_im.append(c_re * p_im + c_im * p_re)
    stack = jnp.concatenate([jnp.concatenate(ca_re, axis=0), jnp.concatenate(ca_im, axis=0)], axis=1)
    zero = jnp.zeros_like(bb_re)
    lhs = jnp.concatenate([
        jnp.concatenate([jnp.where(is_fwd, bb_re, zero), jnp.where(is_fwd, -bb_im, zero)], axis=1),
        jnp.concatenate([jnp.where(is_fwd, zero, bb_re), jnp.where(is_fwd, zero, -bb_im)], axis=1)], axis=0)
    kt = lax.dot_general(lhs, stack, (((1,), (1,)), ((), ())), precision=HIGHEST, preferred_element_type=F32)
    kt_f, kt_r = kt[0:S5_CH], kt[S5_CH:2 * S5_CH]
    blk = lax.broadcasted_iota(jnp.int32, (S5_CH, 2 * LANES), 1) // S5_CH
    for s in range(t):
        strip = (jnp.where(blk >= s, pltpu.roll(kt_f, S5_CH * s, 1), 0.0)
                 + jnp.where(blk <= s, pltpu.roll(kt_r, (S5_CH * (s - t + 1)) % (2 * LANES), 1), 0.0))
        strip = jnp.concatenate([pltpu.roll(strip[:, 0:LANES], gl * S5_CH, 1),
                                 pltpu.roll(strip[:, LANES:2 * LANES], gl * S5_CH, 1)], axis=1)
        m_ref[0, 0, block_rows(s), :] = strip.astype(BF16)

    a_ref[0, 0, :, 0:LANES] = jnp.broadcast_to(pw_re[t:t + 1, :], (SUBLANES, LANES))
    a_ref[0, 0, :, LANES:2 * LANES] = jnp.broadcast_to(pw_im[t:t + 1, :], (SUBLANES, LANES))


def _s5_matrices(lam_re, lam_im, log_dt, b_re, b_im, c_re, c_im):
    depth, _, g, p = lam_re.shape
    hc = b_re.shape[-1]
    width = CHUNK * hc
    both = lambda x: jnp.transpose(x.astype(F32), (0, 2, 1, 3)).reshape(depth, g, 1, 2 * p)
    dt_rows = jnp.broadcast_to(jnp.transpose(log_dt.astype(F32), (0, 2, 1))[..., None], (depth, g, 2, p))
    prm = jnp.concatenate([both(lam_re), both(lam_im), dt_rows.reshape(depth, g, 1, 2 * p),
                           jnp.zeros((depth, g, SUBLANES - 3, 2 * p), F32)], axis=2)
    bt = lambda x: jnp.transpose(x.astype(F32), (0, 2, 4, 1, 3)).reshape(depth, g, hc, 2 * p)
    ct = lambda x: jnp.transpose(x.astype(F32), (0, 2, 3, 1, 4)).reshape(depth, g, hc, 2 * p)
    vec = lambda rows: pl.BlockSpec((1, 1, rows, 2 * p), lambda l, i: (l, i, 0, 0))
    return pl.pallas_call(
        _s5_mats_kernel,
        out_shape=(jax.ShapeDtypeStruct((depth, g // 2, width, 2 * width), BF16),
                   jax.ShapeDtypeStruct((depth, g // 2, width, 2 * width), BF16),
                   jax.ShapeDtypeStruct((depth, g // 2, 2 * width, width), BF16),
                   jax.ShapeDtypeStruct((depth, g, SUBLANES, 4 * p), F32)),
        grid=(depth, g),
        in_specs=[vec(SUBLANES), vec(hc), vec(hc), vec(hc), vec(hc)],
        out_specs=(pl.BlockSpec((1, 1, width, width), lambda l, i: (l, i // 2, 0, i % 2)),
                   pl.BlockSpec((1, 1, width, width), lambda l, i: (l, i // 2, 0, i % 2)),
                   pl.BlockSpec((1, 1, width, width), lambda l, i: (l, i // 2, i % 2, 0)),
                   pl.BlockSpec((1, 1, SUBLANES, 4 * p), lambda l, i: (l, i, 0, 0))),
        compiler_params=_cparams(("parallel", "parallel")),
        name="s5_matrices",
    )(prm, bt(b_re), bt(b_im), ct(c_re), ct(c_im))


def _s5_mixer(qkvu_lat, qkvu_ctx, mats, layer, bsz):
    n_cols = qkvu_lat.shape[1]
    return _s5_scan(qkvu_ctx.reshape(bsz, -1, n_cols), qkvu_lat.reshape(bsz, -1, n_cols), mats, layer)


def _outproj_kernel(na_ref, y_ref, u_ref, h_ref, mod_ref, d_ref, wglu_ref, bglu_ref, wout_ref, g2_ref,
                    ho_ref, f_ref):
    tile_chunks = TM_PROJ // CHUNK
    y = jnp.concatenate(
        [jnp.concatenate([y_ref[0, blk, pl.ds(c, CHUNK, stride=tile_chunks), :] for c in range(tile_chunks)], axis=0)
         for blk in range(S5_WIDTH // LANES)], axis=1)
    z = jax.nn.gelu(y + d_ref[...] * u_ref[...])
    s5 = z * jax.nn.sigmoid(_dot(z.astype(BF16), wglu_ref[...]) + bglu_ref[...])
    mix = (_dot(na_ref[...].astype(BF16), wout_ref[0:NA_WIDTH, :])
           + _dot(s5.astype(BF16), wout_ref[NA_WIDTH:NA_WIDTH + S5_WIDTH, :]))
    d = D_MODEL
    gate = mod_ref[0, :, 2 * d:3 * d]
    h = h_ref[...] + gate * mix
    ho_ref[...] = h
    ms = jnp.mean(h * h, axis=-1, keepdims=True)
    y2 = h * lax.rsqrt(ms + EPS) * g2_ref[...]
    f_ref[...] = y2 * (1.0 + mod_ref[0, :, 4 * d:5 * d]) + mod_ref[0, :, 3 * d:4 * d]


def _out_projection(na, y_all, qkvu, h2d, mod3, d_skip, wglu_bf16, b_glu, wout_bf16, g2,
                    mod_map, rows_per_batch, y_block0):
    r, d = h2d.shape
    tiles_per_batch = rows_per_batch // TM_PROJ
    u_blk = 3 * NA_WIDTH // S5_WIDTH

    def y_map(i):
        return (i // tiles_per_batch, 0, y_block0 + i % tiles_per_batch, 0)

    const = lambda i: (0, 0)
    return pl.pallas_call(
        _outproj_kernel,
        out_shape=(jax.ShapeDtypeStruct((r, d), F32), jax.ShapeDtypeStruct((r, d), F32)),
        grid=(r // TM_PROJ,),
        in_specs=[
            pl.BlockSpec((TM_PROJ, NA_WIDTH), lambda i: (i, 0)),
            pl.BlockSpec((1, S5_WIDTH // LANES, TM_PROJ, LANES), y_map),
            pl.BlockSpec((TM_PROJ, S5_WIDTH), lambda i: (i, u_blk)),
            pl.BlockSpec((TM_PROJ, d), lambda i: (i, 0)),
            pl.BlockSpec((1, 1, N_MOD * d), mod_map),
            pl.BlockSpec((1, S5_WIDTH), const),
            pl.BlockSpec((S5_WIDTH, S5_WIDTH), const),
            pl.BlockSpec((1, S5_WIDTH), const),
            pl.BlockSpec((NA_WIDTH + S5_WIDTH, d), const),
            pl.BlockSpec((1, d), const),
        ],
        out_specs=(pl.BlockSpec((TM_PROJ, d), lambda i: (i, 0)), pl.BlockSpec((TM_PROJ, d), lambda i: (i, 0))),
        compiler_params=_cparams(("parallel",)),
        name="out_projection",
    )(na, y_all, qkvu, h2d, mod3, d_skip.reshape(1, -1), wglu_bf16, b_glu.reshape(1, -1), wout_bf16,
      g2.reshape(1, d))


def _top2(vals):
    best = vals[0]
    bi = jnp.zeros(best.shape, jnp.int32)
    for i in range(1, len(vals)):
        gt = vals[i] > best
        best = jnp.where(gt, vals[i], best)
        bi = jnp.where(gt, i, bi)
    second = jnp.full(best.shape, -jnp.inf, F32)
    si = jnp.zeros(best.shape, jnp.int32)
    for i in range(len(vals)):
        cand = jnp.where(bi == i, -jnp.inf, vals[i])
        gt = cand > second
        second = jnp.where(gt, cand, second)
        si = jnp.where(gt, i, si)
    return best, bi, second, si


def _route(f, rwt, rb):
    logits = lax.dot_general(rwt, f, (((1,), (1,)), ((), ())),
                             precision=HIGHEST, preferred_element_type=F32)
    m = jnp.max(logits, axis=0, keepdims=True)
    e = jnp.exp(logits - m)
    probs = e / jnp.sum(e, axis=0, keepdims=True)
    sel = probs + rb
    sel_rows = [sel[i:i + 1, :] for i in range(N_EXPERTS)]
    prob_rows = [probs[i:i + 1, :] for i in range(N_EXPERTS)]
    scores = []
    for g in range(N_GROUPS):
        b, _, s, _ = _top2(sel_rows[g * EPG:(g + 1) * EPG])
        scores.append(b + s)
    grp = jnp.zeros(scores[0].shape, jnp.int32)
    gbest = scores[0]
    for g in range(1, N_GROUPS):
        gt = scores[g] > gbest
        gbest = jnp.where(gt, scores[g], gbest)
        grp = jnp.where(gt, g, grp)
    in_rows = []
    for j in range(EPG):
        v = sel_rows[j]
        for g in range(1, N_GROUPS):
            v = jnp.where(grp == g, sel_rows[g * EPG + j], v)
        in_rows.append(v)
    _, l1, _, l2 = _top2(in_rows)
    i1 = grp * EPG + l1
    i2 = grp * EPG + l2
    w1 = jnp.zeros(gbest.shape, F32)
    w2 = jnp.zeros(gbest.shape, F32)
    for i in range(N_EXPERTS):
        w1 = jnp.where(i1 == i, prob_rows[i], w1)
        w2 = jnp.where(i2 == i, prob_rows[i], w2)
    tot = w1 + w2
    return i1, i2, w1 / tot, w2 / tot


def _router_kernel(f_ref, rwt_ref, rb_ref, idx_ref, gate_ref):
    i1, i2, g1, g2 = _route(f_ref[...], rwt_ref[...], rb_ref[...])
    idx_ref[0:1, :] = i1
    idx_ref[1:2, :] = i2
    gate_ref[0:1, :] = g1
    gate_ref[1:2, :] = g2


def _router(f_all, router_wt, router_b):
    n, d = f_all.shape
    return pl.pallas_call(
        _router_kernel,
        out_shape=(jax.ShapeDtypeStruct((2, n), jnp.int32), jax.ShapeDtypeStruct((2, n), F32)),
        grid=(n // TM_ROUTE,),
        in_specs=[
            pl.BlockSpec((TM_ROUTE, d), lambda i: (i, 0)),
            pl.BlockSpec((N_EXPERTS, d), lambda i: (0, 0)),
            pl.BlockSpec((N_EXPERTS, 1), lambda i: (0, 0)),
        ],
        out_specs=(pl.BlockSpec((2, TM_ROUTE), lambda i: (0, i)), pl.BlockSpec((2, TM_ROUTE), lambda i: (0, i))),
        compiler_params=_cparams(("parallel",)),
        name="router",
    )(f_all, router_wt, router_b)


def _scatter_rows_kernel(dest_ref, pad_ref, end_ref, *refs, n_tok, seg_tiles):
    f_refs = refs[:len(seg_tiles)]
    xs_ref, zero_scr, stage, sems = refs[len(seg_tiles):]
    i = pl.program_id(0)
    tm = f_refs[0].shape[0]
    n_rows = xs_ref.shape[0]
    slab = TM_EXP + SUBLANES
    fill_sem = sems.at[2]

    def slab_copy(start, rows):
        return pltpu.make_async_copy(zero_scr.at[pl.ds(0, rows), :], xs_ref.at[pl.ds(start, rows), :], fill_sem)

    @pl.when(i == 0)
    def _():
        zero_scr[...] = jnp.zeros(zero_scr.shape, zero_scr.dtype)
        for e in range(N_EXPERTS):
            start = jnp.minimum((pad_ref[e] // SUBLANES) * SUBLANES, n_rows - slab)
            slab_copy(pl.multiple_of(start, SUBLANES), slab).start()
        for e in range(N_EXPERTS):
            slab_copy(0, slab).wait()
        for k in range(N_EXPERTS):
            start = end_ref[0] + k * TM_EXP

            @pl.when(start < n_rows)
            def _():
                cp = slab_copy(pl.multiple_of(start, TM_EXP), TM_EXP)
                cp.start()
                cp.wait()

    slot = i % 2

    def wait_tile(s):
        for _ in range(2):
            pltpu.make_async_copy(stage.at[s], xs_ref.at[pl.ds(0, tm), :], sems.at[s]).wait()

    def scatter_tile(f_ref):
        base = i * tm
        stage[slot] = f_ref[...]

        def row_copy(r, d):
            return pltpu.make_async_copy(stage.at[slot, pl.ds(r, 1), :], xs_ref.at[pl.ds(d, 1), :], sems.at[slot])

        for r in range(tm):
            row_copy(r, dest_ref[base + r]).start()
            row_copy(r, dest_ref[n_tok + base + r]).start()

    tile0 = 0
    for f_ref, n_t in zip(f_refs, seg_tiles):
        pl.when((i >= tile0) & (i < tile0 + n_t))(functools.partial(scatter_tile, f_ref))
        tile0 += n_t

    pl.when(i > 0)(lambda: wait_tile(1 - slot))
    pl.when(i == pl.num_programs(0) - 1)(lambda: wait_tile(slot))


def _scatter_rows(segments, dest_flat, pad_start, total_end):
    d = segments[0].shape[1]
    seg_tiles = tuple(s.shape[0] // TM_PROJ for s in segments)
    n_tok = sum(s.shape[0] for s in segments)
    r_max = 2 * n_tok + N_EXPERTS * TM_EXP
    in_specs = []
    tile0 = 0
    for n_t in seg_tiles:
        in_specs.append(pl.BlockSpec(
            (TM_PROJ, d), lambda i, *_, t0=tile0, nt=n_t: (jnp.clip(i - t0, 0, nt - 1), 0)))
        tile0 += n_t
    grid_spec = pltpu.PrefetchScalarGridSpec(
        num_scalar_prefetch=3,
        grid=(tile0,),
        in_specs=in_specs,
        out_specs=pl.BlockSpec(memory_space=pl.ANY),
        scratch_shapes=[pltpu.VMEM((TM_EXP + SUBLANES, d), F32), pltpu.VMEM((2, TM_PROJ, d), F32),
                        pltpu.SemaphoreType.DMA((3,))],
    )
    return pl.pallas_call(
        functools.partial(_scatter_rows_kernel, n_tok=n_tok, seg_tiles=seg_tiles),
        out_shape=jax.ShapeDtypeStruct((r_max, d), F32),
        grid_spec=grid_spec,
        compiler_params=_cparams(("arbitrary",)),
        name="moe_scatter_rows",
    )(dest_flat, pad_start, total_end, *segments)


def _experts_kernel(te_ref, nv_ref, x_ref, wg_ref, wu_ref, wd_ref, o_ref, wg_scr, wu_scr, wd_scr):
    i = pl.program_id(0)
    e = te_ref[i]
    prev = te_ref[jnp.maximum(i - 1, 0)]
    rows = 128

    @pl.when((i == 0) | (e != prev))
    def _():
        def body(r, carry):
            sl = pl.ds(pl.multiple_of(r * rows, rows), rows)
            wg_scr[sl, :] = wg_ref[0, 0, sl, :].astype(BF16)
            wu_scr[sl, :] = wu_ref[0, 0, sl, :].astype(BF16)
            wd_scr[sl, :] = wd_ref[0, 0, sl, :].astype(BF16)
            return carry
        lax.fori_loop(0, wg_scr.shape[0] // rows, body, 0)

    @pl.when(i < nv_ref[0])
    def _():
        x = x_ref[...].astype(BF16)
        g = _dot(x, wg_scr[...])
        u = _dot(x, wu_scr[...])
        a = (g * jax.nn.sigmoid(g)) * u
        o_ref[...] = _dot(a.astype(BF16), wd_scr[...]).astype(BF16)

    @pl.when(i >= nv_ref[0])
    def _():
        o_ref[...] = jnp.zeros(o_ref.shape, BF16)


def _experts(xs, tile_expert, n_valid, w_gate, w_up, w_down, layer):
    r, d = xs.shape
    de = w_gate.shape[3]
    n_tiles = r // TM_EXP
    x_map = lambda i, te, nv: (jnp.minimum(i, nv[0] - 1), 0)
    grid_spec = pltpu.PrefetchScalarGridSpec(
        num_scalar_prefetch=2,
        grid=(n_tiles,),
        in_specs=[
            pl.BlockSpec((TM_EXP, d), x_map),
            pl.BlockSpec((1, 1, d, de), lambda i, te, nv: (layer, te[i], 0, 0)),
            pl.BlockSpec((1, 1, d, de), lambda i, te, nv: (layer, te[i], 0, 0)),
            pl.BlockSpec((1, 1, de, d), lambda i, te, nv: (layer, te[i], 0, 0)),
        ],
        out_specs=pl.BlockSpec((TM_EXP, d), lambda i, te, nv: (i, 0)),
        scratch_shapes=[pltpu.VMEM((d, de), BF16), pltpu.VMEM((d, de), BF16), pltpu.VMEM((de, d), BF16)],
    )
    return pl.pallas_call(
        _experts_kernel,
        out_shape=jax.ShapeDtypeStruct((r, d), BF16),
        grid_spec=grid_spec,
        compiler_params=_cparams(("arbitrary",)),
        name="experts",
    )(tile_expert, n_valid, xs, w_gate, w_up, w_down)


def _dispatch(idx):
    n = idx.shape[1]
    e_flat = idx.reshape(-1)
    onehot = (e_flat[:, None] == jnp.arange(N_EXPERTS, dtype=jnp.int32)[None, :]).astype(jnp.int32)
    csum = jnp.cumsum(onehot, axis=0)
    rank = jnp.sum(csum * onehot, axis=1) - 1
    counts = csum[-1]
    padded = ((counts + TM_EXP - 1) // TM_EXP) * TM_EXP
    ends = jnp.cumsum(padded)
    starts = ends - padded
    dest = (jnp.sum(onehot * starts[None, :], axis=1) + rank).astype(jnp.int32)
    r_max = 2 * n + N_EXPERTS * TM_EXP
    tile_start = jnp.arange(r_max // TM_EXP, dtype=jnp.int32) * TM_EXP
    tile_expert = jnp.minimum(jnp.sum((tile_start[:, None] >= ends[None, :]).astype(jnp.int32), axis=1),
                              N_EXPERTS - 1).astype(jnp.int32)
    n_valid = (ends[-1] // TM_EXP).astype(jnp.int32).reshape(1)
    pad_start = (starts + counts).astype(jnp.int32)
    total_end = ends[-1].astype(jnp.int32).reshape(1)
    return dest, pad_start, total_end, tile_expert, n_valid


def _combine_kernel(h_ref, y1_ref, y2_ref, gate_ref, mod_ref, o_ref):
    d = D_MODEL
    g = gate_ref[...]
    y = g[:, 0:1] * y1_ref[...].astype(F32) + g[:, 1:2] * y2_ref[...].astype(F32)
    o_ref[...] = h_ref[...] + mod_ref[0, :, 5 * d:6 * d] * y


def _combine(h2d, y1, y2, gates_t, mod3, mod_map, row0):
    r, d = h2d.shape
    blk0 = row0 // TM_PROJ
    row = lambda i: (i, 0)
    seg = lambda i: (blk0 + i, 0)
    return pl.pallas_call(
        _combine_kernel,
        out_shape=jax.ShapeDtypeStruct((r, d), F32),
        grid=(r // TM_PROJ,),
        in_specs=[
            pl.BlockSpec((TM_PROJ, d), row),
            pl.BlockSpec((TM_PROJ, d), seg),
            pl.BlockSpec((TM_PROJ, d), seg),
            pl.BlockSpec((TM_PROJ, 2), seg),
            pl.BlockSpec((1, 1, N_MOD * d), mod_map),
        ],
        out_specs=pl.BlockSpec((TM_PROJ, d), row),
        compiler_params=_cparams(("parallel",)),
        name="moe_combine",
    )(h2d, y1, y2, gates_t, mod3)


def kernel(x, c, ctx, c_ctx, w_mod, b_mod, norm1_g, norm2_g, w_in, w_out, q_norm_g, k_norm_g, na_rpb,
           s5_lam_re, s5_lam_im, s5_log_dt, s5_b_re, s5_b_im, s5_c_re, s5_c_im, s5_d, s5_w_glu, s5_b_glu,
           router_w, router_bias, moe_w_gate, moe_w_up, moe_w_down):
    bsz, n_lat, d = x.shape
    n_ctx = ctx.shape[1]
    depth = w_mod.shape[0]
    ctx_row = bsz
    c_rows = jnp.concatenate([c.astype(F32), c_ctx.astype(F32)[None],
                              jnp.zeros((SUBLANES - bsz - 1, d), F32)], axis=0)
    mod_all = _modulation(c_rows, w_mod.astype(F32), b_mod.astype(F32))

    h_lat = x.reshape(bsz * n_lat, d).astype(F32)
    h_ctx = ctx.reshape(bsz * n_ctx, d).astype(F32)
    lat_map = _mod_row_map(n_lat, 0, True)
    ctx_map = _mod_row_map(n_ctx, ctx_row, False)
    bias_tabs = _na_bias_tables(na_rpb)
    router_wt = router_w.T.astype(F32)
    router_b = router_bias.reshape(N_EXPERTS, 1).astype(F32)
    s5_mats = _s5_matrices(s5_lam_re, s5_lam_im, s5_log_dt, s5_b_re, s5_b_im, s5_c_re, s5_c_im)

    n_l = bsz * n_lat
    pending = None
    for layer in range(depth):
        ctx_out = layer < depth - 1
        mod3 = mod_all[layer].reshape(SUBLANES, 1, N_MOD * d)
        w_in_b = w_in[layer].astype(BF16)
        proj = functools.partial(_in_projection, g=norm1_g[layer], mod3=mod3, w_bf16=w_in_b,
                                 qg=q_norm_g[layer], kg=k_norm_g[layer])
        if pending is None:
            qkvu_lat = proj(h_lat, mod_map=lat_map)
            qkvu_ctx = proj(h_ctx, mod_map=ctx_map)
        else:
            qkvu_lat, h_lat = proj(h_lat, mod_map=lat_map, pending=pending, row0=0)
            qkvu_ctx, h_ctx = proj(h_ctx, mod_map=ctx_map, pending=pending, row0=n_l)
        na_lat = _neighborhood_attention(qkvu_lat, qkvu_ctx, bias_tabs, layer, bsz)
        y_all = _s5_mixer(qkvu_lat, qkvu_ctx, s5_mats, layer, bsz)
        wglu_b = s5_w_glu[layer].astype(BF16)
        wout_b = w_out[layer].astype(BF16)
        h_lat, f_lat = _out_projection(na_lat, y_all, qkvu_lat, h_lat, mod3, s5_d[layer], wglu_b,
                                       s5_b_glu[layer], wout_b, norm2_g[layer],
                                       lat_map, n_lat, n_ctx // TM_PROJ)
        idx, gates = _router(f_lat, router_wt, router_b)
        if ctx_out:
            na_ctx = _context_attention(qkvu_ctx, bsz)
            h_ctx, f_ctx = _out_projection(na_ctx, y_all, qkvu_ctx, h_ctx, mod3, s5_d[layer], wglu_b,
                                           s5_b_glu[layer], wout_b, norm2_g[layer],
                                           ctx_map, n_ctx, 0)
            idx_c, gates_c = _router(f_ctx, router_wt, router_b)
            idx = jnp.concatenate([idx, idx_c], axis=1)
            gates = jnp.concatenate([gates, gates_c], axis=1)
        n_tok = idx.shape[1]
        dest, pad_start, total_end, tile_expert, n_valid = _dispatch(idx)
        xs = _scatter_rows([f_lat, f_ctx] if ctx_out else [f_lat], dest, pad_start, total_end)
        ys = _experts(xs, tile_expert, n_valid, moe_w_gate, moe_w_up, moe_w_down, layer)
        y1 = jnp.take(ys, dest[:n_tok], axis=0, mode="clip")
        y2 = jnp.take(ys, dest[n_tok:], axis=0, mode="clip")
        gates_t = gates.T
        if ctx_out:
            pending = (y1, y2, gates_t, mod3)
        else:
            h_lat = _combine(h_lat, y1, y2, gates_t, mod3, lat_map, 0)
    return h_lat.reshape(bsz, n_lat, d).astype(x.dtype)
```

```python
import functools
import math

import jax
import jax.numpy as jnp
from jax import lax
from jax.experimental import pallas as pl
from jax.experimental.pallas import tpu as pltpu

F32 = jnp.float32
BF16 = jnp.bfloat16
HIGHEST = lax.Precision.HIGHEST

D_MODEL = 1024
GRID_W = 64
HEAD_DIM = 64
NA_WIDTH = 512
S5_WIDTH = 512
S5_CH = 16
S5_GROUPS = 32
S5_STATE = 64
WIN_ROWS = 8
WIN_COLS = 16
N_EXPERTS = 16
N_GROUPS = 4
EPG = 4
N_MOD = 6
EPS = 1e-6

LANES = 128
SUBLANES = 8
VMEM_LIMIT = 56 * 1024 * 1024

TM_PROJ = 256
Q_ROWS = 8
NA_ROW_TILES = 2
Q_COLS = 16
K_ROWS = 16
K_COLS = 32
CHUNK = 16
PAIRS_PER_STEP = 2
S5_PIECE = 32
TM_EXP = 512
TM_ROUTE = 1024
MASK_VALUE = -1e30
RPB_LANE0 = 48


def _cparams(sem):
    return pltpu.CompilerParams(dimension_semantics=sem, vmem_limit_bytes=VMEM_LIMIT)


def _dot(a, b):
    return jnp.dot(a, b, preferred_element_type=F32)


def _dot_nt(a, b):
    return lax.dot_general(a, b, (((1,), (1,)), ((), ())), preferred_element_type=F32)


def _mod_kernel(c_ref, w_ref, b_ref, o_ref):
    a = c_ref[...]
    a = a * jax.nn.sigmoid(a)
    o_ref[0] = jnp.dot(a, w_ref[0], precision=HIGHEST, preferred_element_type=F32) + b_ref[0]


def _modulation(c_rows, w_mod, b_mod):
    depth, d, n = w_mod.shape
    tn = 1536
    return pl.pallas_call(
        _mod_kernel,
        out_shape=jax.ShapeDtypeStruct((depth, SUBLANES, n), F32),
        grid=(depth, n // tn),
        in_specs=[
            pl.BlockSpec((SUBLANES, d), lambda l, j: (0, 0)),
            pl.BlockSpec((1, d, tn), lambda l, j: (l, 0, j)),
            pl.BlockSpec((1, 1, tn), lambda l, j: (l, 0, j)),
        ],
        out_specs=pl.BlockSpec((1, SUBLANES, tn), lambda l, j: (l, 0, j)),
        compiler_params=_cparams(("arbitrary", "arbitrary")),
        name="modulation",
    )(c_rows, w_mod, b_mod.reshape(depth, 1, n))


def _inproj_kernel(x_ref, g_ref, mod_ref, w_ref, qg_ref, kg_ref, *rest, moe_pending):
    x = x_ref[...]
    if moe_pending:
        y1_ref, y2_ref, gate_ref, modp_ref, o_ref, xo_ref = rest
        gw = gate_ref[...]
        y = gw[:, 0:1] * y1_ref[...].astype(F32) + gw[:, 1:2] * y2_ref[...].astype(F32)
        x = x + modp_ref[0, :, 5 * D_MODEL:6 * D_MODEL] * y
        xo_ref[...] = x
    else:
        (o_ref,) = rest
    ms = jnp.mean(x * x, axis=-1, keepdims=True)
    y = x * lax.rsqrt(ms + EPS) * g_ref[...]
    shift = mod_ref[0, :, 0:D_MODEL]
    scale = mod_ref[0, :, D_MODEL:2 * D_MODEL]
    a = y * (1.0 + scale) + shift
    acc = _dot(a.astype(BF16), w_ref[...])
    lo = lax.broadcasted_iota(jnp.int32, (1, LANES), 1) < HEAD_DIM
    n_pairs = NA_WIDTH // LANES
    for blk in range(2 * n_pairs):
        cols = slice(blk * LANES, (blk + 1) * LANES)
        if blk < n_pairs:
            o_ref[:, cols] = _pair_rms(acc[:, cols], qg_ref[...], lo) * (HEAD_DIM ** -0.5)
        else:
            o_ref[:, cols] = _pair_rms(acc[:, cols], kg_ref[...], lo)
    o_ref[:, 2 * NA_WIDTH:] = acc[:, 2 * NA_WIDTH:]


def _mod_row_map(rows_per_batch, mod_row0, per_batch):
    tiles_per_batch = rows_per_batch // TM_PROJ
    if per_batch:
        return lambda i: (mod_row0 + i // tiles_per_batch, 0, 0)
    return lambda i: (mod_row0, 0, 0)


def _in_projection(x2d, g, mod3, w_bf16, mod_map, qg, kg, pending=None, row0=0):
    r, d = x2d.shape
    n = w_bf16.shape[1]
    g2 = lambda v: jnp.concatenate([v, v]).reshape(1, LANES).astype(F32)
    row = lambda i: (i, 0)
    in_specs = [
        pl.BlockSpec((TM_PROJ, d), row),
        pl.BlockSpec((1, d), lambda i: (0, 0)),
        pl.BlockSpec((1, 1, N_MOD * d), mod_map),
        pl.BlockSpec((d, n), lambda i: (0, 0)),
        pl.BlockSpec((1, LANES), lambda i: (0, 0)),
        pl.BlockSpec((1, LANES), lambda i: (0, 0)),
    ]
    args = [x2d, g.reshape(1, d), mod3, w_bf16, g2(qg), g2(kg)]
    out_shape = jax.ShapeDtypeStruct((r, n), F32)
    out_specs = pl.BlockSpec((TM_PROJ, n), row)
    if pending is not None:
        blk0 = row0 // TM_PROJ
        seg = lambda i: (blk0 + i, 0)
        in_specs += [pl.BlockSpec((TM_PROJ, d), seg), pl.BlockSpec((TM_PROJ, d), seg),
                     pl.BlockSpec((TM_PROJ, 2), seg), pl.BlockSpec((1, 1, N_MOD * d), mod_map)]
        args += list(pending)
        out_shape = (out_shape, jax.ShapeDtypeStruct((r, d), F32))
        out_specs = (out_specs, pl.BlockSpec((TM_PROJ, d), row))
    return pl.pallas_call(
        functools.partial(_inproj_kernel, moe_pending=pending is not None),
        out_shape=out_shape,
        grid=(r // TM_PROJ,),
        in_specs=in_specs,
        out_specs=out_specs,
        compiler_params=_cparams(("parallel",)),
        name="in_projection",
    )(*args)


def _pair_rms(x, g, lo):
    ss = x * x
    sa = jnp.sum(jnp.where(lo, ss, 0.0), axis=-1, keepdims=True)
    sb = jnp.sum(jnp.where(lo, 0.0, ss), axis=-1, keepdims=True)
    ms = jnp.where(lo, sa, sb) * (1.0 / HEAD_DIM)
    return x * lax.rsqrt(ms + EPS) * g


def _softmax_pv(qm, kw, kcb, vw, vcb, bias):
    s_nb = _dot_nt(qm, kw) + bias
    s_cx = _dot_nt(qm, kcb)
    m = jnp.maximum(jnp.max(s_nb, axis=-1, keepdims=True), jnp.max(s_cx, axis=-1, keepdims=True))
    p_nb = jnp.exp(s_nb - m)
    p_cx = jnp.exp(s_cx - m)
    l = jnp.sum(p_nb, axis=-1, keepdims=True) + jnp.sum(p_cx, axis=-1, keepdims=True)
    o = _dot(p_nb.astype(BF16), vw) + _dot(p_cx.astype(BF16), vcb)
    return o / l


def _na_kernel(q_ref, k_ref, v_ref, kc_ref, vc_ref, bias_ref, o_ref):
    lo = lax.broadcasted_iota(jnp.int32, (1, LANES), 1) < HEAD_DIM
    n_rows = k_ref.shape[1]
    col_tiles = GRID_W // Q_COLS
    nq = Q_ROWS * Q_COLS
    nk = K_ROWS * K_COLS
    kcb = kc_ref[0].astype(BF16)
    vcb = vc_ref[0].astype(BF16)
    for rr in range(NA_ROW_TILES):
        i = pl.program_id(2) * NA_ROW_TILES + rr
        kr0 = jnp.clip(Q_ROWS * i - WIN_ROWS // 2, 0, n_rows - K_ROWS)
        rt = jnp.where(i == 0, 0, jnp.where(i == n_rows // Q_ROWS - 1, 2, 1))
        q_rows = slice(rr * Q_ROWS, (rr + 1) * Q_ROWS)
        for j in range(col_tiles):
            kc0 = min(max(Q_COLS * j - WIN_COLS // 2, 0), GRID_W - K_COLS)
            ct = 0 if j == 0 else (2 if j == col_tiles - 1 else 1)
            typ = rt * 3 + ct
            cols = slice(j * Q_COLS, (j + 1) * Q_COLS)
            qn = q_ref[0, q_rows, cols, :].reshape(nq, LANES)
            kw = k_ref[0, pl.ds(kr0, K_ROWS), kc0:kc0 + K_COLS, :].reshape(nk, LANES).astype(BF16)
            vw = v_ref[0, pl.ds(kr0, K_ROWS), kc0:kc0 + K_COLS, :].reshape(nk, LANES).astype(BF16)
            q2 = jnp.concatenate([jnp.where(lo, qn, 0.0), jnp.where(lo, 0.0, qn)], axis=0).astype(BF16)
            o2 = _softmax_pv(q2, kw, kcb, vw, vcb, bias_ref[0, 0, typ].reshape(2 * nq, nk))
            o_ref[0, q_rows, cols, :] = jnp.where(lo, o2[0:nq], o2[nq:2 * nq]).reshape(Q_ROWS, Q_COLS, LANES)


def _bias_table_kernel(rpb_ref, o_ref, tt_scr):
    n_off_r = 2 * WIN_ROWS - 1
    lane = lax.broadcasted_iota(jnp.int32, (Q_COLS, LANES), 1)
    qc = lax.broadcasted_iota(jnp.int32, (Q_COLS, LANES), 0)
    kc = lane % K_COLS
    lane_blk = lane // K_COLS
    per_vreg = LANES // K_COLS
    col_rel = (0, -WIN_COLS // 2, -WIN_COLS)
    col_origin = (0, Q_COLS, GRID_W - Q_COLS)
    row_rel = (0, -WIN_ROWS // 2, -WIN_ROWS)
    row_origin = (0, Q_ROWS, GRID_W - Q_ROWS)
    masked = jnp.full((Q_COLS, LANES), MASK_VALUE, F32)

    for ct in range(3):
        c_abs = col_origin[ct] + qc
        k_abs = col_origin[ct] + col_rel[ct] + kc
        start = jnp.clip(c_abs - WIN_COLS // 2, 0, GRID_W - WIN_COLS)
        valid_c = (k_abs >= start) & (k_abs < start + WIN_COLS)
        base = (1 - WIN_COLS - col_rel[ct] - RPB_LANE0) % LANES
        for ro in range(n_off_r):
            row = jnp.broadcast_to(rpb_ref[0, 0, ro:ro + 1, :], (Q_COLS, LANES))
            t = pltpu.roll(row, base, 1, stride=1, stride_axis=0)
            rep = t
            for m in range(1, per_vreg):
                rep = jnp.where(lane_blk == m, pltpu.roll(t, K_COLS * m, 1), rep)
            tt_scr[ct, ro] = jnp.where(valid_c, rep, MASK_VALUE)

    for rt in range(3):
        for ct in range(3):
            for qr in range(Q_ROWS):
                r_abs = row_origin[rt] + qr
                r_start = min(max(r_abs - WIN_ROWS // 2, 0), GRID_W - WIN_ROWS)
                for w in range(K_ROWS // per_vreg):
                    val = None
                    for m in range(per_vreg):
                        k_abs = row_origin[rt] + row_rel[rt] + per_vreg * w + m
                        ok = r_start <= k_abs < r_start + WIN_ROWS
                        src = tt_scr[ct, k_abs - r_abs + WIN_ROWS - 1] if ok else masked
                        val = src if val is None else jnp.where(lane_blk == m, src, val)
                    o_ref[0, 0, rt * 3 + ct, 0, qr * Q_COLS:(qr + 1) * Q_COLS, w * LANES:(w + 1) * LANES] = val


def _na_bias_tables(na_rpb):
    depth, h, n_r, n_c = na_rpb.shape
    rpb_pad = jnp.pad(na_rpb.astype(F32), ((0, 0), (0, 0), (0, 2 * SUBLANES - n_r),
                                            (RPB_LANE0, LANES - RPB_LANE0 - n_c)))
    nq, nk = Q_ROWS * Q_COLS, K_ROWS * K_COLS
    return pl.pallas_call(
        _bias_table_kernel,
        out_shape=jax.ShapeDtypeStruct((depth, h // 2, 9, 2, nq, nk), F32),
        grid=(depth, h),
        in_specs=[pl.BlockSpec((1, 1, 2 * SUBLANES, LANES), lambda l, i: (l, i, 0, 0))],
        out_specs=pl.BlockSpec((1, 1, 9, 1, nq, nk), lambda l, i: (l, i // 2, 0, i % 2, 0, 0)),
        scratch_shapes=[pltpu.VMEM((3, 2 * WIN_ROWS - 1, Q_COLS, LANES), F32)],
        compiler_params=_cparams(("parallel", "parallel")),
        name="na_bias_tables",
    )(rpb_pad)


def _neighborhood_attention(qkvu_lat, qkvu_ctx, bias_tabs, layer, bsz):
    n_lat = qkvu_lat.shape[0] // bsz
    n_ctx = qkvu_ctx.shape[0] // bsz
    rows = n_lat // GRID_W
    n_cols = qkvu_lat.shape[1]
    lat4 = qkvu_lat.reshape(bsz, rows, GRID_W, n_cols)
    ctx3 = qkvu_ctx.reshape(bsz, n_ctx, n_cols)
    n_pairs = NA_WIDTH // LANES
    step_rows = NA_ROW_TILES * Q_ROWS
    n_tiles = rows // step_rows
    out = pl.pallas_call(
        _na_kernel,
        out_shape=jax.ShapeDtypeStruct((bsz, rows, GRID_W, NA_WIDTH), F32),
        grid=(n_pairs, bsz, n_tiles),
        in_specs=[
            pl.BlockSpec((1, step_rows, GRID_W, LANES), lambda p, b, t: (b, t, 0, p)),
            pl.BlockSpec((1, rows, GRID_W, LANES), lambda p, b, t: (b, 0, 0, n_pairs + p)),
            pl.BlockSpec((1, rows, GRID_W, LANES), lambda p, b, t: (b, 0, 0, 2 * n_pairs + p)),
            pl.BlockSpec((1, n_ctx, LANES), lambda p, b, t: (b, 0, n_pairs + p)),
            pl.BlockSpec((1, n_ctx, LANES), lambda p, b, t: (b, 0, 2 * n_pairs + p)),
            pl.BlockSpec((1, 1, 9, 2, Q_ROWS * Q_COLS, K_ROWS * K_COLS), lambda p, b, t: (layer, p, 0, 0, 0, 0)),
        ],
        out_specs=pl.BlockSpec((1, step_rows, GRID_W, LANES), lambda p, b, t: (b, t, 0, p)),
        compiler_params=_cparams(("parallel", "parallel", "parallel")),
        name="neighborhood_attention",
    )(lat4, lat4, lat4, ctx3, ctx3, bias_tabs)
    return out.reshape(bsz * n_lat, NA_WIDTH)


def _ctx_attn_kernel(q_ref, k_ref, v_ref, o_ref):
    lo = lax.broadcasted_iota(jnp.int32, (1, LANES), 1) < HEAD_DIM
    qn = q_ref[0]
    kn = k_ref[0].astype(BF16)
    vb = v_ref[0].astype(BF16)

    def one(qm):
        s = _dot_nt(qm, kn)
        m = jnp.max(s, axis=-1, keepdims=True)
        p = jnp.exp(s - m)
        l = jnp.sum(p, axis=-1, keepdims=True)
        return _dot(p.astype(BF16), vb) / l

    o_a = one(jnp.where(lo, qn, 0.0).astype(BF16))
    o_b = one(jnp.where(lo, 0.0, qn).astype(BF16))
    o_ref[0] = jnp.where(lo, o_a, o_b)


def _context_attention(qkvu_ctx, bsz):
    n_ctx = qkvu_ctx.shape[0] // bsz
    ctx3 = qkvu_ctx.reshape(bsz, n_ctx, qkvu_ctx.shape[1])
    n_pairs = NA_WIDTH // LANES
    out = pl.pallas_call(
        _ctx_attn_kernel,
        out_shape=jax.ShapeDtypeStruct((bsz, n_ctx, NA_WIDTH), F32),
        grid=(bsz, n_pairs),
        in_specs=[
            pl.BlockSpec((1, n_ctx, LANES), lambda b, p: (b, 0, p)),
            pl.BlockSpec((1, n_ctx, LANES), lambda b, p: (b, 0, n_pairs + p)),
            pl.BlockSpec((1, n_ctx, LANES), lambda b, p: (b, 0, 2 * n_pairs + p)),
        ],
        out_specs=pl.BlockSpec((1, n_ctx, LANES), lambda b, p: (b, 0, p)),
        compiler_params=_cparams(("parallel", "parallel")),
        name="context_attention",
    )(ctx3, ctx3, ctx3)
    return out.reshape(bsz * n_ctx, NA_WIDTH)


def _s5_kernel(uc_ref, ul_ref, w_ref, m_ref, v_ref, a_ref, y_ref, x_scr, s_scr, hf_scr, hr_scr, *, n_ctx_chunks):
    bsz = ul_ref.shape[0]
    n_lat_chunks = ul_ref.shape[1] // CHUNK
    n_chunks = n_ctx_chunks + n_lat_chunks
    rows = n_chunks * SUBLANES
    n_pairs = w_ref.shape[1]
    gpb = 2 * n_pairs
    half = 2 * LANES
    tile_chunks = TM_PROJ // CHUNK
    pieces = [(uc_ref, 0, n_ctx_chunks, 0)] if n_ctx_chunks else []
    pieces += [(ul_ref, c0, min(S5_PIECE, n_lat_chunks - c0), n_ctx_chunks + c0)
               for c0 in range(0, n_lat_chunks, S5_PIECE)]

    def lane_block_ids(n):
        return lax.broadcasted_iota(jnp.int32, (n, LANES), 1) // S5_CH

    for src_ref, c_src, n_c, c_dst in pieces:
        lane_blk = lane_block_ids(n_c)
        for b in range(bsz):
            for q in range(2):
                rolled = []
                for j in range(SUBLANES):
                    s = SUBLANES * q + j
                    u_s = src_ref[b, pl.ds(c_src * CHUNK + s, n_c, stride=CHUNK), :]
                    rolled.append(u_s if j == 0 else pltpu.roll(u_s, j * S5_CH, 1))
                for g in range(gpb):
                    xg = rolled[0]
                    for j in range(1, SUBLANES):
                        xg = jnp.where(lane_blk == (g + j) % SUBLANES, rolled[j], xg)
                    x_scr[g // 2, q, pl.ds(c_dst * SUBLANES + (g % 2) * bsz + b, n_c, stride=SUBLANES), :] = xg

    n_blk = 8
    rb = rows // n_blk
    first_group = (lax.broadcasted_iota(jnp.int32, (rb, half), 0) & (SUBLANES // 2)) == 0
    fwd_cols = (lax.broadcasted_iota(jnp.int32, (rb, half), 1) & (LANES - 1)) < S5_STATE
    is_fwd = lax.broadcasted_iota(jnp.int32, (SUBLANES, LANES), 1) < S5_STATE
    first_rows = lax.broadcasted_iota(jnp.int32, (SUBLANES, half), 0) < SUBLANES // 2
    zero = jnp.zeros((SUBLANES, LANES), F32)

    def x_rows(p, sl):
        return jnp.concatenate([x_scr[p, 0, sl, :], x_scr[p, 1, sl, :]], axis=1).astype(BF16)

    def put_cols(scr, i, sl, val):
        scr[i, 0, sl, :] = val[:, 0:LANES]
        scr[i, 1, sl, :] = val[:, LANES:half]

    for hh in range(n_pairs // PAIRS_PER_STEP):
        pairs = [hh * PAIRS_PER_STEP + i for i in range(PAIRS_PER_STEP)]
        for i, p in enumerate(pairs):
            for blk in range(n_blk):
                sl = slice(blk * rb, (blk + 1) * rb)
                r = _dot(x_rows(p, sl), w_ref[0, p])
                put_cols(s_scr, i, sl, jnp.where(first_group, r[:, :half], r[:, half:]))

        a_pair = [jnp.where(first_rows, a_ref[0, 2 * p], a_ref[0, 2 * p + 1]) for p in pairs]
        a_re = [a[:, 0:LANES] for a in a_pair]
        a_im = [a[:, LANES:half] for a in a_pair]

        def body(k, carry):
            kr = jnp.where(k < n_ctx_chunks, n_ctx_chunks - 1 - k, n_chunks + n_ctx_chunks - 1 - k)
            rf = pl.ds(pl.multiple_of(k * SUBLANES, SUBLANES), SUBLANES)
            rr = pl.ds(pl.multiple_of(kr * SUBLANES, SUBLANES), SUBLANES)
            new = []
            for i in range(PAIRS_PER_STEP):
                h_re, h_im = carry[2 * i], carry[2 * i + 1]
                hf_scr[i, 0, rf, :] = h_re
                hf_scr[i, 1, rf, :] = h_im
                hr_scr[i, 0, rr, :] = h_re
                hr_scr[i, 1, rr, :] = h_im
                s_re = jnp.where(is_fwd, s_scr[i, 0, rf, :], s_scr[i, 0, rr, :])
                s_im = jnp.where(is_fwd, s_scr[i, 1, rf, :], s_scr[i, 1, rr, :])
                new.append(a_re[i] * h_re - a_im[i] * h_im + s_re)
                new.append(a_re[i] * h_im + a_im[i] * h_re + s_im)
            return tuple(new)

        lax.fori_loop(0, n_chunks, body, (zero,) * (2 * PAIRS_PER_STEP))

        for i, p in enumerate(pairs):
            for blk in range(n_blk):
                sl = slice(blk * rb, (blk + 1) * rb)
                hf = jnp.concatenate([hf_scr[i, 0, sl, :], hf_scr[i, 1, sl, :]], axis=1)
                hr = jnp.concatenate([hr_scr[i, 0, sl, :], hr_scr[i, 1, sl, :]], axis=1)
                h_in = jnp.where(fwd_cols, hf, hr).astype(BF16)
                r = _dot(x_rows(p, sl), m_ref[0, p]) + _dot_nt(h_in, v_ref[0, p])
                put_cols(s_scr, i, sl, jnp.where(first_group, r[:, :half], r[:, half:]))

        g_lo = 2 * pairs[0]
        n_g = 2 * PAIRS_PER_STEP
        out_blk = lax.broadcasted_iota(jnp.int32, (tile_chunks, LANES), 1) // S5_CH
        lanes_out = (out_blk >= g_lo) & (out_blk < g_lo + n_g)
        for _, _, n_c, c_dst in pieces:
            lane_blk = lane_block_ids(n_c)
            for b in range(bsz):
                for q in range(2):
                    y_g = [s_scr[(g - g_lo) // 2, q,
                                 pl.ds(c_dst * SUBLANES + (g % 2) * bsz + b, n_c, stride=SUBLANES), :]
                           for g in range(g_lo, g_lo + n_g)]
                    for j in range(SUBLANES):
                        z = y_g[0]
                        for gi in range(1, n_g):
                            z = jnp.where(lane_blk == (g_lo + gi + j) % SUBLANES, y_g[gi], z)
                        if j:
                            z = pltpu.roll(z, LANES - j * S5_CH, 1)
                        t = SUBLANES * q + j
                        for ct in range(n_c // tile_chunks):
                            r0 = (c_dst // tile_chunks + ct) * TM_PROJ + t * tile_chunks
                            pltpu.store(y_ref.at[b, 0, r0:r0 + tile_chunks, :],
                                        z[ct * tile_chunks:(ct + 1) * tile_chunks, :], mask=lanes_out)


def _s5_scan(qkvu_ctx3, qkvu_lat3, mats, layer):
    w_c, m_c, vt_c, a_c = mats
    bsz, n_ctx, _ = qkvu_ctx3.shape
    n_lat = qkvu_lat3.shape[1]
    n_seq = n_ctx + n_lat
    rows = n_seq // CHUNK * SUBLANES
    n_blocks = S5_WIDTH // LANES
    ppb = w_c.shape[1] // n_blocks
    u_blk0 = 3 * NA_WIDTH // LANES
    wspec = pl.BlockSpec((1, ppb, 2 * LANES, 4 * LANES), lambda i: (layer, i, 0, 0))
    one = pl.Buffered(1)
    state = pltpu.VMEM((PAIRS_PER_STEP, 2, rows, LANES), F32)
    return pl.pallas_call(
        functools.partial(_s5_kernel, n_ctx_chunks=n_ctx // CHUNK),
        out_shape=jax.ShapeDtypeStruct((bsz, n_blocks, n_seq, LANES), F32),
        grid=(n_blocks,),
        in_specs=[
            pl.BlockSpec((bsz, n_ctx, LANES), lambda i: (0, 0, u_blk0 + i)),
            pl.BlockSpec((bsz, n_lat, LANES), lambda i: (0, 0, u_blk0 + i), pipeline_mode=one),
            wspec, wspec,
            pl.BlockSpec((1, ppb, 4 * LANES, 2 * LANES), lambda i: (layer, i, 0, 0)),
            pl.BlockSpec((1, 2 * ppb, SUBLANES, 2 * LANES), lambda i: (layer, i, 0, 0)),
        ],
        out_specs=pl.BlockSpec((bsz, 1, n_seq, LANES), lambda i: (0, i, 0, 0), pipeline_mode=one),
        scratch_shapes=[pltpu.VMEM((ppb, 2, rows, LANES), F32), state, state, state],
        compiler_params=_cparams(("parallel",)),
        name="s5_scan",
    )(qkvu_ctx3, qkvu_lat3, w_c, m_c, vt_c, a_c)


def _s5_mats_kernel(prm_ref, btr_ref, bti_ref, cr_ref, ci_ref, w_ref, m_ref, vt_ref, a_ref):
    t = CHUNK
    gl = pl.program_id(1) % SUBLANES
    is_fwd = lax.broadcasted_iota(jnp.int32, (1, LANES), 1) < S5_STATE
    lr = prm_ref[0, 0, 0:1, :]
    li = prm_ref[0, 0, 1:2, :]
    dt = jnp.exp(prm_ref[0, 0, 2:3, :])
    n = lax.broadcasted_iota(jnp.int32, (3 * SUBLANES, LANES), 0).astype(F32)
    pmag = jnp.exp(n * (lr * dt))
    pw_re = pmag * jnp.cos(n * (li * dt))
    pw_im = pmag * jnp.sin(n * (li * dt))
    ab_re, ab_im = pw_re[1:2, :], pw_im[1:2, :]
    den = lr * lr + li * li
    nr = ab_re - 1.0
    z_re = (nr * lr + ab_im * li) / den
    z_im = (ab_im * lr - nr * li) / den
    bt_re, bt_im = btr_ref[0, 0], bti_ref[0, 0]
    bb_re = z_re * bt_re - z_im * bt_im
    bb_im = z_re * bt_im + z_im * bt_re
    c_re, c_im = cr_ref[0, 0], ci_ref[0, 0]

    def powers(n_fwd, n_rev):
        return (jnp.where(is_fwd, pw_re[n_fwd:n_fwd + 1, :], pw_re[n_rev:n_rev + 1, :]),
                jnp.where(is_fwd, pw_im[n_fwd:n_fwd + 1, :], pw_im[n_rev:n_rev + 1, :]))

    def block_rows(s):
        pos = SUBLANES * (s // SUBLANES) + (s % SUBLANES + gl) % SUBLANES
        return pl.ds(pl.multiple_of(pos * S5_CH, S5_CH), S5_CH)

    for s in range(t):
        rows = block_rows(s)
        p_re, p_im = powers(t - 1 - s, s)
        w_ref[0, 0, rows, 0:LANES] = (bb_re * p_re - bb_im * p_im).astype(BF16)
        w_ref[0, 0, rows, LANES:2 * LANES] = (bb_re * p_im + bb_im * p_re).astype(BF16)
        q_re, q_im = powers(s + 1, t - s)
        vt_ref[0, 0, rows, 0:LANES] = (c_re * q_re - c_im * q_im).astype(BF16)
        vt_ref[0, 0, rows, LANES:2 * LANES] = (-(c_re * q_im + c_im * q_re)).astype(BF16)

    ca_re, ca_im = [], []
    for lag in range(t):
        p_re, p_im = powers(lag, t - 1 - lag)
        ca_re.append(c_re * p_re - c_im * p_im)
        ca_im.append(c_re * p_im + c_im * p_re)
    stack = jnp.concatenate([jnp.concatenate(ca_re, axis=0), jnp.concatenate(ca_im, axis=0)], axis=1)
    zero = jnp.zeros_like(bb_re)
    lhs = jnp.concatenate([
        jnp.concatenate([jnp.where(is_fwd, bb_re, zero), jnp.where(is_fwd, -bb_im, zero)], axis=1),
        jnp.concatenate([jnp.where(is_fwd, zero, bb_re), jnp.where(is_fwd, zero, -bb_im)], axis=1)], axis=0)
    kt = lax.dot_general(lhs, stack, (((1,), (1,)), ((), ())), precision=HIGHEST, preferred_element_type=F32)
    kt_f, kt_r = kt[0:S5_CH], kt[S5_CH:2 * S5_CH]
    blk = lax.broadcasted_iota(jnp.int32, (S5_CH, 2 * LANES), 1) // S5_CH
    for s in range(t):
        strip = (jnp.where(blk >= s, pltpu.roll(kt_f, S5_CH * s, 1), 0.0)
                 + jnp.where(blk <= s, pltpu.roll(kt_r, (S5_CH * (s - t + 1)) % (2 * LANES), 1), 0.0))
        strip = jnp.concatenate([pltpu.roll(strip[:, 0:LANES], gl * S5_CH, 1),
                                 pltpu.roll(strip[:, LANES:2 * LANES], gl * S5_CH, 1)], axis=1)
        m_ref[0, 0, block_rows(s), :] = strip.astype(BF16)

    a_ref[0, 0, :, 0:LANES] = jnp.broadcast_to(pw_re[t:t + 1, :], (SUBLANES, LANES))
    a_ref[0, 0, :, LANES:2 * LANES] = jnp.broadcast_to(pw_im[t:t + 1, :], (SUBLANES, LANES))


def _s5_matrices(lam_re, lam_im, log_dt, b_re, b_im, c_re, c_im):
    depth, _, g, p = lam_re.shape
    hc = b_re.shape[-1]
    width = CHUNK * hc
    both = lambda x: jnp.transpose(x.astype(F32), (0, 2, 1, 3)).reshape(depth, g, 1, 2 * p)
    dt_rows = jnp.broadcast_to(jnp.transpose(log_dt.astype(F32), (0, 2, 1))[..., None], (depth, g, 2, p))
    prm = jnp.concatenate([both(lam_re), both(lam_im), dt_rows.reshape(depth, g, 1, 2 * p),
                           jnp.zeros((depth, g, SUBLANES - 3, 2 * p), F32)], axis=2)
    bt = lambda x: jnp.transpose(x.astype(F32), (0, 2, 4, 1, 3)).reshape(depth, g, hc, 2 * p)
    ct = lambda x: jnp.transpose(x.astype(F32), (0, 2, 3, 1, 4)).reshape(depth, g, hc, 2 * p)
    vec = lambda rows: pl.BlockSpec((1, 1, rows, 2 * p), lambda l, i: (l, i, 0, 0))
    return pl.pallas_call(
        _s5_mats_kernel,
        out_shape=(jax.ShapeDtypeStruct((depth, g // 2, width, 2 * width), BF16),
                   jax.ShapeDtypeStruct((depth, g // 2, width, 2 * width), BF16),
                   jax.ShapeDtypeStruct((depth, g // 2, 2 * width, width), BF16),
                   jax.ShapeDtypeStruct((depth, g, SUBLANES, 4 * p), F32)),
        grid=(depth, g),
        in_specs=[vec(SUBLANES), vec(hc), vec(hc), vec(hc), vec(hc)],
        out_specs=(pl.BlockSpec((1, 1, width, width), lambda l, i: (l, i // 2, 0, i % 2)),
                   pl.BlockSpec((1, 1, width, width), lambda l, i: (l, i // 2, 0, i % 2)),
                   pl.BlockSpec((1, 1, width, width), lambda l, i: (l, i // 2, i % 2, 0)),
                   pl.BlockSpec((1, 1, SUBLANES, 4 * p), lambda l, i: (l, i, 0, 0))),
        compiler_params=_cparams(("parallel", "parallel")),
        name="s5_matrices",
    )(prm, bt(b_re), bt(b_im), ct(c_re), ct(c_im))


def _s5_mixer(qkvu_lat, qkvu_ctx, mats, layer, bsz):
    n_cols = qkvu_lat.shape[1]
    return _s5_scan(qkvu_ctx.reshape(bsz, -1, n_cols), qkvu_lat.reshape(bsz, -1, n_cols), mats, layer)


def _outproj_kernel(na_ref, y_ref, u_ref, h_ref, mod_ref, d_ref, wglu_ref, bglu_ref, wout_ref, g2_ref,
                    ho_ref, f_ref):
    tile_chunks = TM_PROJ // CHUNK
    y = jnp.concatenate(
        [jnp.concatenate([y_ref[0, blk, pl.ds(c, CHUNK, stride=tile_chunks), :] for c in range(tile_chunks)], axis=0)
         for blk in range(S5_WIDTH // LANES)], axis=1)
    z = jax.nn.gelu(y + d_ref[...] * u_ref[...])
    s5 = z * jax.nn.sigmoid(_dot(z.astype(BF16), wglu_ref[...]) + bglu_ref[...])
    mix = (_dot(na_ref[...].astype(BF16), wout_ref[0:NA_WIDTH, :])
           + _dot(s5.astype(BF16), wout_ref[NA_WIDTH:NA_WIDTH + S5_WIDTH, :]))
    d = D_MODEL
    gate = mod_ref[0, :, 2 * d:3 * d]
    h = h_ref[...] + gate * mix
    ho_ref[...] = h
    ms = jnp.mean(h * h, axis=-1, keepdims=True)
    y2 = h * lax.rsqrt(ms + EPS) * g2_ref[...]
    f_ref[...] = y2 * (1.0 + mod_ref[0, :, 4 * d:5 * d]) + mod_ref[0, :, 3 * d:4 * d]


def _out_projection(na, y_all, qkvu, h2d, mod3, d_skip, wglu_bf16, b_glu, wout_bf16, g2,
                    mod_map, rows_per_batch, y_block0):
    r, d = h2d.shape
    tiles_per_batch = rows_per_batch // TM_PROJ
    u_blk = 3 * NA_WIDTH // S5_WIDTH

    def y_map(i):
        return (i // tiles_per_batch, 0, y_block0 + i % tiles_per_batch, 0)

    const = lambda i: (0, 0)
    return pl.pallas_call(
        _outproj_kernel,
        out_shape=(jax.ShapeDtypeStruct((r, d), F32), jax.ShapeDtypeStruct((r, d), F32)),
        grid=(r // TM_PROJ,),
        in_specs=[
            pl.BlockSpec((TM_PROJ, NA_WIDTH), lambda i: (i, 0)),
            pl.BlockSpec((1, S5_WIDTH // LANES, TM_PROJ, LANES), y_map),
            pl.BlockSpec((TM_PROJ, S5_WIDTH), lambda i: (i, u_blk)),
            pl.BlockSpec((TM_PROJ, d), lambda i: (i, 0)),
            pl.BlockSpec((1, 1, N_MOD * d), mod_map),
            pl.BlockSpec((1, S5_WIDTH), const),
            pl.BlockSpec((S5_WIDTH, S5_WIDTH), const),
            pl.BlockSpec((1, S5_WIDTH), const),
            pl.BlockSpec((NA_WIDTH + S5_WIDTH, d), const),
            pl.BlockSpec((1, d), const),
        ],
        out_specs=(pl.BlockSpec((TM_PROJ, d), lambda i: (i, 0)), pl.BlockSpec((TM_PROJ, d), lambda i: (i, 0))),
        compiler_params=_cparams(("parallel",)),
        name="out_projection",
    )(na, y_all, qkvu, h2d, mod3, d_skip.reshape(1, -1), wglu_bf16, b_glu.reshape(1, -1), wout_bf16,
      g2.reshape(1, d))


def _top2(vals):
    best = vals[0]
    bi = jnp.zeros(best.shape, jnp.int32)
    for i in range(1, len(vals)):
        gt = vals[i] > best
        best = jnp.where(gt, vals[i], best)
        bi = jnp.where(gt, i, bi)
    second = jnp.full(best.shape, -jnp.inf, F32)
    si = jnp.zeros(best.shape, jnp.int32)
    for i in range(len(vals)):
        cand = jnp.where(bi == i, -jnp.inf, vals[i])
        gt = cand > second
        second = jnp.where(gt, cand, second)
        si = jnp.where(gt, i, si)
    return best, bi, second, si


def _route(f, rwt, rb):
    logits = lax.dot_general(rwt, f, (((1,), (1,)), ((), ())),
                             precision=HIGHEST, preferred_element_type=F32)
    m = jnp.max(logits, axis=0, keepdims=True)
    e = jnp.exp(logits - m)
    probs = e / jnp.sum(e, axis=0, keepdims=True)
    sel = probs + rb
    sel_rows = [sel[i:i + 1, :] for i in range(N_EXPERTS)]
    prob_rows = [probs[i:i + 1, :] for i in range(N_EXPERTS)]
    scores = []
    for g in range(N_GROUPS):
        b, _, s, _ = _top2(sel_rows[g * EPG:(g + 1) * EPG])
        scores.append(b + s)
    grp = jnp.zeros(scores[0].shape, jnp.int32)
    gbest = scores[0]
    for g in range(1, N_GROUPS):
        gt = scores[g] > gbest
        gbest = jnp.where(gt, scores[g], gbest)
        grp = jnp.where(gt, g, grp)
    in_rows = []
    for j in range(EPG):
        v = sel_rows[j]
        for g in range(1, N_GROUPS):
            v = jnp.where(grp == g, sel_rows[g * EPG + j], v)
        in_rows.append(v)
    _, l1, _, l2 = _top2(in_rows)
    i1 = grp * EPG + l1
    i2 = grp * EPG + l2
    w1 = jnp.zeros(gbest.shape, F32)
    w2 = jnp.zeros(gbest.shape, F32)
    for i in range(N_EXPERTS):
        w1 = jnp.where(i1 == i, prob_rows[i], w1)
        w2 = jnp.where(i2 == i, prob_rows[i], w2)
    tot = w1 + w2
    return i1, i2, w1 / tot, w2 / tot


def _router_kernel(f_ref, rwt_ref, rb_ref, idx_ref, gate_ref):
    i1, i2, g1, g2 = _route(f_ref[...], rwt_ref[...], rb_ref[...])
    idx_ref[0:1, :] = i1
    idx_ref[1:2, :] = i2
    gate_ref[0:1, :] = g1
    gate_ref[1:2, :] = g2


def _router(f_all, router_wt, router_b):
    n, d = f_all.shape
    return pl.pallas_call(
        _router_kernel,
        out_shape=(jax.ShapeDtypeStruct((2, n), jnp.int32), jax.ShapeDtypeStruct((2, n), F32)),
        grid=(n // TM_ROUTE,),
        in_specs=[
            pl.BlockSpec((TM_ROUTE, d), lambda i: (i, 0)),
            pl.BlockSpec((N_EXPERTS, d), lambda i: (0, 0)),
            pl.BlockSpec((N_EXPERTS, 1), lambda i: (0, 0)),
        ],
        out_specs=(pl.BlockSpec((2, TM_ROUTE), lambda i: (0, i)), pl.BlockSpec((2, TM_ROUTE), lambda i: (0, i))),
        compiler_params=_cparams(("parallel",)),
        name="router",
    )(f_all, router_wt, router_b)


def _scatter_rows_kernel(dest_ref, pad_ref, end_ref, *refs, n_tok, seg_tiles):
    f_refs = refs[:len(seg_tiles)]
    xs_ref, zero_scr, stage, sems = refs[len(seg_tiles):]
    i = pl.program_id(0)
    tm = f_refs[0].shape[0]
    d_blocks = f_refs[0].shape[1] // LANES
    n_rows = xs_ref.shape[0] // d_blocks
    fill_sem = sems.at[2]

    def token_rows(ref, first, n):
        start = first * d_blocks
        if not isinstance(first, int):
            start = pl.multiple_of(start, d_blocks)
        return ref.at[pl.ds(start, n * d_blocks), :]

    def slab_copy(start):
        return pltpu.make_async_copy(zero_scr, token_rows(xs_ref, start, TM_EXP), fill_sem)

    @pl.when(i == 0)
    def _():
        zero_scr[...] = jnp.zeros(zero_scr.shape, zero_scr.dtype)
        for e in range(N_EXPERTS):
            slab_copy(jnp.minimum(pad_ref[e], n_rows - TM_EXP)).start()
        for e in range(N_EXPERTS):
            slab_copy(0).wait()
        for k in range(N_EXPERTS):
            start = end_ref[0] + k * TM_EXP

            @pl.when(start < n_rows)
            def _():
                cp = slab_copy(start)
                cp.start()
                cp.wait()

    slot = i % 2

    def wait_tile(s):
        for _ in range(2):
            pltpu.make_async_copy(stage.at[s], token_rows(xs_ref, 0, tm), sems.at[s]).wait()

    def scatter_tile(f_ref):
        base = i * tm
        for k in range(d_blocks):
            stage[slot, pl.ds(k, tm, stride=d_blocks), :] = f_ref[:, k * LANES:(k + 1) * LANES]

        def row_copy(r, d):
            return pltpu.make_async_copy(token_rows(stage.at[slot], r, 1), token_rows(xs_ref, d, 1), sems.at[slot])

        for r in range(tm):
            row_copy(r, dest_ref[base + r]).start()
            row_copy(r, dest_ref[n_tok + base + r]).start()

    tile0 = 0
    for f_ref, n_t in zip(f_refs, seg_tiles):
        pl.when((i >= tile0) & (i < tile0 + n_t))(functools.partial(scatter_tile, f_ref))
        tile0 += n_t

    pl.when(i > 0)(lambda: wait_tile(1 - slot))
    pl.when(i == pl.num_programs(0) - 1)(lambda: wait_tile(slot))


def _scatter_rows(segments, dest_flat, pad_start, total_end):
    d = segments[0].shape[1]
    seg_tiles = tuple(s.shape[0] // TM_PROJ for s in segments)
    n_tok = sum(s.shape[0] for s in segments)
    r_max = 2 * n_tok + N_EXPERTS * TM_EXP
    in_specs = []
    tile0 = 0
    for n_t in seg_tiles:
        in_specs.append(pl.BlockSpec(
            (TM_PROJ, d), lambda i, *_, t0=tile0, nt=n_t: (jnp.clip(i - t0, 0, nt - 1), 0)))
        tile0 += n_t
    grid_spec = pltpu.PrefetchScalarGridSpec(
        num_scalar_prefetch=3,
        grid=(tile0,),
        in_specs=in_specs,
        out_specs=pl.BlockSpec(memory_space=pl.ANY),
        scratch_shapes=[pltpu.VMEM((TM_EXP * d // LANES, LANES), F32),
                        pltpu.VMEM((2, TM_PROJ * d // LANES, LANES), F32),
                        pltpu.SemaphoreType.DMA((3,))],
    )
    return pl.pallas_call(
        functools.partial(_scatter_rows_kernel, n_tok=n_tok, seg_tiles=seg_tiles),
        out_shape=jax.ShapeDtypeStruct((r_max * d // LANES, LANES), F32),
        grid_spec=grid_spec,
        compiler_params=_cparams(("arbitrary",)),
        name="moe_scatter_rows",
    )(dest_flat, pad_start, total_end, *segments)


def _experts_kernel(te_ref, nv_ref, x_ref, wg_ref, wu_ref, wd_ref, o_ref, wg_scr, wu_scr, wd_scr):
    i = pl.program_id(0)
    e = te_ref[i]
    prev = te_ref[jnp.maximum(i - 1, 0)]
    rows = 128

    @pl.when((i == 0) | (e != prev))
    def _():
        def body(r, carry):
            sl = pl.ds(pl.multiple_of(r * rows, rows), rows)
            wg_scr[sl, :] = wg_ref[0, 0, sl, :].astype(BF16)
            wu_scr[sl, :] = wu_ref[0, 0, sl, :].astype(BF16)
            wd_scr[sl, :] = wd_ref[0, 0, sl, :].astype(BF16)
            return carry
        lax.fori_loop(0, wg_scr.shape[0] // rows, body, 0)

    @pl.when(i < nv_ref[0])
    def _():
        d_blocks = wg_scr.shape[0] // LANES
        x = jnp.concatenate([x_ref[pl.ds(k, TM_EXP, stride=d_blocks), :].astype(BF16) for k in range(d_blocks)],
                            axis=1)
        g = _dot(x, wg_scr[...])
        u = _dot(x, wu_scr[...])
        a = (g * jax.nn.sigmoid(g)) * u
        o_ref[...] = _dot(a.astype(BF16), wd_scr[...]).astype(BF16)

    @pl.when(i >= nv_ref[0])
    def _():
        o_ref[...] = jnp.zeros(o_ref.shape, BF16)


def _experts(xs, tile_expert, n_valid, w_gate, w_up, w_down, layer):
    d, de = w_gate.shape[2], w_gate.shape[3]
    d_blocks = d // LANES
    r = xs.shape[0] // d_blocks
    n_tiles = r // TM_EXP
    x_map = lambda i, te, nv: (jnp.minimum(i, nv[0] - 1), 0)
    grid_spec = pltpu.PrefetchScalarGridSpec(
        num_scalar_prefetch=2,
        grid=(n_tiles,),
        in_specs=[
            pl.BlockSpec((TM_EXP * d_blocks, LANES), x_map),
            pl.BlockSpec((1, 1, d, de), lambda i, te, nv: (layer, te[i], 0, 0)),
            pl.BlockSpec((1, 1, d, de), lambda i, te, nv: (layer, te[i], 0, 0)),
            pl.BlockSpec((1, 1, de, d), lambda i, te, nv: (layer, te[i], 0, 0)),
        ],
        out_specs=pl.BlockSpec((TM_EXP, d), lambda i, te, nv: (i, 0)),
        scratch_shapes=[pltpu.VMEM((d, de), BF16), pltpu.VMEM((d, de), BF16), pltpu.VMEM((de, d), BF16)],
    )
    return pl.pallas_call(
        _experts_kernel,
        out_shape=jax.ShapeDtypeStruct((r, d), BF16),
        grid_spec=grid_spec,
        compiler_params=_cparams(("arbitrary",)),
        name="experts",
    )(tile_expert, n_valid, xs, w_gate, w_up, w_down)


def _dispatch(idx):
    n = idx.shape[1]
    e_flat = idx.reshape(-1)
    onehot = (e_flat[:, None] == jnp.arange(N_EXPERTS, dtype=jnp.int32)[None, :]).astype(jnp.int32)
    csum = jnp.cumsum(onehot, axis=0)
    rank = jnp.sum(csum * onehot, axis=1) - 1
    counts = csum[-1]
    padded = ((counts + TM_EXP - 1) // TM_EXP) * TM_EXP
    ends = jnp.cumsum(padded)
    starts = ends - padded
    dest = (jnp.sum(onehot * starts[None, :], axis=1) + rank).astype(jnp.int32)
    r_max = 2 * n + N_EXPERTS * TM_EXP
    tile_start = jnp.arange(r_max // TM_EXP, dtype=jnp.int32) * TM_EXP
    tile_expert = jnp.minimum(jnp.sum((tile_start[:, None] >= ends[None, :]).astype(jnp.int32), axis=1),
                              N_EXPERTS - 1).astype(jnp.int32)
    n_valid = (ends[-1] // TM_EXP).astype(jnp.int32).reshape(1)
    pad_start = (starts + counts).astype(jnp.int32)
    total_end = ends[-1].astype(jnp.int32).reshape(1)
    return dest, pad_start, total_end, tile_expert, n_valid


def _combine_kernel(h_ref, y1_ref, y2_ref, gate_ref, mod_ref, o_ref):
    d = D_MODEL
    g = gate_ref[...]
    y = g[:, 0:1] * y1_ref[...].astype(F32) + g[:, 1:2] * y2_ref[...].astype(F32)
    o_ref[...] = h_ref[...] + mod_ref[0, :, 5 * d:6 * d] * y


def _combine(h2d, y1, y2, gates_t, mod3, mod_map, row0):
    r, d = h2d.shape
    blk0 = row0 // TM_PROJ
    row = lambda i: (i, 0)
    seg = lambda i: (blk0 + i, 0)
    return pl.pallas_call(
        _combine_kernel,
        out_shape=jax.ShapeDtypeStruct((r, d), F32),
        grid=(r // TM_PROJ,),
        in_specs=[
            pl.BlockSpec((TM_PROJ, d), row),
            pl.BlockSpec((TM_PROJ, d), seg),
            pl.BlockSpec((TM_PROJ, d), seg),
            pl.BlockSpec((TM_PROJ, 2), seg),
            pl.BlockSpec((1, 1, N_MOD * d), mod_map),
        ],
        out_specs=pl.BlockSpec((TM_PROJ, d), row),
        compiler_params=_cparams(("parallel",)),
        name="moe_combine",
    )(h2d, y1, y2, gates_t, mod3)


def kernel(x, c, ctx, c_ctx, w_mod, b_mod, norm1_g, norm2_g, w_in, w_out, q_norm_g, k_norm_g, na_rpb,
           s5_lam_re, s5_lam_im, s5_log_dt, s5_b_re, s5_b_im, s5_c_re, s5_c_im, s5_d, s5_w_glu, s5_b_glu,
           router_w, router_bias, moe_w_gate, moe_w_up, moe_w_down):
    bsz, n_lat, d = x.shape
    n_ctx = ctx.shape[1]
    depth = w_mod.shape[0]
    ctx_row = bsz
    c_rows = jnp.concatenate([c.astype(F32), c_ctx.astype(F32)[None],
                              jnp.zeros((SUBLANES - bsz - 1, d), F32)], axis=0)
    mod_all = _modulation(c_rows, w_mod.astype(F32), b_mod.astype(F32))

    h_lat = x.reshape(bsz * n_lat, d).astype(F32)
    h_ctx = ctx.reshape(bsz * n_ctx, d).astype(F32)
    lat_map = _mod_row_map(n_lat, 0, True)
    ctx_map = _mod_row_map(n_ctx, ctx_row, False)
    bias_tabs = _na_bias_tables(na_rpb)
    router_wt = router_w.T.astype(F32)
    router_b = router_bias.reshape(N_EXPERTS, 1).astype(F32)
    s5_mats = _s5_matrices(s5_lam_re, s5_lam_im, s5_log_dt, s5_b_re, s5_b_im, s5_c_re, s5_c_im)

    n_l = bsz * n_lat
    pending = None
    for layer in range(depth):
        ctx_out = layer < depth - 1
        mod3 = mod_all[layer].reshape(SUBLANES, 1, N_MOD * d)
        w_in_b = w_in[layer].astype(BF16)
        proj = functools.partial(_in_projection, g=norm1_g[layer], mod3=mod3, w_bf16=w_in_b,
                                 qg=q_norm_g[layer], kg=k_norm_g[layer])
        if pending is None:
            qkvu_lat = proj(h_lat, mod_map=lat_map)
            qkvu_ctx = proj(h_ctx, mod_map=ctx_map)
        else:
            qkvu_lat, h_lat = proj(h_lat, mod_map=lat_map, pending=pending, row0=0)
            qkvu_ctx, h_ctx = proj(h_ctx, mod_map=ctx_map, pending=pending, row0=n_l)
        na_lat = _neighborhood_attention(qkvu_lat, qkvu_ctx, bias_tabs, layer, bsz)
        y_all = _s5_mixer(qkvu_lat, qkvu_ctx, s5_mats, layer, bsz)
        wglu_b = s5_w_glu[layer].astype(BF16)
        wout_b = w_out[layer].astype(BF16)
        h_lat, f_lat = _out_projection(na_lat, y_all, qkvu_lat, h_lat, mod3, s5_d[layer], wglu_b,
                                       s5_b_glu[layer], wout_b, norm2_g[layer],
                                       lat_map, n_lat, n_ctx // TM_PROJ)
        idx, gates = _router(f_lat, router_wt, router_b)
        if ctx_out:
            na_ctx = _context_attention(qkvu_ctx, bsz)
            h_ctx, f_ctx = _out_projection(na_ctx, y_all, qkvu_ctx, h_ctx, mod3, s5_d[layer], wglu_b,
                                           s5_b_glu[layer], wout_b, norm2_g[layer],
                                           ctx_map, n_ctx, 0)
            idx_c, gates_c = _router(f_ctx, router_wt, router_b)
            idx = jnp.concatenate([idx, idx_c], axis=1)
            gates = jnp.concatenate([gates, gates_c], axis=1)
        n_tok = idx.shape[1]
        dest, pad_start, total_end, tile_expert, n_valid = _dispatch(idx)
        xs = _scatter_rows([f_lat, f_ctx] if ctx_out else [f_lat], dest, pad_start, total_end)
        ys = _experts(xs, tile_expert, n_valid, moe_w_gate, moe_w_up, moe_w_down, layer)
        y1 = jnp.take(ys, dest[:n_tok], axis=0, mode="clip")
        y2 = jnp.take(ys, dest[n_tok:], axis=0, mode="clip")
        gates_t = gates.T
        if ctx_out:
            pending = (y1, y2, gates_t, mod3)
        else:
            h_lat = _combine(h_lat, y1, y2, gates_t, mod3, lat_map, 0)
    return h_lat.reshape(bsz, n_lat, d).astype(x.dtype)
```

```python
import functools
import math

import jax
import jax.numpy as jnp
from jax import lax
from jax.experimental import pallas as pl
from jax.experimental.pallas import tpu as pltpu

F32 = jnp.float32
BF16 = jnp.bfloat16
HIGHEST = lax.Precision.HIGHEST

D_MODEL = 1024
GRID_W = 64
HEAD_DIM = 64
NA_WIDTH = 512
S5_WIDTH = 512
S5_CH = 16
S5_GROUPS = 32
S5_STATE = 64
WIN_ROWS = 8
WIN_COLS = 16
N_EXPERTS = 16
N_GROUPS = 4
EPG = 4
N_MOD = 6
EPS = 1e-6

LANES = 128
SUBLANES = 8
VMEM_LIMIT = 56 * 1024 * 1024

TM_PROJ = 256
Q_ROWS = 8
NA_ROW_TILES = 2
Q_COLS = 16
K_ROWS = 16
K_COLS = 32
CHUNK = 16
PAIRS_PER_STEP = 2
S5_PIECE = 32
TM_EXP = 512
TM_ROUTE = 1024
MASK_VALUE = -1e30
RPB_LANE0 = 48


def _cparams(sem):
    return pltpu.CompilerParams(dimension_semantics=sem, vmem_limit_bytes=VMEM_LIMIT)


def _dot(a, b):
    return jnp.dot(a, b, preferred_element_type=F32)


def _dot_nt(a, b):
    return lax.dot_general(a, b, (((1,), (1,)), ((), ())), preferred_element_type=F32)


def _mod_kernel(c_ref, w_ref, b_ref, o_ref):
    a = c_ref[...]
    a = a * jax.nn.sigmoid(a)
    o_ref[0] = jnp.dot(a, w_ref[0], precision=HIGHEST, preferred_element_type=F32) + b_ref[0]


def _modulation(c_rows, w_mod, b_mod):
    depth, d, n = w_mod.shape
    tn = 1536
    return pl.pallas_call(
        _mod_kernel,
        out_shape=jax.ShapeDtypeStruct((depth, SUBLANES, n), F32),
        grid=(depth, n // tn),
        in_specs=[
            pl.BlockSpec((SUBLANES, d), lambda l, j: (0, 0)),
            pl.BlockSpec((1, d, tn), lambda l, j: (l, 0, j)),
            pl.BlockSpec((1, 1, tn), lambda l, j: (l, 0, j)),
        ],
        out_specs=pl.BlockSpec((1, SUBLANES, tn), lambda l, j: (l, 0, j)),
        compiler_params=_cparams(("arbitrary", "arbitrary")),
        name="modulation",
    )(c_rows, w_mod, b_mod.reshape(depth, 1, n))


def _inproj_kernel(x_ref, g_ref, mod_ref, w_ref, qg_ref, kg_ref, *rest, moe_pending):
    x = x_ref[...]
    if moe_pending:
        y1_ref, y2_ref, gate_ref, modp_ref, o_ref, xo_ref = rest
        gw = gate_ref[...]
        y = gw[:, 0:1] * y1_ref[...].astype(F32) + gw[:, 1:2] * y2_ref[...].astype(F32)
        x = x + modp_ref[0, :, 5 * D_MODEL:6 * D_MODEL] * y
        xo_ref[...] = x
    else:
        (o_ref,) = rest
    ms = jnp.mean(x * x, axis=-1, keepdims=True)
    y = x * lax.rsqrt(ms + EPS) * g_ref[...]
    shift = mod_ref[0, :, 0:D_MODEL]
    scale = mod_ref[0, :, D_MODEL:2 * D_MODEL]
    a = y * (1.0 + scale) + shift
    acc = _dot(a.astype(BF16), w_ref[...])
    lo = lax.broadcasted_iota(jnp.int32, (1, LANES), 1) < HEAD_DIM
    n_pairs = NA_WIDTH // LANES
    for blk in range(2 * n_pairs):
        cols = slice(blk * LANES, (blk + 1) * LANES)
        if blk < n_pairs:
            o_ref[:, cols] = _pair_rms(acc[:, cols], qg_ref[...], lo) * (HEAD_DIM ** -0.5)
        else:
            o_ref[:, cols] = _pair_rms(acc[:, cols], kg_ref[...], lo)
    o_ref[:, 2 * NA_WIDTH:] = acc[:, 2 * NA_WIDTH:]


def _mod_row_map(rows_per_batch, mod_row0, per_batch):
    tiles_per_batch = rows_per_batch // TM_PROJ
    if per_batch:
        return lambda i: (mod_row0 + i // tiles_per_batch, 0, 0)
    return lambda i: (mod_row0, 0, 0)


def _in_projection(x2d, g, mod3, w_bf16, mod_map, qg, kg, pending=None, row0=0):
    r, d = x2d.shape
    n = w_bf16.shape[1]
    g2 = lambda v: jnp.concatenate([v, v]).reshape(1, LANES).astype(F32)
    row = lambda i: (i, 0)
    in_specs = [
        pl.BlockSpec((TM_PROJ, d), row),
        pl.BlockSpec((1, d), lambda i: (0, 0)),
        pl.BlockSpec((1, 1, N_MOD * d), mod_map),
        pl.BlockSpec((d, n), lambda i: (0, 0)),
        pl.BlockSpec((1, LANES), lambda i: (0, 0)),
        pl.BlockSpec((1, LANES), lambda i: (0, 0)),
    ]
    args = [x2d, g.reshape(1, d), mod3, w_bf16, g2(qg), g2(kg)]
    out_shape = jax.ShapeDtypeStruct((r, n), F32)
    out_specs = pl.BlockSpec((TM_PROJ, n), row)
    if pending is not None:
        blk0 = row0 // TM_PROJ
        seg = lambda i: (blk0 + i, 0)
        in_specs += [pl.BlockSpec((TM_PROJ, d), seg), pl.BlockSpec((TM_PROJ, d), seg),
                     pl.BlockSpec((TM_PROJ, 2), seg), pl.BlockSpec((1, 1, N_MOD * d), mod_map)]
        args += list(pending)
        out_shape = (out_shape, jax.ShapeDtypeStruct((r, d), F32))
        out_specs = (out_specs, pl.BlockSpec((TM_PROJ, d), row))
    return pl.pallas_call(
        functools.partial(_inproj_kernel, moe_pending=pending is not None),
        out_shape=out_shape,
        grid=(r // TM_PROJ,),
        in_specs=in_specs,
        out_specs=out_specs,
        compiler_params=_cparams(("parallel",)),
        name="in_projection",
    )(*args)


def _pair_rms(x, g, lo):
    ss = x * x
    sa = jnp.sum(jnp.where(lo, ss, 0.0), axis=-1, keepdims=True)
    sb = jnp.sum(jnp.where(lo, 0.0, ss), axis=-1, keepdims=True)
    ms = jnp.where(lo, sa, sb) * (1.0 / HEAD_DIM)
    return x * lax.rsqrt(ms + EPS) * g


def _na_kernel(q_ref, k_ref, v_ref, kc_ref, vc_ref, bias_ref, o_ref):
    lo = lax.broadcasted_iota(jnp.int32, (1, LANES), 1) < HEAD_DIM
    n_rows = k_ref.shape[1]
    col_tiles = GRID_W // Q_COLS
    nq = Q_ROWS * Q_COLS
    nk = K_ROWS * K_COLS
    kcb = kc_ref[0].astype(BF16)
    vcb = vc_ref[0].astype(BF16)
    tiles = [(rr, j) for rr in range(NA_ROW_TILES) for j in range(col_tiles)]
    q2 = []
    for rr, j in tiles:
        qn = q_ref[0, rr * Q_ROWS:(rr + 1) * Q_ROWS, j * Q_COLS:(j + 1) * Q_COLS, :].reshape(nq, LANES)
        q2.append(jnp.concatenate([jnp.where(lo, qn, 0.0), jnp.where(lo, 0.0, qn)], axis=0).astype(BF16))
    s_cx_all = _dot_nt(jnp.concatenate(q2, axis=0), kcb)
    p_cx, o_nb, denom = [], [], []
    for t, (rr, j) in enumerate(tiles):
        i = pl.program_id(2) * NA_ROW_TILES + rr
        kr0 = jnp.clip(Q_ROWS * i - WIN_ROWS // 2, 0, n_rows - K_ROWS)
        rt = jnp.where(i == 0, 0, jnp.where(i == n_rows // Q_ROWS - 1, 2, 1))
        kc0 = min(max(Q_COLS * j - WIN_COLS // 2, 0), GRID_W - K_COLS)
        ct = 0 if j == 0 else (2 if j == col_tiles - 1 else 1)
        kw = k_ref[0, pl.ds(kr0, K_ROWS), kc0:kc0 + K_COLS, :].reshape(nk, LANES).astype(BF16)
        vw = v_ref[0, pl.ds(kr0, K_ROWS), kc0:kc0 + K_COLS, :].reshape(nk, LANES).astype(BF16)
        s_nb = _dot_nt(q2[t], kw) + bias_ref[0, 0, rt * 3 + ct].reshape(2 * nq, nk)
        s_cx = s_cx_all[t * 2 * nq:(t + 1) * 2 * nq]
        m = jnp.maximum(jnp.max(s_nb, axis=-1, keepdims=True), jnp.max(s_cx, axis=-1, keepdims=True))
        p_nb = jnp.exp(s_nb - m)
        p_c = jnp.exp(s_cx - m)
        denom.append(jnp.sum(p_nb, axis=-1, keepdims=True) + jnp.sum(p_c, axis=-1, keepdims=True))
        p_cx.append(p_c.astype(BF16))
        o_nb.append(_dot(p_nb.astype(BF16), vw))
    o_cx_all = _dot(jnp.concatenate(p_cx, axis=0), vcb)
    for t, (rr, j) in enumerate(tiles):
        o2 = (o_nb[t] + o_cx_all[t * 2 * nq:(t + 1) * 2 * nq]) / denom[t]
        o_ref[0, rr * Q_ROWS:(rr + 1) * Q_ROWS, j * Q_COLS:(j + 1) * Q_COLS, :] = \
            jnp.where(lo, o2[0:nq], o2[nq:2 * nq]).reshape(Q_ROWS, Q_COLS, LANES)


def _bias_table_kernel(rpb_ref, o_ref, tt_scr):
    n_off_r = 2 * WIN_ROWS - 1
    lane = lax.broadcasted_iota(jnp.int32, (Q_COLS, LANES), 1)
    qc = lax.broadcasted_iota(jnp.int32, (Q_COLS, LANES), 0)
    kc = lane % K_COLS
    lane_blk = lane // K_COLS
    per_vreg = LANES // K_COLS
    col_rel = (0, -WIN_COLS // 2, -WIN_COLS)
    col_origin = (0, Q_COLS, GRID_W - Q_COLS)
    row_rel = (0, -WIN_ROWS // 2, -WIN_ROWS)
    row_origin = (0, Q_ROWS, GRID_W - Q_ROWS)
    masked = jnp.full((Q_COLS, LANES), MASK_VALUE, F32)

    for ct in range(3):
        c_abs = col_origin[ct] + qc
        k_abs = col_origin[ct] + col_rel[ct] + kc
        start = jnp.clip(c_abs - WIN_COLS // 2, 0, GRID_W - WIN_COLS)
        valid_c = (k_abs >= start) & (k_abs < start + WIN_COLS)
        base = (1 - WIN_COLS - col_rel[ct] - RPB_LANE0) % LANES
        for ro in range(n_off_r):
            row = jnp.broadcast_to(rpb_ref[0, 0, ro:ro + 1, :], (Q_COLS, LANES))
            t = pltpu.roll(row, base, 1, stride=1, stride_axis=0)
            rep = t
            for m in range(1, per_vreg):
                rep = jnp.where(lane_blk == m, pltpu.roll(t, K_COLS * m, 1), rep)
            tt_scr[ct, ro] = jnp.where(valid_c, rep, MASK_VALUE)

    for rt in range(3):
        for ct in range(3):
            for qr in range(Q_ROWS):
                r_abs = row_origin[rt] + qr
                r_start = min(max(r_abs - WIN_ROWS // 2, 0), GRID_W - WIN_ROWS)
                for w in range(K_ROWS // per_vreg):
                    val = None
                    for m in range(per_vreg):
                        k_abs = row_origin[rt] + row_rel[rt] + per_vreg * w + m
                        ok = r_start <= k_abs < r_start + WIN_ROWS
                        src = tt_scr[ct, k_abs - r_abs + WIN_ROWS - 1] if ok else masked
                        val = src if val is None else jnp.where(lane_blk == m, src, val)
                    o_ref[0, 0, rt * 3 + ct, 0, qr * Q_COLS:(qr + 1) * Q_COLS, w * LANES:(w + 1) * LANES] = val


def _na_bias_tables(na_rpb):
    depth, h, n_r, n_c = na_rpb.shape
    rpb_pad = jnp.pad(na_rpb.astype(F32), ((0, 0), (0, 0), (0, 2 * SUBLANES - n_r),
                                            (RPB_LANE0, LANES - RPB_LANE0 - n_c)))
    nq, nk = Q_ROWS * Q_COLS, K_ROWS * K_COLS
    return pl.pallas_call(
        _bias_table_kernel,
        out_shape=jax.ShapeDtypeStruct((depth, h // 2, 9, 2, nq, nk), F32),
        grid=(depth, h),
        in_specs=[pl.BlockSpec((1, 1, 2 * SUBLANES, LANES), lambda l, i: (l, i, 0, 0))],
        out_specs=pl.BlockSpec((1, 1, 9, 1, nq, nk), lambda l, i: (l, i // 2, 0, i % 2, 0, 0)),
        scratch_shapes=[pltpu.VMEM((3, 2 * WIN_ROWS - 1, Q_COLS, LANES), F32)],
        compiler_params=_cparams(("parallel", "parallel")),
        name="na_bias_tables",
    )(rpb_pad)


def _neighborhood_attention(qkvu_lat, qkvu_ctx, bias_tabs, layer, bsz):
    n_lat = qkvu_lat.shape[0] // bsz
    n_ctx = qkvu_ctx.shape[0] // bsz
    rows = n_lat // GRID_W
    n_cols = qkvu_lat.shape[1]
    lat4 = qkvu_lat.reshape(bsz, rows, GRID_W, n_cols)
    ctx3 = qkvu_ctx.reshape(bsz, n_ctx, n_cols)
    n_pairs = NA_WIDTH // LANES
    step_rows = NA_ROW_TILES * Q_ROWS
    n_tiles = rows // step_rows
    out = pl.pallas_call(
        _na_kernel,
        out_shape=jax.ShapeDtypeStruct((bsz, rows, GRID_W, NA_WIDTH), F32),
        grid=(n_pairs, bsz, n_tiles),
        in_specs=[
            pl.BlockSpec((1, step_rows, GRID_W, LANES), lambda p, b, t: (b, t, 0, p)),
            pl.BlockSpec((1, rows, GRID_W, LANES), lambda p, b, t: (b, 0, 0, n_pairs + p)),
            pl.BlockSpec((1, rows, GRID_W, LANES), lambda p, b, t: (b, 0, 0, 2 * n_pairs + p)),
            pl.BlockSpec((1, n_ctx, LANES), lambda p, b, t: (b, 0, n_pairs + p)),
            pl.BlockSpec((1, n_ctx, LANES), lambda p, b, t: (b, 0, 2 * n_pairs + p)),
            pl.BlockSpec((1, 1, 9, 2, Q_ROWS * Q_COLS, K_ROWS * K_COLS), lambda p, b, t: (layer, p, 0, 0, 0, 0)),
        ],
        out_specs=pl.BlockSpec((1, step_rows, GRID_W, LANES), lambda p, b, t: (b, t, 0, p)),
        compiler_params=_cparams(("parallel", "parallel", "parallel")),
        name="neighborhood_attention",
    )(lat4, lat4, lat4, ctx3, ctx3, bias_tabs)
    return out.reshape(bsz * n_lat, NA_WIDTH)


def _ctx_attn_kernel(q_ref, k_ref, v_ref, o_ref):
    lo = lax.broadcasted_iota(jnp.int32, (1, LANES), 1) < HEAD_DIM
    qn = q_ref[0]
    kn = k_ref[0].astype(BF16)
    vb = v_ref[0].astype(BF16)

    def one(qm):
        s = _dot_nt(qm, kn)
        m = jnp.max(s, axis=-1, keepdims=True)
        p = jnp.exp(s - m)
        l = jnp.sum(p, axis=-1, keepdims=True)
        return _dot(p.astype(BF16), vb) / l

    o_a = one(jnp.where(lo, qn, 0.0).astype(BF16))
    o_b = one(jnp.where(lo, 0.0, qn).astype(BF16))
    o_ref[0] = jnp.where(lo, o_a, o_b)


def _context_attention(qkvu_ctx, bsz):
    n_ctx = qkvu_ctx.shape[0] // bsz
    ctx3 = qkvu_ctx.reshape(bsz, n_ctx, qkvu_ctx.shape[1])
    n_pairs = NA_WIDTH // LANES
    out = pl.pallas_call(
        _ctx_attn_kernel,
        out_shape=jax.ShapeDtypeStruct((bsz, n_ctx, NA_WIDTH), F32),
        grid=(bsz, n_pairs),
        in_specs=[
            pl.BlockSpec((1, n_ctx, LANES), lambda b, p: (b, 0, p)),
            pl.BlockSpec((1, n_ctx, LANES), lambda b, p: (b, 0, n_pairs + p)),
            pl.BlockSpec((1, n_ctx, LANES), lambda b, p: (b, 0, 2 * n_pairs + p)),
        ],
        out_specs=pl.BlockSpec((1, n_ctx, LANES), lambda b, p: (b, 0, p)),
        compiler_params=_cparams(("parallel", "parallel")),
        name="context_attention",
    )(ctx3, ctx3, ctx3)
    return out.reshape(bsz * n_ctx, NA_WIDTH)


def _s5_kernel(uc_ref, ul_ref, w_ref, m_ref, v_ref, a_ref, y_ref, x_scr, s_scr, hf_scr, hr_scr, *, n_ctx_chunks):
    bsz = ul_ref.shape[0]
    n_lat_chunks = ul_ref.shape[1] // CHUNK
    n_chunks = n_ctx_chunks + n_lat_chunks
    rows = n_chunks * SUBLANES
    n_pairs = w_ref.shape[1]
    gpb = 2 * n_pairs
    half = 2 * LANES
    tile_chunks = TM_PROJ // CHUNK
    pieces = [(uc_ref, 0, n_ctx_chunks, 0)] if n_ctx_chunks else []
    pieces += [(ul_ref, c0, min(S5_PIECE, n_lat_chunks - c0), n_ctx_chunks + c0)
               for c0 in range(0, n_lat_chunks, S5_PIECE)]

    def lane_block_ids(n):
        return lax.broadcasted_iota(jnp.int32, (n, LANES), 1) // S5_CH

    for src_ref, c_src, n_c, c_dst in pieces:
        lane_blk = lane_block_ids(n_c)
        for b in range(bsz):
            for q in range(2):
                rolled = []
                for j in range(SUBLANES):
                    s = SUBLANES * q + j
                    u_s = src_ref[b, pl.ds(c_src * CHUNK + s, n_c, stride=CHUNK), :]
                    rolled.append(u_s if j == 0 else pltpu.roll(u_s, j * S5_CH, 1))
                for g in range(gpb):
                    xg = rolled[0]
                    for j in range(1, SUBLANES):
                        xg = jnp.where(lane_blk == (g + j) % SUBLANES, rolled[j], xg)
                    x_scr[g // 2, q, pl.ds(c_dst * SUBLANES + (g % 2) * bsz + b, n_c, stride=SUBLANES), :] = xg

    n_blk = 8
    rb = rows // n_blk
    first_group = (lax.broadcasted_iota(jnp.int32, (rb, half), 0) & (SUBLANES // 2)) == 0
    fwd_cols = (lax.broadcasted_iota(jnp.int32, (rb, half), 1) & (LANES - 1)) < S5_STATE
    is_fwd = lax.broadcasted_iota(jnp.int32, (SUBLANES, LANES), 1) < S5_STATE
    first_rows = lax.broadcasted_iota(jnp.int32, (SUBLANES, half), 0) < SUBLANES // 2
    zero = jnp.zeros((SUBLANES, LANES), F32)

    def x_rows(p, sl):
        return jnp.concatenate([x_scr[p, 0, sl, :], x_scr[p, 1, sl, :]], axis=1).astype(BF16)

    def put_cols(scr, i, sl, val):
        scr[i, 0, sl, :] = val[:, 0:LANES]
        scr[i, 1, sl, :] = val[:, LANES:half]

    for hh in range(n_pairs // PAIRS_PER_STEP):
        pairs = [hh * PAIRS_PER_STEP + i for i in range(PAIRS_PER_STEP)]
        for i, p in enumerate(pairs):
            for blk in range(n_blk):
                sl = slice(blk * rb, (blk + 1) * rb)
                r = _dot(x_rows(p, sl), w_ref[0, p])
                put_cols(s_scr, i, sl, jnp.where(first_group, r[:, :half], r[:, half:]))

        a_pair = [jnp.where(first_rows, a_ref[0, 2 * p], a_ref[0, 2 * p + 1]) for p in pairs]
        a_re = [a[:, 0:LANES] for a in a_pair]
        a_im = [a[:, LANES:half] for a in a_pair]

        def body(k, carry):
            kr = jnp.where(k < n_ctx_chunks, n_ctx_chunks - 1 - k, n_chunks + n_ctx_chunks - 1 - k)
            rf = pl.ds(pl.multiple_of(k * SUBLANES, SUBLANES), SUBLANES)
            rr = pl.ds(pl.multiple_of(kr * SUBLANES, SUBLANES), SUBLANES)
            new = []
            for i in range(PAIRS_PER_STEP):
                h_re, h_im = carry[2 * i], carry[2 * i + 1]
                hf_scr[i, 0, rf, :] = h_re
                hf_scr[i, 1, rf, :] = h_im
                hr_scr[i, 0, rr, :] = h_re
                hr_scr[i, 1, rr, :] = h_im
                s_re = jnp.where(is_fwd, s_scr[i, 0, rf, :], s_scr[i, 0, rr, :])
                s_im = jnp.where(is_fwd, s_scr[i, 1, rf, :], s_scr[i, 1, rr, :])
                new.append(a_re[i] * h_re - a_im[i] * h_im + s_re)
                new.append(a_re[i] * h_im + a_im[i] * h_re + s_im)
            return tuple(new)

        lax.fori_loop(0, n_chunks, body, (zero,) * (2 * PAIRS_PER_STEP))

        for i, p in enumerate(pairs):
            for blk in range(n_blk):
                sl = slice(blk * rb, (blk + 1) * rb)
                hf = jnp.concatenate([hf_scr[i, 0, sl, :], hf_scr[i, 1, sl, :]], axis=1)
                hr = jnp.concatenate([hr_scr[i, 0, sl, :], hr_scr[i, 1, sl, :]], axis=1)
                h_in = jnp.where(fwd_cols, hf, hr).astype(BF16)
                r = _dot(x_rows(p, sl), m_ref[0, p]) + _dot_nt(h_in, v_ref[0, p])
                put_cols(s_scr, i, sl, jnp.where(first_group, r[:, :half], r[:, half:]))

        g_lo = 2 * pairs[0]
        n_g = 2 * PAIRS_PER_STEP
        out_blk = lax.broadcasted_iota(jnp.int32, (tile_chunks, LANES), 1) // S5_CH
        lanes_out = (out_blk >= g_lo) & (out_blk < g_lo + n_g)
        for _, _, n_c, c_dst in pieces:
            lane_blk = lane_block_ids(n_c)
            for b in range(bsz):
                for q in range(2):
                    y_g = [s_scr[(g - g_lo) // 2, q,
                                 pl.ds(c_dst * SUBLANES + (g % 2) * bsz + b, n_c, stride=SUBLANES), :]
                           for g in range(g_lo, g_lo + n_g)]
                    for j in range(SUBLANES):
                        z = y_g[0]
                        for gi in range(1, n_g):
                            z = jnp.where(lane_blk == (g_lo + gi + j) % SUBLANES, y_g[gi], z)
                        if j:
                            z = pltpu.roll(z, LANES - j * S5_CH, 1)
                        t = SUBLANES * q + j
                        for ct in range(n_c // tile_chunks):
                            r0 = (c_dst // tile_chunks + ct) * TM_PROJ + t * tile_chunks
                            pltpu.store(y_ref.at[b, 0, r0:r0 + tile_chunks, :],
                                        z[ct * tile_chunks:(ct + 1) * tile_chunks, :], mask=lanes_out)


def _s5_scan(qkvu_ctx3, qkvu_lat3, mats, layer):
    w_c, m_c, vt_c, a_c = mats
    bsz, n_ctx, _ = qkvu_ctx3.shape
    n_lat = qkvu_lat3.shape[1]
    n_seq = n_ctx + n_lat
    rows = n_seq // CHUNK * SUBLANES
    n_blocks = S5_WIDTH // LANES
    ppb = w_c.shape[1] // n_blocks
    u_blk0 = 3 * NA_WIDTH // LANES
    wspec = pl.BlockSpec((1, ppb, 2 * LANES, 4 * LANES), lambda i: (layer, i, 0, 0))
    one = pl.Buffered(1)
    state = pltpu.VMEM((PAIRS_PER_STEP, 2, rows, LANES), F32)
    return pl.pallas_call(
        functools.partial(_s5_kernel, n_ctx_chunks=n_ctx // CHUNK),
        out_shape=jax.ShapeDtypeStruct((bsz, n_blocks, n_seq, LANES), F32),
        grid=(n_blocks,),
        in_specs=[
            pl.BlockSpec((bsz, n_ctx, LANES), lambda i: (0, 0, u_blk0 + i)),
            pl.BlockSpec((bsz, n_lat, LANES), lambda i: (0, 0, u_blk0 + i), pipeline_mode=one),
            wspec, wspec,
            pl.BlockSpec((1, ppb, 4 * LANES, 2 * LANES), lambda i: (layer, i, 0, 0)),
            pl.BlockSpec((1, 2 * ppb, SUBLANES, 2 * LANES), lambda i: (layer, i, 0, 0)),
        ],
        out_specs=pl.BlockSpec((bsz, 1, n_seq, LANES), lambda i: (0, i, 0, 0), pipeline_mode=one),
        scratch_shapes=[pltpu.VMEM((ppb, 2, rows, LANES), F32), state, state, state],
        compiler_params=_cparams(("parallel",)),
        name="s5_scan",
    )(qkvu_ctx3, qkvu_lat3, w_c, m_c, vt_c, a_c)


def _s5_mats_kernel(prm_ref, btr_ref, bti_ref, cr_ref, ci_ref, w_ref, m_ref, vt_ref, a_ref):
    t = CHUNK
    gl = pl.program_id(1) % SUBLANES
    is_fwd = lax.broadcasted_iota(jnp.int32, (1, LANES), 1) < S5_STATE
    lr = prm_ref[0, 0, 0:1, :]
    li = prm_ref[0, 0, 1:2, :]
    dt = jnp.exp(prm_ref[0, 0, 2:3, :])
    n = lax.broadcasted_iota(jnp.int32, (3 * SUBLANES, LANES), 0).astype(F32)
    pmag = jnp.exp(n * (lr * dt))
    pw_re = pmag * jnp.cos(n * (li * dt))
    pw_im = pmag * jnp.sin(n * (li * dt))
    ab_re, ab_im = pw_re[1:2, :], pw_im[1:2, :]
    den = lr * lr + li * li
    nr = ab_re - 1.0
    z_re = (nr * lr + ab_im * li) / den
    z_im = (ab_im * lr - nr * li) / den
    bt_re, bt_im = btr_ref[0, 0], bti_ref[0, 0]
    bb_re = z_re * bt_re - z_im * bt_im
    bb_im = z_re * bt_im + z_im * bt_re
    c_re, c_im = cr_ref[0, 0], ci_ref[0, 0]

    def powers(n_fwd, n_rev):
        return (jnp.where(is_fwd, pw_re[n_fwd:n_fwd + 1, :], pw_re[n_rev:n_rev + 1, :]),
                jnp.where(is_fwd, pw_im[n_fwd:n_fwd + 1, :], pw_im[n_rev:n_rev + 1, :]))

    def block_rows(s):
        pos = SUBLANES * (s // SUBLANES) + (s % SUBLANES + gl) % SUBLANES
        return pl.ds(pl.multiple_of(pos * S5_CH, S5_CH), S5_CH)

    for s in range(t):
        rows = block_rows(s)
        p_re, p_im = powers(t - 1 - s, s)
        w_ref[0, 0, rows, 0:LANES] = (bb_re * p_re - bb_im * p_im).astype(BF16)
        w_ref[0, 0, rows, LANES:2 * LANES] = (bb_re * p_im + bb_im * p_re).astype(BF16)
        q_re, q_im = powers(s + 1, t - s)
        vt_ref[0, 0, rows, 0:LANES] = (c_re * q_re - c_im * q_im).astype(BF16)
        vt_ref[0, 0, rows, LANES:2 * LANES] = (-(c_re * q_im + c_im * q_re)).astype(BF16)

    ca_re, ca_im = [], []
    for lag in range(t):
        p_re, p_im = powers(lag, t - 1 - lag)
        ca_re.append(c_re * p_re - c_im * p_im)
        ca_im.append(c_re * p_im + c_im * p_re)
    stack = jnp.concatenate([jnp.concatenate(ca_re, axis=0), jnp.concatenate(ca_im, axis=0)], axis=1)
    zero = jnp.zeros_like(bb_re)
    lhs = jnp.concatenate([
        jnp.concatenate([jnp.where(is_fwd, bb_re, zero), jnp.where(is_fwd, -bb_im, zero)], axis=1),
        jnp.concatenate([jnp.where(is_fwd, zero, bb_re), jnp.where(is_fwd, zero, -bb_im)], axis=1)], axis=0)
    kt = lax.dot_general(lhs, stack, (((1,), (1,)), ((), ())), precision=HIGHEST, preferred_element_type=F32)
    kt_f, kt_r = kt[0:S5_CH], kt[S5_CH:2 * S5_CH]
    blk = lax.broadcasted_iota(jnp.int32, (S5_CH, 2 * LANES), 1) // S5_CH
    for s in range(t):
        strip = (jnp.where(blk >= s, pltpu.roll(kt_f, S5_CH * s, 1), 0.0)
                 + jnp.where(blk <= s, pltpu.roll(kt_r, (S5_CH * (s - t + 1)) % (2 * LANES), 1), 0.0))
        strip = jnp.concatenate([pltpu.roll(strip[:, 0:LANES], gl * S5_CH, 1),
                                 pltpu.roll(strip[:, LANES:2 * LANES], gl * S5_CH, 1)], axis=1)
        m_ref[0, 0, block_rows(s), :] = strip.astype(BF16)

    a_ref[0, 0, :, 0:LANES] = jnp.broadcast_to(pw_re[t:t + 1, :], (SUBLANES, LANES))
    a_ref[0, 0, :, LANES:2 * LANES] = jnp.broadcast_to(pw_im[t:t + 1, :], (SUBLANES, LANES))


def _s5_matrices(lam_re, lam_im, log_dt, b_re, b_im, c_re, c_im):
    depth, _, g, p = lam_re.shape
    hc = b_re.shape[-1]
    width = CHUNK * hc
    both = lambda x: jnp.transpose(x.astype(F32), (0, 2, 1, 3)).reshape(depth, g, 1, 2 * p)
    dt_rows = jnp.broadcast_to(jnp.transpose(log_dt.astype(F32), (0, 2, 1))[..., None], (depth, g, 2, p))
    prm = jnp.concatenate([both(lam_re), both(lam_im), dt_rows.reshape(depth, g, 1, 2 * p),
                           jnp.zeros((depth, g, SUBLANES - 3, 2 * p), F32)], axis=2)
    bt = lambda x: jnp.transpose(x.astype(F32), (0, 2, 4, 1, 3)).reshape(depth, g, hc, 2 * p)
    ct = lambda x: jnp.transpose(x.astype(F32), (0, 2, 3, 1, 4)).reshape(depth, g, hc, 2 * p)
    vec = lambda rows: pl.BlockSpec((1, 1, rows, 2 * p), lambda l, i: (l, i, 0, 0))
    return pl.pallas_call(
        _s5_mats_kernel,
        out_shape=(jax.ShapeDtypeStruct((depth, g // 2, width, 2 * width), BF16),
                   jax.ShapeDtypeStruct((depth, g // 2, width, 2 * width), BF16),
                   jax.ShapeDtypeStruct((depth, g // 2, 2 * width, width), BF16),
                   jax.ShapeDtypeStruct((depth, g, SUBLANES, 4 * p), F32)),
        grid=(depth, g),
        in_specs=[vec(SUBLANES), vec(hc), vec(hc), vec(hc), vec(hc)],
        out_specs=(pl.BlockSpec((1, 1, width, width), lambda l, i: (l, i // 2, 0, i % 2)),
                   pl.BlockSpec((1, 1, width, width), lambda l, i: (l, i // 2, 0, i % 2)),
                   pl.BlockSpec((1, 1, width, width), lambda l, i: (l, i // 2, i % 2, 0)),
                   pl.BlockSpec((1, 1, SUBLANES, 4 * p), lambda l, i: (l, i, 0, 0))),
        compiler_params=_cparams(("parallel", "parallel")),
        name="s5_matrices",
    )(prm, bt(b_re), bt(b_im), ct(c_re), ct(c_im))


def _s5_mixer(qkvu_lat, qkvu_ctx, mats, layer, bsz):
    n_cols = qkvu_lat.shape[1]
    return _s5_scan(qkvu_ctx.reshape(bsz, -1, n_cols), qkvu_lat.reshape(bsz, -1, n_cols), mats, layer)


def _outproj_kernel(na_ref, y_ref, u_ref, h_ref, mod_ref, d_ref, wglu_ref, bglu_ref, wout_ref, g2_ref,
                    ho_ref, f_ref):
    tile_chunks = TM_PROJ // CHUNK
    y = jnp.concatenate(
        [jnp.concatenate([y_ref[0, blk, pl.ds(c, CHUNK, stride=tile_chunks), :] for c in range(tile_chunks)], axis=0)
         for blk in range(S5_WIDTH // LANES)], axis=1)
    z = jax.nn.gelu(y + d_ref[...] * u_ref[...])
    s5 = z * jax.nn.sigmoid(_dot(z.astype(BF16), wglu_ref[...]) + bglu_ref[...])
    mix = (_dot(na_ref[...].astype(BF16), wout_ref[0:NA_WIDTH, :])
           + _dot(s5.astype(BF16), wout_ref[NA_WIDTH:NA_WIDTH + S5_WIDTH, :]))
    d = D_MODEL
    gate = mod_ref[0, :, 2 * d:3 * d]
    h = h_ref[...] + gate * mix
    ho_ref[...] = h
    ms = jnp.mean(h * h, axis=-1, keepdims=True)
    y2 = h * lax.rsqrt(ms + EPS) * g2_ref[...]
    f_ref[...] = y2 * (1.0 + mod_ref[0, :, 4 * d:5 * d]) + mod_ref[0, :, 3 * d:4 * d]


def _out_projection(na, y_all, qkvu, h2d, mod3, d_skip, wglu_bf16, b_glu, wout_bf16, g2,
                    mod_map, rows_per_batch, y_block0):
    r, d = h2d.shape
    tiles_per_batch = rows_per_batch // TM_PROJ
    u_blk = 3 * NA_WIDTH // S5_WIDTH

    def y_map(i):
        return (i // tiles_per_batch, 0, y_block0 + i % tiles_per_batch, 0)

    const = lambda i: (0, 0)
    return pl.pallas_call(
        _outproj_kernel,
        out_shape=(jax.ShapeDtypeStruct((r, d), F32), jax.ShapeDtypeStruct((r, d), F32)),
        grid=(r // TM_PROJ,),
        in_specs=[
            pl.BlockSpec((TM_PROJ, NA_WIDTH), lambda i: (i, 0)),
            pl.BlockSpec((1, S5_WIDTH // LANES, TM_PROJ, LANES), y_map),
            pl.BlockSpec((TM_PROJ, S5_WIDTH), lambda i: (i, u_blk)),
            pl.BlockSpec((TM_PROJ, d), lambda i: (i, 0)),
            pl.BlockSpec((1, 1, N_MOD * d), mod_map),
            pl.BlockSpec((1, S5_WIDTH), const),
            pl.BlockSpec((S5_WIDTH, S5_WIDTH), const),
            pl.BlockSpec((1, S5_WIDTH), const),
            pl.BlockSpec((NA_WIDTH + S5_WIDTH, d), const),
            pl.BlockSpec((1, d), const),
        ],
        out_specs=(pl.BlockSpec((TM_PROJ, d), lambda i: (i, 0)), pl.BlockSpec((TM_PROJ, d), lambda i: (i, 0))),
        compiler_params=_cparams(("parallel",)),
        name="out_projection",
    )(na, y_all, qkvu, h2d, mod3, d_skip.reshape(1, -1), wglu_bf16, b_glu.reshape(1, -1), wout_bf16,
      g2.reshape(1, d))


def _top2(vals):
    best = vals[0]
    bi = jnp.zeros(best.shape, jnp.int32)
    for i in range(1, len(vals)):
        gt = vals[i] > best
        best = jnp.where(gt, vals[i], best)
        bi = jnp.where(gt, i, bi)
    second = jnp.full(best.shape, -jnp.inf, F32)
    si = jnp.zeros(best.shape, jnp.int32)
    for i in range(len(vals)):
        cand = jnp.where(bi == i, -jnp.inf, vals[i])
        gt = cand > second
        second = jnp.where(gt, cand, second)
        si = jnp.where(gt, i, si)
    return best, bi, second, si


def _route(f, rwt, rb):
    logits = lax.dot_general(rwt, f, (((1,), (1,)), ((), ())),
                             precision=HIGHEST, preferred_element_type=F32)
    m = jnp.max(logits, axis=0, keepdims=True)
    e = jnp.exp(logits - m)
    probs = e / jnp.sum(e, axis=0, keepdims=True)
    sel = probs + rb
    sel_rows = [sel[i:i + 1, :] for i in range(N_EXPERTS)]
    prob_rows = [probs[i:i + 1, :] for i in range(N_EXPERTS)]
    scores = []
    for g in range(N_GROUPS):
        b, _, s, _ = _top2(sel_rows[g * EPG:(g + 1) * EPG])
        scores.append(b + s)
    grp = jnp.zeros(scores[0].shape, jnp.int32)
    gbest = scores[0]
    for g in range(1, N_GROUPS):
        gt = scores[g] > gbest
        gbest = jnp.where(gt, scores[g], gbest)
        grp = jnp.where(gt, g, grp)
    in_rows = []
    for j in range(EPG):
        v = sel_rows[j]
        for g in range(1, N_GROUPS):
            v = jnp.where(grp == g, sel_rows[g * EPG + j], v)
        in_rows.append(v)
    _, l1, _, l2 = _top2(in_rows)
    i1 = grp * EPG + l1
    i2 = grp * EPG + l2
    w1 = jnp.zeros(gbest.shape, F32)
    w2 = jnp.zeros(gbest.shape, F32)
    for i in range(N_EXPERTS):
        w1 = jnp.where(i1 == i, prob_rows[i], w1)
        w2 = jnp.where(i2 == i, prob_rows[i], w2)
    tot = w1 + w2
    return i1, i2, w1 / tot, w2 / tot


def _router_kernel(f_ref, rwt_ref, rb_ref, idx_ref, gate_ref):
    i1, i2, g1, g2 = _route(f_ref[...], rwt_ref[...], rb_ref[...])
    idx_ref[0:1, :] = i1
    idx_ref[1:2, :] = i2
    gate_ref[0:1, :] = g1
    gate_ref[1:2, :] = g2


def _router(f_all, router_wt, router_b):
    n, d = f_all.shape
    return pl.pallas_call(
        _router_kernel,
        out_shape=(jax.ShapeDtypeStruct((2, n), jnp.int32), jax.ShapeDtypeStruct((2, n), F32)),
        grid=(n // TM_ROUTE,),
        in_specs=[
            pl.BlockSpec((TM_ROUTE, d), lambda i: (i, 0)),
            pl.BlockSpec((N_EXPERTS, d), lambda i: (0, 0)),
            pl.BlockSpec((N_EXPERTS, 1), lambda i: (0, 0)),
        ],
        out_specs=(pl.BlockSpec((2, TM_ROUTE), lambda i: (0, i)), pl.BlockSpec((2, TM_ROUTE), lambda i: (0, i))),
        compiler_params=_cparams(("parallel",)),
        name="router",
    )(f_all, router_wt, router_b)


def _scatter_rows_kernel(dest_ref, pad_ref, end_ref, *refs, n_tok, seg_tiles):
    f_refs = refs[:len(seg_tiles)]
    xs_ref, zero_scr, stage, sems = refs[len(seg_tiles):]
    i = pl.program_id(0)
    tm = f_refs[0].shape[0]
    d_blocks = f_refs[0].shape[1] // LANES
    n_rows = xs_ref.shape[0] // d_blocks
    fill_sem = sems.at[2]

    def token_rows(ref, first, n):
        start = first * d_blocks
        if not isinstance(first, int):
            start = pl.multiple_of(start, d_blocks)
        return ref.at[pl.ds(start, n * d_blocks), :]

    def slab_copy(start):
        return pltpu.make_async_copy(zero_scr, token_rows(xs_ref, start, TM_EXP), fill_sem)

    @pl.when(i == 0)
    def _():
        zero_scr[...] = jnp.zeros(zero_scr.shape, zero_scr.dtype)
        for e in range(N_EXPERTS):
            slab_copy(jnp.minimum(pad_ref[e], n_rows - TM_EXP)).start()
        for e in range(N_EXPERTS):
            slab_copy(0).wait()
        for k in range(N_EXPERTS):
            start = end_ref[0] + k * TM_EXP

            @pl.when(start < n_rows)
            def _():
                cp = slab_copy(start)
                cp.start()
                cp.wait()

    slot = i % 2

    def wait_tile(s):
        for _ in range(2):
            pltpu.make_async_copy(stage.at[s], token_rows(xs_ref, 0, tm), sems.at[s]).wait()

    def scatter_tile(f_ref):
        base = i * tm
        for k in range(d_blocks):
            stage[slot, pl.ds(k, tm, stride=d_blocks), :] = f_ref[:, k * LANES:(k + 1) * LANES]

        def row_copy(r, d):
            return pltpu.make_async_copy(token_rows(stage.at[slot], r, 1), token_rows(xs_ref, d, 1), sems.at[slot])

        for r in range(tm):
            row_copy(r, dest_ref[base + r]).start(priority=0)
            row_copy(r, dest_ref[n_tok + base + r]).start(priority=1)

    tile0 = 0
    for f_ref, n_t in zip(f_refs, seg_tiles):
        pl.when((i >= tile0) & (i < tile0 + n_t))(functools.partial(scatter_tile, f_ref))
        tile0 += n_t

    pl.when(i > 0)(lambda: wait_tile(1 - slot))
    pl.when(i == pl.num_programs(0) - 1)(lambda: wait_tile(slot))


def _scatter_rows(segments, dest_flat, pad_start, total_end):
    d = segments[0].shape[1]
    seg_tiles = tuple(s.shape[0] // TM_PROJ for s in segments)
    n_tok = sum(s.shape[0] for s in segments)
    r_max = 2 * n_tok + N_EXPERTS * TM_EXP
    in_specs = []
    tile0 = 0
    for n_t in seg_tiles:
        in_specs.append(pl.BlockSpec(
            (TM_PROJ, d), lambda i, *_, t0=tile0, nt=n_t: (jnp.clip(i - t0, 0, nt - 1), 0)))
        tile0 += n_t
    grid_spec = pltpu.PrefetchScalarGridSpec(
        num_scalar_prefetch=3,
        grid=(tile0,),
        in_specs=in_specs,
        out_specs=pl.BlockSpec(memory_space=pl.ANY),
        scratch_shapes=[pltpu.VMEM((TM_EXP * d // LANES, LANES), F32),
                        pltpu.VMEM((2, TM_PROJ * d // LANES, LANES), F32),
                        pltpu.SemaphoreType.DMA((3,))],
    )
    return pl.pallas_call(
        functools.partial(_scatter_rows_kernel, n_tok=n_tok, seg_tiles=seg_tiles),
        out_shape=jax.ShapeDtypeStruct((r_max * d // LANES, LANES), F32),
        grid_spec=grid_spec,
        compiler_params=_cparams(("arbitrary",)),
        name="moe_scatter_rows",
    )(dest_flat, pad_start, total_end, *segments)


def _experts_kernel(te_ref, nv_ref, x_ref, wg_ref, wu_ref, wd_ref, o_ref, wg_scr, wu_scr, wd_scr):
    i = pl.program_id(0)
    e = te_ref[i]
    prev = te_ref[jnp.maximum(i - 1, 0)]
    rows = 128

    @pl.when((i == 0) | (e != prev))
    def _():
        def body(r, carry):
            sl = pl.ds(pl.multiple_of(r * rows, rows), rows)
            wg_scr[sl, :] = wg_ref[0, 0, sl, :].astype(BF16)
            wu_scr[sl, :] = wu_ref[0, 0, sl, :].astype(BF16)
            wd_scr[sl, :] = wd_ref[0, 0, sl, :].astype(BF16)
            return carry
        lax.fori_loop(0, wg_scr.shape[0] // rows, body, 0)

    @pl.when(i < nv_ref[0])
    def _():
        d_blocks = wg_scr.shape[0] // LANES
        x = jnp.concatenate([x_ref[pl.ds(k, TM_EXP, stride=d_blocks), :].astype(BF16) for k in range(d_blocks)],
                            axis=1)
        g = _dot(x, wg_scr[...])
        u = _dot(x, wu_scr[...])
        a = (g * jax.nn.sigmoid(g)) * u
        o_ref[...] = _dot(a.astype(BF16), wd_scr[...]).astype(BF16)

    @pl.when(i >= nv_ref[0])
    def _():
        o_ref[...] = jnp.zeros(o_ref.shape, BF16)


def _experts(xs, tile_expert, n_valid, w_gate, w_up, w_down, layer):
    d, de = w_gate.shape[2], w_gate.shape[3]
    d_blocks = d // LANES
    r = xs.shape[0] // d_blocks
    n_tiles = r // TM_EXP
    x_map = lambda i, te, nv: (jnp.minimum(i, nv[0] - 1), 0)
    grid_spec = pltpu.PrefetchScalarGridSpec(
        num_scalar_prefetch=2,
        grid=(n_tiles,),
        in_specs=[
            pl.BlockSpec((TM_EXP * d_blocks, LANES), x_map),
            pl.BlockSpec((1, 1, d, de), lambda i, te, nv: (layer, te[i], 0, 0)),
            pl.BlockSpec((1, 1, d, de), lambda i, te, nv: (layer, te[i], 0, 0)),
            pl.BlockSpec((1, 1, de, d), lambda i, te, nv: (layer, te[i], 0, 0)),
        ],
        out_specs=pl.BlockSpec((TM_EXP, d), lambda i, te, nv: (i, 0)),
        scratch_shapes=[pltpu.VMEM((d, de), BF16), pltpu.VMEM((d, de), BF16), pltpu.VMEM((de, d), BF16)],
    )
    return pl.pallas_call(
        _experts_kernel,
        out_shape=jax.ShapeDtypeStruct((r, d), BF16),
        grid_spec=grid_spec,
        compiler_params=_cparams(("arbitrary",)),
        name="experts",
    )(tile_expert, n_valid, xs, w_gate, w_up, w_down)


def _dispatch(idx):
    n = idx.shape[1]
    e_flat = idx.reshape(-1)
    onehot = (e_flat[:, None] == jnp.arange(N_EXPERTS, dtype=jnp.int32)[None, :]).astype(jnp.int32)
    csum = jnp.cumsum(onehot, axis=0)
    rank = jnp.sum(csum * onehot, axis=1) - 1
    counts = csum[-1]
    padded = ((counts + TM_EXP - 1) // TM_EXP) * TM_EXP
    ends = jnp.cumsum(padded)
    starts = ends - padded
    dest = (jnp.sum(onehot * starts[None, :], axis=1) + rank).astype(jnp.int32)
    r_max = 2 * n + N_EXPERTS * TM_EXP
    tile_start = jnp.arange(r_max // TM_EXP, dtype=jnp.int32) * TM_EXP
    tile_expert = jnp.minimum(jnp.sum((tile_start[:, None] >= ends[None, :]).astype(jnp.int32), axis=1),
                              N_EXPERTS - 1).astype(jnp.int32)
    n_valid = (ends[-1] // TM_EXP).astype(jnp.int32).reshape(1)
    pad_start = (starts + counts).astype(jnp.int32)
    total_end = ends[-1].astype(jnp.int32).reshape(1)
    return dest, pad_start, total_end, tile_expert, n_valid


def _combine_kernel(h_ref, y1_ref, y2_ref, gate_ref, mod_ref, o_ref):
    d = D_MODEL
    g = gate_ref[...]
    y = g[:, 0:1] * y1_ref[...].astype(F32) + g[:, 1:2] * y2_ref[...].astype(F32)
    o_ref[...] = h_ref[...] + mod_ref[0, :, 5 * d:6 * d] * y


def _combine(h2d, y1, y2, gates_t, mod3, mod_map, row0):
    r, d = h2d.shape
    blk0 = row0 // TM_PROJ
    row = lambda i: (i, 0)
    seg = lambda i: (blk0 + i, 0)
    return pl.pallas_call(
        _combine_kernel,
        out_shape=jax.ShapeDtypeStruct((r, d), F32),
        grid=(r // TM_PROJ,),
        in_specs=[
            pl.BlockSpec((TM_PROJ, d), row),
            pl.BlockSpec((TM_PROJ, d), seg),
            pl.BlockSpec((TM_PROJ, d), seg),
            pl.BlockSpec((TM_PROJ, 2), seg),
            pl.BlockSpec((1, 1, N_MOD * d), mod_map),
        ],
        out_specs=pl.BlockSpec((TM_PROJ, d), row),
        compiler_params=_cparams(("parallel",)),
        name="moe_combine",
    )(h2d, y1, y2, gates_t, mod3)


def kernel(x, c, ctx, c_ctx, w_mod, b_mod, norm1_g, norm2_g, w_in, w_out, q_norm_g, k_norm_g, na_rpb,
           s5_lam_re, s5_lam_im, s5_log_dt, s5_b_re, s5_b_im, s5_c_re, s5_c_im, s5_d, s5_w_glu, s5_b_glu,
           router_w, router_bias, moe_w_gate, moe_w_up, moe_w_down):
    bsz, n_lat, d = x.shape
    n_ctx = ctx.shape[1]
    depth = w_mod.shape[0]
    ctx_row = bsz
    c_rows = jnp.concatenate([c.astype(F32), c_ctx.astype(F32)[None],
                              jnp.zeros((SUBLANES - bsz - 1, d), F32)], axis=0)
    mod_all = _modulation(c_rows, w_mod.astype(F32), b_mod.astype(F32))

    h_lat = x.reshape(bsz * n_lat, d).astype(F32)
    h_ctx = ctx.reshape(bsz * n_ctx, d).astype(F32)
    lat_map = _mod_row_map(n_lat, 0, True)
    ctx_map = _mod_row_map(n_ctx, ctx_row, False)
    bias_tabs = _na_bias_tables(na_rpb)
    router_wt = router_w.T.astype(F32)
    router_b = router_bias.reshape(N_EXPERTS, 1).astype(F32)
    s5_mats = _s5_matrices(s5_lam_re, s5_lam_im, s5_log_dt, s5_b_re, s5_b_im, s5_c_re, s5_c_im)

    n_l = bsz * n_lat
    pending = None
    for layer in range(depth):
        ctx_out = layer < depth - 1
        mod3 = mod_all[layer].reshape(SUBLANES, 1, N_MOD * d)
        w_in_b = w_in[layer].astype(BF16)
        proj = functools.partial(_in_projection, g=norm1_g[layer], mod3=mod3, w_bf16=w_in_b,
                                 qg=q_norm_g[layer], kg=k_norm_g[layer])
        if pending is None:
            qkvu_lat = proj(h_lat, mod_map=lat_map)
            qkvu_ctx = proj(h_ctx, mod_map=ctx_map)
        else:
            qkvu_lat, h_lat = proj(h_lat, mod_map=lat_map, pending=pending, row0=0)
            qkvu_ctx, h_ctx = proj(h_ctx, mod_map=ctx_map, pending=pending, row0=n_l)
        na_lat = _neighborhood_attention(qkvu_lat, qkvu_ctx, bias_tabs, layer, bsz)
        y_all = _s5_mixer(qkvu_lat, qkvu_ctx, s5_mats, layer, bsz)
        wglu_b = s5_w_glu[layer].astype(BF16)
        wout_b = w_out[layer].astype(BF16)
        h_lat, f_lat = _out_projection(na_lat, y_all, qkvu_lat, h_lat, mod3, s5_d[layer], wglu_b,
                                       s5_b_glu[layer], wout_b, norm2_g[layer],
                                       lat_map, n_lat, n_ctx // TM_PROJ)
        idx, gates = _router(f_lat, router_wt, router_b)
        if ctx_out:
            na_ctx = _context_attention(qkvu_ctx, bsz)
            h_ctx, f_ctx = _out_projection(na_ctx, y_all, qkvu_ctx, h_ctx, mod3, s5_d[layer], wglu_b,
                                           s5_b_glu[layer], wout_b, norm2_g[layer],
                                           ctx_map, n_ctx, 0)
            idx_c, gates_c = _router(f_ctx, router_wt, router_b)
            idx = jnp.concatenate([idx, idx_c], axis=1)
            gates = jnp.concatenate([gates, gates_c], axis=1)
        n_tok = idx.shape[1]
        dest, pad_start, total_end, tile_expert, n_valid = _dispatch(idx)
        xs = _scatter_rows([f_lat, f_ctx] if ctx_out else [f_lat], dest, pad_start, total_end)
        ys = _experts(xs, tile_expert, n_valid, moe_w_gate, moe_w_up, moe_w_down, layer)
        y1 = jnp.take(ys, dest[:n_tok], axis=0, mode="clip")
        y2 = jnp.take(ys, dest[n_tok:], axis=0, mode="clip")
        gates_t = gates.T
        if ctx_out:
            pending = (y1, y2, gates_t, mod3)
        else:
            h_lat = _combine(h_lat, y1, y2, gates_t, mod3, lat_map, 0)
    return h_lat.reshape(bsz, n_lat, d).astype(x.dtype)
```

```python
import functools
import math

import jax
import jax.numpy as jnp
from jax import lax
from jax.experimental import pallas as pl
from jax.experimental.pallas import tpu as pltpu

F32 = jnp.float32
BF16 = jnp.bfloat16
HIGHEST = lax.Precision.HIGHEST

D_MODEL = 1024
GRID_W = 64
HEAD_DIM = 64
NA_WIDTH = 512
S5_WIDTH = 512
S5_CH = 16
S5_GROUPS = 32
S5_STATE = 64
WIN_ROWS = 8
WIN_COLS = 16
N_EXPERTS = 16
N_GROUPS = 4
EPG = 4
N_MOD = 6
EPS = 1e-6

LANES = 128
SUBLANES = 8
VMEM_LIMIT = 56 * 1024 * 1024

TM_PROJ = 256
TM_IN = 512
Q_ROWS = 8
NA_ROW_TILES = 2
Q_COLS = 16
K_ROWS = 16
K_COLS = 32
CHUNK = 16
PAIRS_PER_STEP = 2
S5_PIECE = 32
TM_EXP = 512
TM_ROUTE = 1024
MASK_VALUE = -1e30
RPB_LANE0 = 48


def _cparams(sem):
    return pltpu.CompilerParams(dimension_semantics=sem, vmem_limit_bytes=VMEM_LIMIT)


def _dot(a, b):
    return jnp.dot(a, b, preferred_element_type=F32)


def _dot_nt(a, b):
    return lax.dot_general(a, b, (((1,), (1,)), ((), ())), preferred_element_type=F32)


def _mod_kernel(c_ref, w_ref, b_ref, o_ref):
    a = c_ref[...]
    a = a * jax.nn.sigmoid(a)
    o_ref[0] = jnp.dot(a, w_ref[0], precision=HIGHEST, preferred_element_type=F32) + b_ref[0]


def _modulation(c_rows, w_mod, b_mod):
    depth, d, n = w_mod.shape
    tn = 1536
    return pl.pallas_call(
        _mod_kernel,
        out_shape=jax.ShapeDtypeStruct((depth, SUBLANES, n), F32),
        grid=(depth, n // tn),
        in_specs=[
            pl.BlockSpec((SUBLANES, d), lambda l, j: (0, 0)),
            pl.BlockSpec((1, d, tn), lambda l, j: (l, 0, j)),
            pl.BlockSpec((1, 1, tn), lambda l, j: (l, 0, j)),
        ],
        out_specs=pl.BlockSpec((1, SUBLANES, tn), lambda l, j: (l, 0, j)),
        compiler_params=_cparams(("arbitrary", "arbitrary")),
        name="modulation",
    )(c_rows, w_mod, b_mod.reshape(depth, 1, n))


def _inproj_kernel(x_ref, g_ref, mod_ref, w_ref, qg_ref, kg_ref, *rest, moe_pending):
    x = x_ref[...]
    if moe_pending:
        y1_ref, y2_ref, gate_ref, modp_ref, o_ref, xo_ref = rest
        gw = gate_ref[...]
        y = gw[:, 0:1] * y1_ref[...].astype(F32) + gw[:, 1:2] * y2_ref[...].astype(F32)
        x = x + modp_ref[0, :, 5 * D_MODEL:6 * D_MODEL] * y
        xo_ref[...] = x
    else:
        (o_ref,) = rest
    ms = jnp.mean(x * x, axis=-1, keepdims=True)
    y = x * lax.rsqrt(ms + EPS) * g_ref[...]
    shift = mod_ref[0, :, 0:D_MODEL]
    scale = mod_ref[0, :, D_MODEL:2 * D_MODEL]
    a = y * (1.0 + scale) + shift
    acc = _dot(a.astype(BF16), w_ref[...])
    lo = lax.broadcasted_iota(jnp.int32, (1, LANES), 1) < HEAD_DIM
    n_pairs = NA_WIDTH // LANES
    for blk in range(2 * n_pairs):
        cols = slice(blk * LANES, (blk + 1) * LANES)
        if blk < n_pairs:
            o_ref[:, cols] = _pair_rms(acc[:, cols], qg_ref[...], lo) * (HEAD_DIM ** -0.5)
        else:
            o_ref[:, cols] = _pair_rms(acc[:, cols], kg_ref[...], lo)
    o_ref[:, 2 * NA_WIDTH:] = acc[:, 2 * NA_WIDTH:]


def _mod_row_map(rows_per_batch, mod_row0, per_batch, tm=TM_PROJ):
    tiles_per_batch = rows_per_batch // tm
    if per_batch:
        return lambda i: (mod_row0 + i // tiles_per_batch, 0, 0)
    return lambda i: (mod_row0, 0, 0)


def _in_projection(x2d, g, mod3, w_bf16, mod_map, qg, kg, pending=None, row0=0, tm=TM_PROJ):
    r, d = x2d.shape
    n = w_bf16.shape[1]
    g2 = lambda v: jnp.concatenate([v, v]).reshape(1, LANES).astype(F32)
    row = lambda i: (i, 0)
    in_specs = [
        pl.BlockSpec((tm, d), row),
        pl.BlockSpec((1, d), lambda i: (0, 0)),
        pl.BlockSpec((1, 1, N_MOD * d), mod_map),
        pl.BlockSpec((d, n), lambda i: (0, 0)),
        pl.BlockSpec((1, LANES), lambda i: (0, 0)),
        pl.BlockSpec((1, LANES), lambda i: (0, 0)),
    ]
    args = [x2d, g.reshape(1, d), mod3, w_bf16, g2(qg), g2(kg)]
    out_shape = jax.ShapeDtypeStruct((r, n), F32)
    out_specs = pl.BlockSpec((tm, n), row)
    if pending is not None:
        blk0 = row0 // tm
        seg = lambda i: (blk0 + i, 0)
        in_specs += [pl.BlockSpec((tm, d), seg), pl.BlockSpec((tm, d), seg),
                     pl.BlockSpec((tm, 2), seg), pl.BlockSpec((1, 1, N_MOD * d), mod_map)]
        args += list(pending)
        out_shape = (out_shape, jax.ShapeDtypeStruct((r, d), F32))
        out_specs = (out_specs, pl.BlockSpec((tm, d), row))
    return pl.pallas_call(
        functools.partial(_inproj_kernel, moe_pending=pending is not None),
        out_shape=out_shape,
        grid=(r // tm,),
        in_specs=in_specs,
        out_specs=out_specs,
        compiler_params=_cparams(("parallel",)),
        name="in_projection",
    )(*args)


def _pair_rms(x, g, lo):
    ss = x * x
    sa = jnp.sum(jnp.where(lo, ss, 0.0), axis=-1, keepdims=True)
    sb = jnp.sum(jnp.where(lo, 0.0, ss), axis=-1, keepdims=True)
    ms = jnp.where(lo, sa, sb) * (1.0 / HEAD_DIM)
    return x * lax.rsqrt(ms + EPS) * g


def _na_kernel(q_ref, k_ref, v_ref, kc_ref, vc_ref, bias_ref, o_ref):
    lo = lax.broadcasted_iota(jnp.int32, (1, LANES), 1) < HEAD_DIM
    n_rows = k_ref.shape[1]
    col_tiles = GRID_W // Q_COLS
    nq = Q_ROWS * Q_COLS
    nk = K_ROWS * K_COLS
    kcb = kc_ref[0].astype(BF16)
    vcb = vc_ref[0].astype(BF16)
    tiles = [(rr, j) for rr in range(NA_ROW_TILES) for j in range(col_tiles)]
    q2 = []
    for rr, j in tiles:
        qn = q_ref[0, rr * Q_ROWS:(rr + 1) * Q_ROWS, j * Q_COLS:(j + 1) * Q_COLS, :].reshape(nq, LANES)
        q2.append(jnp.concatenate([jnp.where(lo, qn, 0.0), jnp.where(lo, 0.0, qn)], axis=0).astype(BF16))
    s_cx_all = _dot_nt(jnp.concatenate(q2, axis=0), kcb)
    p_cx, o_nb, denom = [], [], []
    for t, (rr, j) in enumerate(tiles):
        i = pl.program_id(2) * NA_ROW_TILES + rr
        kr0 = jnp.clip(Q_ROWS * i - WIN_ROWS // 2, 0, n_rows - K_ROWS)
        rt = jnp.where(i == 0, 0, jnp.where(i == n_rows // Q_ROWS - 1, 2, 1))
        kc0 = min(max(Q_COLS * j - WIN_COLS // 2, 0), GRID_W - K_COLS)
        ct = 0 if j == 0 else (2 if j == col_tiles - 1 else 1)
        kw = k_ref[0, pl.ds(kr0, K_ROWS), kc0:kc0 + K_COLS, :].reshape(nk, LANES).astype(BF16)
        vw = v_ref[0, pl.ds(kr0, K_ROWS), kc0:kc0 + K_COLS, :].reshape(nk, LANES).astype(BF16)
        s_nb = _dot_nt(q2[t], kw) + bias_ref[0, 0, rt * 3 + ct].reshape(2 * nq, nk)
        s_cx = s_cx_all[t * 2 * nq:(t + 1) * 2 * nq]
        m = jnp.maximum(jnp.max(s_nb, axis=-1, keepdims=True), jnp.max(s_cx, axis=-1, keepdims=True))
        p_nb = jnp.exp(s_nb - m)
        p_c = jnp.exp(s_cx - m)
        denom.append(jnp.sum(p_nb, axis=-1, keepdims=True) + jnp.sum(p_c, axis=-1, keepdims=True))
        p_cx.append(p_c.astype(BF16))
        o_nb.append(_dot(p_nb.astype(BF16), vw))
    o_cx_all = _dot(jnp.concatenate(p_cx, axis=0), vcb)
    for t, (rr, j) in enumerate(tiles):
        o2 = (o_nb[t] + o_cx_all[t * 2 * nq:(t + 1) * 2 * nq]) / denom[t]
        o_ref[0, rr * Q_ROWS:(rr + 1) * Q_ROWS, j * Q_COLS:(j + 1) * Q_COLS, :] = \
            jnp.where(lo, o2[0:nq], o2[nq:2 * nq]).reshape(Q_ROWS, Q_COLS, LANES)


def _bias_table_kernel(rpb_ref, o_ref, tt_scr):
    n_off_r = 2 * WIN_ROWS - 1
    lane = lax.broadcasted_iota(jnp.int32, (Q_COLS, LANES), 1)
    qc = lax.broadcasted_iota(jnp.int32, (Q_COLS, LANES), 0)
    kc = lane % K_COLS
    lane_blk = lane // K_COLS
    per_vreg = LANES // K_COLS
    col_rel = (0, -WIN_COLS // 2, -WIN_COLS)
    col_origin = (0, Q_COLS, GRID_W - Q_COLS)
    row_rel = (0, -WIN_ROWS // 2, -WIN_ROWS)
    row_origin = (0, Q_ROWS, GRID_W - Q_ROWS)
    masked = jnp.full((Q_COLS, LANES), MASK_VALUE, F32)

    for ct in range(3):
        c_abs = col_origin[ct] + qc
        k_abs = col_origin[ct] + col_rel[ct] + kc
        start = jnp.clip(c_abs - WIN_COLS // 2, 0, GRID_W - WIN_COLS)
        valid_c = (k_abs >= start) & (k_abs < start + WIN_COLS)
        base = (1 - WIN_COLS - col_rel[ct] - RPB_LANE0) % LANES
        for ro in range(n_off_r):
            row = jnp.broadcast_to(rpb_ref[0, 0, ro:ro + 1, :], (Q_COLS, LANES))
            t = pltpu.roll(row, base, 1, stride=1, stride_axis=0)
            rep = t
            for m in range(1, per_vreg):
                rep = jnp.where(lane_blk == m, pltpu.roll(t, K_COLS * m, 1), rep)
            tt_scr[ct, ro] = jnp.where(valid_c, rep, MASK_VALUE)

    for rt in range(3):
        for ct in range(3):
            for qr in range(Q_ROWS):
                r_abs = row_origin[rt] + qr
                r_start = min(max(r_abs - WIN_ROWS // 2, 0), GRID_W - WIN_ROWS)
                for w in range(K_ROWS // per_vreg):
                    val = None
                    for m in range(per_vreg):
                        k_abs = row_origin[rt] + row_rel[rt] + per_vreg * w + m
                        ok = r_start <= k_abs < r_start + WIN_ROWS
                        src = tt_scr[ct, k_abs - r_abs + WIN_ROWS - 1] if ok else masked
                        val = src if val is None else jnp.where(lane_blk == m, src, val)
                    o_ref[0, 0, rt * 3 + ct, 0, qr * Q_COLS:(qr + 1) * Q_COLS, w * LANES:(w + 1) * LANES] = val


def _na_bias_tables(na_rpb):
    depth, h, n_r, n_c = na_rpb.shape
    rpb_pad = jnp.pad(na_rpb.astype(F32), ((0, 0), (0, 0), (0, 2 * SUBLANES - n_r),
                                            (RPB_LANE0, LANES - RPB_LANE0 - n_c)))
    nq, nk = Q_ROWS * Q_COLS, K_ROWS * K_COLS
    return pl.pallas_call(
        _bias_table_kernel,
        out_shape=jax.ShapeDtypeStruct((depth, h // 2, 9, 2, nq, nk), F32),
        grid=(depth, h),
        in_specs=[pl.BlockSpec((1, 1, 2 * SUBLANES, LANES), lambda l, i: (l, i, 0, 0))],
        out_specs=pl.BlockSpec((1, 1, 9, 1, nq, nk), lambda l, i: (l, i // 2, 0, i % 2, 0, 0)),
        scratch_shapes=[pltpu.VMEM((3, 2 * WIN_ROWS - 1, Q_COLS, LANES), F32)],
        compiler_params=_cparams(("parallel", "parallel")),
        name="na_bias_tables",
    )(rpb_pad)


def _neighborhood_attention(qkvu_lat, qkvu_ctx, bias_tabs, layer, bsz):
    n_lat = qkvu_lat.shape[0] // bsz
    n_ctx = qkvu_ctx.shape[0] // bsz
    rows = n_lat // GRID_W
    n_cols = qkvu_lat.shape[1]
    lat4 = qkvu_lat.reshape(bsz, rows, GRID_W, n_cols)
    ctx3 = qkvu_ctx.reshape(bsz, n_ctx, n_cols)
    n_pairs = NA_WIDTH // LANES
    step_rows = NA_ROW_TILES * Q_ROWS
    n_tiles = rows // step_rows
    out = pl.pallas_call(
        _na_kernel,
        out_shape=jax.ShapeDtypeStruct((bsz, rows, GRID_W, NA_WIDTH), F32),
        grid=(n_pairs, bsz, n_tiles),
        in_specs=[
            pl.BlockSpec((1, step_rows, GRID_W, LANES), lambda p, b, t: (b, t, 0, p)),
            pl.BlockSpec((1, rows, GRID_W, LANES), lambda p, b, t: (b, 0, 0, n_pairs + p)),
            pl.BlockSpec((1, rows, GRID_W, LANES), lambda p, b, t: (b, 0, 0, 2 * n_pairs + p)),
            pl.BlockSpec((1, n_ctx, LANES), lambda p, b, t: (b, 0, n_pairs + p)),
            pl.BlockSpec((1, n_ctx, LANES), lambda p, b, t: (b, 0, 2 * n_pairs + p)),
            pl.BlockSpec((1, 1, 9, 2, Q_ROWS * Q_COLS, K_ROWS * K_COLS), lambda p, b, t: (layer, p, 0, 0, 0, 0)),
        ],
        out_specs=pl.BlockSpec((1, step_rows, GRID_W, LANES), lambda p, b, t: (b, t, 0, p)),
        compiler_params=_cparams(("parallel", "parallel", "parallel")),
        name="neighborhood_attention",
    )(lat4, lat4, lat4, ctx3, ctx3, bias_tabs)
    return out.reshape(bsz * n_lat, NA_WIDTH)


def _ctx_attn_kernel(q_ref, k_ref, v_ref, o_ref):
    lo = lax.broadcasted_iota(jnp.int32, (1, LANES), 1) < HEAD_DIM
    qn = q_ref[0]
    kn = k_ref[0].astype(BF16)
    vb = v_ref[0].astype(BF16)

    def one(qm):
        s = _dot_nt(qm, kn)
        m = jnp.max(s, axis=-1, keepdims=True)
        p = jnp.exp(s - m)
        l = jnp.sum(p, axis=-1, keepdims=True)
        return _dot(p.astype(BF16), vb) / l

    o_a = one(jnp.where(lo, qn, 0.0).astype(BF16))
    o_b = one(jnp.where(lo, 0.0, qn).astype(BF16))
    o_ref[0] = jnp.where(lo, o_a, o_b)


def _context_attention(qkvu_ctx, bsz):
    n_ctx = qkvu_ctx.shape[0] // bsz
    ctx3 = qkvu_ctx.reshape(bsz, n_ctx, qkvu_ctx.shape[1])
    n_pairs = NA_WIDTH // LANES
    out = pl.pallas_call(
        _ctx_attn_kernel,
        out_shape=jax.ShapeDtypeStruct((bsz, n_ctx, NA_WIDTH), F32),
        grid=(bsz, n_pairs),
        in_specs=[
            pl.BlockSpec((1, n_ctx, LANES), lambda b, p: (b, 0, p)),
            pl.BlockSpec((1, n_ctx, LANES), lambda b, p: (b, 0, n_pairs + p)),
            pl.BlockSpec((1, n_ctx, LANES), lambda b, p: (b, 0, 2 * n_pairs + p)),
        ],
        out_specs=pl.BlockSpec((1, n_ctx, LANES), lambda b, p: (b, 0, p)),
        compiler_params=_cparams(("parallel", "parallel")),
        name="context_attention",
    )(ctx3, ctx3, ctx3)
    return out.reshape(bsz * n_ctx, NA_WIDTH)


def _s5_kernel(uc_ref, ul_ref, w_ref, m_ref, v_ref, a_ref, y_ref, x_scr, s_scr, hf_scr, hr_scr, *, n_ctx_chunks):
    bsz = ul_ref.shape[0]
    n_lat_chunks = ul_ref.shape[1] // CHUNK
    n_chunks = n_ctx_chunks + n_lat_chunks
    rows = n_chunks * SUBLANES
    n_pairs = w_ref.shape[1]
    gpb = 2 * n_pairs
    half = 2 * LANES
    tile_chunks = TM_PROJ // CHUNK
    pieces = [(uc_ref, 0, n_ctx_chunks, 0)] if n_ctx_chunks else []
    pieces += [(ul_ref, c0, min(S5_PIECE, n_lat_chunks - c0), n_ctx_chunks + c0)
               for c0 in range(0, n_lat_chunks, S5_PIECE)]

    def lane_block_ids(n):
        return lax.broadcasted_iota(jnp.int32, (n, LANES), 1) // S5_CH

    for src_ref, c_src, n_c, c_dst in pieces:
        lane_blk = lane_block_ids(n_c)
        for b in range(bsz):
            for q in range(2):
                rolled = []
                for j in range(SUBLANES):
                    s = SUBLANES * q + j
                    u_s = src_ref[b, pl.ds(c_src * CHUNK + s, n_c, stride=CHUNK), :]
                    rolled.append(u_s if j == 0 else pltpu.roll(u_s, j * S5_CH, 1))
                for g in range(gpb):
                    xg = rolled[0]
                    for j in range(1, SUBLANES):
                        xg = jnp.where(lane_blk == (g + j) % SUBLANES, rolled[j], xg)
                    x_scr[g // 2, q, pl.ds(c_dst * SUBLANES + (g % 2) * bsz + b, n_c, stride=SUBLANES), :] = xg

    n_blk = 8
    rb = rows // n_blk
    first_group = (lax.broadcasted_iota(jnp.int32, (rb, half), 0) & (SUBLANES // 2)) == 0
    fwd_cols = (lax.broadcasted_iota(jnp.int32, (rb, half), 1) & (LANES - 1)) < S5_STATE
    is_fwd = lax.broadcasted_iota(jnp.int32, (SUBLANES, LANES), 1) < S5_STATE
    first_rows = lax.broadcasted_iota(jnp.int32, (SUBLANES, half), 0) < SUBLANES // 2
    zero = jnp.zeros((SUBLANES, LANES), F32)

    def x_rows(p, sl):
        return jnp.concatenate([x_scr[p, 0, sl, :], x_scr[p, 1, sl, :]], axis=1).astype(BF16)

    def put_cols(scr, i, sl, val):
        scr[i, 0, sl, :] = val[:, 0:LANES]
        scr[i, 1, sl, :] = val[:, LANES:half]

    for hh in range(n_pairs // PAIRS_PER_STEP):
        pairs = [hh * PAIRS_PER_STEP + i for i in range(PAIRS_PER_STEP)]
        for i, p in enumerate(pairs):
            for blk in range(n_blk):
                sl = slice(blk * rb, (blk + 1) * rb)
                r = _dot(x_rows(p, sl), w_ref[0, p])
                put_cols(s_scr, i, sl, jnp.where(first_group, r[:, :half], r[:, half:]))

        a_pair = [jnp.where(first_rows, a_ref[0, 2 * p], a_ref[0, 2 * p + 1]) for p in pairs]
        a_re = [a[:, 0:LANES] for a in a_pair]
        a_im = [a[:, LANES:half] for a in a_pair]

        def body(k, carry):
            kr = jnp.where(k < n_ctx_chunks, n_ctx_chunks - 1 - k, n_chunks + n_ctx_chunks - 1 - k)
            rf = pl.ds(pl.multiple_of(k * SUBLANES, SUBLANES), SUBLANES)
            rr = pl.ds(pl.multiple_of(kr * SUBLANES, SUBLANES), SUBLANES)
            new = []
            for i in range(PAIRS_PER_STEP):
                h_re, h_im = carry[2 * i], carry[2 * i + 1]
                hf_scr[i, 0, rf, :] = h_re
                hf_scr[i, 1, rf, :] = h_im
                hr_scr[i, 0, rr, :] = h_re
                hr_scr[i, 1, rr, :] = h_im
                s_re = jnp.where(is_fwd, s_scr[i, 0, rf, :], s_scr[i, 0, rr, :])
                s_im = jnp.where(is_fwd, s_scr[i, 1, rf, :], s_scr[i, 1, rr, :])
                new.append(a_re[i] * h_re - a_im[i] * h_im + s_re)
                new.append(a_re[i] * h_im + a_im[i] * h_re + s_im)
            return tuple(new)

        lax.fori_loop(0, n_chunks, body, (zero,) * (2 * PAIRS_PER_STEP))

        for i, p in enumerate(pairs):
            for blk in range(n_blk):
                sl = slice(blk * rb, (blk + 1) * rb)
                hf = jnp.concatenate([hf_scr[i, 0, sl, :], hf_scr[i, 1, sl, :]], axis=1)
                hr = jnp.concatenate([hr_scr[i, 0, sl, :], hr_scr[i, 1, sl, :]], axis=1)
                h_in = jnp.where(fwd_cols, hf, hr).astype(BF16)
                r = _dot(x_rows(p, sl), m_ref[0, p]) + _dot_nt(h_in, v_ref[0, p])
                put_cols(s_scr, i, sl, jnp.where(first_group, r[:, :half], r[:, half:]))

        g_lo = 2 * pairs[0]
        n_g = 2 * PAIRS_PER_STEP
        out_blk = lax.broadcasted_iota(jnp.int32, (tile_chunks, LANES), 1) // S5_CH
        lanes_out = (out_blk >= g_lo) & (out_blk < g_lo + n_g)
        for _, _, n_c, c_dst in pieces:
            lane_blk = lane_block_ids(n_c)
            for b in range(bsz):
                for q in range(2):
                    y_g = [s_scr[(g - g_lo) // 2, q,
                                 pl.ds(c_dst * SUBLANES + (g % 2) * bsz + b, n_c, stride=SUBLANES), :]
                           for g in range(g_lo, g_lo + n_g)]
                    for j in range(SUBLANES):
                        z = y_g[0]
                        for gi in range(1, n_g):
                            z = jnp.where(lane_blk == (g_lo + gi + j) % SUBLANES, y_g[gi], z)
                        if j:
                            z = pltpu.roll(z, LANES - j * S5_CH, 1)
                        t = SUBLANES * q + j
                        for ct in range(n_c // tile_chunks):
                            tile = (c_dst // tile_chunks + ct - n_ctx_chunks // tile_chunks) % (n_chunks // tile_chunks)
                            r0 = tile * TM_PROJ + t * tile_chunks
                            pltpu.store(y_ref.at[b, 0, r0:r0 + tile_chunks, :],
                                        z[ct * tile_chunks:(ct + 1) * tile_chunks, :], mask=lanes_out)


def _s5_scan(qkvu_ctx3, qkvu_lat3, mats, layer):
    w_c, m_c, vt_c, a_c = mats
    bsz, n_ctx, _ = qkvu_ctx3.shape
    n_lat = qkvu_lat3.shape[1]
    n_seq = n_ctx + n_lat
    rows = n_seq // CHUNK * SUBLANES
    n_blocks = S5_WIDTH // LANES
    ppb = w_c.shape[1] // n_blocks
    u_blk0 = 3 * NA_WIDTH // LANES
    wspec = pl.BlockSpec((1, ppb, 2 * LANES, 4 * LANES), lambda i: (layer, i, 0, 0))
    one = pl.Buffered(1)
    state = pltpu.VMEM((PAIRS_PER_STEP, 2, rows, LANES), F32)
    return pl.pallas_call(
        functools.partial(_s5_kernel, n_ctx_chunks=n_ctx // CHUNK),
        out_shape=jax.ShapeDtypeStruct((bsz, n_blocks, n_seq, LANES), F32),
        grid=(n_blocks,),
        in_specs=[
            pl.BlockSpec((bsz, n_ctx, LANES), lambda i: (0, 0, u_blk0 + i)),
            pl.BlockSpec((bsz, n_lat, LANES), lambda i: (0, 0, u_blk0 + i), pipeline_mode=one),
            wspec, wspec,
            pl.BlockSpec((1, ppb, 4 * LANES, 2 * LANES), lambda i: (layer, i, 0, 0)),
            pl.BlockSpec((1, 2 * ppb, SUBLANES, 2 * LANES), lambda i: (layer, i, 0, 0)),
        ],
        out_specs=pl.BlockSpec((bsz, 1, n_seq, LANES), lambda i: (0, i, 0, 0), pipeline_mode=one),
        scratch_shapes=[pltpu.VMEM((ppb, 2, rows, LANES), F32), state, state, state],
        compiler_params=_cparams(("parallel",)),
        name="s5_scan",
    )(qkvu_ctx3, qkvu_lat3, w_c, m_c, vt_c, a_c)


def _s5_mats_kernel(prm_ref, btr_ref, bti_ref, cr_ref, ci_ref, w_ref, m_ref, vt_ref, a_ref):
    t = CHUNK
    gl = pl.program_id(1) % SUBLANES
    is_fwd = lax.broadcasted_iota(jnp.int32, (1, LANES), 1) < S5_STATE
    lr = prm_ref[0, 0, 0:1, :]
    li = prm_ref[0, 0, 1:2, :]
    dt = jnp.exp(prm_ref[0, 0, 2:3, :])
    n = lax.broadcasted_iota(jnp.int32, (3 * SUBLANES, LANES), 0).astype(F32)
    pmag = jnp.exp(n * (lr * dt))
    pw_re = pmag * jnp.cos(n * (li * dt))
    pw_im = pmag * jnp.sin(n * (li * dt))
    ab_re, ab_im = pw_re[1:2, :], pw_im[1:2, :]
    den = lr * lr + li * li
    nr = ab_re - 1.0
    z_re = (nr * lr + ab_im * li) / den
    z_im = (ab_im * lr - nr * li) / den
    bt_re, bt_im = btr_ref[0, 0], bti_ref[0, 0]
    bb_re = z_re * bt_re - z_im * bt_im
    bb_im = z_re * bt_im + z_im * bt_re
    c_re, c_im = cr_ref[0, 0], ci_ref[0, 0]

    def powers(n_fwd, n_rev):
        return (jnp.where(is_fwd, pw_re[n_fwd:n_fwd + 1, :], pw_re[n_rev:n_rev + 1, :]),
                jnp.where(is_fwd, pw_im[n_fwd:n_fwd + 1, :], pw_im[n_rev:n_rev + 1, :]))

    def block_rows(s):
        pos = SUBLANES * (s // SUBLANES) + (s % SUBLANES + gl) % SUBLANES
        return pl.ds(pl.multiple_of(pos * S5_CH, S5_CH), S5_CH)

    for s in range(t):
        rows = block_rows(s)
        p_re, p_im = powers(t - 1 - s, s)
        w_ref[0, 0, rows, 0:LANES] = (bb_re * p_re - bb_im * p_im).astype(BF16)
        w_ref[0, 0, rows, LANES:2 * LANES] = (bb_re * p_im + bb_im * p_re).astype(BF16)
        q_re, q_im = powers(s + 1, t - s)
        vt_ref[0, 0, rows, 0:LANES] = (c_re * q_re - c_im * q_im).astype(BF16)
        vt_ref[0, 0, rows, LANES:2 * LANES] = (-(c_re * q_im + c_im * q_re)).astype(BF16)

    ca_re, ca_im = [], []
    for lag in range(t):
        p_re, p_im = powers(lag, t - 1 - lag)
        ca_re.append(c_re * p_re - c_im * p_im)
        ca_im.append(c_re * p_im + c_im * p_re)
    stack = jnp.concatenate([jnp.concatenate(ca_re, axis=0), jnp.concatenate(ca_im, axis=0)], axis=1)
    zero = jnp.zeros_like(bb_re)
    lhs = jnp.concatenate([
        jnp.concatenate([jnp.where(is_fwd, bb_re, zero), jnp.where(is_fwd, -bb_im, zero)], axis=1),
        jnp.concatenate([jnp.where(is_fwd, zero, bb_re), jnp.where(is_fwd, zero, -bb_im)], axis=1)], axis=0)
    kt = lax.dot_general(lhs, stack, (((1,), (1,)), ((), ())), precision=HIGHEST, preferred_element_type=F32)
    kt_f, kt_r = kt[0:S5_CH], kt[S5_CH:2 * S5_CH]
    blk = lax.broadcasted_iota(jnp.int32, (S5_CH, 2 * LANES), 1) // S5_CH
    for s in range(t):
        strip = (jnp.where(blk >= s, pltpu.roll(kt_f, S5_CH * s, 1), 0.0)
                 + jnp.where(blk <= s, pltpu.roll(kt_r, (S5_CH * (s - t + 1)) % (2 * LANES), 1), 0.0))
        strip = jnp.concatenate([pltpu.roll(strip[:, 0:LANES], gl * S5_CH, 1),
                                 pltpu.roll(strip[:, LANES:2 * LANES], gl * S5_CH, 1)], axis=1)
        m_ref[0, 0, block_rows(s), :] = strip.astype(BF16)

    a_ref[0, 0, :, 0:LANES] = jnp.broadcast_to(pw_re[t:t + 1, :], (SUBLANES, LANES))
    a_ref[0, 0, :, LANES:2 * LANES] = jnp.broadcast_to(pw_im[t:t + 1, :], (SUBLANES, LANES))


def _s5_matrices(lam_re, lam_im, log_dt, b_re, b_im, c_re, c_im):
    depth, _, g, p = lam_re.shape
    hc = b_re.shape[-1]
    width = CHUNK * hc
    both = lambda x: jnp.transpose(x.astype(F32), (0, 2, 1, 3)).reshape(depth, g, 1, 2 * p)
    dt_rows = jnp.broadcast_to(jnp.transpose(log_dt.astype(F32), (0, 2, 1))[..., None], (depth, g, 2, p))
    prm = jnp.concatenate([both(lam_re), both(lam_im), dt_rows.reshape(depth, g, 1, 2 * p),
                           jnp.zeros((depth, g, SUBLANES - 3, 2 * p), F32)], axis=2)
    bt = lambda x: jnp.transpose(x.astype(F32), (0, 2, 4, 1, 3)).reshape(depth, g, hc, 2 * p)
    ct = lambda x: jnp.transpose(x.astype(F32), (0, 2, 3, 1, 4)).reshape(depth, g, hc, 2 * p)
    vec = lambda rows: pl.BlockSpec((1, 1, rows, 2 * p), lambda l, i: (l, i, 0, 0))
    return pl.pallas_call(
        _s5_mats_kernel,
        out_shape=(jax.ShapeDtypeStruct((depth, g // 2, width, 2 * width), BF16),
                   jax.ShapeDtypeStruct((depth, g // 2, width, 2 * width), BF16),
                   jax.ShapeDtypeStruct((depth, g // 2, 2 * width, width), BF16),
                   jax.ShapeDtypeStruct((depth, g, SUBLANES, 4 * p), F32)),
        grid=(depth, g),
        in_specs=[vec(SUBLANES), vec(hc), vec(hc), vec(hc), vec(hc)],
        out_specs=(pl.BlockSpec((1, 1, width, width), lambda l, i: (l, i // 2, 0, i % 2)),
                   pl.BlockSpec((1, 1, width, width), lambda l, i: (l, i // 2, 0, i % 2)),
                   pl.BlockSpec((1, 1, width, width), lambda l, i: (l, i // 2, i % 2, 0)),
                   pl.BlockSpec((1, 1, SUBLANES, 4 * p), lambda l, i: (l, i, 0, 0))),
        compiler_params=_cparams(("parallel", "parallel")),
        name="s5_matrices",
    )(prm, bt(b_re), bt(b_im), ct(c_re), ct(c_im))


def _s5_mixer(qkvu_lat, qkvu_ctx, mats, layer, bsz):
    n_cols = qkvu_lat.shape[1]
    return _s5_scan(qkvu_ctx.reshape(bsz, -1, n_cols), qkvu_lat.reshape(bsz, -1, n_cols), mats, layer)


def _outproj_kernel(na_ref, y_ref, u_ref, h_ref, mod_ref, d_ref, wglu_ref, bglu_ref, wout_ref, g2_ref,
                    ho_ref, f_ref):
    tile_chunks = TM_PROJ // CHUNK
    y = jnp.concatenate(
        [jnp.concatenate([y_ref[0, blk, pl.ds(sub * TM_PROJ + c, CHUNK, stride=tile_chunks), :]
                          for sub in range(y_ref.shape[2] // TM_PROJ) for c in range(tile_chunks)], axis=0)
         for blk in range(S5_WIDTH // LANES)], axis=1)
    z = jax.nn.gelu(y + d_ref[...] * u_ref[...])
    s5 = z * jax.nn.sigmoid(_dot(z.astype(BF16), wglu_ref[...]) + bglu_ref[...])
    mix = (_dot(na_ref[...].astype(BF16), wout_ref[0:NA_WIDTH, :])
           + _dot(s5.astype(BF16), wout_ref[NA_WIDTH:NA_WIDTH + S5_WIDTH, :]))
    d = D_MODEL
    gate = mod_ref[0, :, 2 * d:3 * d]
    h = h_ref[...] + gate * mix
    ho_ref[...] = h
    ms = jnp.mean(h * h, axis=-1, keepdims=True)
    y2 = h * lax.rsqrt(ms + EPS) * g2_ref[...]
    f_ref[...] = y2 * (1.0 + mod_ref[0, :, 4 * d:5 * d]) + mod_ref[0, :, 3 * d:4 * d]


def _out_projection(na, y_all, qkvu, h2d, mod3, d_skip, wglu_bf16, b_glu, wout_bf16, g2,
                    mod_map, rows_per_batch, y_row0, tm=TM_PROJ):
    r, d = h2d.shape
    tiles_per_batch = rows_per_batch // tm
    u_blk = 3 * NA_WIDTH // S5_WIDTH
    y_block0 = y_row0 // tm

    def y_map(i):
        return (i // tiles_per_batch, 0, y_block0 + i % tiles_per_batch, 0)

    const = lambda i: (0, 0)
    return pl.pallas_call(
        _outproj_kernel,
        out_shape=(jax.ShapeDtypeStruct((r, d), F32), jax.ShapeDtypeStruct((r, d), F32)),
        grid=(r // tm,),
        in_specs=[
            pl.BlockSpec((tm, NA_WIDTH), lambda i: (i, 0)),
            pl.BlockSpec((1, S5_WIDTH // LANES, tm, LANES), y_map),
            pl.BlockSpec((tm, S5_WIDTH), lambda i: (i, u_blk)),
            pl.BlockSpec((tm, d), lambda i: (i, 0)),
            pl.BlockSpec((1, 1, N_MOD * d), mod_map),
            pl.BlockSpec((1, S5_WIDTH), const),
            pl.BlockSpec((S5_WIDTH, S5_WIDTH), const),
            pl.BlockSpec((1, S5_WIDTH), const),
            pl.BlockSpec((NA_WIDTH + S5_WIDTH, d), const),
            pl.BlockSpec((1, d), const),
        ],
        out_specs=(pl.BlockSpec((tm, d), lambda i: (i, 0)), pl.BlockSpec((tm, d), lambda i: (i, 0))),
        compiler_params=_cparams(("parallel",)),
        name="out_projection",
    )(na, y_all, qkvu, h2d, mod3, d_skip.reshape(1, -1), wglu_bf16, b_glu.reshape(1, -1), wout_bf16,
      g2.reshape(1, d))


def _top2(vals):
    best = vals[0]
    bi = jnp.zeros(best.shape, jnp.int32)
    for i in range(1, len(vals)):
        gt = vals[i] > best
        best = jnp.where(gt, vals[i], best)
        bi = jnp.where(gt, i, bi)
    second = jnp.full(best.shape, -jnp.inf, F32)
    si = jnp.zeros(best.shape, jnp.int32)
    for i in range(len(vals)):
        cand = jnp.where(bi == i, -jnp.inf, vals[i])
        gt = cand > second
        second = jnp.where(gt, cand, second)
        si = jnp.where(gt, i, si)
    return best, bi, second, si


def _route(f, rwt, rb):
    logits = lax.dot_general(rwt, f, (((1,), (1,)), ((), ())),
                             precision=HIGHEST, preferred_element_type=F32)
    m = jnp.max(logits, axis=0, keepdims=True)
    e = jnp.exp(logits - m)
    probs = e / jnp.sum(e, axis=0, keepdims=True)
    sel = probs + rb
    sel_rows = [sel[i:i + 1, :] for i in range(N_EXPERTS)]
    prob_rows = [probs[i:i + 1, :] for i in range(N_EXPERTS)]
    scores = []
    for g in range(N_GROUPS):
        b, _, s, _ = _top2(sel_rows[g * EPG:(g + 1) * EPG])
        scores.append(b + s)
    grp = jnp.zeros(scores[0].shape, jnp.int32)
    gbest = scores[0]
    for g in range(1, N_GROUPS):
        gt = scores[g] > gbest
        gbest = jnp.where(gt, scores[g], gbest)
        grp = jnp.where(gt, g, grp)
    in_rows = []
    for j in range(EPG):
        v = sel_rows[j]
        for g in range(1, N_GROUPS):
            v = jnp.where(grp == g, sel_rows[g * EPG + j], v)
        in_rows.append(v)
    _, l1, _, l2 = _top2(in_rows)
    i1 = grp * EPG + l1
    i2 = grp * EPG + l2
    w1 = jnp.zeros(gbest.shape, F32)
    w2 = jnp.zeros(gbest.shape, F32)
    for i in range(N_EXPERTS):
        w1 = jnp.where(i1 == i, prob_rows[i], w1)
        w2 = jnp.where(i2 == i, prob_rows[i], w2)
    tot = w1 + w2
    return i1, i2, w1 / tot, w2 / tot


def _router_kernel(f_ref, rwt_ref, rb_ref, idx_ref, gate_ref):
    i1, i2, g1, g2 = _route(f_ref[...], rwt_ref[...], rb_ref[...])
    idx_ref[0:1, :] = i1
    idx_ref[1:2, :] = i2
    gate_ref[0:1, :] = g1
    gate_ref[1:2, :] = g2


def _router(f_all, router_wt, router_b):
    n, d = f_all.shape
    return pl.pallas_call(
        _router_kernel,
        out_shape=(jax.ShapeDtypeStruct((2, n), jnp.int32), jax.ShapeDtypeStruct((2, n), F32)),
        grid=(n // TM_ROUTE,),
        in_specs=[
            pl.BlockSpec((TM_ROUTE, d), lambda i: (i, 0)),
            pl.BlockSpec((N_EXPERTS, d), lambda i: (0, 0)),
            pl.BlockSpec((N_EXPERTS, 1), lambda i: (0, 0)),
        ],
        out_specs=(pl.BlockSpec((2, TM_ROUTE), lambda i: (0, i)), pl.BlockSpec((2, TM_ROUTE), lambda i: (0, i))),
        compiler_params=_cparams(("parallel",)),
        name="router",
    )(f_all, router_wt, router_b)


def _scatter_rows_kernel(dest_ref, pad_ref, end_ref, *refs, n_tok, seg_tiles):
    f_refs = refs[:len(seg_tiles)]
    xs_ref, zero_scr, stage, sems = refs[len(seg_tiles):]
    i = pl.program_id(0)
    tm = f_refs[0].shape[0]
    d_blocks = f_refs[0].shape[1] // LANES
    n_rows = xs_ref.shape[0] // d_blocks
    fill_sem = sems.at[2]

    def token_rows(ref, first, n):
        start = first * d_blocks
        if not isinstance(first, int):
            start = pl.multiple_of(start, d_blocks)
        return ref.at[pl.ds(start, n * d_blocks), :]

    def slab_copy(start):
        return pltpu.make_async_copy(zero_scr, token_rows(xs_ref, start, TM_EXP), fill_sem)

    @pl.when(i == 0)
    def _():
        zero_scr[...] = jnp.zeros(zero_scr.shape, zero_scr.dtype)
        for e in range(N_EXPERTS):
            slab_copy(jnp.minimum(pad_ref[e], n_rows - TM_EXP)).start()
        for e in range(N_EXPERTS):
            slab_copy(0).wait()
        for k in range(N_EXPERTS):
            start = end_ref[0] + k * TM_EXP

            @pl.when(start < n_rows)
            def _():
                cp = slab_copy(start)
                cp.start()
                cp.wait()

    slot = i % 2

    def wait_tile(s):
        for _ in range(2):
            pltpu.make_async_copy(stage.at[s], token_rows(xs_ref, 0, tm), sems.at[s]).wait()

    def scatter_tile(f_ref):
        base = i * tm
        for k in range(d_blocks):
            stage[slot, pl.ds(k, tm, stride=d_blocks), :] = f_ref[:, k * LANES:(k + 1) * LANES]

        def row_copy(r, d):
            return pltpu.make_async_copy(token_rows(stage.at[slot], r, 1), token_rows(xs_ref, d, 1), sems.at[slot])

        for r in range(tm):
            row_copy(r, dest_ref[base + r]).start(priority=0)
            row_copy(r, dest_ref[n_tok + base + r]).start(priority=1)

    tile0 = 0
    for f_ref, n_t in zip(f_refs, seg_tiles):
        pl.when((i >= tile0) & (i < tile0 + n_t))(functools.partial(scatter_tile, f_ref))
        tile0 += n_t

    pl.when(i > 0)(lambda: wait_tile(1 - slot))
    pl.when(i == pl.num_programs(0) - 1)(lambda: wait_tile(slot))


def _scatter_rows(segments, dest_flat, pad_start, total_end):
    d = segments[0].shape[1]
    seg_tiles = tuple(s.shape[0] // TM_PROJ for s in segments)
    n_tok = sum(s.shape[0] for s in segments)
    r_max = 2 * n_tok + N_EXPERTS * TM_EXP
    in_specs = []
    tile0 = 0
    for n_t in seg_tiles:
        in_specs.append(pl.BlockSpec(
            (TM_PROJ, d), lambda i, *_, t0=tile0, nt=n_t: (jnp.clip(i - t0, 0, nt - 1), 0)))
        tile0 += n_t
    grid_spec = pltpu.PrefetchScalarGridSpec(
        num_scalar_prefetch=3,
        grid=(tile0,),
        in_specs=in_specs,
        out_specs=pl.BlockSpec(memory_space=pl.ANY),
        scratch_shapes=[pltpu.VMEM((TM_EXP * d // LANES, LANES), F32),
                        pltpu.VMEM((2, TM_PROJ * d // LANES, LANES), F32),
                        pltpu.SemaphoreType.DMA((3,))],
    )
    return pl.pallas_call(
        functools.partial(_scatter_rows_kernel, n_tok=n_tok, seg_tiles=seg_tiles),
        out_shape=jax.ShapeDtypeStruct((r_max * d // LANES, LANES), F32),
        grid_spec=grid_spec,
        compiler_params=_cparams(("arbitrary",)),
        name="moe_scatter_rows",
    )(dest_flat, pad_start, total_end, *segments)


def _experts_kernel(te_ref, nv_ref, x_ref, wg_ref, wu_ref, wd_ref, o_ref, wg_scr, wu_scr, wd_scr):
    i = pl.program_id(0)
    e = te_ref[i]
    prev = te_ref[jnp.maximum(i - 1, 0)]
    rows = 128

    @pl.when((i == 0) | (e != prev))
    def _():
        def body(r, carry):
            sl = pl.ds(pl.multiple_of(r * rows, rows), rows)
            wg_scr[sl, :] = wg_ref[0, 0, sl, :].astype(BF16)
            wu_scr[sl, :] = wu_ref[0, 0, sl, :].astype(BF16)
            wd_scr[sl, :] = wd_ref[0, 0, sl, :].astype(BF16)
            return carry
        lax.fori_loop(0, wg_scr.shape[0] // rows, body, 0)

    @pl.when(i < nv_ref[0])
    def _():
        d_blocks = wg_scr.shape[0] // LANES
        x = jnp.concatenate([x_ref[pl.ds(k, TM_EXP, stride=d_blocks), :].astype(BF16) for k in range(d_blocks)],
                            axis=1)
        g = _dot(x, wg_scr[...])
        u = _dot(x, wu_scr[...])
        a = (g * jax.nn.sigmoid(g)) * u
        o_ref[...] = _dot(a.astype(BF16), wd_scr[...]).astype(BF16)

    @pl.when(i >= nv_ref[0])
    def _():
        o_ref[...] = jnp.zeros(o_ref.shape, BF16)


def _experts(xs, tile_expert, n_valid, w_gate, w_up, w_down, layer):
    d, de = w_gate.shape[2], w_gate.shape[3]
    d_blocks = d // LANES
    r = xs.shape[0] // d_blocks
    n_tiles = r // TM_EXP
    x_map = lambda i, te, nv: (jnp.minimum(i, nv[0] - 1), 0)
    grid_spec = pltpu.PrefetchScalarGridSpec(
        num_scalar_prefetch=2,
        grid=(n_tiles,),
        in_specs=[
            pl.BlockSpec((TM_EXP * d_blocks, LANES), x_map),
            pl.BlockSpec((1, 1, d, de), lambda i, te, nv: (layer, te[i], 0, 0)),
            pl.BlockSpec((1, 1, d, de), lambda i, te, nv: (layer, te[i], 0, 0)),
            pl.BlockSpec((1, 1, de, d), lambda i, te, nv: (layer, te[i], 0, 0)),
        ],
        out_specs=pl.BlockSpec((TM_EXP, d), lambda i, te, nv: (i, 0)),
        scratch_shapes=[pltpu.VMEM((d, de), BF16), pltpu.VMEM((d, de), BF16), pltpu.VMEM((de, d), BF16)],
    )
    return pl.pallas_call(
        _experts_kernel,
        out_shape=jax.ShapeDtypeStruct((r, d), BF16),
        grid_spec=grid_spec,
        compiler_params=_cparams(("arbitrary",)),
        name="experts",
    )(tile_expert, n_valid, xs, w_gate, w_up, w_down)


def _dispatch(idx):
    n = idx.shape[1]
    e_flat = idx.reshape(-1)
    onehot = (e_flat[:, None] == jnp.arange(N_EXPERTS, dtype=jnp.int32)[None, :]).astype(jnp.int32)
    csum = jnp.cumsum(onehot, axis=0)
    rank = jnp.sum(csum * onehot, axis=1) - 1
    counts = csum[-1]
    padded = ((counts + TM_EXP - 1) // TM_EXP) * TM_EXP
    ends = jnp.cumsum(padded)
    starts = ends - padded
    dest = (jnp.sum(onehot * starts[None, :], axis=1) + rank).astype(jnp.int32)
    r_max = 2 * n + N_EXPERTS * TM_EXP
    tile_start = jnp.arange(r_max // TM_EXP, dtype=jnp.int32) * TM_EXP
    tile_expert = jnp.minimum(jnp.sum((tile_start[:, None] >= ends[None, :]).astype(jnp.int32), axis=1),
                              N_EXPERTS - 1).astype(jnp.int32)
    n_valid = (ends[-1] // TM_EXP).astype(jnp.int32).reshape(1)
    pad_start = (starts + counts).astype(jnp.int32)
    total_end = ends[-1].astype(jnp.int32).reshape(1)
    return dest, pad_start, total_end, tile_expert, n_valid


def _combine_kernel(h_ref, y1_ref, y2_ref, gate_ref, mod_ref, o_ref):
    d = D_MODEL
    g = gate_ref[...]
    y = g[:, 0:1] * y1_ref[...].astype(F32) + g[:, 1:2] * y2_ref[...].astype(F32)
    o_ref[...] = h_ref[...] + mod_ref[0, :, 5 * d:6 * d] * y


def _combine(h2d, y1, y2, gates_t, mod3, mod_map, row0):
    r, d = h2d.shape
    blk0 = row0 // TM_PROJ
    row = lambda i: (i, 0)
    seg = lambda i: (blk0 + i, 0)
    return pl.pallas_call(
        _combine_kernel,
        out_shape=jax.ShapeDtypeStruct((r, d), F32),
        grid=(r // TM_PROJ,),
        in_specs=[
            pl.BlockSpec((TM_PROJ, d), row),
            pl.BlockSpec((TM_PROJ, d), seg),
            pl.BlockSpec((TM_PROJ, d), seg),
            pl.BlockSpec((TM_PROJ, 2), seg),
            pl.BlockSpec((1, 1, N_MOD * d), mod_map),
        ],
        out_specs=pl.BlockSpec((TM_PROJ, d), row),
        compiler_params=_cparams(("parallel",)),
        name="moe_combine",
    )(h2d, y1, y2, gates_t, mod3)


def kernel(x, c, ctx, c_ctx, w_mod, b_mod, norm1_g, norm2_g, w_in, w_out, q_norm_g, k_norm_g, na_rpb,
           s5_lam_re, s5_lam_im, s5_log_dt, s5_b_re, s5_b_im, s5_c_re, s5_c_im, s5_d, s5_w_glu, s5_b_glu,
           router_w, router_bias, moe_w_gate, moe_w_up, moe_w_down):
    bsz, n_lat, d = x.shape
    n_ctx = ctx.shape[1]
    depth = w_mod.shape[0]
    ctx_row = bsz
    c_rows = jnp.concatenate([c.astype(F32), c_ctx.astype(F32)[None],
                              jnp.zeros((SUBLANES - bsz - 1, d), F32)], axis=0)
    mod_all = _modulation(c_rows, w_mod.astype(F32), b_mod.astype(F32))

    h_lat = x.reshape(bsz * n_lat, d).astype(F32)
    h_ctx = ctx.reshape(bsz * n_ctx, d).astype(F32)
    lat_map = _mod_row_map(n_lat, 0, True)
    lat_map_in = _mod_row_map(n_lat, 0, True, TM_IN)
    ctx_map = _mod_row_map(n_ctx, ctx_row, False)
    bias_tabs = _na_bias_tables(na_rpb)
    router_wt = router_w.T.astype(F32)
    router_b = router_bias.reshape(N_EXPERTS, 1).astype(F32)
    s5_mats = _s5_matrices(s5_lam_re, s5_lam_im, s5_log_dt, s5_b_re, s5_b_im, s5_c_re, s5_c_im)

    n_l = bsz * n_lat
    pending = None
    for layer in range(depth):
        ctx_out = layer < depth - 1
        mod3 = mod_all[layer].reshape(SUBLANES, 1, N_MOD * d)
        w_in_b = w_in[layer].astype(BF16)
        proj = functools.partial(_in_projection, g=norm1_g[layer], mod3=mod3, w_bf16=w_in_b,
                                 qg=q_norm_g[layer], kg=k_norm_g[layer])
        if pending is None:
            qkvu_lat = proj(h_lat, mod_map=lat_map_in, tm=TM_IN)
            qkvu_ctx = proj(h_ctx, mod_map=ctx_map)
        else:
            qkvu_lat, h_lat = proj(h_lat, mod_map=lat_map_in, pending=pending, row0=0, tm=TM_IN)
            qkvu_ctx, h_ctx = proj(h_ctx, mod_map=ctx_map, pending=pending, row0=n_l)
        na_lat = _neighborhood_attention(qkvu_lat, qkvu_ctx, bias_tabs, layer, bsz)
        y_all = _s5_mixer(qkvu_lat, qkvu_ctx, s5_mats, layer, bsz)
        wglu_b = s5_w_glu[layer].astype(BF16)
        wout_b = w_out[layer].astype(BF16)
        h_lat, f_lat = _out_projection(na_lat, y_all, qkvu_lat, h_lat, mod3, s5_d[layer], wglu_b,
                                       s5_b_glu[layer], wout_b, norm2_g[layer],
                                       lat_map_in, n_lat, 0, tm=TM_IN)
        idx, gates = _router(f_lat, router_wt, router_b)
        if ctx_out:
            na_ctx = _context_attention(qkvu_ctx, bsz)
            h_ctx, f_ctx = _out_projection(na_ctx, y_all, qkvu_ctx, h_ctx, mod3, s5_d[layer], wglu_b,
                                           s5_b_glu[layer], wout_b, norm2_g[layer],
                                           ctx_map, n_ctx, n_lat)
            idx_c, gates_c = _router(f_ctx, router_wt, router_b)
            idx = jnp.concatenate([idx, idx_c], axis=1)
            gates = jnp.concatenate([gates, gates_c], axis=1)
        n_tok = idx.shape[1]
        dest, pad_start, total_end, tile_expert, n_valid = _dispatch(idx)
        xs = _scatter_rows([f_lat, f_ctx] if ctx_out else [f_lat], dest, pad_start, total_end)
        ys = _experts(xs, tile_expert, n_valid, moe_w_gate, moe_w_up, moe_w_down, layer)
        y1 = jnp.take(ys, dest[:n_tok], axis=0, mode="clip")
        y2 = jnp.take(ys, dest[n_tok:], axis=0, mode="clip")
        gates_t = gates.T
        if ctx_out:
            pending = (y1, y2, gates_t, mod3)
        else:
            h_lat = _combine(h_lat, y1, y2, gates_t, mod3, lat_map, 0)
    return h_lat.reshape(bsz, n_lat, d).astype(x.dtype)
```

```python
import functools
import math

import jax
import jax.numpy as jnp
from jax import lax
from jax.experimental import pallas as pl
from jax.experimental.pallas import tpu as pltpu

F32 = jnp.float32
BF16 = jnp.bfloat16
HIGHEST = lax.Precision.HIGHEST

D_MODEL = 1024
GRID_W = 64
HEAD_DIM = 64
NA_WIDTH = 512
S5_WIDTH = 512
S5_CH = 16
S5_GROUPS = 32
S5_STATE = 64
WIN_ROWS = 8
WIN_COLS = 16
N_EXPERTS = 16
N_GROUPS = 4
EPG = 4
N_MOD = 6
EPS = 1e-6

LANES = 128
SUBLANES = 8
VMEM_LIMIT = 56 * 1024 * 1024

TM_PROJ = 256
TM_IN = 512
Q_ROWS = 8
NA_ROW_TILES = 2
Q_COLS = 16
K_ROWS = 16
K_COLS = 32
CHUNK = 16
PAIRS_PER_STEP = 2
S5_PIECE = 32
TM_EXP = 512
TM_ROUTE = 1024
MASK_VALUE = -1e30
RPB_LANE0 = 48


def _cparams(sem):
    return pltpu.CompilerParams(dimension_semantics=sem, vmem_limit_bytes=VMEM_LIMIT)


def _dot(a, b):
    return jnp.dot(a, b, preferred_element_type=F32)


def _dot_nt(a, b):
    return lax.dot_general(a, b, (((1,), (1,)), ((), ())), preferred_element_type=F32)


def _mod_kernel(c_ref, w_ref, b_ref, o_ref):
    a = c_ref[...]
    a = a * jax.nn.sigmoid(a)
    o_ref[0] = jnp.dot(a, w_ref[0], precision=HIGHEST, preferred_element_type=F32) + b_ref[0]


def _modulation(c_rows, w_mod, b_mod):
    depth, d, n = w_mod.shape
    tn = 1536
    return pl.pallas_call(
        _mod_kernel,
        out_shape=jax.ShapeDtypeStruct((depth, SUBLANES, n), F32),
        grid=(depth, n // tn),
        in_specs=[
            pl.BlockSpec((SUBLANES, d), lambda l, j: (0, 0)),
            pl.BlockSpec((1, d, tn), lambda l, j: (l, 0, j)),
            pl.BlockSpec((1, 1, tn), lambda l, j: (l, 0, j)),
        ],
        out_specs=pl.BlockSpec((1, SUBLANES, tn), lambda l, j: (l, 0, j)),
        compiler_params=_cparams(("arbitrary", "arbitrary")),
        name="modulation",
    )(c_rows, w_mod, b_mod.reshape(depth, 1, n))


def _inproj_kernel(x_ref, g_ref, mod_ref, w_ref, qg_ref, kg_ref, *rest, moe_pending):
    x = x_ref[...]
    if moe_pending:
        y1_ref, y2_ref, gate_ref, modp_ref, o_ref, xo_ref = rest
        gw = gate_ref[...]
        y = gw[:, 0:1] * y1_ref[...].astype(F32) + gw[:, 1:2] * y2_ref[...].astype(F32)
        x = x + modp_ref[0, :, 5 * D_MODEL:6 * D_MODEL] * y
        xo_ref[...] = x
    else:
        (o_ref,) = rest
    ms = jnp.mean(x * x, axis=-1, keepdims=True)
    y = x * lax.rsqrt(ms + EPS) * g_ref[...]
    shift = mod_ref[0, :, 0:D_MODEL]
    scale = mod_ref[0, :, D_MODEL:2 * D_MODEL]
    a = y * (1.0 + scale) + shift
    acc = _dot(a.astype(BF16), w_ref[...])
    lo = lax.broadcasted_iota(jnp.int32, (1, LANES), 1) < HEAD_DIM
    n_pairs = NA_WIDTH // LANES
    for blk in range(2 * n_pairs):
        cols = slice(blk * LANES, (blk + 1) * LANES)
        if blk < n_pairs:
            o_ref[:, cols] = _pair_rms(acc[:, cols], qg_ref[...], lo) * (HEAD_DIM ** -0.5)
        else:
            o_ref[:, cols] = _pair_rms(acc[:, cols], kg_ref[...], lo)
    o_ref[:, 2 * NA_WIDTH:] = acc[:, 2 * NA_WIDTH:]


def _mod_row_map(rows_per_batch, mod_row0, per_batch, tm=TM_PROJ):
    tiles_per_batch = rows_per_batch // tm
    if per_batch:
        return lambda i: (mod_row0 + i // tiles_per_batch, 0, 0)
    return lambda i: (mod_row0, 0, 0)


def _in_projection(x2d, g, mod3, w_bf16, mod_map, qg, kg, pending=None, row0=0, tm=TM_PROJ):
    r, d = x2d.shape
    n = w_bf16.shape[1]
    g2 = lambda v: jnp.concatenate([v, v]).reshape(1, LANES).astype(F32)
    row = lambda i: (i, 0)
    in_specs = [
        pl.BlockSpec((tm, d), row),
        pl.BlockSpec((1, d), lambda i: (0, 0)),
        pl.BlockSpec((1, 1, N_MOD * d), mod_map),
        pl.BlockSpec((d, n), lambda i: (0, 0)),
        pl.BlockSpec((1, LANES), lambda i: (0, 0)),
        pl.BlockSpec((1, LANES), lambda i: (0, 0)),
    ]
    args = [x2d, g.reshape(1, d), mod3, w_bf16, g2(qg), g2(kg)]
    out_shape = jax.ShapeDtypeStruct((r, n), F32)
    out_specs = pl.BlockSpec((tm, n), row)
    if pending is not None:
        blk0 = row0 // tm
        seg = lambda i: (blk0 + i, 0)
        in_specs += [pl.BlockSpec((tm, d), seg), pl.BlockSpec((tm, d), seg),
                     pl.BlockSpec((tm, 2), seg), pl.BlockSpec((1, 1, N_MOD * d), mod_map)]
        args += list(pending)
        out_shape = (out_shape, jax.ShapeDtypeStruct((r, d), F32))
        out_specs = (out_specs, pl.BlockSpec((tm, d), row))
    return pl.pallas_call(
        functools.partial(_inproj_kernel, moe_pending=pending is not None),
        out_shape=out_shape,
        grid=(r // tm,),
        in_specs=in_specs,
        out_specs=out_specs,
        compiler_params=_cparams(("parallel",)),
        name="in_projection",
    )(*args)


def _pair_rms(x, g, lo):
    ss = x * x
    sa = jnp.sum(jnp.where(lo, ss, 0.0), axis=-1, keepdims=True)
    sb = jnp.sum(jnp.where(lo, 0.0, ss), axis=-1, keepdims=True)
    ms = jnp.where(lo, sa, sb) * (1.0 / HEAD_DIM)
    return x * lax.rsqrt(ms + EPS) * g


def _na_kernel(q_ref, k_ref, v_ref, kc_ref, vc_ref, bias_ref, o_ref):
    lo = lax.broadcasted_iota(jnp.int32, (1, LANES), 1) < HEAD_DIM
    n_rows = k_ref.shape[1]
    col_tiles = GRID_W // Q_COLS
    nq = Q_ROWS * Q_COLS
    nk = K_ROWS * K_COLS
    kcb = kc_ref[0].astype(BF16)
    vcb = vc_ref[0].astype(BF16)
    tiles = [(rr, j) for rr in range(NA_ROW_TILES) for j in range(col_tiles)]
    q2 = []
    for rr, j in tiles:
        qn = q_ref[0, rr * Q_ROWS:(rr + 1) * Q_ROWS, j * Q_COLS:(j + 1) * Q_COLS, :].reshape(nq, LANES)
        q2.append(jnp.concatenate([jnp.where(lo, qn, 0.0), jnp.where(lo, 0.0, qn)], axis=0).astype(BF16))
    s_cx_all = _dot_nt(jnp.concatenate(q2, axis=0), kcb)
    p_cx, o_nb, denom = [], [], []
    for t, (rr, j) in enumerate(tiles):
        i = pl.program_id(2) * NA_ROW_TILES + rr
        kr0 = jnp.clip(Q_ROWS * i - WIN_ROWS // 2, 0, n_rows - K_ROWS)
        rt = jnp.where(i == 0, 0, jnp.where(i == n_rows // Q_ROWS - 1, 2, 1))
        kc0 = min(max(Q_COLS * j - WIN_COLS // 2, 0), GRID_W - K_COLS)
        ct = 0 if j == 0 else (2 if j == col_tiles - 1 else 1)
        kw = k_ref[0, pl.ds(kr0, K_ROWS), kc0:kc0 + K_COLS, :].reshape(nk, LANES).astype(BF16)
        vw = v_ref[0, pl.ds(kr0, K_ROWS), kc0:kc0 + K_COLS, :].reshape(nk, LANES).astype(BF16)
        s_nb = _dot_nt(q2[t], kw) + bias_ref[0, 0, rt * 3 + ct].reshape(2 * nq, nk)
        s_cx = s_cx_all[t * 2 * nq:(t + 1) * 2 * nq]
        m = jnp.maximum(jnp.max(s_nb, axis=-1, keepdims=True), jnp.max(s_cx, axis=-1, keepdims=True))
        p_nb = jnp.exp(s_nb - m)
        p_c = jnp.exp(s_cx - m)
        denom.append(jnp.sum(p_nb, axis=-1, keepdims=True) + jnp.sum(p_c, axis=-1, keepdims=True))
        p_cx.append(p_c.astype(BF16))
        o_nb.append(_dot(p_nb.astype(BF16), vw))
    o_cx_all = _dot(jnp.concatenate(p_cx, axis=0), vcb)
    for t, (rr, j) in enumerate(tiles):
        o2 = (o_nb[t] + o_cx_all[t * 2 * nq:(t + 1) * 2 * nq]) / denom[t]
        o_ref[0, rr * Q_ROWS:(rr + 1) * Q_ROWS, j * Q_COLS:(j + 1) * Q_COLS, :] = \
            jnp.where(lo, o2[0:nq], o2[nq:2 * nq]).reshape(Q_ROWS, Q_COLS, LANES)


def _bias_table_kernel(rpb_ref, o_ref, tt_scr):
    n_off_r = 2 * WIN_ROWS - 1
    lane = lax.broadcasted_iota(jnp.int32, (Q_COLS, LANES), 1)
    qc = lax.broadcasted_iota(jnp.int32, (Q_COLS, LANES), 0)
    kc = lane % K_COLS
    lane_blk = lane // K_COLS
    per_vreg = LANES // K_COLS
    col_rel = (0, -WIN_COLS // 2, -WIN_COLS)
    col_origin = (0, Q_COLS, GRID_W - Q_COLS)
    row_rel = (0, -WIN_ROWS // 2, -WIN_ROWS)
    row_origin = (0, Q_ROWS, GRID_W - Q_ROWS)
    masked = jnp.full((Q_COLS, LANES), MASK_VALUE, F32)

    for ct in range(3):
        c_abs = col_origin[ct] + qc
        k_abs = col_origin[ct] + col_rel[ct] + kc
        start = jnp.clip(c_abs - WIN_COLS // 2, 0, GRID_W - WIN_COLS)
        valid_c = (k_abs >= start) & (k_abs < start + WIN_COLS)
        base = (1 - WIN_COLS - col_rel[ct] - RPB_LANE0) % LANES
        for ro in range(n_off_r):
            row = jnp.broadcast_to(rpb_ref[0, 0, ro:ro + 1, :], (Q_COLS, LANES))
            t = pltpu.roll(row, base, 1, stride=1, stride_axis=0)
            rep = t
            for m in range(1, per_vreg):
                rep = jnp.where(lane_blk == m, pltpu.roll(t, K_COLS * m, 1), rep)
            tt_scr[ct, ro] = jnp.where(valid_c, rep, MASK_VALUE)

    for rt in range(3):
        for ct in range(3):
            for qr in range(Q_ROWS):
                r_abs = row_origin[rt] + qr
                r_start = min(max(r_abs - WIN_ROWS // 2, 0), GRID_W - WIN_ROWS)
                for w in range(K_ROWS // per_vreg):
                    val = None
                    for m in range(per_vreg):
                        k_abs = row_origin[rt] + row_rel[rt] + per_vreg * w + m
                        ok = r_start <= k_abs < r_start + WIN_ROWS
                        src = tt_scr[ct, k_abs - r_abs + WIN_ROWS - 1] if ok else masked
                        val = src if val is None else jnp.where(lane_blk == m, src, val)
                    o_ref[0, 0, rt * 3 + ct, 0, qr * Q_COLS:(qr + 1) * Q_COLS, w * LANES:(w + 1) * LANES] = val


def _na_bias_tables(na_rpb):
    depth, h, n_r, n_c = na_rpb.shape
    rpb_pad = jnp.pad(na_rpb.astype(F32), ((0, 0), (0, 0), (0, 2 * SUBLANES - n_r),
                                            (RPB_LANE0, LANES - RPB_LANE0 - n_c)))
    nq, nk = Q_ROWS * Q_COLS, K_ROWS * K_COLS
    return pl.pallas_call(
        _bias_table_kernel,
        out_shape=jax.ShapeDtypeStruct((depth, h // 2, 9, 2, nq, nk), F32),
        grid=(depth, h),
        in_specs=[pl.BlockSpec((1, 1, 2 * SUBLANES, LANES), lambda l, i: (l, i, 0, 0))],
        out_specs=pl.BlockSpec((1, 1, 9, 1, nq, nk), lambda l, i: (l, i // 2, 0, i % 2, 0, 0)),
        scratch_shapes=[pltpu.VMEM((3, 2 * WIN_ROWS - 1, Q_COLS, LANES), F32)],
        compiler_params=_cparams(("parallel", "parallel")),
        name="na_bias_tables",
    )(rpb_pad)


def _neighborhood_attention(qkvu_lat, qkvu_ctx, bias_tabs, layer, bsz):
    n_lat = qkvu_lat.shape[0] // bsz
    n_ctx = qkvu_ctx.shape[0] // bsz
    rows = n_lat // GRID_W
    n_cols = qkvu_lat.shape[1]
    lat4 = qkvu_lat.reshape(bsz, rows, GRID_W, n_cols)
    ctx3 = qkvu_ctx.reshape(bsz, n_ctx, n_cols)
    n_pairs = NA_WIDTH // LANES
    step_rows = NA_ROW_TILES * Q_ROWS
    n_tiles = rows // step_rows
    out = pl.pallas_call(
        _na_kernel,
        out_shape=jax.ShapeDtypeStruct((bsz, rows, GRID_W, NA_WIDTH), F32),
        grid=(n_pairs, bsz, n_tiles),
        in_specs=[
            pl.BlockSpec((1, step_rows, GRID_W, LANES), lambda p, b, t: (b, t, 0, p)),
            pl.BlockSpec((1, rows, GRID_W, LANES), lambda p, b, t: (b, 0, 0, n_pairs + p)),
            pl.BlockSpec((1, rows, GRID_W, LANES), lambda p, b, t: (b, 0, 0, 2 * n_pairs + p)),
            pl.BlockSpec((1, n_ctx, LANES), lambda p, b, t: (b, 0, n_pairs + p)),
            pl.BlockSpec((1, n_ctx, LANES), lambda p, b, t: (b, 0, 2 * n_pairs + p)),
            pl.BlockSpec((1, 1, 9, 2, Q_ROWS * Q_COLS, K_ROWS * K_COLS), lambda p, b, t: (layer, p, 0, 0, 0, 0)),
        ],
        out_specs=pl.BlockSpec((1, step_rows, GRID_W, LANES), lambda p, b, t: (b, t, 0, p)),
        compiler_params=_cparams(("parallel", "parallel", "parallel")),
        name="neighborhood_attention",
    )(lat4, lat4, lat4, ctx3, ctx3, bias_tabs)
    return out.reshape(bsz * n_lat, NA_WIDTH)


def _ctx_attn_kernel(q_ref, k_ref, v_ref, o_ref):
    lo = lax.broadcasted_iota(jnp.int32, (1, LANES), 1) < HEAD_DIM
    qn = q_ref[0]
    kn = k_ref[0].astype(BF16)
    vb = v_ref[0].astype(BF16)

    def one(qm):
        s = _dot_nt(qm, kn)
        m = jnp.max(s, axis=-1, keepdims=True)
        p = jnp.exp(s - m)
        l = jnp.sum(p, axis=-1, keepdims=True)
        return _dot(p.astype(BF16), vb) / l

    o_a = one(jnp.where(lo, qn, 0.0).astype(BF16))
    o_b = one(jnp.where(lo, 0.0, qn).astype(BF16))
    o_ref[0] = jnp.where(lo, o_a, o_b)


def _context_attention(qkvu_ctx, bsz):
    n_ctx = qkvu_ctx.shape[0] // bsz
    ctx3 = qkvu_ctx.reshape(bsz, n_ctx, qkvu_ctx.shape[1])
    n_pairs = NA_WIDTH // LANES
    out = pl.pallas_call(
        _ctx_attn_kernel,
        out_shape=jax.ShapeDtypeStruct((bsz, n_ctx, NA_WIDTH), F32),
        grid=(bsz, n_pairs),
        in_specs=[
            pl.BlockSpec((1, n_ctx, LANES), lambda b, p: (b, 0, p)),
            pl.BlockSpec((1, n_ctx, LANES), lambda b, p: (b, 0, n_pairs + p)),
            pl.BlockSpec((1, n_ctx, LANES), lambda b, p: (b, 0, 2 * n_pairs + p)),
        ],
        out_specs=pl.BlockSpec((1, n_ctx, LANES), lambda b, p: (b, 0, p)),
        compiler_params=_cparams(("parallel", "parallel")),
        name="context_attention",
    )(ctx3, ctx3, ctx3)
    return out.reshape(bsz * n_ctx, NA_WIDTH)


def _s5_kernel(uc_ref, ul_ref, w_ref, m_ref, v_ref, a_ref, y_ref, x_scr, s_scr, hf_scr, hr_scr, *, n_ctx_chunks):
    bsz = ul_ref.shape[0]
    n_lat_chunks = ul_ref.shape[1] // CHUNK
    n_chunks = n_ctx_chunks + n_lat_chunks
    rows = n_chunks * SUBLANES
    n_pairs = w_ref.shape[1]
    gpb = 2 * n_pairs
    half = 2 * LANES
    tile_chunks = TM_PROJ // CHUNK
    pieces = [(uc_ref, 0, n_ctx_chunks, 0)] if n_ctx_chunks else []
    pieces += [(ul_ref, c0, min(S5_PIECE, n_lat_chunks - c0), n_ctx_chunks + c0)
               for c0 in range(0, n_lat_chunks, S5_PIECE)]

    def lane_block_ids(n):
        return lax.broadcasted_iota(jnp.int32, (n, LANES), 1) // S5_CH

    def lag_bit_clear(lane_blk, g, bit):
        return ((((lane_blk - g) & (SUBLANES - 1)) >> bit) & 1) == 0

    for src_ref, c_src, n_c, c_dst in pieces:
        lane_blk = lane_block_ids(n_c)
        m0 = [lag_bit_clear(lane_blk, g, 0) for g in range(2)]
        m1 = [lag_bit_clear(lane_blk, g, 1) for g in range(4)]
        m2 = [lag_bit_clear(lane_blk, g, 2) for g in range(gpb)]
        for b in range(bsz):
            for q in range(2):
                rolled = []
                for j in range(SUBLANES):
                    s = SUBLANES * q + j
                    u_s = src_ref[b, pl.ds(c_src * CHUNK + s, n_c, stride=CHUNK), :]
                    rolled.append(u_s if j == 0 else pltpu.roll(u_s, j * S5_CH, 1))
                st1 = [[jnp.where(m0[g], rolled[2 * j], rolled[2 * j + 1]) for j in range(4)] for g in range(2)]
                st2 = [[jnp.where(m1[g], st1[g & 1][2 * j], st1[g & 1][2 * j + 1]) for j in range(2)]
                       for g in range(4)]
                for g in range(gpb):
                    xg = jnp.where(m2[g], st2[g & 3][0], st2[g & 3][1])
                    x_scr[g // 2, q, pl.ds(c_dst * SUBLANES + (g % 2) * bsz + b, n_c, stride=SUBLANES), :] = xg

    n_blk = 8
    rb = rows // n_blk
    first_group = (lax.broadcasted_iota(jnp.int32, (rb, half), 0) & (SUBLANES // 2)) == 0
    fwd_cols = (lax.broadcasted_iota(jnp.int32, (rb, half), 1) & (LANES - 1)) < S5_STATE
    is_fwd = lax.broadcasted_iota(jnp.int32, (SUBLANES, LANES), 1) < S5_STATE
    first_rows = lax.broadcasted_iota(jnp.int32, (SUBLANES, half), 0) < SUBLANES // 2
    zero = jnp.zeros((SUBLANES, LANES), F32)

    def x_rows(p, sl):
        return jnp.concatenate([x_scr[p, 0, sl, :], x_scr[p, 1, sl, :]], axis=1).astype(BF16)

    def put_cols(scr, i, sl, val):
        scr[i, 0, sl, :] = val[:, 0:LANES]
        scr[i, 1, sl, :] = val[:, LANES:half]

    for hh in range(n_pairs // PAIRS_PER_STEP):
        pairs = [hh * PAIRS_PER_STEP + i for i in range(PAIRS_PER_STEP)]
        for i, p in enumerate(pairs):
            for blk in range(n_blk):
                sl = slice(blk * rb, (blk + 1) * rb)
                r = _dot(x_rows(p, sl), w_ref[0, p])
                put_cols(s_scr, i, sl, jnp.where(first_group, r[:, :half], r[:, half:]))

        a_pair = [jnp.where(first_rows, a_ref[0, 2 * p], a_ref[0, 2 * p + 1]) for p in pairs]
        a_re = [a[:, 0:LANES] for a in a_pair]
        a_im = [a[:, LANES:half] for a in a_pair]

        def body(k, carry):
            kr = jnp.where(k < n_ctx_chunks, n_ctx_chunks - 1 - k, n_chunks + n_ctx_chunks - 1 - k)
            rf = pl.ds(pl.multiple_of(k * SUBLANES, SUBLANES), SUBLANES)
            rr = pl.ds(pl.multiple_of(kr * SUBLANES, SUBLANES), SUBLANES)
            new = []
            for i in range(PAIRS_PER_STEP):
                h_re, h_im = carry[2 * i], carry[2 * i + 1]
                hf_scr[i, 0, rf, :] = h_re
                hf_scr[i, 1, rf, :] = h_im
                hr_scr[i, 0, rr, :] = h_re
                hr_scr[i, 1, rr, :] = h_im
                s_re = jnp.where(is_fwd, s_scr[i, 0, rf, :], s_scr[i, 0, rr, :])
                s_im = jnp.where(is_fwd, s_scr[i, 1, rf, :], s_scr[i, 1, rr, :])
                new.append(a_re[i] * h_re - a_im[i] * h_im + s_re)
                new.append(a_re[i] * h_im + a_im[i] * h_re + s_im)
            return tuple(new)

        lax.fori_loop(0, n_chunks, body, (zero,) * (2 * PAIRS_PER_STEP))

        for i, p in enumerate(pairs):
            for blk in range(n_blk):
                sl = slice(blk * rb, (blk + 1) * rb)
                hf = jnp.concatenate([hf_scr[i, 0, sl, :], hf_scr[i, 1, sl, :]], axis=1)
                hr = jnp.concatenate([hr_scr[i, 0, sl, :], hr_scr[i, 1, sl, :]], axis=1)
                h_in = jnp.where(fwd_cols, hf, hr).astype(BF16)
                r = _dot(x_rows(p, sl), m_ref[0, p]) + _dot_nt(h_in, v_ref[0, p])
                put_cols(s_scr, i, sl, jnp.where(first_group, r[:, :half], r[:, half:]))

        g_lo = 2 * pairs[0]
        n_g = 2 * PAIRS_PER_STEP
        out_blk = lax.broadcasted_iota(jnp.int32, (tile_chunks, LANES), 1) // S5_CH
        lanes_out = (out_blk >= g_lo) & (out_blk < g_lo + n_g)
        assert n_g == 4
        for _, _, n_c, c_dst in pieces:
            lane_blk = lane_block_ids(n_c)
            n0 = [lag_bit_clear(lane_blk, e, 0) for e in range(2)]
            n1 = [lag_bit_clear(lane_blk, e, 1) for e in range(4)]
            for b in range(bsz):
                for q in range(2):
                    y_g = [s_scr[(g - g_lo) // 2, q,
                                 pl.ds(c_dst * SUBLANES + (g % 2) * bsz + b, n_c, stride=SUBLANES), :]
                           for g in range(g_lo, g_lo + n_g)]
                    d1 = [[jnp.where(n0[e], y_g[2 * i], y_g[2 * i + 1]) for i in range(2)] for e in range(2)]
                    merged = [jnp.where(n1[e], d1[e & 1][0], d1[e & 1][1]) for e in range(4)]
                    for j in range(SUBLANES):
                        z = merged[(j + g_lo) % 4]
                        if j:
                            z = pltpu.roll(z, LANES - j * S5_CH, 1)
                        t = SUBLANES * q + j
                        for ct in range(n_c // tile_chunks):
                            tile = (c_dst // tile_chunks + ct - n_ctx_chunks // tile_chunks) % (n_chunks // tile_chunks)
                            r0 = tile * TM_PROJ + t * tile_chunks
                            pltpu.store(y_ref.at[b, 0, r0:r0 + tile_chunks, :],
                                        z[ct * tile_chunks:(ct + 1) * tile_chunks, :], mask=lanes_out)


def _s5_scan(qkvu_ctx3, qkvu_lat3, mats, layer):
    w_c, m_c, vt_c, a_c = mats
    bsz, n_ctx, _ = qkvu_ctx3.shape
    n_lat = qkvu_lat3.shape[1]
    n_seq = n_ctx + n_lat
    rows = n_seq // CHUNK * SUBLANES
    n_blocks = S5_WIDTH // LANES
    ppb = w_c.shape[1] // n_blocks
    u_blk0 = 3 * NA_WIDTH // LANES
    wspec = pl.BlockSpec((1, ppb, 2 * LANES, 4 * LANES), lambda i: (layer, i, 0, 0))
    one = pl.Buffered(1)
    state = pltpu.VMEM((PAIRS_PER_STEP, 2, rows, LANES), F32)
    return pl.pallas_call(
        functools.partial(_s5_kernel, n_ctx_chunks=n_ctx // CHUNK),
        out_shape=jax.ShapeDtypeStruct((bsz, n_blocks, n_seq, LANES), F32),
        grid=(n_blocks,),
        in_specs=[
            pl.BlockSpec((bsz, n_ctx, LANES), lambda i: (0, 0, u_blk0 + i)),
            pl.BlockSpec((bsz, n_lat, LANES), lambda i: (0, 0, u_blk0 + i), pipeline_mode=one),
            wspec, wspec,
            pl.BlockSpec((1, ppb, 4 * LANES, 2 * LANES), lambda i: (layer, i, 0, 0)),
            pl.BlockSpec((1, 2 * ppb, SUBLANES, 2 * LANES), lambda i: (layer, i, 0, 0)),
        ],
        out_specs=pl.BlockSpec((bsz, 1, n_seq, LANES), lambda i: (0, i, 0, 0), pipeline_mode=one),
        scratch_shapes=[pltpu.VMEM((ppb, 2, rows, LANES), F32), state, state, state],
        compiler_params=_cparams(("parallel",)),
        name="s5_scan",
    )(qkvu_ctx3, qkvu_lat3, w_c, m_c, vt_c, a_c)


def _s5_mats_kernel(prm_ref, btr_ref, bti_ref, cr_ref, ci_ref, w_ref, m_ref, vt_ref, a_ref):
    t = CHUNK
    gl = pl.program_id(1) % SUBLANES
    is_fwd = lax.broadcasted_iota(jnp.int32, (1, LANES), 1) < S5_STATE
    lr = prm_ref[0, 0, 0:1, :]
    li = prm_ref[0, 0, 1:2, :]
    dt = jnp.exp(prm_ref[0, 0, 2:3, :])
    n = lax.broadcasted_iota(jnp.int32, (3 * SUBLANES, LANES), 0).astype(F32)
    pmag = jnp.exp(n * (lr * dt))
    pw_re = pmag * jnp.cos(n * (li * dt))
    pw_im = pmag * jnp.sin(n * (li * dt))
    ab_re, ab_im = pw_re[1:2, :], pw_im[1:2, :]
    den = lr * lr + li * li
    nr = ab_re - 1.0
    z_re = (nr * lr + ab_im * li) / den
    z_im = (ab_im * lr - nr * li) / den
    bt_re, bt_im = btr_ref[0, 0], bti_ref[0, 0]
    bb_re = z_re * bt_re - z_im * bt_im
    bb_im = z_re * bt_im + z_im * bt_re
    c_re, c_im = cr_ref[0, 0], ci_ref[0, 0]

    def powers(n_fwd, n_rev):
        return (jnp.where(is_fwd, pw_re[n_fwd:n_fwd + 1, :], pw_re[n_rev:n_rev + 1, :]),
                jnp.where(is_fwd, pw_im[n_fwd:n_fwd + 1, :], pw_im[n_rev:n_rev + 1, :]))

    def block_rows(s):
        pos = SUBLANES * (s // SUBLANES) + (s % SUBLANES + gl) % SUBLANES
        return pl.ds(pl.multiple_of(pos * S5_CH, S5_CH), S5_CH)

    for s in range(t):
        rows = block_rows(s)
        p_re, p_im = powers(t - 1 - s, s)
        w_ref[0, 0, rows, 0:LANES] = (bb_re * p_re - bb_im * p_im).astype(BF16)
        w_ref[0, 0, rows, LANES:2 * LANES] = (bb_re * p_im + bb_im * p_re).astype(BF16)
        q_re, q_im = powers(s + 1, t - s)
        vt_ref[0, 0, rows, 0:LANES] = (c_re * q_re - c_im * q_im).astype(BF16)
        vt_ref[0, 0, rows, LANES:2 * LANES] = (-(c_re * q_im + c_im * q_re)).astype(BF16)

    ca_re, ca_im = [], []
    for lag in range(t):
        p_re, p_im = powers(lag, t - 1 - lag)
        ca_re.append(c_re * p_re - c_im * p_im)
        ca_im.append(c_re * p_im + c_im * p_re)
    stack = jnp.concatenate([jnp.concatenate(ca_re, axis=0), jnp.concatenate(ca_im, axis=0)], axis=1)
    zero = jnp.zeros_like(bb_re)
    lhs = jnp.concatenate([
        jnp.concatenate([jnp.where(is_fwd, bb_re, zero), jnp.where(is_fwd, -bb_im, zero)], axis=1),
        jnp.concatenate([jnp.where(is_fwd, zero, bb_re), jnp.where(is_fwd, zero, -bb_im)], axis=1)], axis=0)
    kt = lax.dot_general(lhs, stack, (((1,), (1,)), ((), ())), precision=HIGHEST, preferred_element_type=F32)
    kt_f, kt_r = kt[0:S5_CH], kt[S5_CH:2 * S5_CH]
    blk = lax.broadcasted_iota(jnp.int32, (S5_CH, 2 * LANES), 1) // S5_CH
    for s in range(t):
        strip = (jnp.where(blk >= s, pltpu.roll(kt_f, S5_CH * s, 1), 0.0)
                 + jnp.where(blk <= s, pltpu.roll(kt_r, (S5_CH * (s - t + 1)) % (2 * LANES), 1), 0.0))
        strip = jnp.concatenate([pltpu.roll(strip[:, 0:LANES], gl * S5_CH, 1),
                                 pltpu.roll(strip[:, LANES:2 * LANES], gl * S5_CH, 1)], axis=1)
        m_ref[0, 0, block_rows(s), :] = strip.astype(BF16)

    a_ref[0, 0, :, 0:LANES] = jnp.broadcast_to(pw_re[t:t + 1, :], (SUBLANES, LANES))
    a_ref[0, 0, :, LANES:2 * LANES] = jnp.broadcast_to(pw_im[t:t + 1, :], (SUBLANES, LANES))


def _s5_matrices(lam_re, lam_im, log_dt, b_re, b_im, c_re, c_im):
    depth, _, g, p = lam_re.shape
    hc = b_re.shape[-1]
    width = CHUNK * hc
    both = lambda x: jnp.transpose(x.astype(F32), (0, 2, 1, 3)).reshape(depth, g, 1, 2 * p)
    dt_rows = jnp.broadcast_to(jnp.transpose(log_dt.astype(F32), (0, 2, 1))[..., None], (depth, g, 2, p))
    prm = jnp.concatenate([both(lam_re), both(lam_im), dt_rows.reshape(depth, g, 1, 2 * p),
                           jnp.zeros((depth, g, SUBLANES - 3, 2 * p), F32)], axis=2)
    bt = lambda x: jnp.transpose(x.astype(F32), (0, 2, 4, 1, 3)).reshape(depth, g, hc, 2 * p)
    ct = lambda x: jnp.transpose(x.astype(F32), (0, 2, 3, 1, 4)).reshape(depth, g, hc, 2 * p)
    vec = lambda rows: pl.BlockSpec((1, 1, rows, 2 * p), lambda l, i: (l, i, 0, 0))
    return pl.pallas_call(
        _s5_mats_kernel,
        out_shape=(jax.ShapeDtypeStruct((depth, g // 2, width, 2 * width), BF16),
                   jax.ShapeDtypeStruct((depth, g // 2, width, 2 * width), BF16),
                   jax.ShapeDtypeStruct((depth, g // 2, 2 * width, width), BF16),
                   jax.ShapeDtypeStruct((depth, g, SUBLANES, 4 * p), F32)),
        grid=(depth, g),
        in_specs=[vec(SUBLANES), vec(hc), vec(hc), vec(hc), vec(hc)],
        out_specs=(pl.BlockSpec((1, 1, width, width), lambda l, i: (l, i // 2, 0, i % 2)),
                   pl.BlockSpec((1, 1, width, width), lambda l, i: (l, i // 2, 0, i % 2)),
                   pl.BlockSpec((1, 1, width, width), lambda l, i: (l, i // 2, i % 2, 0)),
                   pl.BlockSpec((1, 1, SUBLANES, 4 * p), lambda l, i: (l, i, 0, 0))),
        compiler_params=_cparams(("parallel", "parallel")),
        name="s5_matrices",
    )(prm, bt(b_re), bt(b_im), ct(c_re), ct(c_im))


def _s5_mixer(qkvu_lat, qkvu_ctx, mats, layer, bsz):
    n_cols = qkvu_lat.shape[1]
    return _s5_scan(qkvu_ctx.reshape(bsz, -1, n_cols), qkvu_lat.reshape(bsz, -1, n_cols), mats, layer)


def _outproj_kernel(na_ref, y_ref, u_ref, h_ref, mod_ref, d_ref, wglu_ref, bglu_ref, wout_ref, g2_ref,
                    ho_ref, f_ref):
    tile_chunks = TM_PROJ // CHUNK
    y = jnp.concatenate(
        [jnp.concatenate([y_ref[0, blk, pl.ds(sub * TM_PROJ + c, CHUNK, stride=tile_chunks), :]
                          for sub in range(y_ref.shape[2] // TM_PROJ) for c in range(tile_chunks)], axis=0)
         for blk in range(S5_WIDTH // LANES)], axis=1)
    z = jax.nn.gelu(y + d_ref[...] * u_ref[...])
    s5 = z * jax.nn.sigmoid(_dot(z.astype(BF16), wglu_ref[...]) + bglu_ref[...])
    mix = (_dot(na_ref[...].astype(BF16), wout_ref[0:NA_WIDTH, :])
           + _dot(s5.astype(BF16), wout_ref[NA_WIDTH:NA_WIDTH + S5_WIDTH, :]))
    d = D_MODEL
    gate = mod_ref[0, :, 2 * d:3 * d]
    h = h_ref[...] + gate * mix
    ho_ref[...] = h
    ms = jnp.mean(h * h, axis=-1, keepdims=True)
    y2 = h * lax.rsqrt(ms + EPS) * g2_ref[...]
    f_ref[...] = y2 * (1.0 + mod_ref[0, :, 4 * d:5 * d]) + mod_ref[0, :, 3 * d:4 * d]


def _out_projection(na, y_all, qkvu, h2d, mod3, d_skip, wglu_bf16, b_glu, wout_bf16, g2,
                    mod_map, rows_per_batch, y_row0, tm=TM_PROJ):
    r, d = h2d.shape
    tiles_per_batch = rows_per_batch // tm
    u_blk = 3 * NA_WIDTH // S5_WIDTH
    y_block0 = y_row0 // tm

    def y_map(i):
        return (i // tiles_per_batch, 0, y_block0 + i % tiles_per_batch, 0)

    const = lambda i: (0, 0)
    return pl.pallas_call(
        _outproj_kernel,
        out_shape=(jax.ShapeDtypeStruct((r, d), F32), jax.ShapeDtypeStruct((r, d), F32)),
        grid=(r // tm,),
        in_specs=[
            pl.BlockSpec((tm, NA_WIDTH), lambda i: (i, 0)),
            pl.BlockSpec((1, S5_WIDTH // LANES, tm, LANES), y_map),
            pl.BlockSpec((tm, S5_WIDTH), lambda i: (i, u_blk)),
            pl.BlockSpec((tm, d), lambda i: (i, 0)),
            pl.BlockSpec((1, 1, N_MOD * d), mod_map),
            pl.BlockSpec((1, S5_WIDTH), const),
            pl.BlockSpec((S5_WIDTH, S5_WIDTH), const),
            pl.BlockSpec((1, S5_WIDTH), const),
            pl.BlockSpec((NA_WIDTH + S5_WIDTH, d), const),
            pl.BlockSpec((1, d), const),
        ],
        out_specs=(pl.BlockSpec((tm, d), lambda i: (i, 0)), pl.BlockSpec((tm, d), lambda i: (i, 0))),
        compiler_params=_cparams(("parallel",)),
        name="out_projection",
    )(na, y_all, qkvu, h2d, mod3, d_skip.reshape(1, -1), wglu_bf16, b_glu.reshape(1, -1), wout_bf16,
      g2.reshape(1, d))


def _top2(vals):
    best = vals[0]
    bi = jnp.zeros(best.shape, jnp.int32)
    for i in range(1, len(vals)):
        gt = vals[i] > best
        best = jnp.where(gt, vals[i], best)
        bi = jnp.where(gt, i, bi)
    second = jnp.full(best.shape, -jnp.inf, F32)
    si = jnp.zeros(best.shape, jnp.int32)
    for i in range(len(vals)):
        cand = jnp.where(bi == i, -jnp.inf, vals[i])
        gt = cand > second
        second = jnp.where(gt, cand, second)
        si = jnp.where(gt, i, si)
    return best, bi, second, si


def _route(f, rwt, rb):
    logits = lax.dot_general(rwt, f, (((1,), (1,)), ((), ())),
                             precision=HIGHEST, preferred_element_type=F32)
    m = jnp.max(logits, axis=0, keepdims=True)
    e = jnp.exp(logits - m)
    probs = e / jnp.sum(e, axis=0, keepdims=True)
    sel = probs + rb
    sel_rows = [sel[i:i + 1, :] for i in range(N_EXPERTS)]
    prob_rows = [probs[i:i + 1, :] for i in range(N_EXPERTS)]
    scores = []
    for g in range(N_GROUPS):
        b, _, s, _ = _top2(sel_rows[g * EPG:(g + 1) * EPG])
        scores.append(b + s)
    grp = jnp.zeros(scores[0].shape, jnp.int32)
    gbest = scores[0]
    for g in range(1, N_GROUPS):
        gt = scores[g] > gbest
        gbest = jnp.where(gt, scores[g], gbest)
        grp = jnp.where(gt, g, grp)
    in_rows = []
    for j in range(EPG):
        v = sel_rows[j]
        for g in range(1, N_GROUPS):
            v = jnp.where(grp == g, sel_rows[g * EPG + j], v)
        in_rows.append(v)
    _, l1, _, l2 = _top2(in_rows)
    i1 = grp * EPG + l1
    i2 = grp * EPG + l2
    w1 = jnp.zeros(gbest.shape, F32)
    w2 = jnp.zeros(gbest.shape, F32)
    for i in range(N_EXPERTS):
        w1 = jnp.where(i1 == i, prob_rows[i], w1)
        w2 = jnp.where(i2 == i, prob_rows[i], w2)
    tot = w1 + w2
    return i1, i2, w1 / tot, w2 / tot


def _router_kernel(f_ref, rwt_ref, rb_ref, idx_ref, gate_ref):
    i1, i2, g1, g2 = _route(f_ref[...], rwt_ref[...], rb_ref[...])
    idx_ref[0:1, :] = i1
    idx_ref[1:2, :] = i2
    gate_ref[0:1, :] = g1
    gate_ref[1:2, :] = g2


def _router(f_all, router_wt, router_b):
    n, d = f_all.shape
    return pl.pallas_call(
        _router_kernel,
        out_shape=(jax.ShapeDtypeStruct((2, n), jnp.int32), jax.ShapeDtypeStruct((2, n), F32)),
        grid=(n // TM_ROUTE,),
        in_specs=[
            pl.BlockSpec((TM_ROUTE, d), lambda i: (i, 0)),
            pl.BlockSpec((N_EXPERTS, d), lambda i: (0, 0)),
            pl.BlockSpec((N_EXPERTS, 1), lambda i: (0, 0)),
        ],
        out_specs=(pl.BlockSpec((2, TM_ROUTE), lambda i: (0, i)), pl.BlockSpec((2, TM_ROUTE), lambda i: (0, i))),
        compiler_params=_cparams(("parallel",)),
        name="router",
    )(f_all, router_wt, router_b)


def _scatter_rows_kernel(dest_ref, pad_ref, end_ref, *refs, n_tok, seg_tiles):
    f_refs = refs[:len(seg_tiles)]
    xs_ref, zero_scr, stage, sems = refs[len(seg_tiles):]
    i = pl.program_id(0)
    tm = f_refs[0].shape[0]
    d_blocks = f_refs[0].shape[1] // LANES
    n_rows = xs_ref.shape[0] // d_blocks
    fill_sem = sems.at[2]

    def token_rows(ref, first, n):
        start = first * d_blocks
        if not isinstance(first, int):
            start = pl.multiple_of(start, d_blocks)
        return ref.at[pl.ds(start, n * d_blocks), :]

    def slab_copy(start):
        return pltpu.make_async_copy(zero_scr, token_rows(xs_ref, start, TM_EXP), fill_sem)

    @pl.when(i == 0)
    def _():
        zero_scr[...] = jnp.zeros(zero_scr.shape, zero_scr.dtype)
        for e in range(N_EXPERTS):
            slab_copy(jnp.minimum(pad_ref[e], n_rows - TM_EXP)).start()
        for e in range(N_EXPERTS):
            slab_copy(0).wait()
        for k in range(N_EXPERTS):
            start = end_ref[0] + k * TM_EXP

            @pl.when(start < n_rows)
            def _():
                cp = slab_copy(start)
                cp.start()
                cp.wait()

    slot = i % 2

    def wait_tile(s):
        for _ in range(2):
            pltpu.make_async_copy(stage.at[s], token_rows(xs_ref, 0, tm), sems.at[s]).wait()

    def scatter_tile(f_ref):
        base = i * tm
        for k in range(d_blocks):
            stage[slot, pl.ds(k, tm, stride=d_blocks), :] = f_ref[:, k * LANES:(k + 1) * LANES]

        def row_copy(r, d):
            return pltpu.make_async_copy(token_rows(stage.at[slot], r, 1), token_rows(xs_ref, d, 1), sems.at[slot])

        for r in range(tm):
            row_copy(r, dest_ref[base + r]).start(priority=0)
            row_copy(r, dest_ref[n_tok + base + r]).start(priority=1)

    tile0 = 0
    for f_ref, n_t in zip(f_refs, seg_tiles):
        pl.when((i >= tile0) & (i < tile0 + n_t))(functools.partial(scatter_tile, f_ref))
        tile0 += n_t

    pl.when(i > 0)(lambda: wait_tile(1 - slot))
    pl.when(i == pl.num_programs(0) - 1)(lambda: wait_tile(slot))


def _scatter_rows(segments, dest_flat, pad_start, total_end):
    d = segments[0].shape[1]
    seg_tiles = tuple(s.shape[0] // TM_PROJ for s in segments)
    n_tok = sum(s.shape[0] for s in segments)
    r_max = 2 * n_tok + N_EXPERTS * TM_EXP
    in_specs = []
    tile0 = 0
    for n_t in seg_tiles:
        in_specs.append(pl.BlockSpec(
            (TM_PROJ, d), lambda i, *_, t0=tile0, nt=n_t: (jnp.clip(i - t0, 0, nt - 1), 0)))
        tile0 += n_t
    grid_spec = pltpu.PrefetchScalarGridSpec(
        num_scalar_prefetch=3,
        grid=(tile0,),
        in_specs=in_specs,
        out_specs=pl.BlockSpec(memory_space=pl.ANY),
        scratch_shapes=[pltpu.VMEM((TM_EXP * d // LANES, LANES), F32),
                        pltpu.VMEM((2, TM_PROJ * d // LANES, LANES), F32),
                        pltpu.SemaphoreType.DMA((3,))],
    )
    return pl.pallas_call(
        functools.partial(_scatter_rows_kernel, n_tok=n_tok, seg_tiles=seg_tiles),
        out_shape=jax.ShapeDtypeStruct((r_max * d // LANES, LANES), F32),
        grid_spec=grid_spec,
        compiler_params=_cparams(("arbitrary",)),
        name="moe_scatter_rows",
    )(dest_flat, pad_start, total_end, *segments)


def _experts_kernel(te_ref, nv_ref, x_ref, wg_ref, wu_ref, wd_ref, o_ref, wg_scr, wu_scr, wd_scr):
    i = pl.program_id(0)
    e = te_ref[i]
    prev = te_ref[jnp.maximum(i - 1, 0)]
    rows = 128

    @pl.when((i == 0) | (e != prev))
    def _():
        def body(r, carry):
            sl = pl.ds(pl.multiple_of(r * rows, rows), rows)
            wg_scr[sl, :] = wg_ref[0, 0, sl, :].astype(BF16)
            wu_scr[sl, :] = wu_ref[0, 0, sl, :].astype(BF16)
            wd_scr[sl, :] = wd_ref[0, 0, sl, :].astype(BF16)
            return carry
        lax.fori_loop(0, wg_scr.shape[0] // rows, body, 0)

    @pl.when(i < nv_ref[0])
    def _():
        d_blocks = wg_scr.shape[0] // LANES
        x = jnp.concatenate([x_ref[pl.ds(k, TM_EXP, stride=d_blocks), :].astype(BF16) for k in range(d_blocks)],
                            axis=1)
        g = _dot(x, wg_scr[...])
        u = _dot(x, wu_scr[...])
        a = (g * jax.nn.sigmoid(g)) * u
        o_ref[...] = _dot(a.astype(BF16), wd_scr[...]).astype(BF16)

    @pl.when(i >= nv_ref[0])
    def _():
        o_ref[...] = jnp.zeros(o_ref.shape, BF16)


def _experts(xs, tile_expert, n_valid, w_gate, w_up, w_down, layer):
    d, de = w_gate.shape[2], w_gate.shape[3]
    d_blocks = d // LANES
    r = xs.shape[0] // d_blocks
    n_tiles = r // TM_EXP
    x_map = lambda i, te, nv: (jnp.minimum(i, nv[0] - 1), 0)
    grid_spec = pltpu.PrefetchScalarGridSpec(
        num_scalar_prefetch=2,
        grid=(n_tiles,),
        in_specs=[
            pl.BlockSpec((TM_EXP * d_blocks, LANES), x_map),
            pl.BlockSpec((1, 1, d, de), lambda i, te, nv: (layer, te[i], 0, 0)),
            pl.BlockSpec((1, 1, d, de), lambda i, te, nv: (layer, te[i], 0, 0)),
            pl.BlockSpec((1, 1, de, d), lambda i, te, nv: (layer, te[i], 0, 0)),
        ],
        out_specs=pl.BlockSpec((TM_EXP, d), lambda i, te, nv: (i, 0)),
        scratch_shapes=[pltpu.VMEM((d, de), BF16), pltpu.VMEM((d, de), BF16), pltpu.VMEM((de, d), BF16)],
    )
    return pl.pallas_call(
        _experts_kernel,
        out_shape=jax.ShapeDtypeStruct((r, d), BF16),
        grid_spec=grid_spec,
        compiler_params=_cparams(("arbitrary",)),
        name="experts",
    )(tile_expert, n_valid, xs, w_gate, w_up, w_down)


def _dispatch(idx):
    n = idx.shape[1]
    e_flat = idx.reshape(-1)
    onehot = (e_flat[:, None] == jnp.arange(N_EXPERTS, dtype=jnp.int32)[None, :]).astype(jnp.int32)
    csum = jnp.cumsum(onehot, axis=0)
    rank = jnp.sum(csum * onehot, axis=1) - 1
    counts = csum[-1]
    padded = ((counts + TM_EXP - 1) // TM_EXP) * TM_EXP
    ends = jnp.cumsum(padded)
    starts = ends - padded
    dest = (jnp.sum(onehot * starts[None, :], axis=1) + rank).astype(jnp.int32)
    r_max = 2 * n + N_EXPERTS * TM_EXP
    tile_start = jnp.arange(r_max // TM_EXP, dtype=jnp.int32) * TM_EXP
    tile_expert = jnp.minimum(jnp.sum((tile_start[:, None] >= ends[None, :]).astype(jnp.int32), axis=1),
                              N_EXPERTS - 1).astype(jnp.int32)
    n_valid = (ends[-1] // TM_EXP).astype(jnp.int32).reshape(1)
    pad_start = (starts + counts).astype(jnp.int32)
    total_end = ends[-1].astype(jnp.int32).reshape(1)
    return dest, pad_start, total_end, tile_expert, n_valid


def _combine_kernel(h_ref, y1_ref, y2_ref, gate_ref, mod_ref, o_ref):
    d = D_MODEL
    g = gate_ref[...]
    y = g[:, 0:1] * y1_ref[...].astype(F32) + g[:, 1:2] * y2_ref[...].astype(F32)
    o_ref[...] = h_ref[...] + mod_ref[0, :, 5 * d:6 * d] * y


def _combine(h2d, y1, y2, gates_t, mod3, mod_map, row0, tm=TM_PROJ):
    r, d = h2d.shape
    blk0 = row0 // tm
    row = lambda i: (i, 0)
    seg = lambda i: (blk0 + i, 0)
    return pl.pallas_call(
        _combine_kernel,
        out_shape=jax.ShapeDtypeStruct((r, d), F32),
        grid=(r // tm,),
        in_specs=[
            pl.BlockSpec((tm, d), row),
            pl.BlockSpec((tm, d), seg),
            pl.BlockSpec((tm, d), seg),
            pl.BlockSpec((tm, 2), seg),
            pl.BlockSpec((1, 1, N_MOD * d), mod_map),
        ],
        out_specs=pl.BlockSpec((tm, d), row),
        compiler_params=_cparams(("parallel",)),
        name="moe_combine",
    )(h2d, y1, y2, gates_t, mod3)


def kernel(x, c, ctx, c_ctx, w_mod, b_mod, norm1_g, norm2_g, w_in, w_out, q_norm_g, k_norm_g, na_rpb,
           s5_lam_re, s5_lam_im, s5_log_dt, s5_b_re, s5_b_im, s5_c_re, s5_c_im, s5_d, s5_w_glu, s5_b_glu,
           router_w, router_bias, moe_w_gate, moe_w_up, moe_w_down):
    bsz, n_lat, d = x.shape
    n_ctx = ctx.shape[1]
    depth = w_mod.shape[0]
    ctx_row = bsz
    c_rows = jnp.concatenate([c.astype(F32), c_ctx.astype(F32)[None],
                              jnp.zeros((SUBLANES - bsz - 1, d), F32)], axis=0)
    mod_all = _modulation(c_rows, w_mod.astype(F32), b_mod.astype(F32))

    h_lat = x.reshape(bsz * n_lat, d).astype(F32)
    h_ctx = ctx.reshape(bsz * n_ctx, d).astype(F32)
    lat_map = _mod_row_map(n_lat, 0, True)
    lat_map_in = _mod_row_map(n_lat, 0, True, TM_IN)
    ctx_map = _mod_row_map(n_ctx, ctx_row, False)
    bias_tabs = _na_bias_tables(na_rpb)
    router_wt = router_w.T.astype(F32)
    router_b = router_bias.reshape(N_EXPERTS, 1).astype(F32)
    s5_mats = _s5_matrices(s5_lam_re, s5_lam_im, s5_log_dt, s5_b_re, s5_b_im, s5_c_re, s5_c_im)

    n_l = bsz * n_lat
    pending = None
    for layer in range(depth):
        ctx_out = layer < depth - 1
        mod3 = mod_all[layer].reshape(SUBLANES, 1, N_MOD * d)
        w_in_b = w_in[layer].astype(BF16)
        proj = functools.partial(_in_projection, g=norm1_g[layer], mod3=mod3, w_bf16=w_in_b,
                                 qg=q_norm_g[layer], kg=k_norm_g[layer])
        if pending is None:
            qkvu_lat = proj(h_lat, mod_map=lat_map_in, tm=TM_IN)
            qkvu_ctx = proj(h_ctx, mod_map=ctx_map)
        else:
            qkvu_lat, h_lat = proj(h_lat, mod_map=lat_map_in, pending=pending, row0=0, tm=TM_IN)
            qkvu_ctx, h_ctx = proj(h_ctx, mod_map=ctx_map, pending=pending, row0=n_l)
        na_lat = _neighborhood_attention(qkvu_lat, qkvu_ctx, bias_tabs, layer, bsz)
        y_all = _s5_mixer(qkvu_lat, qkvu_ctx, s5_mats, layer, bsz)
        wglu_b = s5_w_glu[layer].astype(BF16)
        wout_b = w_out[layer].astype(BF16)
        h_lat, f_lat = _out_projection(na_lat, y_all, qkvu_lat, h_lat, mod3, s5_d[layer], wglu_b,
                                       s5_b_glu[layer], wout_b, norm2_g[layer],
                                       lat_map_in, n_lat, 0, tm=TM_IN)
        idx, gates = _router(f_lat, router_wt, router_b)
        if ctx_out:
            na_ctx = _context_attention(qkvu_ctx, bsz)
            h_ctx, f_ctx = _out_projection(na_ctx, y_all, qkvu_ctx, h_ctx, mod3, s5_d[layer], wglu_b,
                                           s5_b_glu[layer], wout_b, norm2_g[layer],
                                           ctx_map, n_ctx, n_lat)
            idx_c, gates_c = _router(f_ctx, router_wt, router_b)
            idx = jnp.concatenate([idx, idx_c], axis=1)
            gates = jnp.concatenate([gates, gates_c], axis=1)
        n_tok = idx.shape[1]
        dest, pad_start, total_end, tile_expert, n_valid = _dispatch(idx)
        xs = _scatter_rows([f_lat, f_ctx] if ctx_out else [f_lat], dest, pad_start, total_end)
        ys = _experts(xs, tile_expert, n_valid, moe_w_gate, moe_w_up, moe_w_down, layer)
        y1 = jnp.take(ys, dest[:n_tok], axis=0, mode="clip")
        y2 = jnp.take(ys, dest[n_tok:], axis=0, mode="clip")
        gates_t = gates.T
        if ctx_out:
            pending = (y1, y2, gates_t, mod3)
        else:
            h_lat = _combine(h_lat, y1, y2, gates_t, mod3, lat_map_in, 0, tm=TM_IN)
    return h_lat.reshape(bsz, n_lat, d).astype(x.dtype)
```

```python
import functools
import math

import jax
import jax.numpy as jnp
from jax import lax
from jax.experimental import pallas as pl
from jax.experimental.pallas import tpu as pltpu

F32 = jnp.float32
BF16 = jnp.bfloat16
HIGHEST = lax.Precision.HIGHEST

D_MODEL = 1024
GRID_W = 64
HEAD_DIM = 64
NA_WIDTH = 512
S5_WIDTH = 512
S5_CH = 16
S5_GROUPS = 32
S5_STATE = 64
WIN_ROWS = 8
WIN_COLS = 16
N_EXPERTS = 16
N_GROUPS = 4
EPG = 4
N_MOD = 6
EPS = 1e-6

LANES = 128
SUBLANES = 8
VMEM_LIMIT = 56 * 1024 * 1024

TM_PROJ = 256
TM_IN = 512
Q_ROWS = 8
NA_ROW_TILES = 2
Q_COLS = 16
K_ROWS = 16
K_COLS = 32
CHUNK = 16
PAIRS_PER_STEP = 2
S5_PIECE = 32
TM_EXP = 512
TM_ROUTE = 1024
MASK_VALUE = -1e30
LOG2E = 1.4426950408889634
RPB_LANE0 = 48


def _cparams(sem):
    return pltpu.CompilerParams(dimension_semantics=sem, vmem_limit_bytes=VMEM_LIMIT)


def _dot(a, b):
    return jnp.dot(a, b, preferred_element_type=F32)


def _dot_nt(a, b):
    return lax.dot_general(a, b, (((1,), (1,)), ((), ())), preferred_element_type=F32)


def _mod_kernel(c_ref, w_ref, b_ref, o_ref):
    a = c_ref[...]
    a = a * jax.nn.sigmoid(a)
    o_ref[0] = jnp.dot(a, w_ref[0], precision=HIGHEST, preferred_element_type=F32) + b_ref[0]


def _modulation(c_rows, w_mod, b_mod):
    depth, d, n = w_mod.shape
    tn = 1536
    return pl.pallas_call(
        _mod_kernel,
        out_shape=jax.ShapeDtypeStruct((depth, SUBLANES, n), F32),
        grid=(depth, n // tn),
        in_specs=[
            pl.BlockSpec((SUBLANES, d), lambda l, j: (0, 0)),
            pl.BlockSpec((1, d, tn), lambda l, j: (l, 0, j)),
            pl.BlockSpec((1, 1, tn), lambda l, j: (l, 0, j)),
        ],
        out_specs=pl.BlockSpec((1, SUBLANES, tn), lambda l, j: (l, 0, j)),
        compiler_params=_cparams(("arbitrary", "arbitrary")),
        name="modulation",
    )(c_rows, w_mod, b_mod.reshape(depth, 1, n))


def _inproj_kernel(x_ref, g_ref, mod_ref, w_ref, qg_ref, kg_ref, *rest, moe_pending):
    x = x_ref[...]
    if moe_pending:
        y1_ref, y2_ref, gate_ref, modp_ref, o_ref, xo_ref = rest
        gw = gate_ref[...]
        y = gw[:, 0:1] * y1_ref[...].astype(F32) + gw[:, 1:2] * y2_ref[...].astype(F32)
        x = x + modp_ref[0, :, 5 * D_MODEL:6 * D_MODEL] * y
        xo_ref[...] = x
    else:
        (o_ref,) = rest
    ms = jnp.mean(x * x, axis=-1, keepdims=True)
    y = x * lax.rsqrt(ms + EPS) * g_ref[...]
    shift = mod_ref[0, :, 0:D_MODEL]
    scale = mod_ref[0, :, D_MODEL:2 * D_MODEL]
    a = y * (1.0 + scale) + shift
    acc = _dot(a.astype(BF16), w_ref[...])
    lo = lax.broadcasted_iota(jnp.int32, (1, LANES), 1) < HEAD_DIM
    n_pairs = NA_WIDTH // LANES
    for blk in range(2 * n_pairs):
        cols = slice(blk * LANES, (blk + 1) * LANES)
        if blk < n_pairs:
            o_ref[:, cols] = _pair_rms(acc[:, cols], qg_ref[...], lo) * (HEAD_DIM ** -0.5 * LOG2E)
        else:
            o_ref[:, cols] = _pair_rms(acc[:, cols], kg_ref[...], lo)
    o_ref[:, 2 * NA_WIDTH:] = acc[:, 2 * NA_WIDTH:]


def _mod_row_map(rows_per_batch, mod_row0, per_batch, tm=TM_PROJ):
    tiles_per_batch = rows_per_batch // tm
    if per_batch:
        return lambda i: (mod_row0 + i // tiles_per_batch, 0, 0)
    return lambda i: (mod_row0, 0, 0)


def _in_projection(x2d, g, mod3, w_bf16, mod_map, qg, kg, pending=None, row0=0, tm=TM_PROJ):
    r, d = x2d.shape
    n = w_bf16.shape[1]
    g2 = lambda v: jnp.concatenate([v, v]).reshape(1, LANES).astype(F32)
    row = lambda i: (i, 0)
    in_specs = [
        pl.BlockSpec((tm, d), row),
        pl.BlockSpec((1, d), lambda i: (0, 0)),
        pl.BlockSpec((1, 1, N_MOD * d), mod_map),
        pl.BlockSpec((d, n), lambda i: (0, 0)),
        pl.BlockSpec((1, LANES), lambda i: (0, 0)),
        pl.BlockSpec((1, LANES), lambda i: (0, 0)),
    ]
    args = [x2d, g.reshape(1, d), mod3, w_bf16, g2(qg), g2(kg)]
    out_shape = jax.ShapeDtypeStruct((r, n), F32)
    out_specs = pl.BlockSpec((tm, n), row)
    if pending is not None:
        blk0 = row0 // tm
        seg = lambda i: (blk0 + i, 0)
        in_specs += [pl.BlockSpec((tm, d), seg), pl.BlockSpec((tm, d), seg),
                     pl.BlockSpec((tm, 2), seg), pl.BlockSpec((1, 1, N_MOD * d), mod_map)]
        args += list(pending)
        out_shape = (out_shape, jax.ShapeDtypeStruct((r, d), F32))
        out_specs = (out_specs, pl.BlockSpec((tm, d), row))
    return pl.pallas_call(
        functools.partial(_inproj_kernel, moe_pending=pending is not None),
        out_shape=out_shape,
        grid=(r // tm,),
        in_specs=in_specs,
        out_specs=out_specs,
        compiler_params=_cparams(("parallel",)),
        name="in_projection",
    )(*args)


def _pair_rms(x, g, lo):
    ss = x * x
    sa = jnp.sum(jnp.where(lo, ss, 0.0), axis=-1, keepdims=True)
    sb = jnp.sum(jnp.where(lo, 0.0, ss), axis=-1, keepdims=True)
    ms = jnp.where(lo, sa, sb) * (1.0 / HEAD_DIM)
    return x * lax.rsqrt(ms + EPS) * g


def _na_kernel(q_ref, k_ref, v_ref, kc_ref, vc_ref, bias_ref, o_ref):
    lo = lax.broadcasted_iota(jnp.int32, (1, LANES), 1) < HEAD_DIM
    n_rows = k_ref.shape[1]
    col_tiles = GRID_W // Q_COLS
    nq = Q_ROWS * Q_COLS
    nk = K_ROWS * K_COLS
    kcb = kc_ref[0].astype(BF16)
    vcb = vc_ref[0].astype(BF16)
    tiles = [(rr, j) for rr in range(NA_ROW_TILES) for j in range(col_tiles)]
    q2 = []
    for rr, j in tiles:
        qn = q_ref[0, rr * Q_ROWS:(rr + 1) * Q_ROWS, j * Q_COLS:(j + 1) * Q_COLS, :].reshape(nq, LANES)
        q2.append(jnp.concatenate([jnp.where(lo, qn, 0.0), jnp.where(lo, 0.0, qn)], axis=0).astype(BF16))
    s_cx_all = _dot_nt(jnp.concatenate(q2, axis=0), kcb)
    p_cx, o_nb, denom = [], [], []
    for t, (rr, j) in enumerate(tiles):
        i = pl.program_id(2) * NA_ROW_TILES + rr
        kr0 = jnp.clip(Q_ROWS * i - WIN_ROWS // 2, 0, n_rows - K_ROWS)
        rt = jnp.where(i == 0, 0, jnp.where(i == n_rows // Q_ROWS - 1, 2, 1))
        kc0 = min(max(Q_COLS * j - WIN_COLS // 2, 0), GRID_W - K_COLS)
        ct = 0 if j == 0 else (2 if j == col_tiles - 1 else 1)
        kw = k_ref[0, pl.ds(kr0, K_ROWS), kc0:kc0 + K_COLS, :].reshape(nk, LANES).astype(BF16)
        vw = v_ref[0, pl.ds(kr0, K_ROWS), kc0:kc0 + K_COLS, :].reshape(nk, LANES).astype(BF16)
        s_nb = _dot_nt(q2[t], kw) + bias_ref[0, 0, rt * 3 + ct].reshape(2 * nq, nk)
        s_cx = s_cx_all[t * 2 * nq:(t + 1) * 2 * nq]
        m = jnp.maximum(jnp.max(s_nb, axis=-1, keepdims=True), jnp.max(s_cx, axis=-1, keepdims=True))
        p_nb = jnp.exp2(s_nb - m)
        p_c = jnp.exp2(s_cx - m)
        denom.append(jnp.sum(p_nb, axis=-1, keepdims=True) + jnp.sum(p_c, axis=-1, keepdims=True))
        p_cx.append(p_c.astype(BF16))
        o_nb.append(_dot(p_nb.astype(BF16), vw))
    o_cx_all = _dot(jnp.concatenate(p_cx, axis=0), vcb)
    for t, (rr, j) in enumerate(tiles):
        o2 = (o_nb[t] + o_cx_all[t * 2 * nq:(t + 1) * 2 * nq]) / denom[t]
        o_ref[0, rr * Q_ROWS:(rr + 1) * Q_ROWS, j * Q_COLS:(j + 1) * Q_COLS, :] = \
            jnp.where(lo, o2[0:nq], o2[nq:2 * nq]).reshape(Q_ROWS, Q_COLS, LANES)


def _bias_table_kernel(rpb_ref, o_ref, tt_scr):
    n_off_r = 2 * WIN_ROWS - 1
    lane = lax.broadcasted_iota(jnp.int32, (Q_COLS, LANES), 1)
    qc = lax.broadcasted_iota(jnp.int32, (Q_COLS, LANES), 0)
    kc = lane % K_COLS
    lane_blk = lane // K_COLS
    per_vreg = LANES // K_COLS
    col_rel = (0, -WIN_COLS // 2, -WIN_COLS)
    col_origin = (0, Q_COLS, GRID_W - Q_COLS)
    row_rel = (0, -WIN_ROWS // 2, -WIN_ROWS)
    row_origin = (0, Q_ROWS, GRID_W - Q_ROWS)
    masked = jnp.full((Q_COLS, LANES), MASK_VALUE, F32)

    for ct in range(3):
        c_abs = col_origin[ct] + qc
        k_abs = col_origin[ct] + col_rel[ct] + kc
        start = jnp.clip(c_abs - WIN_COLS // 2, 0, GRID_W - WIN_COLS)
        valid_c = (k_abs >= start) & (k_abs < start + WIN_COLS)
        base = (1 - WIN_COLS - col_rel[ct] - RPB_LANE0) % LANES
        for ro in range(n_off_r):
            row = jnp.broadcast_to(rpb_ref[0, 0, ro:ro + 1, :], (Q_COLS, LANES))
            t = pltpu.roll(row, base, 1, stride=1, stride_axis=0)
            rep = t
            for m in range(1, per_vreg):
                rep = jnp.where(lane_blk == m, pltpu.roll(t, K_COLS * m, 1), rep)
            tt_scr[ct, ro] = jnp.where(valid_c, rep * LOG2E, MASK_VALUE)

    for rt in range(3):
        for ct in range(3):
            for qr in range(Q_ROWS):
                r_abs = row_origin[rt] + qr
                r_start = min(max(r_abs - WIN_ROWS // 2, 0), GRID_W - WIN_ROWS)
                for w in range(K_ROWS // per_vreg):
                    val = None
                    for m in range(per_vreg):
                        k_abs = row_origin[rt] + row_rel[rt] + per_vreg * w + m
                        ok = r_start <= k_abs < r_start + WIN_ROWS
                        src = tt_scr[ct, k_abs - r_abs + WIN_ROWS - 1] if ok else masked
                        val = src if val is None else jnp.where(lane_blk == m, src, val)
                    o_ref[0, 0, rt * 3 + ct, 0, qr * Q_COLS:(qr + 1) * Q_COLS, w * LANES:(w + 1) * LANES] = val


def _na_bias_tables(na_rpb):
    depth, h, n_r, n_c = na_rpb.shape
    rpb_pad = jnp.pad(na_rpb.astype(F32), ((0, 0), (0, 0), (0, 2 * SUBLANES - n_r),
                                            (RPB_LANE0, LANES - RPB_LANE0 - n_c)))
    nq, nk = Q_ROWS * Q_COLS, K_ROWS * K_COLS
    return pl.pallas_call(
        _bias_table_kernel,
        out_shape=jax.ShapeDtypeStruct((depth, h // 2, 9, 2, nq, nk), F32),
        grid=(depth, h),
        in_specs=[pl.BlockSpec((1, 1, 2 * SUBLANES, LANES), lambda l, i: (l, i, 0, 0))],
        out_specs=pl.BlockSpec((1, 1, 9, 1, nq, nk), lambda l, i: (l, i // 2, 0, i % 2, 0, 0)),
        scratch_shapes=[pltpu.VMEM((3, 2 * WIN_ROWS - 1, Q_COLS, LANES), F32)],
        compiler_params=_cparams(("parallel", "parallel")),
        name="na_bias_tables",
    )(rpb_pad)


def _neighborhood_attention(qkvu_lat, qkvu_ctx, bias_tabs, layer, bsz):
    n_lat = qkvu_lat.shape[0] // bsz
    n_ctx = qkvu_ctx.shape[0] // bsz
    rows = n_lat // GRID_W
    n_cols = qkvu_lat.shape[1]
    lat4 = qkvu_lat.reshape(bsz, rows, GRID_W, n_cols)
    ctx3 = qkvu_ctx.reshape(bsz, n_ctx, n_cols)
    n_pairs = NA_WIDTH // LANES
    step_rows = NA_ROW_TILES * Q_ROWS
    n_tiles = rows // step_rows
    out = pl.pallas_call(
        _na_kernel,
        out_shape=jax.ShapeDtypeStruct((bsz, rows, GRID_W, NA_WIDTH), F32),
        grid=(n_pairs, bsz, n_tiles),
        in_specs=[
            pl.BlockSpec((1, step_rows, GRID_W, LANES), lambda p, b, t: (b, t, 0, p)),
            pl.BlockSpec((1, rows, GRID_W, LANES), lambda p, b, t: (b, 0, 0, n_pairs + p)),
            pl.BlockSpec((1, rows, GRID_W, LANES), lambda p, b, t: (b, 0, 0, 2 * n_pairs + p)),
            pl.BlockSpec((1, n_ctx, LANES), lambda p, b, t: (b, 0, n_pairs + p)),
            pl.BlockSpec((1, n_ctx, LANES), lambda p, b, t: (b, 0, 2 * n_pairs + p)),
            pl.BlockSpec((1, 1, 9, 2, Q_ROWS * Q_COLS, K_ROWS * K_COLS), lambda p, b, t: (layer, p, 0, 0, 0, 0)),
        ],
        out_specs=pl.BlockSpec((1, step_rows, GRID_W, LANES), lambda p, b, t: (b, t, 0, p)),
        compiler_params=_cparams(("parallel", "parallel", "parallel")),
        name="neighborhood_attention",
    )(lat4, lat4, lat4, ctx3, ctx3, bias_tabs)
    return out.reshape(bsz * n_lat, NA_WIDTH)


def _ctx_attn_kernel(q_ref, k_ref, v_ref, o_ref):
    lo = lax.broadcasted_iota(jnp.int32, (1, LANES), 1) < HEAD_DIM
    qn = q_ref[0]
    kn = k_ref[0].astype(BF16)
    vb = v_ref[0].astype(BF16)

    def one(qm):
        s = _dot_nt(qm, kn)
        m = jnp.max(s, axis=-1, keepdims=True)
        p = jnp.exp2(s - m)
        l = jnp.sum(p, axis=-1, keepdims=True)
        return _dot(p.astype(BF16), vb) / l

    o_a = one(jnp.where(lo, qn, 0.0).astype(BF16))
    o_b = one(jnp.where(lo, 0.0, qn).astype(BF16))
    o_ref[0] = jnp.where(lo, o_a, o_b)


def _context_attention(qkvu_ctx, bsz):
    n_ctx = qkvu_ctx.shape[0] // bsz
    ctx3 = qkvu_ctx.reshape(bsz, n_ctx, qkvu_ctx.shape[1])
    n_pairs = NA_WIDTH // LANES
    out = pl.pallas_call(
        _ctx_attn_kernel,
        out_shape=jax.ShapeDtypeStruct((bsz, n_ctx, NA_WIDTH), F32),
        grid=(bsz, n_pairs),
        in_specs=[
            pl.BlockSpec((1, n_ctx, LANES), lambda b, p: (b, 0, p)),
            pl.BlockSpec((1, n_ctx, LANES), lambda b, p: (b, 0, n_pairs + p)),
            pl.BlockSpec((1, n_ctx, LANES), lambda b, p: (b, 0, 2 * n_pairs + p)),
        ],
        out_specs=pl.BlockSpec((1, n_ctx, LANES), lambda b, p: (b, 0, p)),
        compiler_params=_cparams(("parallel", "parallel")),
        name="context_attention",
    )(ctx3, ctx3, ctx3)
    return out.reshape(bsz * n_ctx, NA_WIDTH)


def _s5_kernel(uc_ref, ul_ref, w_ref, m_ref, v_ref, a_ref, y_ref, x_scr, s_scr, hf_scr, hr_scr, *, n_ctx_chunks):
    bsz = ul_ref.shape[0]
    n_lat_chunks = ul_ref.shape[1] // CHUNK
    n_chunks = n_ctx_chunks + n_lat_chunks
    rows = n_chunks * SUBLANES
    n_pairs = w_ref.shape[1]
    gpb = 2 * n_pairs
    half = 2 * LANES
    tile_chunks = TM_PROJ // CHUNK
    pieces = [(uc_ref, 0, n_ctx_chunks, 0)] if n_ctx_chunks else []
    pieces += [(ul_ref, c0, min(S5_PIECE, n_lat_chunks - c0), n_ctx_chunks + c0)
               for c0 in range(0, n_lat_chunks, S5_PIECE)]

    def lane_block_ids(n):
        return lax.broadcasted_iota(jnp.int32, (n, LANES), 1) // S5_CH

    def lag_bit_clear(lane_blk, g, bit):
        return ((((lane_blk - g) & (SUBLANES - 1)) >> bit) & 1) == 0

    for src_ref, c_src, n_c, c_dst in pieces:
        lane_blk = lane_block_ids(n_c)
        m0 = [lag_bit_clear(lane_blk, g, 0) for g in range(2)]
        m1 = [lag_bit_clear(lane_blk, g, 1) for g in range(4)]
        m2 = [lag_bit_clear(lane_blk, g, 2) for g in range(gpb)]
        for b in range(bsz):
            for q in range(2):
                rolled = []
                for j in range(SUBLANES):
                    s = SUBLANES * q + j
                    u_s = src_ref[b, pl.ds(c_src * CHUNK + s, n_c, stride=CHUNK), :]
                    rolled.append(u_s if j == 0 else pltpu.roll(u_s, j * S5_CH, 1))
                st1 = [[jnp.where(m0[g], rolled[2 * j], rolled[2 * j + 1]) for j in range(4)] for g in range(2)]
                st2 = [[jnp.where(m1[g], st1[g & 1][2 * j], st1[g & 1][2 * j + 1]) for j in range(2)]
                       for g in range(4)]
                for g in range(gpb):
                    xg = jnp.where(m2[g], st2[g & 3][0], st2[g & 3][1])
                    x_scr[g // 2, q, pl.ds(c_dst * SUBLANES + (g % 2) * bsz + b, n_c, stride=SUBLANES), :] = xg

    n_blk = 8
    rb = rows // n_blk
    first_group = (lax.broadcasted_iota(jnp.int32, (rb, half), 0) & (SUBLANES // 2)) == 0
    fwd_cols = (lax.broadcasted_iota(jnp.int32, (rb, half), 1) & (LANES - 1)) < S5_STATE
    is_fwd = lax.broadcasted_iota(jnp.int32, (SUBLANES, LANES), 1) < S5_STATE
    first_rows = lax.broadcasted_iota(jnp.int32, (SUBLANES, half), 0) < SUBLANES // 2
    zero = jnp.zeros((SUBLANES, LANES), F32)

    def x_rows(p, sl):
        return jnp.concatenate([x_scr[p, 0, sl, :], x_scr[p, 1, sl, :]], axis=1).astype(BF16)

    def put_cols(scr, i, sl, val):
        scr[i, 0, sl, :] = val[:, 0:LANES]
        scr[i, 1, sl, :] = val[:, LANES:half]

    for hh in range(n_pairs // PAIRS_PER_STEP):
        pairs = [hh * PAIRS_PER_STEP + i for i in range(PAIRS_PER_STEP)]
        for i, p in enumerate(pairs):
            for blk in range(n_blk):
                sl = slice(blk * rb, (blk + 1) * rb)
                r = _dot(x_rows(p, sl), w_ref[0, p])
                put_cols(s_scr, i, sl, jnp.where(first_group, r[:, :half], r[:, half:]))

        a_pair = [jnp.where(first_rows, a_ref[0, 2 * p], a_ref[0, 2 * p + 1]) for p in pairs]
        a_re = [a[:, 0:LANES] for a in a_pair]
        a_im = [a[:, LANES:half] for a in a_pair]

        def body(k, carry):
            kr = jnp.where(k < n_ctx_chunks, n_ctx_chunks - 1 - k, n_chunks + n_ctx_chunks - 1 - k)
            rf = pl.ds(pl.multiple_of(k * SUBLANES, SUBLANES), SUBLANES)
            rr = pl.ds(pl.multiple_of(kr * SUBLANES, SUBLANES), SUBLANES)
            new = []
            for i in range(PAIRS_PER_STEP):
                h_re, h_im = carry[2 * i], carry[2 * i + 1]
                hf_scr[i, 0, rf, :] = h_re
                hf_scr[i, 1, rf, :] = h_im
                hr_scr[i, 0, rr, :] = h_re
                hr_scr[i, 1, rr, :] = h_im
                s_re = jnp.where(is_fwd, s_scr[i, 0, rf, :], s_scr[i, 0, rr, :])
                s_im = jnp.where(is_fwd, s_scr[i, 1, rf, :], s_scr[i, 1, rr, :])
                new.append(a_re[i] * h_re - a_im[i] * h_im + s_re)
                new.append(a_re[i] * h_im + a_im[i] * h_re + s_im)
            return tuple(new)

        lax.fori_loop(0, n_chunks, body, (zero,) * (2 * PAIRS_PER_STEP))

        for i, p in enumerate(pairs):
            for blk in range(n_blk):
                sl = slice(blk * rb, (blk + 1) * rb)
                hf = jnp.concatenate([hf_scr[i, 0, sl, :], hf_scr[i, 1, sl, :]], axis=1)
                hr = jnp.concatenate([hr_scr[i, 0, sl, :], hr_scr[i, 1, sl, :]], axis=1)
                h_in = jnp.where(fwd_cols, hf, hr).astype(BF16)
                r = _dot(x_rows(p, sl), m_ref[0, p]) + _dot_nt(h_in, v_ref[0, p])
                put_cols(s_scr, i, sl, jnp.where(first_group, r[:, :half], r[:, half:]))

        g_lo = 2 * pairs[0]
        n_g = 2 * PAIRS_PER_STEP
        out_blk = lax.broadcasted_iota(jnp.int32, (tile_chunks, LANES), 1) // S5_CH
        lanes_out = (out_blk >= g_lo) & (out_blk < g_lo + n_g)
        assert n_g == 4
        for _, _, n_c, c_dst in pieces:
            lane_blk = lane_block_ids(n_c)
            n0 = [lag_bit_clear(lane_blk, e, 0) for e in range(2)]
            n1 = [lag_bit_clear(lane_blk, e, 1) for e in range(4)]
            for b in range(bsz):
                for q in range(2):
                    y_g = [s_scr[(g - g_lo) // 2, q,
                                 pl.ds(c_dst * SUBLANES + (g % 2) * bsz + b, n_c, stride=SUBLANES), :]
                           for g in range(g_lo, g_lo + n_g)]
                    d1 = [[jnp.where(n0[e], y_g[2 * i], y_g[2 * i + 1]) for i in range(2)] for e in range(2)]
                    merged = [jnp.where(n1[e], d1[e & 1][0], d1[e & 1][1]) for e in range(4)]
                    for j in range(SUBLANES):
                        z = merged[(j + g_lo) % 4]
                        if j:
                            z = pltpu.roll(z, LANES - j * S5_CH, 1)
                        t = SUBLANES * q + j
                        for ct in range(n_c // tile_chunks):
                            tile = (c_dst // tile_chunks + ct - n_ctx_chunks // tile_chunks) % (n_chunks // tile_chunks)
                            r0 = tile * TM_PROJ + t * tile_chunks
                            pltpu.store(y_ref.at[b, 0, r0:r0 + tile_chunks, :],
                                        z[ct * tile_chunks:(ct + 1) * tile_chunks, :], mask=lanes_out)


def _s5_scan(qkvu_ctx3, qkvu_lat3, mats, layer):
    w_c, m_c, vt_c, a_c = mats
    bsz, n_ctx, _ = qkvu_ctx3.shape
    n_lat = qkvu_lat3.shape[1]
    n_seq = n_ctx + n_lat
    rows = n_seq // CHUNK * SUBLANES
    n_blocks = S5_WIDTH // LANES
    ppb = w_c.shape[1] // n_blocks
    u_blk0 = 3 * NA_WIDTH // LANES
    wspec = pl.BlockSpec((1, ppb, 2 * LANES, 4 * LANES), lambda i: (layer, i, 0, 0))
    one = pl.Buffered(1)
    state = pltpu.VMEM((PAIRS_PER_STEP, 2, rows, LANES), F32)
    return pl.pallas_call(
        functools.partial(_s5_kernel, n_ctx_chunks=n_ctx // CHUNK),
        out_shape=jax.ShapeDtypeStruct((bsz, n_blocks, n_seq, LANES), F32),
        grid=(n_blocks,),
        in_specs=[
            pl.BlockSpec((bsz, n_ctx, LANES), lambda i: (0, 0, u_blk0 + i)),
            pl.BlockSpec((bsz, n_lat, LANES), lambda i: (0, 0, u_blk0 + i), pipeline_mode=one),
            wspec, wspec,
            pl.BlockSpec((1, ppb, 4 * LANES, 2 * LANES), lambda i: (layer, i, 0, 0)),
            pl.BlockSpec((1, 2 * ppb, SUBLANES, 2 * LANES), lambda i: (layer, i, 0, 0)),
        ],
        out_specs=pl.BlockSpec((bsz, 1, n_seq, LANES), lambda i: (0, i, 0, 0), pipeline_mode=one),
        scratch_shapes=[pltpu.VMEM((ppb, 2, rows, LANES), F32), state, state, state],
        compiler_params=_cparams(("parallel",)),
        name="s5_scan",
    )(qkvu_ctx3, qkvu_lat3, w_c, m_c, vt_c, a_c)


def _s5_mats_kernel(prm_ref, btr_ref, bti_ref, cr_ref, ci_ref, w_ref, m_ref, vt_ref, a_ref):
    t = CHUNK
    gl = pl.program_id(1) % SUBLANES
    is_fwd = lax.broadcasted_iota(jnp.int32, (1, LANES), 1) < S5_STATE
    lr = prm_ref[0, 0, 0:1, :]
    li = prm_ref[0, 0, 1:2, :]
    dt = jnp.exp(prm_ref[0, 0, 2:3, :])
    n = lax.broadcasted_iota(jnp.int32, (3 * SUBLANES, LANES), 0).astype(F32)
    pmag = jnp.exp(n * (lr * dt))
    pw_re = pmag * jnp.cos(n * (li * dt))
    pw_im = pmag * jnp.sin(n * (li * dt))
    ab_re, ab_im = pw_re[1:2, :], pw_im[1:2, :]
    den = lr * lr + li * li
    nr = ab_re - 1.0
    z_re = (nr * lr + ab_im * li) / den
    z_im = (ab_im * lr - nr * li) / den
    bt_re, bt_im = btr_ref[0, 0], bti_ref[0, 0]
    bb_re = z_re * bt_re - z_im * bt_im
    bb_im = z_re * bt_im + z_im * bt_re
    c_re, c_im = cr_ref[0, 0], ci_ref[0, 0]

    def powers(n_fwd, n_rev):
        return (jnp.where(is_fwd, pw_re[n_fwd:n_fwd + 1, :], pw_re[n_rev:n_rev + 1, :]),
                jnp.where(is_fwd, pw_im[n_fwd:n_fwd + 1, :], pw_im[n_rev:n_rev + 1, :]))

    def block_rows(s):
        pos = SUBLANES * (s // SUBLANES) + (s % SUBLANES + gl) % SUBLANES
        return pl.ds(pl.multiple_of(pos * S5_CH, S5_CH), S5_CH)

    for s in range(t):
        rows = block_rows(s)
        p_re, p_im = powers(t - 1 - s, s)
        w_ref[0, 0, rows, 0:LANES] = (bb_re * p_re - bb_im * p_im).astype(BF16)
        w_ref[0, 0, rows, LANES:2 * LANES] = (bb_re * p_im + bb_im * p_re).astype(BF16)
        q_re, q_im = powers(s + 1, t - s)
        vt_ref[0, 0, rows, 0:LANES] = (c_re * q_re - c_im * q_im).astype(BF16)
        vt_ref[0, 0, rows, LANES:2 * LANES] = (-(c_re * q_im + c_im * q_re)).astype(BF16)

    ca_re, ca_im = [], []
    for lag in range(t):
        p_re, p_im = powers(lag, t - 1 - lag)
        ca_re.append(c_re * p_re - c_im * p_im)
        ca_im.append(c_re * p_im + c_im * p_re)
    stack = jnp.concatenate([jnp.concatenate(ca_re, axis=0), jnp.concatenate(ca_im, axis=0)], axis=1)
    zero = jnp.zeros_like(bb_re)
    lhs = jnp.concatenate([
        jnp.concatenate([jnp.where(is_fwd, bb_re, zero), jnp.where(is_fwd, -bb_im, zero)], axis=1),
        jnp.concatenate([jnp.where(is_fwd, zero, bb_re), jnp.where(is_fwd, zero, -bb_im)], axis=1)], axis=0)
    kt = lax.dot_general(lhs, stack, (((1,), (1,)), ((), ())), precision=HIGHEST, preferred_element_type=F32)
    kt_f, kt_r = kt[0:S5_CH], kt[S5_CH:2 * S5_CH]
    blk = lax.broadcasted_iota(jnp.int32, (S5_CH, 2 * LANES), 1) // S5_CH
    for s in range(t):
        strip = (jnp.where(blk >= s, pltpu.roll(kt_f, S5_CH * s, 1), 0.0)
                 + jnp.where(blk <= s, pltpu.roll(kt_r, (S5_CH * (s - t + 1)) % (2 * LANES), 1), 0.0))
        strip = jnp.concatenate([pltpu.roll(strip[:, 0:LANES], gl * S5_CH, 1),
                                 pltpu.roll(strip[:, LANES:2 * LANES], gl * S5_CH, 1)], axis=1)
        m_ref[0, 0, block_rows(s), :] = strip.astype(BF16)

    a_ref[0, 0, :, 0:LANES] = jnp.broadcast_to(pw_re[t:t + 1, :], (SUBLANES, LANES))
    a_ref[0, 0, :, LANES:2 * LANES] = jnp.broadcast_to(pw_im[t:t + 1, :], (SUBLANES, LANES))


def _s5_matrices(lam_re, lam_im, log_dt, b_re, b_im, c_re, c_im):
    depth, _, g, p = lam_re.shape
    hc = b_re.shape[-1]
    width = CHUNK * hc
    both = lambda x: jnp.transpose(x.astype(F32), (0, 2, 1, 3)).reshape(depth, g, 1, 2 * p)
    dt_rows = jnp.broadcast_to(jnp.transpose(log_dt.astype(F32), (0, 2, 1))[..., None], (depth, g, 2, p))
    prm = jnp.concatenate([both(lam_re), both(lam_im), dt_rows.reshape(depth, g, 1, 2 * p),
                           jnp.zeros((depth, g, SUBLANES - 3, 2 * p), F32)], axis=2)
    bt = lambda x: jnp.transpose(x.astype(F32), (0, 2, 4, 1, 3)).reshape(depth, g, hc, 2 * p)
    ct = lambda x: jnp.transpose(x.astype(F32), (0, 2, 3, 1, 4)).reshape(depth, g, hc, 2 * p)
    vec = lambda rows: pl.BlockSpec((1, 1, rows, 2 * p), lambda l, i: (l, i, 0, 0))
    return pl.pallas_call(
        _s5_mats_kernel,
        out_shape=(jax.ShapeDtypeStruct((depth, g // 2, width, 2 * width), BF16),
                   jax.ShapeDtypeStruct((depth, g // 2, width, 2 * width), BF16),
                   jax.ShapeDtypeStruct((depth, g // 2, 2 * width, width), BF16),
                   jax.ShapeDtypeStruct((depth, g, SUBLANES, 4 * p), F32)),
        grid=(depth, g),
        in_specs=[vec(SUBLANES), vec(hc), vec(hc), vec(hc), vec(hc)],
        out_specs=(pl.BlockSpec((1, 1, width, width), lambda l, i: (l, i // 2, 0, i % 2)),
                   pl.BlockSpec((1, 1, width, width), lambda l, i: (l, i // 2, 0, i % 2)),
                   pl.BlockSpec((1, 1, width, width), lambda l, i: (l, i // 2, i % 2, 0)),
                   pl.BlockSpec((1, 1, SUBLANES, 4 * p), lambda l, i: (l, i, 0, 0))),
        compiler_params=_cparams(("parallel", "parallel")),
        name="s5_matrices",
    )(prm, bt(b_re), bt(b_im), ct(c_re), ct(c_im))


def _s5_mixer(qkvu_lat, qkvu_ctx, mats, layer, bsz):
    n_cols = qkvu_lat.shape[1]
    return _s5_scan(qkvu_ctx.reshape(bsz, -1, n_cols), qkvu_lat.reshape(bsz, -1, n_cols), mats, layer)


def _outproj_kernel(na_ref, y_ref, u_ref, h_ref, mod_ref, d_ref, wglu_ref, bglu_ref, wout_ref, g2_ref,
                    ho_ref, f_ref):
    tile_chunks = TM_PROJ // CHUNK
    y = jnp.concatenate(
        [jnp.concatenate([y_ref[0, blk, pl.ds(sub * TM_PROJ + c, CHUNK, stride=tile_chunks), :]
                          for sub in range(y_ref.shape[2] // TM_PROJ) for c in range(tile_chunks)], axis=0)
         for blk in range(S5_WIDTH // LANES)], axis=1)
    z = jax.nn.gelu(y + d_ref[...] * u_ref[...])
    s5 = z * jax.nn.sigmoid(_dot(z.astype(BF16), wglu_ref[...]) + bglu_ref[...])
    mix = (_dot(na_ref[...].astype(BF16), wout_ref[0:NA_WIDTH, :])
           + _dot(s5.astype(BF16), wout_ref[NA_WIDTH:NA_WIDTH + S5_WIDTH, :]))
    d = D_MODEL
    gate = mod_ref[0, :, 2 * d:3 * d]
    h = h_ref[...] + gate * mix
    ho_ref[...] = h
    ms = jnp.mean(h * h, axis=-1, keepdims=True)
    y2 = h * lax.rsqrt(ms + EPS) * g2_ref[...]
    f_ref[...] = y2 * (1.0 + mod_ref[0, :, 4 * d:5 * d]) + mod_ref[0, :, 3 * d:4 * d]


def _out_projection(na, y_all, qkvu, h2d, mod3, d_skip, wglu_bf16, b_glu, wout_bf16, g2,
                    mod_map, rows_per_batch, y_row0, tm=TM_PROJ):
    r, d = h2d.shape
    tiles_per_batch = rows_per_batch // tm
    u_blk = 3 * NA_WIDTH // S5_WIDTH
    y_block0 = y_row0 // tm

    def y_map(i):
        return (i // tiles_per_batch, 0, y_block0 + i % tiles_per_batch, 0)

    const = lambda i: (0, 0)
    return pl.pallas_call(
        _outproj_kernel,
        out_shape=(jax.ShapeDtypeStruct((r, d), F32), jax.ShapeDtypeStruct((r, d), F32)),
        grid=(r // tm,),
        in_specs=[
            pl.BlockSpec((tm, NA_WIDTH), lambda i: (i, 0)),
            pl.BlockSpec((1, S5_WIDTH // LANES, tm, LANES), y_map),
            pl.BlockSpec((tm, S5_WIDTH), lambda i: (i, u_blk)),
            pl.BlockSpec((tm, d), lambda i: (i, 0)),
            pl.BlockSpec((1, 1, N_MOD * d), mod_map),
            pl.BlockSpec((1, S5_WIDTH), const),
            pl.BlockSpec((S5_WIDTH, S5_WIDTH), const),
            pl.BlockSpec((1, S5_WIDTH), const),
            pl.BlockSpec((NA_WIDTH + S5_WIDTH, d), const),
            pl.BlockSpec((1, d), const),
        ],
        out_specs=(pl.BlockSpec((tm, d), lambda i: (i, 0)), pl.BlockSpec((tm, d), lambda i: (i, 0))),
        compiler_params=_cparams(("parallel",)),
        name="out_projection",
    )(na, y_all, qkvu, h2d, mod3, d_skip.reshape(1, -1), wglu_bf16, b_glu.reshape(1, -1), wout_bf16,
      g2.reshape(1, d))


def _top2(vals):
    best = vals[0]
    bi = jnp.zeros(best.shape, jnp.int32)
    for i in range(1, len(vals)):
        gt = vals[i] > best
        best = jnp.where(gt, vals[i], best)
        bi = jnp.where(gt, i, bi)
    second = jnp.full(best.shape, -jnp.inf, F32)
    si = jnp.zeros(best.shape, jnp.int32)
    for i in range(len(vals)):
        cand = jnp.where(bi == i, -jnp.inf, vals[i])
        gt = cand > second
        second = jnp.where(gt, cand, second)
        si = jnp.where(gt, i, si)
    return best, bi, second, si


def _route(f, rwt_hi, rwt_lo, rb):
    f_hi = f.astype(BF16)
    f_lo = (f - f_hi.astype(F32)).astype(BF16)
    logits = _dot_nt(rwt_hi, f_hi) + (_dot_nt(rwt_hi, f_lo) + _dot_nt(rwt_lo, f_hi))
    m = jnp.max(logits, axis=0, keepdims=True)
    e = jnp.exp(logits - m)
    probs = e / jnp.sum(e, axis=0, keepdims=True)
    sel = probs + rb
    sel_rows = [sel[i:i + 1, :] for i in range(N_EXPERTS)]
    prob_rows = [probs[i:i + 1, :] for i in range(N_EXPERTS)]
    scores = []
    for g in range(N_GROUPS):
        b, _, s, _ = _top2(sel_rows[g * EPG:(g + 1) * EPG])
        scores.append(b + s)
    grp = jnp.zeros(scores[0].shape, jnp.int32)
    gbest = scores[0]
    for g in range(1, N_GROUPS):
        gt = scores[g] > gbest
        gbest = jnp.where(gt, scores[g], gbest)
        grp = jnp.where(gt, g, grp)
    in_rows = []
    for j in range(EPG):
        v = sel_rows[j]
        for g in range(1, N_GROUPS):
            v = jnp.where(grp == g, sel_rows[g * EPG + j], v)
        in_rows.append(v)
    _, l1, _, l2 = _top2(in_rows)
    i1 = grp * EPG + l1
    i2 = grp * EPG + l2
    w1 = jnp.zeros(gbest.shape, F32)
    w2 = jnp.zeros(gbest.shape, F32)
    for i in range(N_EXPERTS):
        w1 = jnp.where(i1 == i, prob_rows[i], w1)
        w2 = jnp.where(i2 == i, prob_rows[i], w2)
    tot = w1 + w2
    return i1, i2, w1 / tot, w2 / tot


def _router_kernel(f_ref, rwh_ref, rwl_ref, rb_ref, idx_ref, gate_ref):
    i1, i2, g1, g2 = _route(f_ref[...], rwh_ref[...], rwl_ref[...], rb_ref[...])
    idx_ref[0:1, :] = i1
    idx_ref[1:2, :] = i2
    gate_ref[0:1, :] = g1
    gate_ref[1:2, :] = g2


def _router(f_all, router_wt, router_b):
    n, d = f_all.shape
    rwt_hi = router_wt.astype(BF16)
    rwt_lo = (router_wt - rwt_hi.astype(F32)).astype(BF16)
    return pl.pallas_call(
        _router_kernel,
        out_shape=(jax.ShapeDtypeStruct((2, n), jnp.int32), jax.ShapeDtypeStruct((2, n), F32)),
        grid=(n // TM_ROUTE,),
        in_specs=[
            pl.BlockSpec((TM_ROUTE, d), lambda i: (i, 0)),
            pl.BlockSpec((N_EXPERTS, d), lambda i: (0, 0)),
            pl.BlockSpec((N_EXPERTS, d), lambda i: (0, 0)),
            pl.BlockSpec((N_EXPERTS, 1), lambda i: (0, 0)),
        ],
        out_specs=(pl.BlockSpec((2, TM_ROUTE), lambda i: (0, i)), pl.BlockSpec((2, TM_ROUTE), lambda i: (0, i))),
        compiler_params=_cparams(("parallel",)),
        name="router",
    )(f_all, rwt_hi, rwt_lo, router_b)


def _scatter_rows_kernel(dest_ref, pad_ref, end_ref, *refs, n_tok, seg_tiles):
    f_refs = refs[:len(seg_tiles)]
    xs_ref, zero_scr, stage, sems = refs[len(seg_tiles):]
    i = pl.program_id(0)
    tm = f_refs[0].shape[0]
    d_blocks = f_refs[0].shape[1] // LANES
    n_rows = xs_ref.shape[0] // d_blocks
    fill_sem = sems.at[2]

    def token_rows(ref, first, n):
        start = first * d_blocks
        if not isinstance(first, int):
            start = pl.multiple_of(start, d_blocks)
        return ref.at[pl.ds(start, n * d_blocks), :]

    def slab_copy(start):
        return pltpu.make_async_copy(zero_scr, token_rows(xs_ref, start, TM_EXP), fill_sem)

    @pl.when(i == 0)
    def _():
        zero_scr[...] = jnp.zeros(zero_scr.shape, zero_scr.dtype)
        for e in range(N_EXPERTS):
            slab_copy(jnp.minimum(pad_ref[e], n_rows - TM_EXP)).start()
        for e in range(N_EXPERTS):
            slab_copy(0).wait()
        for k in range(N_EXPERTS):
            start = end_ref[0] + k * TM_EXP

            @pl.when(start < n_rows)
            def _():
                cp = slab_copy(start)
                cp.start()
                cp.wait()

    slot = i % 2

    def wait_tile(s):
        for _ in range(2):
            pltpu.make_async_copy(stage.at[s], token_rows(xs_ref, 0, tm), sems.at[s]).wait()

    def scatter_tile(f_ref):
        base = i * tm
        for k in range(d_blocks):
            stage[slot, pl.ds(k, tm, stride=d_blocks), :] = f_ref[:, k * LANES:(k + 1) * LANES]

        def row_copy(r, d):
            return pltpu.make_async_copy(token_rows(stage.at[slot], r, 1), token_rows(xs_ref, d, 1), sems.at[slot])

        for r in range(tm):
            row_copy(r, dest_ref[base + r]).start(priority=0)
            row_copy(r, dest_ref[n_tok + base + r]).start(priority=1)

    tile0 = 0
    for f_ref, n_t in zip(f_refs, seg_tiles):
        pl.when((i >= tile0) & (i < tile0 + n_t))(functools.partial(scatter_tile, f_ref))
        tile0 += n_t

    pl.when(i > 0)(lambda: wait_tile(1 - slot))
    pl.when(i == pl.num_programs(0) - 1)(lambda: wait_tile(slot))


def _scatter_rows(segments, dest_flat, pad_start, total_end):
    d = segments[0].shape[1]
    seg_tiles = tuple(s.shape[0] // TM_PROJ for s in segments)
    n_tok = sum(s.shape[0] for s in segments)
    r_max = 2 * n_tok + N_EXPERTS * TM_EXP
    in_specs = []
    tile0 = 0
    for n_t in seg_tiles:
        in_specs.append(pl.BlockSpec(
            (TM_PROJ, d), lambda i, *_, t0=tile0, nt=n_t: (jnp.clip(i - t0, 0, nt - 1), 0)))
        tile0 += n_t
    grid_spec = pltpu.PrefetchScalarGridSpec(
        num_scalar_prefetch=3,
        grid=(tile0,),
        in_specs=in_specs,
        out_specs=pl.BlockSpec(memory_space=pl.ANY),
        scratch_shapes=[pltpu.VMEM((TM_EXP * d // LANES, LANES), F32),
                        pltpu.VMEM((2, TM_PROJ * d // LANES, LANES), F32),
                        pltpu.SemaphoreType.DMA((3,))],
    )
    return pl.pallas_call(
        functools.partial(_scatter_rows_kernel, n_tok=n_tok, seg_tiles=seg_tiles),
        out_shape=jax.ShapeDtypeStruct((r_max * d // LANES, LANES), F32),
        grid_spec=grid_spec,
        compiler_params=_cparams(("arbitrary",)),
        name="moe_scatter_rows",
    )(dest_flat, pad_start, total_end, *segments)


def _experts_kernel(te_ref, nv_ref, x_ref, wg_ref, wu_ref, wd_ref, o_ref, wg_scr, wu_scr, wd_scr):
    i = pl.program_id(0)
    e = te_ref[i]
    prev = te_ref[jnp.maximum(i - 1, 0)]
    rows = 128

    @pl.when((i == 0) | (e != prev))
    def _():
        def body(r, carry):
            sl = pl.ds(pl.multiple_of(r * rows, rows), rows)
            wg_scr[sl, :] = wg_ref[0, 0, sl, :].astype(BF16)
            wu_scr[sl, :] = wu_ref[0, 0, sl, :].astype(BF16)
            wd_scr[sl, :] = wd_ref[0, 0, sl, :].astype(BF16)
            return carry
        lax.fori_loop(0, wg_scr.shape[0] // rows, body, 0)

    @pl.when(i < nv_ref[0])
    def _():
        d_blocks = wg_scr.shape[0] // LANES
        x = jnp.concatenate([x_ref[pl.ds(k, TM_EXP, stride=d_blocks), :].astype(BF16) for k in range(d_blocks)],
                            axis=1)
        g = _dot(x, wg_scr[...])
        u = _dot(x, wu_scr[...])
        a = (g * jax.nn.sigmoid(g)) * u
        o_ref[...] = _dot(a.astype(BF16), wd_scr[...]).astype(BF16)

    @pl.when(i >= nv_ref[0])
    def _():
        o_ref[...] = jnp.zeros(o_ref.shape, BF16)


def _experts(xs, tile_expert, n_valid, w_gate, w_up, w_down, layer):
    d, de = w_gate.shape[2], w_gate.shape[3]
    d_blocks = d // LANES
    r = xs.shape[0] // d_blocks
    n_tiles = r // TM_EXP
    x_map = lambda i, te, nv: (jnp.minimum(i, nv[0] - 1), 0)
    grid_spec = pltpu.PrefetchScalarGridSpec(
        num_scalar_prefetch=2,
        grid=(n_tiles,),
        in_specs=[
            pl.BlockSpec((TM_EXP * d_blocks, LANES), x_map),
            pl.BlockSpec((1, 1, d, de), lambda i, te, nv: (layer, te[i], 0, 0)),
            pl.BlockSpec((1, 1, d, de), lambda i, te, nv: (layer, te[i], 0, 0)),
            pl.BlockSpec((1, 1, de, d), lambda i, te, nv: (layer, te[i], 0, 0)),
        ],
        out_specs=pl.BlockSpec((TM_EXP, d), lambda i, te, nv: (i, 0)),
        scratch_shapes=[pltpu.VMEM((d, de), BF16), pltpu.VMEM((d, de), BF16), pltpu.VMEM((de, d), BF16)],
    )
    return pl.pallas_call(
        _experts_kernel,
        out_shape=jax.ShapeDtypeStruct((r, d), BF16),
        grid_spec=grid_spec,
        compiler_params=_cparams(("arbitrary",)),
        name="experts",
    )(tile_expert, n_valid, xs, w_gate, w_up, w_down)


def _dispatch(idx):
    n = idx.shape[1]
    e_flat = idx.reshape(-1)
    onehot = (e_flat[:, None] == jnp.arange(N_EXPERTS, dtype=jnp.int32)[None, :]).astype(jnp.int32)
    csum = jnp.cumsum(onehot, axis=0)
    rank = jnp.sum(csum * onehot, axis=1) - 1
    counts = csum[-1]
    padded = ((counts + TM_EXP - 1) // TM_EXP) * TM_EXP
    ends = jnp.cumsum(padded)
    starts = ends - padded
    dest = (jnp.sum(onehot * starts[None, :], axis=1) + rank).astype(jnp.int32)
    r_max = 2 * n + N_EXPERTS * TM_EXP
    tile_start = jnp.arange(r_max // TM_EXP, dtype=jnp.int32) * TM_EXP
    tile_expert = jnp.minimum(jnp.sum((tile_start[:, None] >= ends[None, :]).astype(jnp.int32), axis=1),
                              N_EXPERTS - 1).astype(jnp.int32)
    n_valid = (ends[-1] // TM_EXP).astype(jnp.int32).reshape(1)
    pad_start = (starts + counts).astype(jnp.int32)
    total_end = ends[-1].astype(jnp.int32).reshape(1)
    return dest, pad_start, total_end, tile_expert, n_valid


def _combine_kernel(h_ref, y1_ref, y2_ref, gate_ref, mod_ref, o_ref):
    d = D_MODEL
    g = gate_ref[...]
    y = g[:, 0:1] * y1_ref[...].astype(F32) + g[:, 1:2] * y2_ref[...].astype(F32)
    o_ref[...] = h_ref[...] + mod_ref[0, :, 5 * d:6 * d] * y


def _combine(h2d, y1, y2, gates_t, mod3, mod_map, row0, tm=TM_PROJ):
    r, d = h2d.shape
    blk0 = row0 // tm
    row = lambda i: (i, 0)
    seg = lambda i: (blk0 + i, 0)
    return pl.pallas_call(
        _combine_kernel,
        out_shape=jax.ShapeDtypeStruct((r, d), F32),
        grid=(r // tm,),
        in_specs=[
            pl.BlockSpec((tm, d), row),
            pl.BlockSpec((tm, d), seg),
            pl.BlockSpec((tm, d), seg),
            pl.BlockSpec((tm, 2), seg),
            pl.BlockSpec((1, 1, N_MOD * d), mod_map),
        ],
        out_specs=pl.BlockSpec((tm, d), row),
        compiler_params=_cparams(("parallel",)),
        name="moe_combine",
    )(h2d, y1, y2, gates_t, mod3)


def kernel(x, c, ctx, c_ctx, w_mod, b_mod, norm1_g, norm2_g, w_in, w_out, q_norm_g, k_norm_g, na_rpb,
           s5_lam_re, s5_lam_im, s5_log_dt, s5_b_re, s5_b_im, s5_c_re, s5_c_im, s5_d, s5_w_glu, s5_b_glu,
           router_w, router_bias, moe_w_gate, moe_w_up, moe_w_down):
    bsz, n_lat, d = x.shape
    n_ctx = ctx.shape[1]
    depth = w_mod.shape[0]
    ctx_row = bsz
    c_rows = jnp.concatenate([c.astype(F32), c_ctx.astype(F32)[None],
                              jnp.zeros((SUBLANES - bsz - 1, d), F32)], axis=0)
    mod_all = _modulation(c_rows, w_mod.astype(F32), b_mod.astype(F32))

    h_lat = x.reshape(bsz * n_lat, d).astype(F32)
    h_ctx = ctx.reshape(bsz * n_ctx, d).astype(F32)
    lat_map = _mod_row_map(n_lat, 0, True)
    lat_map_in = _mod_row_map(n_lat, 0, True, TM_IN)
    ctx_map = _mod_row_map(n_ctx, ctx_row, False)
    bias_tabs = _na_bias_tables(na_rpb)
    router_wt = router_w.T.astype(F32)
    router_b = router_bias.reshape(N_EXPERTS, 1).astype(F32)
    s5_mats = _s5_matrices(s5_lam_re, s5_lam_im, s5_log_dt, s5_b_re, s5_b_im, s5_c_re, s5_c_im)

    n_l = bsz * n_lat
    pending = None
    for layer in range(depth):
        ctx_out = layer < depth - 1
        mod3 = mod_all[layer].reshape(SUBLANES, 1, N_MOD * d)
        w_in_b = w_in[layer].astype(BF16)
        proj = functools.partial(_in_projection, g=norm1_g[layer], mod3=mod3, w_bf16=w_in_b,
                                 qg=q_norm_g[layer], kg=k_norm_g[layer])
        if pending is None:
            qkvu_lat = proj(h_lat, mod_map=lat_map_in, tm=TM_IN)
            qkvu_ctx = proj(h_ctx, mod_map=ctx_map)
        else:
            qkvu_lat, h_lat = proj(h_lat, mod_map=lat_map_in, pending=pending, row0=0, tm=TM_IN)
            qkvu_ctx, h_ctx = proj(h_ctx, mod_map=ctx_map, pending=pending, row0=n_l)
        na_lat = _neighborhood_attention(qkvu_lat, qkvu_ctx, bias_tabs, layer, bsz)
        y_all = _s5_mixer(qkvu_lat, qkvu_ctx, s5_mats, layer, bsz)
        wglu_b = s5_w_glu[layer].astype(BF16)
        wout_b = w_out[layer].astype(BF16)
        h_lat, f_lat = _out_projection(na_lat, y_all, qkvu_lat, h_lat, mod3, s5_d[layer], wglu_b,
                                       s5_b_glu[layer], wout_b, norm2_g[layer],
                                       lat_map_in, n_lat, 0, tm=TM_IN)
        idx, gates = _router(f_lat, router_wt, router_b)
        if ctx_out:
            na_ctx = _context_attention(qkvu_ctx, bsz)
            h_ctx, f_ctx = _out_projection(na_ctx, y_all, qkvu_ctx, h_ctx, mod3, s5_d[layer], wglu_b,
                                           s5_b_glu[layer], wout_b, norm2_g[layer],
                                           ctx_map, n_ctx, n_lat)
            idx_c, gates_c = _router(f_ctx, router_wt, router_b)
            idx = jnp.concatenate([idx, idx_c], axis=1)
            gates = jnp.concatenate([gates, gates_c], axis=1)
        n_tok = idx.shape[1]
        dest, pad_start, total_end, tile_expert, n_valid = _dispatch(idx)
        xs = _scatter_rows([f_lat, f_ctx] if ctx_out else [f_lat], dest, pad_start, total_end)
        ys = _experts(xs, tile_expert, n_valid, moe_w_gate, moe_w_up, moe_w_down, layer)
        y1 = jnp.take(ys, dest[:n_tok], axis=0, mode="clip")
        y2 = jnp.take(ys, dest[n_tok:], axis=0, mode="clip")
        gates_t = gates.T
        if ctx_out:
            pending = (y1, y2, gates_t, mod3)
        else:
            h_lat = _combine(h_lat, y1, y2, gates_t, mod3, lat_map_in, 0, tm=TM_IN)
    return h_lat.reshape(bsz, n_lat, d).astype(x.dtype)
```

```python
import functools
import math

import jax
import jax.numpy as jnp
from jax import lax
from jax.experimental import pallas as pl
from jax.experimental.pallas import tpu as pltpu

F32 = jnp.float32
BF16 = jnp.bfloat16
HIGHEST = lax.Precision.HIGHEST

D_MODEL = 1024
GRID_W = 64
HEAD_DIM = 64
NA_WIDTH = 512
S5_WIDTH = 512
S5_CH = 16
S5_GROUPS = 32
S5_STATE = 64
WIN_ROWS = 8
WIN_COLS = 16
N_EXPERTS = 16
N_GROUPS = 4
EPG = 4
N_MOD = 6
EPS = 1e-6

LANES = 128
SUBLANES = 8
VMEM_LIMIT = 56 * 1024 * 1024

TM_PROJ = 256
TM_IN = 512
Q_ROWS = 8
NA_ROW_TILES = 4
Q_COLS = 16
K_ROWS = 16
K_COLS = 32
CHUNK = 16
PAIRS_PER_STEP = 2
S5_PIECE = 32
S5_MATS_GROUPS = 4
TM_EXP = 512
TM_ROUTE = 1024
MASK_VALUE = -1e30
LOG2E = 1.4426950408889634
RPB_LANE0 = 48


def _cparams(sem):
    return pltpu.CompilerParams(dimension_semantics=sem, vmem_limit_bytes=VMEM_LIMIT)


def _dot(a, b):
    return jnp.dot(a, b, preferred_element_type=F32)


def _dot_nt(a, b):
    return lax.dot_general(a, b, (((1,), (1,)), ((), ())), preferred_element_type=F32)


def _mod_kernel(c_ref, w_ref, b_ref, o_ref):
    a = c_ref[...]
    a = a * jax.nn.sigmoid(a)
    w = w_ref[0]
    a_hi, w_hi = a.astype(BF16), w.astype(BF16)
    a_lo = (a - a_hi.astype(F32)).astype(BF16)
    w_lo = (w - w_hi.astype(F32)).astype(BF16)
    o_ref[0] = _dot(a_hi, w_hi) + (_dot(a_lo, w_hi) + _dot(a_hi, w_lo)) + b_ref[0]


def _modulation(c_rows, w_mod, b_mod):
    depth, d, n = w_mod.shape
    tn = 1536
    return pl.pallas_call(
        _mod_kernel,
        out_shape=jax.ShapeDtypeStruct((depth, SUBLANES, n), F32),
        grid=(depth, n // tn),
        in_specs=[
            pl.BlockSpec((SUBLANES, d), lambda l, j: (0, 0)),
            pl.BlockSpec((1, d, tn), lambda l, j: (l, 0, j)),
            pl.BlockSpec((1, 1, tn), lambda l, j: (l, 0, j)),
        ],
        out_specs=pl.BlockSpec((1, SUBLANES, tn), lambda l, j: (l, 0, j)),
        compiler_params=_cparams(("arbitrary", "arbitrary")),
        name="modulation",
    )(c_rows, w_mod, b_mod.reshape(depth, 1, n))


def _inproj_kernel(x_ref, g_ref, mod_ref, w_ref, qg_ref, kg_ref, *rest, moe_pending):
    x = x_ref[...]
    if moe_pending:
        y1_ref, y2_ref, gate_ref, modp_ref, o_ref, xo_ref = rest
        gw = gate_ref[...]
        y = gw[:, 0:1] * y1_ref[...].astype(F32) + gw[:, 1:2] * y2_ref[...].astype(F32)
        x = x + modp_ref[0, :, 5 * D_MODEL:6 * D_MODEL] * y
        xo_ref[...] = x
    else:
        (o_ref,) = rest
    ms = jnp.mean(x * x, axis=-1, keepdims=True)
    y = x * lax.rsqrt(ms + EPS) * g_ref[...]
    shift = mod_ref[0, :, 0:D_MODEL]
    scale = mod_ref[0, :, D_MODEL:2 * D_MODEL]
    a = y * (1.0 + scale) + shift
    acc = _dot(a.astype(BF16), w_ref[...])
    lo = lax.broadcasted_iota(jnp.int32, (1, LANES), 1) < HEAD_DIM
    n_pairs = NA_WIDTH // LANES
    for blk in range(2 * n_pairs):
        cols = slice(blk * LANES, (blk + 1) * LANES)
        if blk < n_pairs:
            o_ref[:, cols] = _pair_rms(acc[:, cols], qg_ref[...], lo) * (HEAD_DIM ** -0.5 * LOG2E)
        else:
            o_ref[:, cols] = _pair_rms(acc[:, cols], kg_ref[...], lo)
    o_ref[:, 2 * NA_WIDTH:] = acc[:, 2 * NA_WIDTH:]


def _mod_row_map(rows_per_batch, mod_row0, per_batch, tm=TM_PROJ):
    tiles_per_batch = rows_per_batch // tm
    if per_batch:
        return lambda i: (mod_row0 + i // tiles_per_batch, 0, 0)
    return lambda i: (mod_row0, 0, 0)


def _in_projection(x2d, g, mod3, w_bf16, mod_map, qg, kg, pending=None, row0=0, tm=TM_PROJ):
    r, d = x2d.shape
    n = w_bf16.shape[1]
    g2 = lambda v: jnp.concatenate([v, v]).reshape(1, LANES).astype(F32)
    row = lambda i: (i, 0)
    in_specs = [
        pl.BlockSpec((tm, d), row),
        pl.BlockSpec((1, d), lambda i: (0, 0)),
        pl.BlockSpec((1, 1, N_MOD * d), mod_map),
        pl.BlockSpec((d, n), lambda i: (0, 0)),
        pl.BlockSpec((1, LANES), lambda i: (0, 0)),
        pl.BlockSpec((1, LANES), lambda i: (0, 0)),
    ]
    args = [x2d, g.reshape(1, d), mod3, w_bf16, g2(qg), g2(kg)]
    out_shape = jax.ShapeDtypeStruct((r, n), F32)
    out_specs = pl.BlockSpec((tm, n), row)
    if pending is not None:
        blk0 = row0 // tm
        seg = lambda i: (blk0 + i, 0)
        in_specs += [pl.BlockSpec((tm, d), seg), pl.BlockSpec((tm, d), seg),
                     pl.BlockSpec((tm, 2), seg), pl.BlockSpec((1, 1, N_MOD * d), mod_map)]
        args += list(pending)
        out_shape = (out_shape, jax.ShapeDtypeStruct((r, d), F32))
        out_specs = (out_specs, pl.BlockSpec((tm, d), row))
    return pl.pallas_call(
        functools.partial(_inproj_kernel, moe_pending=pending is not None),
        out_shape=out_shape,
        grid=(r // tm,),
        in_specs=in_specs,
        out_specs=out_specs,
        compiler_params=_cparams(("parallel",)),
        name="in_projection",
    )(*args)


def _pair_rms(x, g, lo):
    ss = x * x
    sa = jnp.sum(jnp.where(lo, ss, 0.0), axis=-1, keepdims=True)
    sb = jnp.sum(jnp.where(lo, 0.0, ss), axis=-1, keepdims=True)
    ms = jnp.where(lo, sa, sb) * (1.0 / HEAD_DIM)
    return x * lax.rsqrt(ms + EPS) * g


def _na_kernel(q_ref, k_ref, v_ref, kc_ref, vc_ref, bias_ref, o_ref):
    lo = lax.broadcasted_iota(jnp.int32, (1, LANES), 1) < HEAD_DIM
    n_rows = k_ref.shape[1]
    col_tiles = GRID_W // Q_COLS
    nq = Q_ROWS * Q_COLS
    nk = K_ROWS * K_COLS
    kcb = kc_ref[0].astype(BF16)
    vcb = vc_ref[0].astype(BF16)
    tiles = [(rr, j) for rr in range(NA_ROW_TILES) for j in range(col_tiles)]
    q2 = []
    for rr, j in tiles:
        qn = q_ref[0, rr * Q_ROWS:(rr + 1) * Q_ROWS, j * Q_COLS:(j + 1) * Q_COLS, :].reshape(nq, LANES)
        q2.append(jnp.concatenate([jnp.where(lo, qn, 0.0), jnp.where(lo, 0.0, qn)], axis=0).astype(BF16))
    s_cx_all = _dot_nt(jnp.concatenate(q2, axis=0), kcb)
    p_cx, o_nb, denom = [], [], []
    for t, (rr, j) in enumerate(tiles):
        i = pl.program_id(2) * NA_ROW_TILES + rr
        kr0 = jnp.clip(Q_ROWS * i - WIN_ROWS // 2, 0, n_rows - K_ROWS)
        rt = jnp.where(i == 0, 0, jnp.where(i == n_rows // Q_ROWS - 1, 2, 1))
        kc0 = min(max(Q_COLS * j - WIN_COLS // 2, 0), GRID_W - K_COLS)
        ct = 0 if j == 0 else (2 if j == col_tiles - 1 else 1)
        kw = k_ref[0, pl.ds(kr0, K_ROWS), kc0:kc0 + K_COLS, :].reshape(nk, LANES).astype(BF16)
        vw = v_ref[0, pl.ds(kr0, K_ROWS), kc0:kc0 + K_COLS, :].reshape(nk, LANES).astype(BF16)
        s_nb = _dot_nt(q2[t], kw) + bias_ref[0, 0, rt * 3 + ct].reshape(2 * nq, nk)
        s_cx = s_cx_all[t * 2 * nq:(t + 1) * 2 * nq]
        m = jnp.maximum(jnp.max(s_nb, axis=-1, keepdims=True), jnp.max(s_cx, axis=-1, keepdims=True))
        p_nb = jnp.exp2(s_nb - m)
        p_c = jnp.exp2(s_cx - m)
        denom.append(jnp.sum(p_nb, axis=-1, keepdims=True) + jnp.sum(p_c, axis=-1, keepdims=True))
        p_cx.append(p_c.astype(BF16))
        o_nb.append(_dot(p_nb.astype(BF16), vw))
    o_cx_all = _dot(jnp.concatenate(p_cx, axis=0), vcb)
    for t, (rr, j) in enumerate(tiles):
        o2 = (o_nb[t] + o_cx_all[t * 2 * nq:(t + 1) * 2 * nq]) / denom[t]
        o_ref[0, rr * Q_ROWS:(rr + 1) * Q_ROWS, j * Q_COLS:(j + 1) * Q_COLS, :] = \
            jnp.where(lo, o2[0:nq], o2[nq:2 * nq]).reshape(Q_ROWS, Q_COLS, LANES)


def _bias_table_kernel(rpb_ref, o_ref, tt_scr):
    n_off_r = 2 * WIN_ROWS - 1
    lane = lax.broadcasted_iota(jnp.int32, (Q_COLS, LANES), 1)
    qc = lax.broadcasted_iota(jnp.int32, (Q_COLS, LANES), 0)
    kc = lane % K_COLS
    lane_blk = lane // K_COLS
    per_vreg = LANES // K_COLS
    col_rel = (0, -WIN_COLS // 2, -WIN_COLS)
    col_origin = (0, Q_COLS, GRID_W - Q_COLS)
    row_rel = (0, -WIN_ROWS // 2, -WIN_ROWS)
    row_origin = (0, Q_ROWS, GRID_W - Q_ROWS)
    masked = jnp.full((Q_COLS, LANES), MASK_VALUE, F32)

    for ct in range(3):
        c_abs = col_origin[ct] + qc
        k_abs = col_origin[ct] + col_rel[ct] + kc
        start = jnp.clip(c_abs - WIN_COLS // 2, 0, GRID_W - WIN_COLS)
        valid_c = (k_abs >= start) & (k_abs < start + WIN_COLS)
        base = (1 - WIN_COLS - col_rel[ct] - RPB_LANE0) % LANES
        for ro in range(n_off_r):
            row = jnp.broadcast_to(rpb_ref[0, 0, ro:ro + 1, :], (Q_COLS, LANES))
            t = pltpu.roll(row, base, 1, stride=1, stride_axis=0)
            rep = t
            for m in range(1, per_vreg):
                rep = jnp.where(lane_blk == m, pltpu.roll(t, K_COLS * m, 1), rep)
            tt_scr[ct, ro] = jnp.where(valid_c, rep * LOG2E, MASK_VALUE)

    for rt in range(3):
        for ct in range(3):
            for qr in range(Q_ROWS):
                r_abs = row_origin[rt] + qr
                r_start = min(max(r_abs - WIN_ROWS // 2, 0), GRID_W - WIN_ROWS)
                for w in range(K_ROWS // per_vreg):
                    val = None
                    for m in range(per_vreg):
                        k_abs = row_origin[rt] + row_rel[rt] + per_vreg * w + m
                        ok = r_start <= k_abs < r_start + WIN_ROWS
                        src = tt_scr[ct, k_abs - r_abs + WIN_ROWS - 1] if ok else masked
                        val = src if val is None else jnp.where(lane_blk == m, src, val)
                    o_ref[0, 0, rt * 3 + ct, 0, qr * Q_COLS:(qr + 1) * Q_COLS, w * LANES:(w + 1) * LANES] = val


def _na_bias_tables(na_rpb):
    depth, h, n_r, n_c = na_rpb.shape
    rpb_pad = jnp.pad(na_rpb.astype(F32), ((0, 0), (0, 0), (0, 2 * SUBLANES - n_r),
                                            (RPB_LANE0, LANES - RPB_LANE0 - n_c)))
    nq, nk = Q_ROWS * Q_COLS, K_ROWS * K_COLS
    return pl.pallas_call(
        _bias_table_kernel,
        out_shape=jax.ShapeDtypeStruct((depth, h // 2, 9, 2, nq, nk), F32),
        grid=(depth, h),
        in_specs=[pl.BlockSpec((1, 1, 2 * SUBLANES, LANES), lambda l, i: (l, i, 0, 0))],
        out_specs=pl.BlockSpec((1, 1, 9, 1, nq, nk), lambda l, i: (l, i // 2, 0, i % 2, 0, 0)),
        scratch_shapes=[pltpu.VMEM((3, 2 * WIN_ROWS - 1, Q_COLS, LANES), F32)],
        compiler_params=_cparams(("parallel", "parallel")),
        name="na_bias_tables",
    )(rpb_pad)


def _neighborhood_attention(qkvu_lat, qkvu_ctx, bias_tabs, layer, bsz):
    n_lat = qkvu_lat.shape[0] // bsz
    n_ctx = qkvu_ctx.shape[0] // bsz
    rows = n_lat // GRID_W
    n_cols = qkvu_lat.shape[1]
    lat4 = qkvu_lat.reshape(bsz, rows, GRID_W, n_cols)
    ctx3 = qkvu_ctx.reshape(bsz, n_ctx, n_cols)
    n_pairs = NA_WIDTH // LANES
    step_rows = NA_ROW_TILES * Q_ROWS
    n_tiles = rows // step_rows
    out = pl.pallas_call(
        _na_kernel,
        out_shape=jax.ShapeDtypeStruct((bsz, rows, GRID_W, NA_WIDTH), F32),
        grid=(n_pairs, bsz, n_tiles),
        in_specs=[
            pl.BlockSpec((1, step_rows, GRID_W, LANES), lambda p, b, t: (b, t, 0, p)),
            pl.BlockSpec((1, rows, GRID_W, LANES), lambda p, b, t: (b, 0, 0, n_pairs + p)),
            pl.BlockSpec((1, rows, GRID_W, LANES), lambda p, b, t: (b, 0, 0, 2 * n_pairs + p)),
            pl.BlockSpec((1, n_ctx, LANES), lambda p, b, t: (b, 0, n_pairs + p)),
            pl.BlockSpec((1, n_ctx, LANES), lambda p, b, t: (b, 0, 2 * n_pairs + p)),
            pl.BlockSpec((1, 1, 9, 2, Q_ROWS * Q_COLS, K_ROWS * K_COLS), lambda p, b, t: (layer, p, 0, 0, 0, 0)),
        ],
        out_specs=pl.BlockSpec((1, step_rows, GRID_W, LANES), lambda p, b, t: (b, t, 0, p)),
        compiler_params=_cparams(("parallel", "parallel", "parallel")),
        name="neighborhood_attention",
    )(lat4, lat4, lat4, ctx3, ctx3, bias_tabs)
    return out.reshape(bsz * n_lat, NA_WIDTH)


def _ctx_attn_kernel(q_ref, k_ref, v_ref, o_ref):
    lo = lax.broadcasted_iota(jnp.int32, (1, LANES), 1) < HEAD_DIM
    qn = q_ref[0]
    kn = k_ref[0].astype(BF16)
    vb = v_ref[0].astype(BF16)

    def one(qm):
        s = _dot_nt(qm, kn)
        m = jnp.max(s, axis=-1, keepdims=True)
        p = jnp.exp2(s - m)
        l = jnp.sum(p, axis=-1, keepdims=True)
        return _dot(p.astype(BF16), vb) / l

    o_a = one(jnp.where(lo, qn, 0.0).astype(BF16))
    o_b = one(jnp.where(lo, 0.0, qn).astype(BF16))
    o_ref[0] = jnp.where(lo, o_a, o_b)


def _context_attention(qkvu_ctx, bsz):
    n_ctx = qkvu_ctx.shape[0] // bsz
    ctx3 = qkvu_ctx.reshape(bsz, n_ctx, qkvu_ctx.shape[1])
    n_pairs = NA_WIDTH // LANES
    out = pl.pallas_call(
        _ctx_attn_kernel,
        out_shape=jax.ShapeDtypeStruct((bsz, n_ctx, NA_WIDTH), F32),
        grid=(bsz, n_pairs),
        in_specs=[
            pl.BlockSpec((1, n_ctx, LANES), lambda b, p: (b, 0, p)),
            pl.BlockSpec((1, n_ctx, LANES), lambda b, p: (b, 0, n_pairs + p)),
            pl.BlockSpec((1, n_ctx, LANES), lambda b, p: (b, 0, 2 * n_pairs + p)),
        ],
        out_specs=pl.BlockSpec((1, n_ctx, LANES), lambda b, p: (b, 0, p)),
        compiler_params=_cparams(("parallel", "parallel")),
        name="context_attention",
    )(ctx3, ctx3, ctx3)
    return out.reshape(bsz * n_ctx, NA_WIDTH)


def _s5_kernel(uc_ref, ul_ref, w_ref, m_ref, v_ref, a_ref, y_ref, x_scr, s_scr, hf_scr, hr_scr, *, n_ctx_chunks):
    bsz = ul_ref.shape[0]
    n_lat_chunks = ul_ref.shape[1] // CHUNK
    n_chunks = n_ctx_chunks + n_lat_chunks
    rows = n_chunks * SUBLANES
    n_pairs = w_ref.shape[1]
    gpb = 2 * n_pairs
    half = 2 * LANES
    tile_chunks = TM_PROJ // CHUNK
    pieces = [(uc_ref, 0, n_ctx_chunks, 0)] if n_ctx_chunks else []
    pieces += [(ul_ref, c0, min(S5_PIECE, n_lat_chunks - c0), n_ctx_chunks + c0)
               for c0 in range(0, n_lat_chunks, S5_PIECE)]

    def lane_block_ids(n):
        return lax.broadcasted_iota(jnp.int32, (n, LANES), 1) // S5_CH

    def lag_bit_clear(lane_blk, g, bit):
        return ((((lane_blk - g) & (SUBLANES - 1)) >> bit) & 1) == 0

    for src_ref, c_src, n_c, c_dst in pieces:
        lane_blk = lane_block_ids(n_c)
        m0 = [lag_bit_clear(lane_blk, g, 0) for g in range(2)]
        m1 = [lag_bit_clear(lane_blk, g, 1) for g in range(4)]
        m2 = [lag_bit_clear(lane_blk, g, 2) for g in range(gpb)]
        for b in range(bsz):
            for q in range(2):
                rolled = []
                for j in range(SUBLANES):
                    s = SUBLANES * q + j
                    u_s = src_ref[b, pl.ds(c_src * CHUNK + s, n_c, stride=CHUNK), :]
                    rolled.append(u_s if j == 0 else pltpu.roll(u_s, j * S5_CH, 1))
                st1 = [[jnp.where(m0[g], rolled[2 * j], rolled[2 * j + 1]) for j in range(4)] for g in range(2)]
                st2 = [[jnp.where(m1[g], st1[g & 1][2 * j], st1[g & 1][2 * j + 1]) for j in range(2)]
                       for g in range(4)]
                for g in range(gpb):
                    xg = jnp.where(m2[g], st2[g & 3][0], st2[g & 3][1])
                    x_scr[g // 2, q, pl.ds(c_dst * SUBLANES + (g % 2) * bsz + b, n_c, stride=SUBLANES), :] = xg

    n_blk = 8
    rb = rows // n_blk
    first_group = (lax.broadcasted_iota(jnp.int32, (rb, half), 0) & (SUBLANES // 2)) == 0
    fwd_cols = (lax.broadcasted_iota(jnp.int32, (rb, half), 1) & (LANES - 1)) < S5_STATE
    is_fwd = lax.broadcasted_iota(jnp.int32, (SUBLANES, LANES), 1) < S5_STATE
    first_rows = lax.broadcasted_iota(jnp.int32, (SUBLANES, half), 0) < SUBLANES // 2
    zero = jnp.zeros((SUBLANES, LANES), F32)

    def x_rows(p, sl):
        return jnp.concatenate([x_scr[p, 0, sl, :], x_scr[p, 1, sl, :]], axis=1).astype(BF16)

    def put_cols(scr, i, sl, val):
        scr[i, 0, sl, :] = val[:, 0:LANES]
        scr[i, 1, sl, :] = val[:, LANES:half]

    for hh in range(n_pairs // PAIRS_PER_STEP):
        pairs = [hh * PAIRS_PER_STEP + i for i in range(PAIRS_PER_STEP)]
        for i, p in enumerate(pairs):
            for blk in range(n_blk):
                sl = slice(blk * rb, (blk + 1) * rb)
                r = _dot(x_rows(p, sl), w_ref[0, p])
                put_cols(s_scr, i, sl, jnp.where(first_group, r[:, :half], r[:, half:]))

        a_pair = [jnp.where(first_rows, a_ref[0, 2 * p], a_ref[0, 2 * p + 1]) for p in pairs]
        a_re = [a[:, 0:LANES] for a in a_pair]
        a_im = [a[:, LANES:half] for a in a_pair]

        def body(k, carry):
            kr = jnp.where(k < n_ctx_chunks, n_ctx_chunks - 1 - k, n_chunks + n_ctx_chunks - 1 - k)
            rf = pl.ds(pl.multiple_of(k * SUBLANES, SUBLANES), SUBLANES)
            rr = pl.ds(pl.multiple_of(kr * SUBLANES, SUBLANES), SUBLANES)
            new = []
            for i in range(PAIRS_PER_STEP):
                h_re, h_im = carry[2 * i], carry[2 * i + 1]
                hf_scr[i, 0, rf, :] = h_re
                hf_scr[i, 1, rf, :] = h_im
                hr_scr[i, 0, rr, :] = h_re
                hr_scr[i, 1, rr, :] = h_im
                s_re = jnp.where(is_fwd, s_scr[i, 0, rf, :], s_scr[i, 0, rr, :])
                s_im = jnp.where(is_fwd, s_scr[i, 1, rf, :], s_scr[i, 1, rr, :])
                new.append(a_re[i] * h_re - a_im[i] * h_im + s_re)
                new.append(a_re[i] * h_im + a_im[i] * h_re + s_im)
            return tuple(new)

        lax.fori_loop(0, n_chunks, body, (zero,) * (2 * PAIRS_PER_STEP))

        for i, p in enumerate(pairs):
            for blk in range(n_blk):
                sl = slice(blk * rb, (blk + 1) * rb)
                hf = jnp.concatenate([hf_scr[i, 0, sl, :], hf_scr[i, 1, sl, :]], axis=1)
                hr = jnp.concatenate([hr_scr[i, 0, sl, :], hr_scr[i, 1, sl, :]], axis=1)
                h_in = jnp.where(fwd_cols, hf, hr).astype(BF16)
                r = _dot(x_rows(p, sl), m_ref[0, p]) + _dot_nt(h_in, v_ref[0, p])
                put_cols(s_scr, i, sl, jnp.where(first_group, r[:, :half], r[:, half:]))

        g_lo = 2 * pairs[0]
        n_g = 2 * PAIRS_PER_STEP
        out_blk = lax.broadcasted_iota(jnp.int32, (tile_chunks, LANES), 1) // S5_CH
        lanes_out = (out_blk >= g_lo) & (out_blk < g_lo + n_g)
        assert n_g == 4
        for _, _, n_c, c_dst in pieces:
            lane_blk = lane_block_ids(n_c)
            n0 = [lag_bit_clear(lane_blk, e, 0) for e in range(2)]
            n1 = [lag_bit_clear(lane_blk, e, 1) for e in range(4)]
            for b in range(bsz):
                for q in range(2):
                    y_g = [s_scr[(g - g_lo) // 2, q,
                                 pl.ds(c_dst * SUBLANES + (g % 2) * bsz + b, n_c, stride=SUBLANES), :]
                           for g in range(g_lo, g_lo + n_g)]
                    d1 = [[jnp.where(n0[e], y_g[2 * i], y_g[2 * i + 1]) for i in range(2)] for e in range(2)]
                    merged = [jnp.where(n1[e], d1[e & 1][0], d1[e & 1][1]) for e in range(4)]
                    for j in range(SUBLANES):
                        z = merged[(j + g_lo) % 4]
                        if j:
                            z = pltpu.roll(z, LANES - j * S5_CH, 1)
                        t = SUBLANES * q + j
                        for ct in range(n_c // tile_chunks):
                            tile = (c_dst // tile_chunks + ct - n_ctx_chunks // tile_chunks) % (n_chunks // tile_chunks)
                            r0 = tile * TM_PROJ + t * tile_chunks
                            pltpu.store(y_ref.at[b, 0, r0:r0 + tile_chunks, :],
                                        z[ct * tile_chunks:(ct + 1) * tile_chunks, :], mask=lanes_out)


def _s5_scan(qkvu_ctx3, qkvu_lat3, mats, layer):
    w_c, m_c, vt_c, a_c = mats
    bsz, n_ctx, _ = qkvu_ctx3.shape
    n_lat = qkvu_lat3.shape[1]
    n_seq = n_ctx + n_lat
    rows = n_seq // CHUNK * SUBLANES
    n_blocks = S5_WIDTH // LANES
    ppb = w_c.shape[1] // n_blocks
    u_blk0 = 3 * NA_WIDTH // LANES
    wspec = pl.BlockSpec((1, ppb, 2 * LANES, 4 * LANES), lambda i: (layer, i, 0, 0))
    one = pl.Buffered(1)
    state = pltpu.VMEM((PAIRS_PER_STEP, 2, rows, LANES), F32)
    return pl.pallas_call(
        functools.partial(_s5_kernel, n_ctx_chunks=n_ctx // CHUNK),
        out_shape=jax.ShapeDtypeStruct((bsz, n_blocks, n_seq, LANES), F32),
        grid=(n_blocks,),
        in_specs=[
            pl.BlockSpec((bsz, n_ctx, LANES), lambda i: (0, 0, u_blk0 + i)),
            pl.BlockSpec((bsz, n_lat, LANES), lambda i: (0, 0, u_blk0 + i), pipeline_mode=one),
            wspec, wspec,
            pl.BlockSpec((1, ppb, 4 * LANES, 2 * LANES), lambda i: (layer, i, 0, 0)),
            pl.BlockSpec((1, 2 * ppb, SUBLANES, 2 * LANES), lambda i: (layer, i, 0, 0)),
        ],
        out_specs=pl.BlockSpec((bsz, 1, n_seq, LANES), lambda i: (0, i, 0, 0), pipeline_mode=one),
        scratch_shapes=[pltpu.VMEM((ppb, 2, rows, LANES), F32), state, state, state],
        compiler_params=_cparams(("parallel",)),
        name="s5_scan",
    )(qkvu_ctx3, qkvu_lat3, w_c, m_c, vt_c, a_c)


def _s5_mats_kernel(*refs):
    for gg in range(S5_MATS_GROUPS):
        _s5_group_mats(gg, *refs)


def _s5_group_mats(gg, prm_ref, btr_ref, bti_ref, cr_ref, ci_ref, w_ref, m_ref, vt_ref, a_ref):
    t = CHUNK
    width = CHUNK * S5_CH
    gl = (pl.program_id(1) * S5_MATS_GROUPS + gg) % SUBLANES
    pair, side = gg // 2, gg % 2
    is_fwd = lax.broadcasted_iota(jnp.int32, (1, LANES), 1) < S5_STATE
    lr = prm_ref[0, gg, 0:1, :]
    li = prm_ref[0, gg, 1:2, :]
    dt = jnp.exp(prm_ref[0, gg, 2:3, :])
    n = lax.broadcasted_iota(jnp.int32, (3 * SUBLANES, LANES), 0).astype(F32)
    pmag = jnp.exp(n * (lr * dt))
    pw_re = pmag * jnp.cos(n * (li * dt))
    pw_im = pmag * jnp.sin(n * (li * dt))
    ab_re, ab_im = pw_re[1:2, :], pw_im[1:2, :]
    den = lr * lr + li * li
    nr = ab_re - 1.0
    z_re = (nr * lr + ab_im * li) / den
    z_im = (ab_im * lr - nr * li) / den
    bt_re, bt_im = btr_ref[0, gg], bti_ref[0, gg]
    bb_re = z_re * bt_re - z_im * bt_im
    bb_im = z_re * bt_im + z_im * bt_re
    c_re, c_im = cr_ref[0, gg], ci_ref[0, gg]

    def powers(n_fwd, n_rev):
        return (jnp.where(is_fwd, pw_re[n_fwd:n_fwd + 1, :], pw_re[n_rev:n_rev + 1, :]),
                jnp.where(is_fwd, pw_im[n_fwd:n_fwd + 1, :], pw_im[n_rev:n_rev + 1, :]))

    def block_rows(s, row0=0):
        pos = SUBLANES * (s // SUBLANES) + (s % SUBLANES + gl) % SUBLANES
        return pl.ds(pl.multiple_of(row0 + pos * S5_CH, S5_CH), S5_CH)

    c0 = side * width
    for s in range(t):
        rows = block_rows(s)
        p_re, p_im = powers(t - 1 - s, s)
        w_ref[0, pair, rows, c0:c0 + LANES] = (bb_re * p_re - bb_im * p_im).astype(BF16)
        w_ref[0, pair, rows, c0 + LANES:c0 + 2 * LANES] = (bb_re * p_im + bb_im * p_re).astype(BF16)
        q_re, q_im = powers(s + 1, t - s)
        v_rows = block_rows(s, c0)
        vt_ref[0, pair, v_rows, 0:LANES] = (c_re * q_re - c_im * q_im).astype(BF16)
        vt_ref[0, pair, v_rows, LANES:2 * LANES] = (-(c_re * q_im + c_im * q_re)).astype(BF16)

    ca_re, ca_im = [], []
    for lag in range(t):
        p_re, p_im = powers(lag, t - 1 - lag)
        ca_re.append(c_re * p_re - c_im * p_im)
        ca_im.append(c_re * p_im + c_im * p_re)
    stack = jnp.concatenate([jnp.concatenate(ca_re, axis=0), jnp.concatenate(ca_im, axis=0)], axis=1)
    zero = jnp.zeros_like(bb_re)
    lhs = jnp.concatenate([
        jnp.concatenate([jnp.where(is_fwd, bb_re, zero), jnp.where(is_fwd, -bb_im, zero)], axis=1),
        jnp.concatenate([jnp.where(is_fwd, zero, bb_re), jnp.where(is_fwd, zero, -bb_im)], axis=1)], axis=0)
    kt = lax.dot_general(lhs, stack, (((1,), (1,)), ((), ())), precision=HIGHEST, preferred_element_type=F32)
    kt_f, kt_r = kt[0:S5_CH], kt[S5_CH:2 * S5_CH]
    blk = lax.broadcasted_iota(jnp.int32, (S5_CH, 2 * LANES), 1) // S5_CH
    for s in range(t):
        strip = (jnp.where(blk >= s, pltpu.roll(kt_f, S5_CH * s, 1), 0.0)
                 + jnp.where(blk <= s, pltpu.roll(kt_r, (S5_CH * (s - t + 1)) % (2 * LANES), 1), 0.0))
        strip = jnp.concatenate([pltpu.roll(strip[:, 0:LANES], gl * S5_CH, 1),
                                 pltpu.roll(strip[:, LANES:2 * LANES], gl * S5_CH, 1)], axis=1)
        m_ref[0, pair, block_rows(s), c0:c0 + width] = strip.astype(BF16)

    a_ref[0, gg, :, 0:LANES] = jnp.broadcast_to(pw_re[t:t + 1, :], (SUBLANES, LANES))
    a_ref[0, gg, :, LANES:2 * LANES] = jnp.broadcast_to(pw_im[t:t + 1, :], (SUBLANES, LANES))


def _s5_matrices(lam_re, lam_im, log_dt, b_re, b_im, c_re, c_im):
    depth, _, g, p = lam_re.shape
    hc = b_re.shape[-1]
    width = CHUNK * hc
    both = lambda x: jnp.transpose(x.astype(F32), (0, 2, 1, 3)).reshape(depth, g, 1, 2 * p)
    dt_rows = jnp.broadcast_to(jnp.transpose(log_dt.astype(F32), (0, 2, 1))[..., None], (depth, g, 2, p))
    prm = jnp.concatenate([both(lam_re), both(lam_im), dt_rows.reshape(depth, g, 1, 2 * p),
                           jnp.zeros((depth, g, SUBLANES - 3, 2 * p), F32)], axis=2)
    bt = lambda x: jnp.transpose(x.astype(F32), (0, 2, 4, 1, 3)).reshape(depth, g, hc, 2 * p)
    ct = lambda x: jnp.transpose(x.astype(F32), (0, 2, 3, 1, 4)).reshape(depth, g, hc, 2 * p)
    gs = S5_MATS_GROUPS
    vec = lambda rows: pl.BlockSpec((1, gs, rows, 2 * p), lambda l, i: (l, i, 0, 0))
    return pl.pallas_call(
        _s5_mats_kernel,
        out_shape=(jax.ShapeDtypeStruct((depth, g // 2, width, 2 * width), BF16),
                   jax.ShapeDtypeStruct((depth, g // 2, width, 2 * width), BF16),
                   jax.ShapeDtypeStruct((depth, g // 2, 2 * width, width), BF16),
                   jax.ShapeDtypeStruct((depth, g, SUBLANES, 4 * p), F32)),
        grid=(depth, g // gs),
        in_specs=[vec(SUBLANES), vec(hc), vec(hc), vec(hc), vec(hc)],
        out_specs=(pl.BlockSpec((1, gs // 2, width, 2 * width), lambda l, i: (l, i, 0, 0)),
                   pl.BlockSpec((1, gs // 2, width, 2 * width), lambda l, i: (l, i, 0, 0)),
                   pl.BlockSpec((1, gs // 2, 2 * width, width), lambda l, i: (l, i, 0, 0)),
                   pl.BlockSpec((1, gs, SUBLANES, 4 * p), lambda l, i: (l, i, 0, 0))),
        compiler_params=_cparams(("parallel", "parallel")),
        name="s5_matrices",
    )(prm, bt(b_re), bt(b_im), ct(c_re), ct(c_im))


def _s5_mixer(qkvu_lat, qkvu_ctx, mats, layer, bsz):
    n_cols = qkvu_lat.shape[1]
    return _s5_scan(qkvu_ctx.reshape(bsz, -1, n_cols), qkvu_lat.reshape(bsz, -1, n_cols), mats, layer)


def _outproj_kernel(na_ref, y_ref, u_ref, h_ref, mod_ref, d_ref, wglu_ref, bglu_ref, wout_ref, g2_ref,
                    ho_ref, f_ref):
    tile_chunks = TM_PROJ // CHUNK
    y = jnp.concatenate(
        [jnp.concatenate([y_ref[0, blk, pl.ds(sub * TM_PROJ + c, CHUNK, stride=tile_chunks), :]
                          for sub in range(y_ref.shape[2] // TM_PROJ) for c in range(tile_chunks)], axis=0)
         for blk in range(S5_WIDTH // LANES)], axis=1)
    z = jax.nn.gelu(y + d_ref[...] * u_ref[...])
    s5 = z * jax.nn.sigmoid(_dot(z.astype(BF16), wglu_ref[...]) + bglu_ref[...])
    mix = (_dot(na_ref[...].astype(BF16), wout_ref[0:NA_WIDTH, :])
           + _dot(s5.astype(BF16), wout_ref[NA_WIDTH:NA_WIDTH + S5_WIDTH, :]))
    d = D_MODEL
    gate = mod_ref[0, :, 2 * d:3 * d]
    h = h_ref[...] + gate * mix
    ho_ref[...] = h
    ms = jnp.mean(h * h, axis=-1, keepdims=True)
    y2 = h * lax.rsqrt(ms + EPS) * g2_ref[...]
    f_ref[...] = y2 * (1.0 + mod_ref[0, :, 4 * d:5 * d]) + mod_ref[0, :, 3 * d:4 * d]


def _out_projection(na, y_all, qkvu, h2d, mod3, d_skip, wglu_bf16, b_glu, wout_bf16, g2,
                    mod_map, rows_per_batch, y_row0, tm=TM_PROJ):
    r, d = h2d.shape
    tiles_per_batch = rows_per_batch // tm
    u_blk = 3 * NA_WIDTH // S5_WIDTH
    y_block0 = y_row0 // tm

    def y_map(i):
        return (i // tiles_per_batch, 0, y_block0 + i % tiles_per_batch, 0)

    const = lambda i: (0, 0)
    return pl.pallas_call(
        _outproj_kernel,
        out_shape=(jax.ShapeDtypeStruct((r, d), F32), jax.ShapeDtypeStruct((r, d), F32)),
        grid=(r // tm,),
        in_specs=[
            pl.BlockSpec((tm, NA_WIDTH), lambda i: (i, 0)),
            pl.BlockSpec((1, S5_WIDTH // LANES, tm, LANES), y_map),
            pl.BlockSpec((tm, S5_WIDTH), lambda i: (i, u_blk)),
            pl.BlockSpec((tm, d), lambda i: (i, 0)),
            pl.BlockSpec((1, 1, N_MOD * d), mod_map),
            pl.BlockSpec((1, S5_WIDTH), const),
            pl.BlockSpec((S5_WIDTH, S5_WIDTH), const),
            pl.BlockSpec((1, S5_WIDTH), const),
            pl.BlockSpec((NA_WIDTH + S5_WIDTH, d), const),
            pl.BlockSpec((1, d), const),
        ],
        out_specs=(pl.BlockSpec((tm, d), lambda i: (i, 0)), pl.BlockSpec((tm, d), lambda i: (i, 0))),
        compiler_params=_cparams(("parallel",)),
        name="out_projection",
    )(na, y_all, qkvu, h2d, mod3, d_skip.reshape(1, -1), wglu_bf16, b_glu.reshape(1, -1), wout_bf16,
      g2.reshape(1, d))


def _top2(vals):
    best = vals[0]
    bi = jnp.zeros(best.shape, jnp.int32)
    for i in range(1, len(vals)):
        gt = vals[i] > best
        best = jnp.where(gt, vals[i], best)
        bi = jnp.where(gt, i, bi)
    second = jnp.full(best.shape, -jnp.inf, F32)
    si = jnp.zeros(best.shape, jnp.int32)
    for i in range(len(vals)):
        cand = jnp.where(bi == i, -jnp.inf, vals[i])
        gt = cand > second
        second = jnp.where(gt, cand, second)
        si = jnp.where(gt, i, si)
    return best, bi, second, si


def _route(f, rwt_hi, rwt_lo, rb):
    f_hi = f.astype(BF16)
    f_lo = (f - f_hi.astype(F32)).astype(BF16)
    logits = _dot_nt(rwt_hi, f_hi) + (_dot_nt(rwt_hi, f_lo) + _dot_nt(rwt_lo, f_hi))
    m = jnp.max(logits, axis=0, keepdims=True)
    e = jnp.exp(logits - m)
    probs = e / jnp.sum(e, axis=0, keepdims=True)
    sel = probs + rb
    sel_rows = [sel[i:i + 1, :] for i in range(N_EXPERTS)]
    prob_rows = [probs[i:i + 1, :] for i in range(N_EXPERTS)]
    scores = []
    for g in range(N_GROUPS):
        b, _, s, _ = _top2(sel_rows[g * EPG:(g + 1) * EPG])
        scores.append(b + s)
    grp = jnp.zeros(scores[0].shape, jnp.int32)
    gbest = scores[0]
    for g in range(1, N_GROUPS):
        gt = scores[g] > gbest
        gbest = jnp.where(gt, scores[g], gbest)
        grp = jnp.where(gt, g, grp)
    in_rows = []
    for j in range(EPG):
        v = sel_rows[j]
        for g in range(1, N_GROUPS):
            v = jnp.where(grp == g, sel_rows[g * EPG + j], v)
        in_rows.append(v)
    _, l1, _, l2 = _top2(in_rows)
    i1 = grp * EPG + l1
    i2 = grp * EPG + l2
    w1 = jnp.zeros(gbest.shape, F32)
    w2 = jnp.zeros(gbest.shape, F32)
    for i in range(N_EXPERTS):
        w1 = jnp.where(i1 == i, prob_rows[i], w1)
        w2 = jnp.where(i2 == i, prob_rows[i], w2)
    tot = w1 + w2
    return i1, i2, w1 / tot, w2 / tot


def _router_kernel(f_ref, rwh_ref, rwl_ref, rb_ref, idx_ref, gate_ref):
    i1, i2, g1, g2 = _route(f_ref[...], rwh_ref[...], rwl_ref[...], rb_ref[...])
    idx_ref[0:1, :] = i1
    idx_ref[1:2, :] = i2
    gate_ref[0:1, :] = g1
    gate_ref[1:2, :] = g2


def _router(f_all, router_wt, router_b):
    n, d = f_all.shape
    rwt_hi = router_wt.astype(BF16)
    rwt_lo = (router_wt - rwt_hi.astype(F32)).astype(BF16)
    return pl.pallas_call(
        _router_kernel,
        out_shape=(jax.ShapeDtypeStruct((2, n), jnp.int32), jax.ShapeDtypeStruct((2, n), F32)),
        grid=(n // TM_ROUTE,),
        in_specs=[
            pl.BlockSpec((TM_ROUTE, d), lambda i: (i, 0)),
            pl.BlockSpec((N_EXPERTS, d), lambda i: (0, 0)),
            pl.BlockSpec((N_EXPERTS, d), lambda i: (0, 0)),
            pl.BlockSpec((N_EXPERTS, 1), lambda i: (0, 0)),
        ],
        out_specs=(pl.BlockSpec((2, TM_ROUTE), lambda i: (0, i)), pl.BlockSpec((2, TM_ROUTE), lambda i: (0, i))),
        compiler_params=_cparams(("parallel",)),
        name="router",
    )(f_all, rwt_hi, rwt_lo, router_b)


def _scatter_rows_kernel(dest_ref, pad_ref, end_ref, *refs, n_tok, seg_tiles):
    f_refs = refs[:len(seg_tiles)]
    xs_ref, zero_scr, stage, sems = refs[len(seg_tiles):]
    i = pl.program_id(0)
    tm = f_refs[0].shape[0]
    d_blocks = f_refs[0].shape[1] // LANES
    n_rows = xs_ref.shape[0] // d_blocks
    fill_sem = sems.at[2]

    def token_rows(ref, first, n):
        start = first * d_blocks
        if not isinstance(first, int):
            start = pl.multiple_of(start, d_blocks)
        return ref.at[pl.ds(start, n * d_blocks), :]

    def slab_copy(start):
        return pltpu.make_async_copy(zero_scr, token_rows(xs_ref, start, TM_EXP), fill_sem)

    @pl.when(i == 0)
    def _():
        zero_scr[...] = jnp.zeros(zero_scr.shape, zero_scr.dtype)
        for e in range(N_EXPERTS):
            slab_copy(jnp.minimum(pad_ref[e], n_rows - TM_EXP)).start()
        for e in range(N_EXPERTS):
            slab_copy(0).wait()
        for k in range(N_EXPERTS):
            start = end_ref[0] + k * TM_EXP

            @pl.when(start < n_rows)
            def _():
                cp = slab_copy(start)
                cp.start()
                cp.wait()

    slot = i % 2

    def wait_tile(s):
        for _ in range(2):
            pltpu.make_async_copy(stage.at[s], token_rows(xs_ref, 0, tm), sems.at[s]).wait()

    def scatter_tile(f_ref):
        base = i * tm
        for k in range(d_blocks):
            stage[slot, pl.ds(k, tm, stride=d_blocks), :] = f_ref[:, k * LANES:(k + 1) * LANES]

        def row_copy(r, d):
            return pltpu.make_async_copy(token_rows(stage.at[slot], r, 1), token_rows(xs_ref, d, 1), sems.at[slot])

        for r in range(tm):
            row_copy(r, dest_ref[base + r]).start(priority=0)
            row_copy(r, dest_ref[n_tok + base + r]).start(priority=1)

    tile0 = 0
    for f_ref, n_t in zip(f_refs, seg_tiles):
        pl.when((i >= tile0) & (i < tile0 + n_t))(functools.partial(scatter_tile, f_ref))
        tile0 += n_t

    pl.when(i > 0)(lambda: wait_tile(1 - slot))
    pl.when(i == pl.num_programs(0) - 1)(lambda: wait_tile(slot))


def _scatter_rows(segments, dest_flat, pad_start, total_end):
    d = segments[0].shape[1]
    seg_tiles = tuple(s.shape[0] // TM_PROJ for s in segments)
    n_tok = sum(s.shape[0] for s in segments)
    r_max = 2 * n_tok + N_EXPERTS * TM_EXP
    in_specs = []
    tile0 = 0
    for n_t in seg_tiles:
        in_specs.append(pl.BlockSpec(
            (TM_PROJ, d), lambda i, *_, t0=tile0, nt=n_t: (jnp.clip(i - t0, 0, nt - 1), 0)))
        tile0 += n_t
    grid_spec = pltpu.PrefetchScalarGridSpec(
        num_scalar_prefetch=3,
        grid=(tile0,),
        in_specs=in_specs,
        out_specs=pl.BlockSpec(memory_space=pl.ANY),
        scratch_shapes=[pltpu.VMEM((TM_EXP * d // LANES, LANES), F32),
                        pltpu.VMEM((2, TM_PROJ * d // LANES, LANES), F32),
                        pltpu.SemaphoreType.DMA((3,))],
    )
    return pl.pallas_call(
        functools.partial(_scatter_rows_kernel, n_tok=n_tok, seg_tiles=seg_tiles),
        out_shape=jax.ShapeDtypeStruct((r_max * d // LANES, LANES), F32),
        grid_spec=grid_spec,
        compiler_params=_cparams(("arbitrary",)),
        name="moe_scatter_rows",
    )(dest_flat, pad_start, total_end, *segments)


def _experts_kernel(te_ref, nv_ref, x_ref, wg_ref, wu_ref, wd_ref, o_ref, wg_scr, wu_scr, wd_scr):
    i = pl.program_id(0)
    e = te_ref[i]
    prev = te_ref[jnp.maximum(i - 1, 0)]
    rows = 128

    @pl.when((i == 0) | (e != prev))
    def _():
        def body(r, carry):
            sl = pl.ds(pl.multiple_of(r * rows, rows), rows)
            wg_scr[sl, :] = wg_ref[0, 0, sl, :].astype(BF16)
            wu_scr[sl, :] = wu_ref[0, 0, sl, :].astype(BF16)
            wd_scr[sl, :] = wd_ref[0, 0, sl, :].astype(BF16)
            return carry
        lax.fori_loop(0, wg_scr.shape[0] // rows, body, 0)

    @pl.when(i < nv_ref[0])
    def _():
        d_blocks = wg_scr.shape[0] // LANES
        x = jnp.concatenate([x_ref[pl.ds(k, TM_EXP, stride=d_blocks), :].astype(BF16) for k in range(d_blocks)],
                            axis=1)
        g = _dot(x, wg_scr[...])
        u = _dot(x, wu_scr[...])
        a = (g * jax.nn.sigmoid(g)) * u
        o_ref[...] = _dot(a.astype(BF16), wd_scr[...]).astype(BF16)

    @pl.when(i >= nv_ref[0])
    def _():
        o_ref[...] = jnp.zeros(o_ref.shape, BF16)


def _experts(xs, tile_expert, n_valid, w_gate, w_up, w_down, layer):
    d, de = w_gate.shape[2], w_gate.shape[3]
    d_blocks = d // LANES
    r = xs.shape[0] // d_blocks
    n_tiles = r // TM_EXP
    x_map = lambda i, te, nv: (jnp.minimum(i, nv[0] - 1), 0)
    grid_spec = pltpu.PrefetchScalarGridSpec(
        num_scalar_prefetch=2,
        grid=(n_tiles,),
        in_specs=[
            pl.BlockSpec((TM_EXP * d_blocks, LANES), x_map),
            pl.BlockSpec((1, 1, d, de), lambda i, te, nv: (layer, te[i], 0, 0)),
            pl.BlockSpec((1, 1, d, de), lambda i, te, nv: (layer, te[i], 0, 0)),
            pl.BlockSpec((1, 1, de, d), lambda i, te, nv: (layer, te[i], 0, 0)),
        ],
        out_specs=pl.BlockSpec((TM_EXP, d), lambda i, te, nv: (i, 0)),
        scratch_shapes=[pltpu.VMEM((d, de), BF16), pltpu.VMEM((d, de), BF16), pltpu.VMEM((de, d), BF16)],
    )
    return pl.pallas_call(
        _experts_kernel,
        out_shape=jax.ShapeDtypeStruct((r, d), BF16),
        grid_spec=grid_spec,
        compiler_params=_cparams(("arbitrary",)),
        name="experts",
    )(tile_expert, n_valid, xs, w_gate, w_up, w_down)


def _dispatch(idx):
    n = idx.shape[1]
    e_flat = idx.reshape(-1)
    onehot = (e_flat[:, None] == jnp.arange(N_EXPERTS, dtype=jnp.int32)[None, :]).astype(jnp.int32)
    csum = jnp.cumsum(onehot, axis=0)
    rank = jnp.sum(csum * onehot, axis=1) - 1
    counts = csum[-1]
    padded = ((counts + TM_EXP - 1) // TM_EXP) * TM_EXP
    ends = jnp.cumsum(padded)
    starts = ends - padded
    dest = (jnp.sum(onehot * starts[None, :], axis=1) + rank).astype(jnp.int32)
    r_max = 2 * n + N_EXPERTS * TM_EXP
    tile_start = jnp.arange(r_max // TM_EXP, dtype=jnp.int32) * TM_EXP
    tile_expert = jnp.minimum(jnp.sum((tile_start[:, None] >= ends[None, :]).astype(jnp.int32), axis=1),
                              N_EXPERTS - 1).astype(jnp.int32)
    n_valid = (ends[-1] // TM_EXP).astype(jnp.int32).reshape(1)
    pad_start = (starts + counts).astype(jnp.int32)
    total_end = ends[-1].astype(jnp.int32).reshape(1)
    return dest, pad_start, total_end, tile_expert, n_valid


def _combine_kernel(h_ref, y1_ref, y2_ref, gate_ref, mod_ref, o_ref):
    d = D_MODEL
    g = gate_ref[...]
    y = g[:, 0:1] * y1_ref[...].astype(F32) + g[:, 1:2] * y2_ref[...].astype(F32)
    o_ref[...] = h_ref[...] + mod_ref[0, :, 5 * d:6 * d] * y


def _combine(h2d, y1, y2, gates_t, mod3, mod_map, row0, tm=TM_PROJ):
    r, d = h2d.shape
    blk0 = row0 // tm
    row = lambda i: (i, 0)
    seg = lambda i: (blk0 + i, 0)
    return pl.pallas_call(
        _combine_kernel,
        out_shape=jax.ShapeDtypeStruct((r, d), F32),
        grid=(r // tm,),
        in_specs=[
            pl.BlockSpec((tm, d), row),
            pl.BlockSpec((tm, d), seg),
            pl.BlockSpec((tm, d), seg),
            pl.BlockSpec((tm, 2), seg),
            pl.BlockSpec((1, 1, N_MOD * d), mod_map),
        ],
        out_specs=pl.BlockSpec((tm, d), row),
        compiler_params=_cparams(("parallel",)),
        name="moe_combine",
    )(h2d, y1, y2, gates_t, mod3)


def kernel(x, c, ctx, c_ctx, w_mod, b_mod, norm1_g, norm2_g, w_in, w_out, q_norm_g, k_norm_g, na_rpb,
           s5_lam_re, s5_lam_im, s5_log_dt, s5_b_re, s5_b_im, s5_c_re, s5_c_im, s5_d, s5_w_glu, s5_b_glu,
           router_w, router_bias, moe_w_gate, moe_w_up, moe_w_down):
    bsz, n_lat, d = x.shape
    n_ctx = ctx.shape[1]
    depth = w_mod.shape[0]
    ctx_row = bsz
    c_rows = jnp.concatenate([c.astype(F32), c_ctx.astype(F32)[None],
                              jnp.zeros((SUBLANES - bsz - 1, d), F32)], axis=0)
    mod_all = _modulation(c_rows, w_mod.astype(F32), b_mod.astype(F32))

    h_lat = x.reshape(bsz * n_lat, d).astype(F32)
    h_ctx = ctx.reshape(bsz * n_ctx, d).astype(F32)
    lat_map = _mod_row_map(n_lat, 0, True)
    lat_map_in = _mod_row_map(n_lat, 0, True, TM_IN)
    ctx_map = _mod_row_map(n_ctx, ctx_row, False)
    bias_tabs = _na_bias_tables(na_rpb)
    router_wt = router_w.T.astype(F32)
    router_b = router_bias.reshape(N_EXPERTS, 1).astype(F32)
    s5_mats = _s5_matrices(s5_lam_re, s5_lam_im, s5_log_dt, s5_b_re, s5_b_im, s5_c_re, s5_c_im)

    n_l = bsz * n_lat
    pending = None
    for layer in range(depth):
        ctx_out = layer < depth - 1
        mod3 = mod_all[layer].reshape(SUBLANES, 1, N_MOD * d)
        w_in_b = w_in[layer].astype(BF16)
        proj = functools.partial(_in_projection, g=norm1_g[layer], mod3=mod3, w_bf16=w_in_b,
                                 qg=q_norm_g[layer], kg=k_norm_g[layer])
        if pending is None:
            qkvu_lat = proj(h_lat, mod_map=lat_map_in, tm=TM_IN)
            qkvu_ctx = proj(h_ctx, mod_map=ctx_map)
        else:
            qkvu_lat, h_lat = proj(h_lat, mod_map=lat_map_in, pending=pending, row0=0, tm=TM_IN)
            qkvu_ctx, h_ctx = proj(h_ctx, mod_map=ctx_map, pending=pending, row0=n_l)
        na_lat = _neighborhood_attention(qkvu_lat, qkvu_ctx, bias_tabs, layer, bsz)
        y_all = _s5_mixer(qkvu_lat, qkvu_ctx, s5_mats, layer, bsz)
        wglu_b = s5_w_glu[layer].astype(BF16)
        wout_b = w_out[layer].astype(BF16)
        h_lat, f_lat = _out_projection(na_lat, y_all, qkvu_lat, h_lat, mod3, s5_d[layer], wglu_b,
                                       s5_b_glu[layer], wout_b, norm2_g[layer],
                                       lat_map_in, n_lat, 0, tm=TM_IN)
        idx, gates = _router(f_lat, router_wt, router_b)
        if ctx_out:
            na_ctx = _context_attention(qkvu_ctx, bsz)
            h_ctx, f_ctx = _out_projection(na_ctx, y_all, qkvu_ctx, h_ctx, mod3, s5_d[layer], wglu_b,
                                           s5_b_glu[layer], wout_b, norm2_g[layer],
                                           ctx_map, n_ctx, n_lat)
            idx_c, gates_c = _router(f_ctx, router_wt, router_b)
            idx = jnp.concatenate([idx, idx_c], axis=1)
            gates = jnp.concatenate([gates, gates_c], axis=1)
        n_tok = idx.shape[1]
        dest, pad_start, total_end, tile_expert, n_valid = _dispatch(idx)
        xs = _scatter_rows([f_lat, f_ctx] if ctx_out else [f_lat], dest, pad_start, total_end)
        ys = _experts(xs, tile_expert, n_valid, moe_w_gate, moe_w_up, moe_w_down, layer)
        y1 = jnp.take(ys, dest[:n_tok], axis=0, mode="clip")
        y2 = jnp.take(ys, dest[n_tok:], axis=0, mode="clip")
        gates_t = gates.T
        if ctx_out:
            pending = (y1, y2, gates_t, mod3)
        else:
            h_lat = _combine(h_lat, y1, y2, gates_t, mod3, lat_map_in, 0, tm=TM_IN)
    return h_lat.reshape(bsz, n_lat, d).astype(x.dtype)
```

```python
import functools
import math

import jax
import jax.numpy as jnp
from jax import lax
from jax.experimental import pallas as pl
from jax.experimental.pallas import tpu as pltpu

F32 = jnp.float32
BF16 = jnp.bfloat16
HIGHEST = lax.Precision.HIGHEST

D_MODEL = 1024
GRID_W = 64
HEAD_DIM = 64
NA_WIDTH = 512
S5_WIDTH = 512
S5_CH = 16
S5_GROUPS = 32
S5_STATE = 64
WIN_ROWS = 8
WIN_COLS = 16
N_EXPERTS = 16
N_GROUPS = 4
EPG = 4
N_MOD = 6
EPS = 1e-6

LANES = 128
SUBLANES = 8
VMEM_LIMIT = 56 * 1024 * 1024

TM_PROJ = 256
TM_IN = 512
Q_ROWS = 8
NA_ROW_TILES = 8
Q_COLS = 16
K_ROWS = 16
K_COLS = 32
CHUNK = 16
PAIRS_PER_STEP = 2
S5_PIECE = 32
S5_MATS_GROUPS = 4
TM_EXP = 512
TM_ROUTE = 1024
TM_SCATTER = 512
MASK_VALUE = -1e30
LOG2E = 1.4426950408889634
RPB_LANE0 = 48


def _cparams(sem):
    return pltpu.CompilerParams(dimension_semantics=sem, vmem_limit_bytes=VMEM_LIMIT)


def _dot(a, b):
    return jnp.dot(a, b, preferred_element_type=F32)


def _dot_nt(a, b):
    return lax.dot_general(a, b, (((1,), (1,)), ((), ())), preferred_element_type=F32)


def _mod_kernel(c_ref, w_ref, b_ref, o_ref):
    a = c_ref[...]
    a = a * jax.nn.sigmoid(a)
    w = w_ref[0]
    a_hi, w_hi = a.astype(BF16), w.astype(BF16)
    a_lo = (a - a_hi.astype(F32)).astype(BF16)
    w_lo = (w - w_hi.astype(F32)).astype(BF16)
    o_ref[0] = _dot(a_hi, w_hi) + (_dot(a_lo, w_hi) + _dot(a_hi, w_lo)) + b_ref[0]


def _modulation(c_rows, w_mod, b_mod):
    depth, d, n = w_mod.shape
    tn = 1536
    return pl.pallas_call(
        _mod_kernel,
        out_shape=jax.ShapeDtypeStruct((depth, SUBLANES, n), F32),
        grid=(depth, n // tn),
        in_specs=[
            pl.BlockSpec((SUBLANES, d), lambda l, j: (0, 0)),
            pl.BlockSpec((1, d, tn), lambda l, j: (l, 0, j)),
            pl.BlockSpec((1, 1, tn), lambda l, j: (l, 0, j)),
        ],
        out_specs=pl.BlockSpec((1, SUBLANES, tn), lambda l, j: (l, 0, j)),
        compiler_params=_cparams(("arbitrary", "arbitrary")),
        name="modulation",
    )(c_rows, w_mod, b_mod.reshape(depth, 1, n))


def _inproj_kernel(x_ref, g_ref, mod_ref, w_ref, qg_ref, kg_ref, *rest, moe_pending):
    x = x_ref[...]
    if moe_pending:
        y1_ref, y2_ref, gate_ref, modp_ref, o_ref, xo_ref = rest
        gw = gate_ref[...]
        y = gw[:, 0:1] * y1_ref[...].astype(F32) + gw[:, 1:2] * y2_ref[...].astype(F32)
        x = x + modp_ref[0, :, 5 * D_MODEL:6 * D_MODEL] * y
        xo_ref[...] = x
    else:
        (o_ref,) = rest
    ms = jnp.mean(x * x, axis=-1, keepdims=True)
    y = x * lax.rsqrt(ms + EPS) * g_ref[...]
    shift = mod_ref[0, :, 0:D_MODEL]
    scale = mod_ref[0, :, D_MODEL:2 * D_MODEL]
    a = y * (1.0 + scale) + shift
    acc = _dot(a.astype(BF16), w_ref[...])
    lo = lax.broadcasted_iota(jnp.int32, (1, LANES), 1) < HEAD_DIM
    n_pairs = NA_WIDTH // LANES
    for blk in range(2 * n_pairs):
        cols = slice(blk * LANES, (blk + 1) * LANES)
        if blk < n_pairs:
            o_ref[:, cols] = _pair_rms(acc[:, cols], qg_ref[...], lo) * (HEAD_DIM ** -0.5 * LOG2E)
        else:
            o_ref[:, cols] = _pair_rms(acc[:, cols], kg_ref[...], lo)
    o_ref[:, 2 * NA_WIDTH:] = acc[:, 2 * NA_WIDTH:]


def _mod_row_map(rows_per_batch, mod_row0, per_batch, tm=TM_PROJ):
    tiles_per_batch = rows_per_batch // tm
    if per_batch:
        return lambda i: (mod_row0 + i // tiles_per_batch, 0, 0)
    return lambda i: (mod_row0, 0, 0)


def _in_projection(x2d, g, mod3, w_bf16, mod_map, qg, kg, pending=None, row0=0, tm=TM_PROJ):
    r, d = x2d.shape
    n = w_bf16.shape[1]
    g2 = lambda v: jnp.concatenate([v, v]).reshape(1, LANES).astype(F32)
    row = lambda i: (i, 0)
    in_specs = [
        pl.BlockSpec((tm, d), row),
        pl.BlockSpec((1, d), lambda i: (0, 0)),
        pl.BlockSpec((1, 1, N_MOD * d), mod_map),
        pl.BlockSpec((d, n), lambda i: (0, 0)),
        pl.BlockSpec((1, LANES), lambda i: (0, 0)),
        pl.BlockSpec((1, LANES), lambda i: (0, 0)),
    ]
    args = [x2d, g.reshape(1, d), mod3, w_bf16, g2(qg), g2(kg)]
    out_shape = jax.ShapeDtypeStruct((r, n), F32)
    out_specs = pl.BlockSpec((tm, n), row)
    if pending is not None:
        blk0 = row0 // tm
        seg = lambda i: (blk0 + i, 0)
        in_specs += [pl.BlockSpec((tm, d), seg), pl.BlockSpec((tm, d), seg),
                     pl.BlockSpec((tm, 2), seg), pl.BlockSpec((1, 1, N_MOD * d), mod_map)]
        args += list(pending)
        out_shape = (out_shape, jax.ShapeDtypeStruct((r, d), F32))
        out_specs = (out_specs, pl.BlockSpec((tm, d), row))
    return pl.pallas_call(
        functools.partial(_inproj_kernel, moe_pending=pending is not None),
        out_shape=out_shape,
        grid=(r // tm,),
        in_specs=in_specs,
        out_specs=out_specs,
        compiler_params=_cparams(("parallel",)),
        name="in_projection",
    )(*args)


def _pair_rms(x, g, lo):
    ss = x * x
    sa = jnp.sum(jnp.where(lo, ss, 0.0), axis=-1, keepdims=True)
    sb = jnp.sum(jnp.where(lo, 0.0, ss), axis=-1, keepdims=True)
    ms = jnp.where(lo, sa, sb) * (1.0 / HEAD_DIM)
    return x * lax.rsqrt(ms + EPS) * g


def _na_kernel(q_ref, k_ref, v_ref, kc_ref, vc_ref, bias_ref, o_ref):
    lo = lax.broadcasted_iota(jnp.int32, (1, LANES), 1) < HEAD_DIM
    n_rows = k_ref.shape[1]
    col_tiles = GRID_W // Q_COLS
    nq = Q_ROWS * Q_COLS
    nk = K_ROWS * K_COLS
    kcb = kc_ref[0].astype(BF16)
    vcb = vc_ref[0].astype(BF16)
    tiles = [(rr, j) for rr in range(NA_ROW_TILES) for j in range(col_tiles)]
    q2 = []
    for rr, j in tiles:
        qn = q_ref[0, rr * Q_ROWS:(rr + 1) * Q_ROWS, j * Q_COLS:(j + 1) * Q_COLS, :].reshape(nq, LANES)
        q2.append(jnp.concatenate([jnp.where(lo, qn, 0.0), jnp.where(lo, 0.0, qn)], axis=0).astype(BF16))
    s_cx_all = _dot_nt(jnp.concatenate(q2, axis=0), kcb)
    p_cx, o_nb, denom = [], [], []
    for t, (rr, j) in enumerate(tiles):
        i = pl.program_id(2) * NA_ROW_TILES + rr
        kr0 = jnp.clip(Q_ROWS * i - WIN_ROWS // 2, 0, n_rows - K_ROWS)
        rt = jnp.where(i == 0, 0, jnp.where(i == n_rows // Q_ROWS - 1, 2, 1))
        kc0 = min(max(Q_COLS * j - WIN_COLS // 2, 0), GRID_W - K_COLS)
        ct = 0 if j == 0 else (2 if j == col_tiles - 1 else 1)
        kw = k_ref[0, pl.ds(kr0, K_ROWS), kc0:kc0 + K_COLS, :].reshape(nk, LANES).astype(BF16)
        vw = v_ref[0, pl.ds(kr0, K_ROWS), kc0:kc0 + K_COLS, :].reshape(nk, LANES).astype(BF16)
        s_nb = _dot_nt(q2[t], kw) + bias_ref[0, 0, rt * 3 + ct].reshape(2 * nq, nk)
        s_cx = s_cx_all[t * 2 * nq:(t + 1) * 2 * nq]
        m = jnp.maximum(jnp.max(s_nb, axis=-1, keepdims=True), jnp.max(s_cx, axis=-1, keepdims=True))
        p_nb = jnp.exp2(s_nb - m)
        p_c = jnp.exp2(s_cx - m)
        denom.append(jnp.sum(p_nb, axis=-1, keepdims=True) + jnp.sum(p_c, axis=-1, keepdims=True))
        p_cx.append(p_c.astype(BF16))
        o_nb.append(_dot(p_nb.astype(BF16), vw))
    o_cx_all = _dot(jnp.concatenate(p_cx, axis=0), vcb)
    for t, (rr, j) in enumerate(tiles):
        o2 = (o_nb[t] + o_cx_all[t * 2 * nq:(t + 1) * 2 * nq]) / denom[t]
        o_ref[0, rr * Q_ROWS:(rr + 1) * Q_ROWS, j * Q_COLS:(j + 1) * Q_COLS, :] = \
            jnp.where(lo, o2[0:nq], o2[nq:2 * nq]).reshape(Q_ROWS, Q_COLS, LANES)


def _bias_table_kernel(rpb_ref, o_ref, tt_scr):
    n_off_r = 2 * WIN_ROWS - 1
    lane = lax.broadcasted_iota(jnp.int32, (Q_COLS, LANES), 1)
    qc = lax.broadcasted_iota(jnp.int32, (Q_COLS, LANES), 0)
    kc = lane % K_COLS
    lane_blk = lane // K_COLS
    per_vreg = LANES // K_COLS
    col_rel = (0, -WIN_COLS // 2, -WIN_COLS)
    col_origin = (0, Q_COLS, GRID_W - Q_COLS)
    row_rel = (0, -WIN_ROWS // 2, -WIN_ROWS)
    row_origin = (0, Q_ROWS, GRID_W - Q_ROWS)
    masked = jnp.full((Q_COLS, LANES), MASK_VALUE, F32)

    for ct in range(3):
        c_abs = col_origin[ct] + qc
        k_abs = col_origin[ct] + col_rel[ct] + kc
        start = jnp.clip(c_abs - WIN_COLS // 2, 0, GRID_W - WIN_COLS)
        valid_c = (k_abs >= start) & (k_abs < start + WIN_COLS)
        base = (1 - WIN_COLS - col_rel[ct] - RPB_LANE0) % LANES
        for ro in range(n_off_r):
            row = jnp.broadcast_to(rpb_ref[0, 0, ro:ro + 1, :], (Q_COLS, LANES))
            t = pltpu.roll(row, base, 1, stride=1, stride_axis=0)
            rep = t
            for m in range(1, per_vreg):
                rep = jnp.where(lane_blk == m, pltpu.roll(t, K_COLS * m, 1), rep)
            tt_scr[ct, ro] = jnp.where(valid_c, rep * LOG2E, MASK_VALUE)

    for rt in range(3):
        for ct in range(3):
            for qr in range(Q_ROWS):
                r_abs = row_origin[rt] + qr
                r_start = min(max(r_abs - WIN_ROWS // 2, 0), GRID_W - WIN_ROWS)
                for w in range(K_ROWS // per_vreg):
                    val = None
                    for m in range(per_vreg):
                        k_abs = row_origin[rt] + row_rel[rt] + per_vreg * w + m
                        ok = r_start <= k_abs < r_start + WIN_ROWS
                        src = tt_scr[ct, k_abs - r_abs + WIN_ROWS - 1] if ok else masked
                        val = src if val is None else jnp.where(lane_blk == m, src, val)
                    o_ref[0, 0, rt * 3 + ct, 0, qr * Q_COLS:(qr + 1) * Q_COLS, w * LANES:(w + 1) * LANES] = val


def _na_bias_tables(na_rpb):
    depth, h, n_r, n_c = na_rpb.shape
    rpb_pad = jnp.pad(na_rpb.astype(F32), ((0, 0), (0, 0), (0, 2 * SUBLANES - n_r),
                                            (RPB_LANE0, LANES - RPB_LANE0 - n_c)))
    nq, nk = Q_ROWS * Q_COLS, K_ROWS * K_COLS
    return pl.pallas_call(
        _bias_table_kernel,
        out_shape=jax.ShapeDtypeStruct((depth, h // 2, 9, 2, nq, nk), F32),
        grid=(depth, h),
        in_specs=[pl.BlockSpec((1, 1, 2 * SUBLANES, LANES), lambda l, i: (l, i, 0, 0))],
        out_specs=pl.BlockSpec((1, 1, 9, 1, nq, nk), lambda l, i: (l, i // 2, 0, i % 2, 0, 0)),
        scratch_shapes=[pltpu.VMEM((3, 2 * WIN_ROWS - 1, Q_COLS, LANES), F32)],
        compiler_params=_cparams(("parallel", "parallel")),
        name="na_bias_tables",
    )(rpb_pad)


def _neighborhood_attention(qkvu_lat, qkvu_ctx, bias_tabs, layer, bsz):
    n_lat = qkvu_lat.shape[0] // bsz
    n_ctx = qkvu_ctx.shape[0] // bsz
    rows = n_lat // GRID_W
    n_cols = qkvu_lat.shape[1]
    lat4 = qkvu_lat.reshape(bsz, rows, GRID_W, n_cols)
    ctx3 = qkvu_ctx.reshape(bsz, n_ctx, n_cols)
    n_pairs = NA_WIDTH // LANES
    step_rows = NA_ROW_TILES * Q_ROWS
    n_tiles = rows // step_rows
    out = pl.pallas_call(
        _na_kernel,
        out_shape=jax.ShapeDtypeStruct((bsz, rows, GRID_W, NA_WIDTH), F32),
        grid=(n_pairs, bsz, n_tiles),
        in_specs=[
            pl.BlockSpec((1, step_rows, GRID_W, LANES), lambda p, b, t: (b, t, 0, p)),
            pl.BlockSpec((1, rows, GRID_W, LANES), lambda p, b, t: (b, 0, 0, n_pairs + p)),
            pl.BlockSpec((1, rows, GRID_W, LANES), lambda p, b, t: (b, 0, 0, 2 * n_pairs + p)),
            pl.BlockSpec((1, n_ctx, LANES), lambda p, b, t: (b, 0, n_pairs + p)),
            pl.BlockSpec((1, n_ctx, LANES), lambda p, b, t: (b, 0, 2 * n_pairs + p)),
            pl.BlockSpec((1, 1, 9, 2, Q_ROWS * Q_COLS, K_ROWS * K_COLS), lambda p, b, t: (layer, p, 0, 0, 0, 0)),
        ],
        out_specs=pl.BlockSpec((1, step_rows, GRID_W, LANES), lambda p, b, t: (b, t, 0, p)),
        compiler_params=_cparams(("parallel", "parallel", "parallel")),
        name="neighborhood_attention",
    )(lat4, lat4, lat4, ctx3, ctx3, bias_tabs)
    return out.reshape(bsz * n_lat, NA_WIDTH)


def _ctx_attn_kernel(q_ref, k_ref, v_ref, o_ref):
    lo = lax.broadcasted_iota(jnp.int32, (1, LANES), 1) < HEAD_DIM
    qn = q_ref[0]
    kn = k_ref[0].astype(BF16)
    vb = v_ref[0].astype(BF16)

    def one(qm):
        s = _dot_nt(qm, kn)
        m = jnp.max(s, axis=-1, keepdims=True)
        p = jnp.exp2(s - m)
        l = jnp.sum(p, axis=-1, keepdims=True)
        return _dot(p.astype(BF16), vb) / l

    o_a = one(jnp.where(lo, qn, 0.0).astype(BF16))
    o_b = one(jnp.where(lo, 0.0, qn).astype(BF16))
    o_ref[0] = jnp.where(lo, o_a, o_b)


def _context_attention(qkvu_ctx, bsz):
    n_ctx = qkvu_ctx.shape[0] // bsz
    ctx3 = qkvu_ctx.reshape(bsz, n_ctx, qkvu_ctx.shape[1])
    n_pairs = NA_WIDTH // LANES
    out = pl.pallas_call(
        _ctx_attn_kernel,
        out_shape=jax.ShapeDtypeStruct((bsz, n_ctx, NA_WIDTH), F32),
        grid=(bsz, n_pairs),
        in_specs=[
            pl.BlockSpec((1, n_ctx, LANES), lambda b, p: (b, 0, p)),
            pl.BlockSpec((1, n_ctx, LANES), lambda b, p: (b, 0, n_pairs + p)),
            pl.BlockSpec((1, n_ctx, LANES), lambda b, p: (b, 0, 2 * n_pairs + p)),
        ],
        out_specs=pl.BlockSpec((1, n_ctx, LANES), lambda b, p: (b, 0, p)),
        compiler_params=_cparams(("parallel", "parallel")),
        name="context_attention",
    )(ctx3, ctx3, ctx3)
    return out.reshape(bsz * n_ctx, NA_WIDTH)


def _s5_kernel(uc_ref, ul_ref, w_ref, m_ref, v_ref, a_ref, y_ref, x_scr, s_scr, hf_scr, hr_scr, *, n_ctx_chunks):
    bsz = ul_ref.shape[0]
    n_lat_chunks = ul_ref.shape[1] // CHUNK
    n_chunks = n_ctx_chunks + n_lat_chunks
    rows = n_chunks * SUBLANES
    n_pairs = w_ref.shape[1]
    gpb = 2 * n_pairs
    half = 2 * LANES
    tile_chunks = TM_PROJ // CHUNK
    pieces = [(uc_ref, 0, n_ctx_chunks, 0)] if n_ctx_chunks else []
    pieces += [(ul_ref, c0, min(S5_PIECE, n_lat_chunks - c0), n_ctx_chunks + c0)
               for c0 in range(0, n_lat_chunks, S5_PIECE)]

    def lane_block_ids(n):
        return lax.broadcasted_iota(jnp.int32, (n, LANES), 1) // S5_CH

    def lag_bit_clear(lane_blk, g, bit):
        return ((((lane_blk - g) & (SUBLANES - 1)) >> bit) & 1) == 0

    for src_ref, c_src, n_c, c_dst in pieces:
        lane_blk = lane_block_ids(n_c)
        m0 = [lag_bit_clear(lane_blk, g, 0) for g in range(2)]
        m1 = [lag_bit_clear(lane_blk, g, 1) for g in range(4)]
        m2 = [lag_bit_clear(lane_blk, g, 2) for g in range(gpb)]
        for b in range(bsz):
            for q in range(2):
                rolled = []
                for j in range(SUBLANES):
                    s = SUBLANES * q + j
                    u_s = src_ref[b, pl.ds(c_src * CHUNK + s, n_c, stride=CHUNK), :]
                    rolled.append(u_s if j == 0 else pltpu.roll(u_s, j * S5_CH, 1))
                st1 = [[jnp.where(m0[g], rolled[2 * j], rolled[2 * j + 1]) for j in range(4)] for g in range(2)]
                st2 = [[jnp.where(m1[g], st1[g & 1][2 * j], st1[g & 1][2 * j + 1]) for j in range(2)]
                       for g in range(4)]
                for g in range(gpb):
                    xg = jnp.where(m2[g], st2[g & 3][0], st2[g & 3][1])
                    x_scr[g // 2, q, pl.ds(c_dst * SUBLANES + (g % 2) * bsz + b, n_c, stride=SUBLANES), :] = xg

    n_blk = 8
    rb = rows // n_blk
    first_group = (lax.broadcasted_iota(jnp.int32, (rb, half), 0) & (SUBLANES // 2)) == 0
    fwd_cols = (lax.broadcasted_iota(jnp.int32, (rb, half), 1) & (LANES - 1)) < S5_STATE
    is_fwd = lax.broadcasted_iota(jnp.int32, (SUBLANES, LANES), 1) < S5_STATE
    first_rows = lax.broadcasted_iota(jnp.int32, (SUBLANES, half), 0) < SUBLANES // 2
    zero = jnp.zeros((SUBLANES, LANES), F32)

    def x_rows(p, sl):
        return jnp.concatenate([x_scr[p, 0, sl, :], x_scr[p, 1, sl, :]], axis=1).astype(BF16)

    def put_cols(scr, i, sl, val):
        scr[i, 0, sl, :] = val[:, 0:LANES]
        scr[i, 1, sl, :] = val[:, LANES:half]

    for hh in range(n_pairs // PAIRS_PER_STEP):
        pairs = [hh * PAIRS_PER_STEP + i for i in range(PAIRS_PER_STEP)]
        for i, p in enumerate(pairs):
            for blk in range(n_blk):
                sl = slice(blk * rb, (blk + 1) * rb)
                r = _dot(x_rows(p, sl), w_ref[0, p])
                put_cols(s_scr, i, sl, jnp.where(first_group, r[:, :half], r[:, half:]))

        a_pair = [jnp.where(first_rows, a_ref[0, 2 * p], a_ref[0, 2 * p + 1]) for p in pairs]
        a_re = [a[:, 0:LANES] for a in a_pair]
        a_im = [a[:, LANES:half] for a in a_pair]

        def body(k, carry):
            kr = jnp.where(k < n_ctx_chunks, n_ctx_chunks - 1 - k, n_chunks + n_ctx_chunks - 1 - k)
            rf = pl.ds(pl.multiple_of(k * SUBLANES, SUBLANES), SUBLANES)
            rr = pl.ds(pl.multiple_of(kr * SUBLANES, SUBLANES), SUBLANES)
            new = []
            for i in range(PAIRS_PER_STEP):
                h_re, h_im = carry[2 * i], carry[2 * i + 1]
                hf_scr[i, 0, rf, :] = h_re
                hf_scr[i, 1, rf, :] = h_im
                hr_scr[i, 0, rr, :] = h_re
                hr_scr[i, 1, rr, :] = h_im
                s_re = jnp.where(is_fwd, s_scr[i, 0, rf, :], s_scr[i, 0, rr, :])
                s_im = jnp.where(is_fwd, s_scr[i, 1, rf, :], s_scr[i, 1, rr, :])
                new.append(a_re[i] * h_re - a_im[i] * h_im + s_re)
                new.append(a_re[i] * h_im + a_im[i] * h_re + s_im)
            return tuple(new)

        lax.fori_loop(0, n_chunks, body, (zero,) * (2 * PAIRS_PER_STEP))

        for i, p in enumerate(pairs):
            for blk in range(n_blk):
                sl = slice(blk * rb, (blk + 1) * rb)
                hf = jnp.concatenate([hf_scr[i, 0, sl, :], hf_scr[i, 1, sl, :]], axis=1)
                hr = jnp.concatenate([hr_scr[i, 0, sl, :], hr_scr[i, 1, sl, :]], axis=1)
                h_in = jnp.where(fwd_cols, hf, hr).astype(BF16)
                r = _dot(x_rows(p, sl), m_ref[0, p]) + _dot_nt(h_in, v_ref[0, p])
                put_cols(s_scr, i, sl, jnp.where(first_group, r[:, :half], r[:, half:]))

        g_lo = 2 * pairs[0]
        n_g = 2 * PAIRS_PER_STEP
        out_blk = lax.broadcasted_iota(jnp.int32, (tile_chunks, LANES), 1) // S5_CH
        lanes_out = (out_blk >= g_lo) & (out_blk < g_lo + n_g)
        assert n_g == 4
        for _, _, n_c, c_dst in pieces:
            lane_blk = lane_block_ids(n_c)
            n0 = [lag_bit_clear(lane_blk, e, 0) for e in range(2)]
            n1 = [lag_bit_clear(lane_blk, e, 1) for e in range(4)]
            for b in range(bsz):
                for q in range(2):
                    y_g = [s_scr[(g - g_lo) // 2, q,
                                 pl.ds(c_dst * SUBLANES + (g % 2) * bsz + b, n_c, stride=SUBLANES), :]
                           for g in range(g_lo, g_lo + n_g)]
                    d1 = [[jnp.where(n0[e], y_g[2 * i], y_g[2 * i + 1]) for i in range(2)] for e in range(2)]
                    merged = [jnp.where(n1[e], d1[e & 1][0], d1[e & 1][1]) for e in range(4)]
                    for j in range(SUBLANES):
                        z = merged[(j + g_lo) % 4]
                        if j:
                            z = pltpu.roll(z, LANES - j * S5_CH, 1)
                        t = SUBLANES * q + j
                        for ct in range(n_c // tile_chunks):
                            tile = (c_dst // tile_chunks + ct - n_ctx_chunks // tile_chunks) % (n_chunks // tile_chunks)
                            r0 = tile * TM_PROJ + t * tile_chunks
                            pltpu.store(y_ref.at[b, 0, r0:r0 + tile_chunks, :],
                                        z[ct * tile_chunks:(ct + 1) * tile_chunks, :], mask=lanes_out)


def _s5_scan(qkvu_ctx3, qkvu_lat3, mats, layer):
    w_c, m_c, vt_c, a_c = mats
    bsz, n_ctx, _ = qkvu_ctx3.shape
    n_lat = qkvu_lat3.shape[1]
    n_seq = n_ctx + n_lat
    rows = n_seq // CHUNK * SUBLANES
    n_blocks = S5_WIDTH // LANES
    ppb = w_c.shape[1] // n_blocks
    u_blk0 = 3 * NA_WIDTH // LANES
    wspec = pl.BlockSpec((1, ppb, 2 * LANES, 4 * LANES), lambda i: (layer, i, 0, 0))
    one = pl.Buffered(1)
    state = pltpu.VMEM((PAIRS_PER_STEP, 2, rows, LANES), F32)
    return pl.pallas_call(
        functools.partial(_s5_kernel, n_ctx_chunks=n_ctx // CHUNK),
        out_shape=jax.ShapeDtypeStruct((bsz, n_blocks, n_seq, LANES), F32),
        grid=(n_blocks,),
        in_specs=[
            pl.BlockSpec((bsz, n_ctx, LANES), lambda i: (0, 0, u_blk0 + i)),
            pl.BlockSpec((bsz, n_lat, LANES), lambda i: (0, 0, u_blk0 + i), pipeline_mode=one),
            wspec, wspec,
            pl.BlockSpec((1, ppb, 4 * LANES, 2 * LANES), lambda i: (layer, i, 0, 0)),
            pl.BlockSpec((1, 2 * ppb, SUBLANES, 2 * LANES), lambda i: (layer, i, 0, 0)),
        ],
        out_specs=pl.BlockSpec((bsz, 1, n_seq, LANES), lambda i: (0, i, 0, 0), pipeline_mode=one),
        scratch_shapes=[pltpu.VMEM((ppb, 2, rows, LANES), F32), state, state, state],
        compiler_params=_cparams(("parallel",)),
        name="s5_scan",
    )(qkvu_ctx3, qkvu_lat3, w_c, m_c, vt_c, a_c)


def _s5_mats_kernel(*refs):
    for gg in range(S5_MATS_GROUPS):
        _s5_group_mats(gg, *refs)


def _s5_group_mats(gg, prm_ref, btr_ref, bti_ref, cr_ref, ci_ref, w_ref, m_ref, vt_ref, a_ref):
    t = CHUNK
    width = CHUNK * S5_CH
    gl = (pl.program_id(1) * S5_MATS_GROUPS + gg) % SUBLANES
    pair, side = gg // 2, gg % 2
    is_fwd = lax.broadcasted_iota(jnp.int32, (1, LANES), 1) < S5_STATE
    lr = prm_ref[0, gg, 0:1, :]
    li = prm_ref[0, gg, 1:2, :]
    dt = jnp.exp(prm_ref[0, gg, 2:3, :])
    n = lax.broadcasted_iota(jnp.int32, (3 * SUBLANES, LANES), 0).astype(F32)
    pmag = jnp.exp(n * (lr * dt))
    pw_re = pmag * jnp.cos(n * (li * dt))
    pw_im = pmag * jnp.sin(n * (li * dt))
    ab_re, ab_im = pw_re[1:2, :], pw_im[1:2, :]
    den = lr * lr + li * li
    nr = ab_re - 1.0
    z_re = (nr * lr + ab_im * li) / den
    z_im = (ab_im * lr - nr * li) / den
    bt_re, bt_im = btr_ref[0, gg], bti_ref[0, gg]
    bb_re = z_re * bt_re - z_im * bt_im
    bb_im = z_re * bt_im + z_im * bt_re
    c_re, c_im = cr_ref[0, gg], ci_ref[0, gg]

    def powers(n_fwd, n_rev):
        return (jnp.where(is_fwd, pw_re[n_fwd:n_fwd + 1, :], pw_re[n_rev:n_rev + 1, :]),
                jnp.where(is_fwd, pw_im[n_fwd:n_fwd + 1, :], pw_im[n_rev:n_rev + 1, :]))

    def block_rows(s, row0=0):
        pos = SUBLANES * (s // SUBLANES) + (s % SUBLANES + gl) % SUBLANES
        return pl.ds(pl.multiple_of(row0 + pos * S5_CH, S5_CH), S5_CH)

    c0 = side * width
    for s in range(t):
        rows = block_rows(s)
        p_re, p_im = powers(t - 1 - s, s)
        w_ref[0, pair, rows, c0:c0 + LANES] = (bb_re * p_re - bb_im * p_im).astype(BF16)
        w_ref[0, pair, rows, c0 + LANES:c0 + 2 * LANES] = (bb_re * p_im + bb_im * p_re).astype(BF16)
        q_re, q_im = powers(s + 1, t - s)
        v_rows = block_rows(s, c0)
        vt_ref[0, pair, v_rows, 0:LANES] = (c_re * q_re - c_im * q_im).astype(BF16)
        vt_ref[0, pair, v_rows, LANES:2 * LANES] = (-(c_re * q_im + c_im * q_re)).astype(BF16)

    ca_re, ca_im = [], []
    for lag in range(t):
        p_re, p_im = powers(lag, t - 1 - lag)
        ca_re.append(c_re * p_re - c_im * p_im)
        ca_im.append(c_re * p_im + c_im * p_re)
    stack = jnp.concatenate([jnp.concatenate(ca_re, axis=0), jnp.concatenate(ca_im, axis=0)], axis=1)
    zero = jnp.zeros_like(bb_re)
    lhs = jnp.concatenate([
        jnp.concatenate([jnp.where(is_fwd, bb_re, zero), jnp.where(is_fwd, -bb_im, zero)], axis=1),
        jnp.concatenate([jnp.where(is_fwd, zero, bb_re), jnp.where(is_fwd, zero, -bb_im)], axis=1)], axis=0)
    kt = lax.dot_general(lhs, stack, (((1,), (1,)), ((), ())), precision=HIGHEST, preferred_element_type=F32)
    kt_f, kt_r = kt[0:S5_CH], kt[S5_CH:2 * S5_CH]
    blk = lax.broadcasted_iota(jnp.int32, (S5_CH, 2 * LANES), 1) // S5_CH
    for s in range(t):
        strip = (jnp.where(blk >= s, pltpu.roll(kt_f, S5_CH * s, 1), 0.0)
                 + jnp.where(blk <= s, pltpu.roll(kt_r, (S5_CH * (s - t + 1)) % (2 * LANES), 1), 0.0))
        strip = jnp.concatenate([pltpu.roll(strip[:, 0:LANES], gl * S5_CH, 1),
                                 pltpu.roll(strip[:, LANES:2 * LANES], gl * S5_CH, 1)], axis=1)
        m_ref[0, pair, block_rows(s), c0:c0 + width] = strip.astype(BF16)

    a_ref[0, gg, :, 0:LANES] = jnp.broadcast_to(pw_re[t:t + 1, :], (SUBLANES, LANES))
    a_ref[0, gg, :, LANES:2 * LANES] = jnp.broadcast_to(pw_im[t:t + 1, :], (SUBLANES, LANES))


def _s5_matrices(lam_re, lam_im, log_dt, b_re, b_im, c_re, c_im):
    depth, _, g, p = lam_re.shape
    hc = b_re.shape[-1]
    width = CHUNK * hc
    both = lambda x: jnp.transpose(x.astype(F32), (0, 2, 1, 3)).reshape(depth, g, 1, 2 * p)
    dt_rows = jnp.broadcast_to(jnp.transpose(log_dt.astype(F32), (0, 2, 1))[..., None], (depth, g, 2, p))
    prm = jnp.concatenate([both(lam_re), both(lam_im), dt_rows.reshape(depth, g, 1, 2 * p),
                           jnp.zeros((depth, g, SUBLANES - 3, 2 * p), F32)], axis=2)
    bt = lambda x: jnp.transpose(x.astype(F32), (0, 2, 4, 1, 3)).reshape(depth, g, hc, 2 * p)
    ct = lambda x: jnp.transpose(x.astype(F32), (0, 2, 3, 1, 4)).reshape(depth, g, hc, 2 * p)
    gs = S5_MATS_GROUPS
    vec = lambda rows: pl.BlockSpec((1, gs, rows, 2 * p), lambda l, i: (l, i, 0, 0))
    return pl.pallas_call(
        _s5_mats_kernel,
        out_shape=(jax.ShapeDtypeStruct((depth, g // 2, width, 2 * width), BF16),
                   jax.ShapeDtypeStruct((depth, g // 2, width, 2 * width), BF16),
                   jax.ShapeDtypeStruct((depth, g // 2, 2 * width, width), BF16),
                   jax.ShapeDtypeStruct((depth, g, SUBLANES, 4 * p), F32)),
        grid=(depth, g // gs),
        in_specs=[vec(SUBLANES), vec(hc), vec(hc), vec(hc), vec(hc)],
        out_specs=(pl.BlockSpec((1, gs // 2, width, 2 * width), lambda l, i: (l, i, 0, 0)),
                   pl.BlockSpec((1, gs // 2, width, 2 * width), lambda l, i: (l, i, 0, 0)),
                   pl.BlockSpec((1, gs // 2, 2 * width, width), lambda l, i: (l, i, 0, 0)),
                   pl.BlockSpec((1, gs, SUBLANES, 4 * p), lambda l, i: (l, i, 0, 0))),
        compiler_params=_cparams(("parallel", "parallel")),
        name="s5_matrices",
    )(prm, bt(b_re), bt(b_im), ct(c_re), ct(c_im))


def _s5_mixer(qkvu_lat, qkvu_ctx, mats, layer, bsz):
    n_cols = qkvu_lat.shape[1]
    return _s5_scan(qkvu_ctx.reshape(bsz, -1, n_cols), qkvu_lat.reshape(bsz, -1, n_cols), mats, layer)


def _outproj_kernel(na_ref, y_ref, u_ref, h_ref, mod_ref, d_ref, wglu_ref, bglu_ref, wout_ref, g2_ref,
                    ho_ref, f_ref):
    tile_chunks = TM_PROJ // CHUNK
    y = jnp.concatenate(
        [jnp.concatenate([y_ref[0, blk, pl.ds(sub * TM_PROJ + c, CHUNK, stride=tile_chunks), :]
                          for sub in range(y_ref.shape[2] // TM_PROJ) for c in range(tile_chunks)], axis=0)
         for blk in range(S5_WIDTH // LANES)], axis=1)
    z = jax.nn.gelu(y + d_ref[...] * u_ref[...])
    s5 = z * jax.nn.sigmoid(_dot(z.astype(BF16), wglu_ref[...]) + bglu_ref[...])
    mix = (_dot(na_ref[...].astype(BF16), wout_ref[0:NA_WIDTH, :])
           + _dot(s5.astype(BF16), wout_ref[NA_WIDTH:NA_WIDTH + S5_WIDTH, :]))
    d = D_MODEL
    gate = mod_ref[0, :, 2 * d:3 * d]
    h = h_ref[...] + gate * mix
    ho_ref[...] = h
    ms = jnp.mean(h * h, axis=-1, keepdims=True)
    y2 = h * lax.rsqrt(ms + EPS) * g2_ref[...]
    f_ref[...] = y2 * (1.0 + mod_ref[0, :, 4 * d:5 * d]) + mod_ref[0, :, 3 * d:4 * d]


def _out_projection(na, y_all, qkvu, h2d, mod3, d_skip, wglu_bf16, b_glu, wout_bf16, g2,
                    mod_map, rows_per_batch, y_row0, tm=TM_PROJ):
    r, d = h2d.shape
    tiles_per_batch = rows_per_batch // tm
    u_blk = 3 * NA_WIDTH // S5_WIDTH
    y_block0 = y_row0 // tm

    def y_map(i):
        return (i // tiles_per_batch, 0, y_block0 + i % tiles_per_batch, 0)

    const = lambda i: (0, 0)
    return pl.pallas_call(
        _outproj_kernel,
        out_shape=(jax.ShapeDtypeStruct((r, d), F32), jax.ShapeDtypeStruct((r, d), F32)),
        grid=(r // tm,),
        in_specs=[
            pl.BlockSpec((tm, NA_WIDTH), lambda i: (i, 0)),
            pl.BlockSpec((1, S5_WIDTH // LANES, tm, LANES), y_map),
            pl.BlockSpec((tm, S5_WIDTH), lambda i: (i, u_blk)),
            pl.BlockSpec((tm, d), lambda i: (i, 0)),
            pl.BlockSpec((1, 1, N_MOD * d), mod_map),
            pl.BlockSpec((1, S5_WIDTH), const),
            pl.BlockSpec((S5_WIDTH, S5_WIDTH), const),
            pl.BlockSpec((1, S5_WIDTH), const),
            pl.BlockSpec((NA_WIDTH + S5_WIDTH, d), const),
            pl.BlockSpec((1, d), const),
        ],
        out_specs=(pl.BlockSpec((tm, d), lambda i: (i, 0)), pl.BlockSpec((tm, d), lambda i: (i, 0))),
        compiler_params=_cparams(("parallel",)),
        name="out_projection",
    )(na, y_all, qkvu, h2d, mod3, d_skip.reshape(1, -1), wglu_bf16, b_glu.reshape(1, -1), wout_bf16,
      g2.reshape(1, d))


def _top2(vals):
    best = vals[0]
    bi = jnp.zeros(best.shape, jnp.int32)
    for i in range(1, len(vals)):
        gt = vals[i] > best
        best = jnp.where(gt, vals[i], best)
        bi = jnp.where(gt, i, bi)
    second = jnp.full(best.shape, -jnp.inf, F32)
    si = jnp.zeros(best.shape, jnp.int32)
    for i in range(len(vals)):
        cand = jnp.where(bi == i, -jnp.inf, vals[i])
        gt = cand > second
        second = jnp.where(gt, cand, second)
        si = jnp.where(gt, i, si)
    return best, bi, second, si


def _route(f, rwt_hi, rwt_lo, rb):
    f_hi = f.astype(BF16)
    f_lo = (f - f_hi.astype(F32)).astype(BF16)
    logits = _dot_nt(rwt_hi, f_hi) + (_dot_nt(rwt_hi, f_lo) + _dot_nt(rwt_lo, f_hi))
    m = jnp.max(logits, axis=0, keepdims=True)
    e = jnp.exp(logits - m)
    probs = e / jnp.sum(e, axis=0, keepdims=True)
    sel = probs + rb
    sel_rows = [sel[i:i + 1, :] for i in range(N_EXPERTS)]
    prob_rows = [probs[i:i + 1, :] for i in range(N_EXPERTS)]
    scores = []
    for g in range(N_GROUPS):
        b, _, s, _ = _top2(sel_rows[g * EPG:(g + 1) * EPG])
        scores.append(b + s)
    grp = jnp.zeros(scores[0].shape, jnp.int32)
    gbest = scores[0]
    for g in range(1, N_GROUPS):
        gt = scores[g] > gbest
        gbest = jnp.where(gt, scores[g], gbest)
        grp = jnp.where(gt, g, grp)
    in_rows = []
    for j in range(EPG):
        v = sel_rows[j]
        for g in range(1, N_GROUPS):
            v = jnp.where(grp == g, sel_rows[g * EPG + j], v)
        in_rows.append(v)
    _, l1, _, l2 = _top2(in_rows)
    i1 = grp * EPG + l1
    i2 = grp * EPG + l2
    w1 = jnp.zeros(gbest.shape, F32)
    w2 = jnp.zeros(gbest.shape, F32)
    for i in range(N_EXPERTS):
        w1 = jnp.where(i1 == i, prob_rows[i], w1)
        w2 = jnp.where(i2 == i, prob_rows[i], w2)
    tot = w1 + w2
    return i1, i2, w1 / tot, w2 / tot


def _router_kernel(f_ref, rwh_ref, rwl_ref, rb_ref, idx_ref, gate_ref):
    i1, i2, g1, g2 = _route(f_ref[...], rwh_ref[...], rwl_ref[...], rb_ref[...])
    idx_ref[0:1, :] = i1
    idx_ref[1:2, :] = i2
    gate_ref[0:1, :] = g1
    gate_ref[1:2, :] = g2


def _router(f_all, router_wt, router_b):
    n, d = f_all.shape
    rwt_hi = router_wt.astype(BF16)
    rwt_lo = (router_wt - rwt_hi.astype(F32)).astype(BF16)
    return pl.pallas_call(
        _router_kernel,
        out_shape=(jax.ShapeDtypeStruct((2, n), jnp.int32), jax.ShapeDtypeStruct((2, n), F32)),
        grid=(n // TM_ROUTE,),
        in_specs=[
            pl.BlockSpec((TM_ROUTE, d), lambda i: (i, 0)),
            pl.BlockSpec((N_EXPERTS, d), lambda i: (0, 0)),
            pl.BlockSpec((N_EXPERTS, d), lambda i: (0, 0)),
            pl.BlockSpec((N_EXPERTS, 1), lambda i: (0, 0)),
        ],
        out_specs=(pl.BlockSpec((2, TM_ROUTE), lambda i: (0, i)), pl.BlockSpec((2, TM_ROUTE), lambda i: (0, i))),
        compiler_params=_cparams(("parallel",)),
        name="router",
    )(f_all, rwt_hi, rwt_lo, router_b)


def _scatter_rows_kernel(dest_ref, pad_ref, end_ref, *refs, n_tok, seg_tiles):
    f_refs = refs[:len(seg_tiles)]
    xs_ref, zero_scr, stage, sems = refs[len(seg_tiles):]
    i = pl.program_id(0)
    tm = f_refs[0].shape[0]
    d_blocks = f_refs[0].shape[1] // LANES
    n_rows = xs_ref.shape[0] // d_blocks
    fill_sem = sems.at[2]

    def token_rows(ref, first, n):
        start = first * d_blocks
        if not isinstance(first, int):
            start = pl.multiple_of(start, d_blocks)
        return ref.at[pl.ds(start, n * d_blocks), :]

    def slab_copy(start):
        return pltpu.make_async_copy(zero_scr, token_rows(xs_ref, start, TM_EXP), fill_sem)

    @pl.when(i == 0)
    def _():
        zero_scr[...] = jnp.zeros(zero_scr.shape, zero_scr.dtype)
        for e in range(N_EXPERTS):
            slab_copy(jnp.minimum(pad_ref[e], n_rows - TM_EXP)).start()
        for e in range(N_EXPERTS):
            slab_copy(0).wait()
        for k in range(N_EXPERTS):
            start = end_ref[0] + k * TM_EXP

            @pl.when(start < n_rows)
            def _():
                cp = slab_copy(start)
                cp.start()
                cp.wait()

    slot = i % 2

    def wait_tile(s):
        for _ in range(2):
            pltpu.make_async_copy(stage.at[s], token_rows(xs_ref, 0, tm), sems.at[s]).wait()

    def scatter_tile(f_ref):
        base = i * tm
        for k in range(d_blocks):
            stage[slot, pl.ds(k, tm, stride=d_blocks), :] = f_ref[:, k * LANES:(k + 1) * LANES]

        def row_copy(r, d):
            return pltpu.make_async_copy(token_rows(stage.at[slot], r, 1), token_rows(xs_ref, d, 1), sems.at[slot])

        for r in range(tm):
            row_copy(r, dest_ref[base + r]).start(priority=0)
            row_copy(r, dest_ref[n_tok + base + r]).start(priority=1)

    tile0 = 0
    for f_ref, n_t in zip(f_refs, seg_tiles):
        pl.when((i >= tile0) & (i < tile0 + n_t))(functools.partial(scatter_tile, f_ref))
        tile0 += n_t

    pl.when(i > 0)(lambda: wait_tile(1 - slot))
    pl.when(i == pl.num_programs(0) - 1)(lambda: wait_tile(slot))


def _scatter_rows(segments, dest_flat, pad_start, total_end):
    d = segments[0].shape[1]
    seg_tiles = tuple(s.shape[0] // TM_SCATTER for s in segments)
    n_tok = sum(s.shape[0] for s in segments)
    r_max = 2 * n_tok + N_EXPERTS * TM_EXP
    in_specs = []
    tile0 = 0
    for n_t in seg_tiles:
        in_specs.append(pl.BlockSpec(
            (TM_SCATTER, d), lambda i, *_, t0=tile0, nt=n_t: (jnp.clip(i - t0, 0, nt - 1), 0)))
        tile0 += n_t
    grid_spec = pltpu.PrefetchScalarGridSpec(
        num_scalar_prefetch=3,
        grid=(tile0,),
        in_specs=in_specs,
        out_specs=pl.BlockSpec(memory_space=pl.ANY),
        scratch_shapes=[pltpu.VMEM((TM_EXP * d // LANES, LANES), F32),
                        pltpu.VMEM((2, TM_SCATTER * d // LANES, LANES), F32),
                        pltpu.SemaphoreType.DMA((3,))],
    )
    return pl.pallas_call(
        functools.partial(_scatter_rows_kernel, n_tok=n_tok, seg_tiles=seg_tiles),
        out_shape=jax.ShapeDtypeStruct((r_max * d // LANES, LANES), F32),
        grid_spec=grid_spec,
        compiler_params=_cparams(("arbitrary",)),
        name="moe_scatter_rows",
    )(dest_flat, pad_start, total_end, *segments)


def _experts_kernel(te_ref, nv_ref, x_ref, wg_ref, wu_ref, wd_ref, o_ref, wg_scr, wu_scr, wd_scr):
    i = pl.program_id(0)
    e = te_ref[i]
    prev = te_ref[jnp.maximum(i - 1, 0)]
    rows = 128

    @pl.when((i == 0) | (e != prev))
    def _():
        def body(r, carry):
            sl = pl.ds(pl.multiple_of(r * rows, rows), rows)
            wg_scr[sl, :] = wg_ref[0, 0, sl, :].astype(BF16)
            wu_scr[sl, :] = wu_ref[0, 0, sl, :].astype(BF16)
            wd_scr[sl, :] = wd_ref[0, 0, sl, :].astype(BF16)
            return carry
        lax.fori_loop(0, wg_scr.shape[0] // rows, body, 0)

    @pl.when(i < nv_ref[0])
    def _():
        d_blocks = wg_scr.shape[0] // LANES
        x = jnp.concatenate([x_ref[pl.ds(k, TM_EXP, stride=d_blocks), :].astype(BF16) for k in range(d_blocks)],
                            axis=1)
        g = _dot(x, wg_scr[...])
        u = _dot(x, wu_scr[...])
        a = (g * jax.nn.sigmoid(g)) * u
        o_ref[...] = _dot(a.astype(BF16), wd_scr[...]).astype(BF16)

    @pl.when(i >= nv_ref[0])
    def _():
        o_ref[...] = jnp.zeros(o_ref.shape, BF16)


def _experts(xs, tile_expert, n_valid, w_gate, w_up, w_down, layer):
    d, de = w_gate.shape[2], w_gate.shape[3]
    d_blocks = d // LANES
    r = xs.shape[0] // d_blocks
    n_tiles = r // TM_EXP
    x_map = lambda i, te, nv: (jnp.minimum(i, nv[0] - 1), 0)
    grid_spec = pltpu.PrefetchScalarGridSpec(
        num_scalar_prefetch=2,
        grid=(n_tiles,),
        in_specs=[
            pl.BlockSpec((TM_EXP * d_blocks, LANES), x_map),
            pl.BlockSpec((1, 1, d, de), lambda i, te, nv: (layer, te[i], 0, 0)),
            pl.BlockSpec((1, 1, d, de), lambda i, te, nv: (layer, te[i], 0, 0)),
            pl.BlockSpec((1, 1, de, d), lambda i, te, nv: (layer, te[i], 0, 0)),
        ],
        out_specs=pl.BlockSpec((TM_EXP, d), lambda i, te, nv: (i, 0)),
        scratch_shapes=[pltpu.VMEM((d, de), BF16), pltpu.VMEM((d, de), BF16), pltpu.VMEM((de, d), BF16)],
    )
    return pl.pallas_call(
        _experts_kernel,
        out_shape=jax.ShapeDtypeStruct((r, d), BF16),
        grid_spec=grid_spec,
        compiler_params=_cparams(("arbitrary",)),
        name="experts",
    )(tile_expert, n_valid, xs, w_gate, w_up, w_down)


def _dispatch(idx):
    n = idx.shape[1]
    e_flat = idx.reshape(-1)
    onehot = (e_flat[:, None] == jnp.arange(N_EXPERTS, dtype=jnp.int32)[None, :]).astype(jnp.int32)
    csum = jnp.cumsum(onehot, axis=0)
    rank = jnp.sum(csum * onehot, axis=1) - 1
    counts = csum[-1]
    padded = ((counts + TM_EXP - 1) // TM_EXP) * TM_EXP
    ends = jnp.cumsum(padded)
    starts = ends - padded
    dest = (jnp.sum(onehot * starts[None, :], axis=1) + rank).astype(jnp.int32)
    r_max = 2 * n + N_EXPERTS * TM_EXP
    tile_start = jnp.arange(r_max // TM_EXP, dtype=jnp.int32) * TM_EXP
    tile_expert = jnp.minimum(jnp.sum((tile_start[:, None] >= ends[None, :]).astype(jnp.int32), axis=1),
                              N_EXPERTS - 1).astype(jnp.int32)
    n_valid = (ends[-1] // TM_EXP).astype(jnp.int32).reshape(1)
    pad_start = (starts + counts).astype(jnp.int32)
    total_end = ends[-1].astype(jnp.int32).reshape(1)
    return dest, pad_start, total_end, tile_expert, n_valid


def _combine_kernel(h_ref, y1_ref, y2_ref, gate_ref, mod_ref, o_ref):
    d = D_MODEL
    g = gate_ref[...]
    y = g[:, 0:1] * y1_ref[...].astype(F32) + g[:, 1:2] * y2_ref[...].astype(F32)
    o_ref[...] = h_ref[...] + mod_ref[0, :, 5 * d:6 * d] * y


def _combine(h2d, y1, y2, gates_t, mod3, mod_map, row0, tm=TM_PROJ):
    r, d = h2d.shape
    blk0 = row0 // tm
    row = lambda i: (i, 0)
    seg = lambda i: (blk0 + i, 0)
    return pl.pallas_call(
        _combine_kernel,
        out_shape=jax.ShapeDtypeStruct((r, d), F32),
        grid=(r // tm,),
        in_specs=[
            pl.BlockSpec((tm, d), row),
            pl.BlockSpec((tm, d), seg),
            pl.BlockSpec((tm, d), seg),
            pl.BlockSpec((tm, 2), seg),
            pl.BlockSpec((1, 1, N_MOD * d), mod_map),
        ],
        out_specs=pl.BlockSpec((tm, d), row),
        compiler_params=_cparams(("parallel",)),
        name="moe_combine",
    )(h2d, y1, y2, gates_t, mod3)


def kernel(x, c, ctx, c_ctx, w_mod, b_mod, norm1_g, norm2_g, w_in, w_out, q_norm_g, k_norm_g, na_rpb,
           s5_lam_re, s5_lam_im, s5_log_dt, s5_b_re, s5_b_im, s5_c_re, s5_c_im, s5_d, s5_w_glu, s5_b_glu,
           router_w, router_bias, moe_w_gate, moe_w_up, moe_w_down):
    bsz, n_lat, d = x.shape
    n_ctx = ctx.shape[1]
    depth = w_mod.shape[0]
    ctx_row = bsz
    c_rows = jnp.concatenate([c.astype(F32), c_ctx.astype(F32)[None],
                              jnp.zeros((SUBLANES - bsz - 1, d), F32)], axis=0)
    mod_all = _modulation(c_rows, w_mod.astype(F32), b_mod.astype(F32))

    h_lat = x.reshape(bsz * n_lat, d).astype(F32)
    h_ctx = ctx.reshape(bsz * n_ctx, d).astype(F32)
    lat_map = _mod_row_map(n_lat, 0, True)
    lat_map_in = _mod_row_map(n_lat, 0, True, TM_IN)
    ctx_map = _mod_row_map(n_ctx, ctx_row, False)
    bias_tabs = _na_bias_tables(na_rpb)
    router_wt = router_w.T.astype(F32)
    router_b = router_bias.reshape(N_EXPERTS, 1).astype(F32)
    s5_mats = _s5_matrices(s5_lam_re, s5_lam_im, s5_log_dt, s5_b_re, s5_b_im, s5_c_re, s5_c_im)

    n_l = bsz * n_lat
    pending = None
    for layer in range(depth):
        ctx_out = layer < depth - 1
        mod3 = mod_all[layer].reshape(SUBLANES, 1, N_MOD * d)
        w_in_b = w_in[layer].astype(BF16)
        proj = functools.partial(_in_projection, g=norm1_g[layer], mod3=mod3, w_bf16=w_in_b,
                                 qg=q_norm_g[layer], kg=k_norm_g[layer])
        if pending is None:
            qkvu_lat = proj(h_lat, mod_map=lat_map_in, tm=TM_IN)
            qkvu_ctx = proj(h_ctx, mod_map=ctx_map)
        else:
            qkvu_lat, h_lat = proj(h_lat, mod_map=lat_map_in, pending=pending, row0=0, tm=TM_IN)
            qkvu_ctx, h_ctx = proj(h_ctx, mod_map=ctx_map, pending=pending, row0=n_l)
        na_lat = _neighborhood_attention(qkvu_lat, qkvu_ctx, bias_tabs, layer, bsz)
        y_all = _s5_mixer(qkvu_lat, qkvu_ctx, s5_mats, layer, bsz)
        wglu_b = s5_w_glu[layer].astype(BF16)
        wout_b = w_out[layer].astype(BF16)
        h_lat, f_lat = _out_projection(na_lat, y_all, qkvu_lat, h_lat, mod3, s5_d[layer], wglu_b,
                                       s5_b_glu[layer], wout_b, norm2_g[layer],
                                       lat_map_in, n_lat, 0, tm=TM_IN)
        idx, gates = _router(f_lat, router_wt, router_b)
        if ctx_out:
            na_ctx = _context_attention(qkvu_ctx, bsz)
            h_ctx, f_ctx = _out_projection(na_ctx, y_all, qkvu_ctx, h_ctx, mod3, s5_d[layer], wglu_b,
                                           s5_b_glu[layer], wout_b, norm2_g[layer],
                                           ctx_map, n_ctx, n_lat)
            idx_c, gates_c = _router(f_ctx, router_wt, router_b)
            idx = jnp.concatenate([idx, idx_c], axis=1)
            gates = jnp.concatenate([gates, gates_c], axis=1)
        n_tok = idx.shape[1]
        dest, pad_start, total_end, tile_expert, n_valid = _dispatch(idx)
        xs = _scatter_rows([f_lat, f_ctx] if ctx_out else [f_lat], dest, pad_start, total_end)
        ys = _experts(xs, tile_expert, n_valid, moe_w_gate, moe_w_up, moe_w_down, layer)
        y1 = jnp.take(ys, dest[:n_tok], axis=0, mode="clip")
        y2 = jnp.take(ys, dest[n_tok:], axis=0, mode="clip")
        gates_t = gates.T
        if ctx_out:
            pending = (y1, y2, gates_t, mod3)
        else:
            h_lat = _combine(h_lat, y1, y2, gates_t, mod3, lat_map_in, 0, tm=TM_IN)
    return h_lat.reshape(bsz, n_lat, d).astype(x.dtype)
```
